```python
import jax
import jax.numpy as jnp
from jax import lax
import numpy as np

D_MODEL = 1024
BATCH = 8
SEQ = 2048
DEPTH = 1
DEC_BATCH = 32
DEC_SEQ = 4
PAST_LEN = 8192
PAGE_SIZE = 128

A_HEADS = 8
A_HEAD_DIM = 64
A_WIDTH = A_HEADS * A_HEAD_DIM
A_DECAY_LORA = 64
A_ICLR_LORA = 64
A_GATE_LORA = 128
A_PROJ = 3 * A_WIDTH + A_DECAY_LORA + A_ICLR_LORA + A_GATE_LORA
A_LNX_EPS = 64e-5

B_GROUPS = ((128, 1), (512, 4), (2048, 16))
B_HEADS_PER_GROUP = 4
B_HEAD_DIM = 64
B_HEADS = B_HEADS_PER_GROUP * len(B_GROUPS)
B_WIDTH = B_HEADS * B_HEAD_DIM
B_PROJ = 3 * B_WIDTH
B_OUT = B_HEADS_PER_GROUP * B_HEAD_DIM
ROPE_THETA = 500000.0
ROPE_DIM = B_HEAD_DIM // 4

N_BRANCH = 2
IN_PROJ = A_PROJ + B_PROJ + N_BRANCH * D_MODEL

N_MEM = 256
MEM_HEADS = 4
MEM_HEAD_DIM = 128
MEM_WIDTH = MEM_HEADS * MEM_HEAD_DIM

N_GROUPS = 4
EXPERTS_PER_GROUP = 8
N_EXPERTS = N_GROUPS * EXPERTS_PER_GROUP
TOP_K = 2
EXPERT_FF = 512
MOE_BLOCK = 128

RMS_EPS = 1e-6

kernel_name = 'hybrid_rwkv7_dilated_swa_hmoe_step'


def rmsnorm(x, g):
    xf = x.astype(jnp.float32)
    xf = xf * lax.rsqrt(jnp.mean(xf * xf, axis=-1, keepdims=True) + RMS_EPS)
    return (xf * g.astype(jnp.float32)).astype(x.dtype)


def rope_partial(x, pos):
    half = ROPE_DIM // 2
    inv_freq = ROPE_THETA ** (-jnp.arange(half, dtype=jnp.float32) * 2.0 / ROPE_DIM)
    ang = pos.astype(jnp.float32)[:, None] * inv_freq[None, :]
    cos = jnp.cos(ang)[None, :, None, :]
    sin = jnp.sin(ang)[None, :, None, :]
    xr = x[..., :ROPE_DIM].astype(jnp.float32)
    x1, x2 = xr[..., :half], xr[..., half:]
    rot = jnp.concatenate([x1 * cos - x2 * sin, x2 * cos + x1 * sin], axis=-1)
    return jnp.concatenate([rot.astype(x.dtype), x[..., ROPE_DIM:]], axis=-1)


def rwkv7_mix(u, shift0, s0, mu, w0, w2, a0, a2, g2, k_k, k_a, r_k, lnx_g, lnx_b):
    f32 = jnp.float32
    bsz, t_len, _ = u.shape
    u_prev = jnp.concatenate([shift0[:, None, :].astype(u.dtype), u[:, :-1]], axis=1)
    um = u + (u_prev - u) * mu
    o1, o2, o3 = A_WIDTH, 2 * A_WIDTH, 3 * A_WIDTH
    o4 = o3 + A_DECAY_LORA
    o5 = o4 + A_ICLR_LORA
    r, k, v = um[..., :o1], um[..., o1:o2], um[..., o2:o3]
    xw, xa, xg = um[..., o3:o4], um[..., o4:o5], um[..., o5:]
    w = -jax.nn.softplus(-(w0 + jnp.tanh(xw) @ w2)) - 0.5
    decay = jnp.exp(-jnp.exp(w.astype(f32)))
    a = jax.nn.sigmoid(a0 + xa @ a2)
    g = jax.nn.sigmoid(xg) @ g2

    def heads(t):
        return t.astype(f32).reshape(bsz, t_len, A_HEADS, A_HEAD_DIM)

    kk = heads(k * k_k)
    kk = kk / jnp.maximum(jnp.sqrt(jnp.sum(kk * kk, axis=-1, keepdims=True)), 1e-12)
    k = k * (1 + (a - 1) * k_a)
    r_h, k_h, v_h, a_h, d_h = heads(r), heads(k), heads(v), heads(a), heads(decay)
    b_h = kk * a_h

    def step(state, inp):
        r_t, d_t, k_t, v_t, kk_t, b_t = inp
        sa = jnp.einsum('bhvk,bhk->bhv', state, -kk_t)
        state = state * d_t[:, :, None, :] + sa[..., None] * b_t[:, :, None, :] + v_t[..., None] * k_t[:, :, None, :]
        return state, jnp.einsum('bhvk,bhk->bhv', state, r_t)

    xs = tuple(jnp.swapaxes(t, 0, 1) for t in (r_h, d_h, k_h, v_h, kk, b_h))
    s_fin, ys = lax.scan(step, s0.astype(f32), xs)
    y = jnp.swapaxes(ys, 0, 1)
    mean = jnp.mean(y, axis=-1, keepdims=True)
    var = jnp.mean(jnp.square(y - mean), axis=-1, keepdims=True)
    y = (y - mean) * lax.rsqrt(var + A_LNX_EPS)
    y = y.reshape(bsz, t_len, A_WIDTH) * lnx_g.astype(f32) + lnx_b.astype(f32)
    bonus = jnp.sum(r_h * k_h * r_k.astype(f32), axis=-1, keepdims=True) * v_h
    out = (y + bonus.reshape(bsz, t_len, A_WIDTH)) * g.astype(f32)
    return out.astype(u.dtype), s_fin, u[:, -1]


def masked_softmax_stats(s, valid):
    s = jnp.where(valid, s, -jnp.inf)
    m = jnp.max(s, axis=-1, keepdims=True)
    p = jnp.exp(s - m)
    den = jnp.sum(p, axis=-1, keepdims=True)
    return p / den, (m + jnp.log(den))[..., 0]


def dilated_group_prompt(q, k, v, window, dilation):
    bsz, s_len, nh, dh = q.shape
    nk = window // dilation
    n_cls = s_len // dilation
    nb = -(-n_cls // nk)
    pad = nb * nk - n_cls

    def to_blocks(t):
        t = t.reshape(bsz, n_cls, dilation, nh, dh).transpose(0, 2, 1, 3, 4)
        t = jnp.pad(t, ((0, 0), (0, 0), (0, pad), (0, 0), (0, 0)))
        return t.reshape(bsz, dilation, nb, nk, nh, dh).astype(jnp.float32)

    def band(t):
        prev = jnp.pad(t[:, :, :-1], ((0, 0), (0, 0), (1, 0), (0, 0), (0, 0), (0, 0)))
        return jnp.concatenate([prev, t], axis=3)

    qb = to_blocks(q)
    kb = band(to_blocks(k))
    vb = band(to_blocks(v))
    s = jnp.einsum('brnqhd,brnkhd->brnhqk', qb, kb) * (dh ** -0.5)
    qi = jnp.arange(nk)[:, None] + nk
    ki = jnp.arange(2 * nk)[None, :]
    dist = qi - ki
    first = (jnp.arange(nb) == 0)[:, None, None]
    valid = (dist >= 0) & (dist <= nk) & ~(first & (ki < nk))
    p, lse = masked_softmax_stats(s, valid[:, None])
    o = jnp.einsum('brnhqk,brnkhd->brnqhd', p, vb)
    o = o.reshape(bsz, dilation, nb * nk, nh, dh)[:, :, :n_cls].transpose(0, 2, 1, 3, 4).reshape(bsz, s_len, nh, dh)
    lse = jnp.swapaxes(lse, 3, 4).reshape(bsz, dilation, nb * nk, nh)[:, :, :n_cls]
    lse = lse.transpose(0, 2, 1, 3).reshape(bsz, s_len, nh)
    return o, lse


def dilated_group_sample(q, k_new, v_new, buf, window, dilation):
    t_len = q.shape[1]
    buf_len = buf.shape[2]
    nk = window // dilation
    k_all = jnp.concatenate([buf[:, 0].astype(k_new.dtype), k_new], axis=1)
    v_all = jnp.concatenate([buf[:, 1].astype(v_new.dtype), v_new], axis=1)
    idx = buf_len + jnp.arange(t_len)[:, None] - dilation * jnp.arange(nk + 1)[None, :]
    valid = idx >= 0
    idx = jnp.maximum(idx, 0)
    kg = jnp.take(k_all, idx, axis=1).astype(jnp.float32)
    vg = jnp.take(v_all, idx, axis=1).astype(jnp.float32)
    s = jnp.einsum('bthd,btjhd->bhtj', q.astype(jnp.float32), kg) * (q.shape[-1] ** -0.5)
    p, lse = masked_softmax_stats(s, valid)
    o = jnp.einsum('bhtj,btjhd->bthd', p, vg)
    new_buf = jnp.stack([k_all[:, t_len:], v_all[:, t_len:]], axis=1)
    return o, jnp.swapaxes(lse, 1, 2), new_buf


def mixing_sublayer(x, pos, s0, shift0, swa_bufs, p):
    bsz, t_len, _ = x.shape
    xn = rmsnorm(x, p['norm_mix_g'])
    u = xn @ p['w_in']
    u_a = u[..., :A_PROJ]
    u_b = u[..., A_PROJ:A_PROJ + B_PROJ].reshape(bsz, t_len, 3, B_HEADS, B_HEAD_DIM)
    gates = jax.nn.sigmoid(u[..., A_PROJ + B_PROJ:] + p['b_gate']).reshape(bsz, t_len, N_BRANCH, D_MODEL)
    y_a, s_fin, shift_new = rwkv7_mix(u_a, shift0, s0, p['rwkv_mu'], p['rwkv_w0'], p['rwkv_w2'], p['rwkv_a0'],
                                      p['rwkv_a2'], p['rwkv_g2'], p['rwkv_k_k'], p['rwkv_k_a'], p['rwkv_r_k'],
                                      p['rwkv_lnx_g'], p['rwkv_lnx_b'])
    q = rope_partial(u_b[:, :, 0], pos)
    k = rope_partial(u_b[:, :, 1], pos)
    v = u_b[:, :, 2]
    outs, lses, bufs = [], [], []
    for gi, (window, dilation) in enumerate(B_GROUPS):
        hs = slice(gi * B_HEADS_PER_GROUP, (gi + 1) * B_HEADS_PER_GROUP)
        if swa_bufs is None:
            o, lse = dilated_group_prompt(q[:, :, hs], k[:, :, hs], v[:, :, hs], window, dilation)
            keep = min(window, t_len)
            buf = jnp.stack([k[:, t_len - keep:, hs], v[:, t_len - keep:, hs]], axis=1)
        else:
            o, lse, buf = dilated_group_sample(q[:, :, hs], k[:, :, hs], v[:, :, hs], swa_bufs[gi], window, dilation)
        outs.append(o)
        lses.append(lse)
        bufs.append(buf)
    w_grp = jax.nn.softmax(jnp.stack(lses, axis=0), axis=0)
    y_b = jnp.einsum('gbth,gbthd->bthd', w_grp, jnp.stack(outs, axis=0)).astype(x.dtype).reshape(bsz, t_len, B_OUT)
    merged = gates[:, :, 0] * (y_a @ p['w_branch_a']) + gates[:, :, 1] * (y_b @ p['w_branch_b'])
    return x + merged @ p['w_out'], s_fin, shift_new, bufs


def memory_kv(mem, g, w_mem_kv):
    bsz = mem.shape[0]
    kv = rmsnorm(mem, g) @ w_mem_kv
    return kv.reshape(bsz, N_MEM, 2, MEM_HEADS, MEM_HEAD_DIM).transpose(0, 2, 1, 3, 4)


def memory_sublayer(x, mem_kv, p):
    bsz, t_len, _ = x.shape
    q = (rmsnorm(x, p['norm_mem_g']) @ p['w_mem_q']).reshape(bsz, t_len, MEM_HEADS, MEM_HEAD_DIM)
    s = jnp.einsum('bthd,bmhd->bhtm', q.astype(jnp.float32), mem_kv[:, 0].astype(jnp.float32)) * (MEM_HEAD_DIM ** -0.5)
    o = jnp.einsum('bhtm,bmhd->bthd', jax.nn.softmax(s, axis=-1), mem_kv[:, 1].astype(jnp.float32))
    return x + o.astype(x.dtype).reshape(bsz, t_len, MEM_WIDTH) @ p['w_mem_out']


def expert_dispatch(xt, expert_id, expert_w, w_gate, w_up, w_down):
    n_tok, dm = xt.shape
    n_rows = n_tok * TOP_K
    n_blocks = (n_rows + N_EXPERTS * (MOE_BLOCK - 1) + MOE_BLOCK - 1) // MOE_BLOCK
    flat_e = expert_id.reshape(-1)
    flat_tok = jnp.repeat(jnp.arange(n_tok, dtype=jnp.int32), TOP_K)
    flat_w = expert_w.reshape(-1).astype(jnp.float32)
    order = jnp.argsort(flat_e)
    sorted_e = flat_e[order]
    counts = jnp.bincount(flat_e, length=N_EXPERTS)
    padded = (counts + MOE_BLOCK - 1) // MOE_BLOCK * MOE_BLOCK
    pad_end = jnp.cumsum(padded)
    pad_start = pad_end - padded
    start = jnp.cumsum(counts) - counts
    dest = pad_start[sorted_e] + jnp.arange(n_rows) - start[sorted_e]
    row_tok = jnp.full((n_blocks * MOE_BLOCK,), n_tok, jnp.int32).at[dest].set(flat_tok[order])
    row_w = jnp.zeros((n_blocks * MOE_BLOCK,), jnp.float32).at[dest].set(flat_w[order])
    block_e = jnp.minimum(jnp.searchsorted(pad_end, jnp.arange(n_blocks) * MOE_BLOCK, side='right'), N_EXPERTS - 1)
    x_pad = jnp.concatenate([xt, jnp.zeros((1, dm), xt.dtype)], axis=0)

    def block_fn(args):
        tok, e = args
        xb = x_pad[tok]
        h = jax.nn.silu(xb @ w_gate[e]) * (xb @ w_up[e])
        return (h @ w_down[e]).astype(jnp.float32)

    yb = lax.map(block_fn, (row_tok.reshape(n_blocks, MOE_BLOCK), block_e))
    y = jnp.zeros((n_tok + 1, dm), jnp.float32).at[row_tok].add(yb.reshape(-1, dm) * row_w[:, None])
    return y[:n_tok].astype(xt.dtype)


def hier_moe_sublayer(x, p):
    bsz, t_len, dm = x.shape
    xn = rmsnorm(x, p['norm_ffn_g']).reshape(bsz * t_len, dm)
    logit_g = (xn @ p['w_router_group'] + p['b_router_group']).astype(jnp.float32)
    grp = jnp.argmax(logit_g, axis=-1)
    w_grp = jnp.take_along_axis(jax.nn.softmax(logit_g, axis=-1), grp[:, None], axis=-1)
    logit_e = (xn @ p['w_router_expert'] + p['b_router_expert']).astype(jnp.float32)
    logit_e = logit_e.reshape(-1, N_GROUPS, EXPERTS_PER_GROUP)
    logit_e = jnp.take_along_axis(logit_e, grp[:, None, None], axis=1)[:, 0]
    top_v, top_i = lax.top_k(logit_e, TOP_K)
    gate = jax.nn.softmax(top_v, axis=-1) * w_grp
    eid = grp[:, None] * EXPERTS_PER_GROUP + top_i
    y = expert_dispatch(xn, eid, gate, p['w_exp_gate'], p['w_exp_up'], p['w_exp_down'])
    return x + y.reshape(bsz, t_len, dm)


def decoder_layer(x, pos, s0, shift0, swa_bufs, mem_kv, p):
    x, s_fin, shift_new, bufs = mixing_sublayer(x, pos, s0, shift0, swa_bufs, p)
    x = memory_sublayer(x, mem_kv, p)
    x = hier_moe_sublayer(x, p)
    return x, s_fin, shift_new, bufs


def setup_inputs(seed: int = 0) -> dict:
    key = jax.random.key(seed)
    ks = iter(jax.random.split(key, 64))
    f32 = jnp.float32

    def nrm(shape, scale=1.0):
        return jax.random.normal(next(ks), shape, f32) * scale

    def gain(n):
        return 1.0 + nrm((n,), 0.05)

    swa = [nrm((DEC_BATCH, 2, min(w, PAST_LEN), B_HEADS_PER_GROUP, B_HEAD_DIM)) for w, _ in B_GROUPS]
    return {
        'x_prompt': nrm((BATCH, SEQ, D_MODEL)),
        'x_sample': nrm((DEC_BATCH, DEC_SEQ, D_MODEL)),
        'state_rwkv': nrm((DEC_BATCH, A_HEADS, A_HEAD_DIM, A_HEAD_DIM), 0.5),
        'state_shift': nrm((DEC_BATCH, A_PROJ)),
        'cache_swa_w128': swa[0],
        'cache_swa_w512': swa[1],
        'cache_swa_w2048': swa[2],
        'cache_mem_kv': nrm((DEC_BATCH, 2, N_MEM, MEM_HEADS, MEM_HEAD_DIM)),
        'mem_prompt': nrm((BATCH, N_MEM, D_MODEL)),
        'norm_mix_g': gain(D_MODEL),
        'w_in': nrm((D_MODEL, IN_PROJ), D_MODEL ** -0.5),
        'b_gate': nrm((N_BRANCH * D_MODEL,), 0.02),
        'rwkv_mu': jax.random.uniform(next(ks), (A_PROJ,), f32),
        'rwkv_w0': jax.random.uniform(next(ks), (A_WIDTH,), f32, -6.0, -1.0),
        'rwkv_w2': nrm((A_DECAY_LORA, A_WIDTH), 0.1),
        'rwkv_a0': nrm((A_WIDTH,), 0.1),
        'rwkv_a2': nrm((A_ICLR_LORA, A_WIDTH), 0.1),
        'rwkv_g2': nrm((A_GATE_LORA, A_WIDTH), A_GATE_LORA ** -0.5),
        'rwkv_k_k': 0.85 + nrm((A_WIDTH,), 0.05),
        'rwkv_k_a': 1.0 + nrm((A_WIDTH,), 0.05),
        'rwkv_r_k': nrm((A_HEADS, A_HEAD_DIM), 0.1),
        'rwkv_lnx_g': gain(A_WIDTH),
        'rwkv_lnx_b': nrm((A_WIDTH,), 0.02),
        'w_branch_a': nrm((A_WIDTH, D_MODEL), A_WIDTH ** -0.5),
        'w_branch_b': nrm((B_OUT, D_MODEL), B_OUT ** -0.5),
        'w_out': nrm((D_MODEL, D_MODEL), D_MODEL ** -0.5),
        'norm_mem_g': gain(D_MODEL),
        'norm_memkv_g': gain(D_MODEL),
        'w_mem_q': nrm((D_MODEL, MEM_WIDTH), D_MODEL ** -0.5),
        'w_mem_kv': nrm((D_MODEL, 2 * MEM_WIDTH), D_MODEL ** -0.5),
        'w_mem_out': nrm((MEM_WIDTH, D_MODEL), MEM_WIDTH ** -0.5),
        'norm_ffn_g': gain(D_MODEL),
        'w_router_group': nrm((D_MODEL, N_GROUPS), D_MODEL ** -0.5),
        'b_router_group': nrm((N_GROUPS,), 0.01),
        'w_router_expert': nrm((D_MODEL, N_EXPERTS), D_MODEL ** -0.5),
        'b_router_expert': nrm((N_EXPERTS,), 0.01),
        'w_exp_gate': nrm((N_EXPERTS, D_MODEL, EXPERT_FF), D_MODEL ** -0.5),
        'w_exp_up': nrm((N_EXPERTS, D_MODEL, EXPERT_FF), D_MODEL ** -0.5),
        'w_exp_down': nrm((N_EXPERTS, EXPERT_FF, D_MODEL), EXPERT_FF ** -0.5),
        'norm_final_g': gain(D_MODEL),
    }


def reference(x_prompt, x_sample, state_rwkv, state_shift, cache_swa_w128, cache_swa_w512, cache_swa_w2048,
              cache_mem_kv, mem_prompt, norm_mix_g, w_in, b_gate, rwkv_mu, rwkv_w0, rwkv_w2, rwkv_a0, rwkv_a2,
              rwkv_g2, rwkv_k_k, rwkv_k_a, rwkv_r_k, rwkv_lnx_g, rwkv_lnx_b, w_branch_a, w_branch_b, w_out,
              norm_mem_g, norm_memkv_g, w_mem_q, w_mem_kv, w_mem_out, norm_ffn_g, w_router_group, b_router_group,
              w_router_expert, b_router_expert, w_exp_gate, w_exp_up, w_exp_down, norm_final_g):
    p = {
        'norm_mix_g': norm_mix_g, 'w_in': w_in, 'b_gate': b_gate,
        'rwkv_mu': rwkv_mu, 'rwkv_w0': rwkv_w0, 'rwkv_w2': rwkv_w2, 'rwkv_a0': rwkv_a0, 'rwkv_a2': rwkv_a2,
        'rwkv_g2': rwkv_g2, 'rwkv_k_k': rwkv_k_k, 'rwkv_k_a': rwkv_k_a, 'rwkv_r_k': rwkv_r_k,
        'rwkv_lnx_g': rwkv_lnx_g, 'rwkv_lnx_b': rwkv_lnx_b,
        'w_branch_a': w_branch_a, 'w_branch_b': w_branch_b, 'w_out': w_out,
        'norm_mem_g': norm_mem_g, 'w_mem_q': w_mem_q, 'w_mem_out': w_mem_out,
        'norm_ffn_g': norm_ffn_g, 'w_router_group': w_router_group, 'b_router_group': b_router_group,
        'w_router_expert': w_router_expert, 'b_router_expert': b_router_expert,
        'w_exp_gate': w_exp_gate, 'w_exp_up': w_exp_up, 'w_exp_down': w_exp_down,
    }
    bsz, s_len, _ = x_prompt.shape
    mem_kv_prompt = memory_kv(mem_prompt, norm_memkv_g, w_mem_kv)
    s0 = jnp.zeros((bsz, A_HEADS, A_HEAD_DIM, A_HEAD_DIM), jnp.float32)
    shift0 = jnp.zeros((bsz, A_PROJ), x_prompt.dtype)
    h_p, p_rwkv, p_shift, p_bufs = decoder_layer(x_prompt, jnp.arange(s_len), s0, shift0, None, mem_kv_prompt, p)
    y_prompt = rmsnorm(h_p, norm_final_g)
    t_len = x_sample.shape[1]
    h_s, s_rwkv, s_shift, s_bufs = decoder_layer(x_sample, PAST_LEN + jnp.arange(t_len), state_rwkv, state_shift,
                                                 (cache_swa_w128, cache_swa_w512, cache_swa_w2048), cache_mem_kv, p)
    y_sample = rmsnorm(h_s, norm_final_g)
    return (y_prompt, y_sample, p_rwkv, p_shift, p_bufs[0], p_bufs[1], p_bufs[2], mem_kv_prompt,
            s_rwkv, s_shift, s_bufs[0], s_bufs[1], s_bufs[2])
```

```python
import functools
import math

import jax
import jax.numpy as jnp
import numpy as np
from jax import lax
from jax.experimental import pallas as pl
from jax.experimental.pallas import tpu as pltpu

F32 = jnp.float32
BF16 = jnp.bfloat16

D_MODEL = 1024
A_HEADS = 8
A_HEAD_DIM = 64
A_WIDTH = A_HEADS * A_HEAD_DIM
A_DECAY_LORA = 64
A_ICLR_LORA = 64
A_GATE_LORA = 128
A_PROJ = 3 * A_WIDTH + A_DECAY_LORA + A_ICLR_LORA + A_GATE_LORA
A_LNX_EPS = 64e-5
B_GROUPS = ((128, 1), (512, 4), (2048, 16))
B_HEADS_PER_GROUP = 4
B_HEAD_DIM = 64
B_GROUP_WIDTH = B_HEADS_PER_GROUP * B_HEAD_DIM
B_WIDTH = B_GROUP_WIDTH * len(B_GROUPS)
B_PROJ = 3 * B_WIDTH
ROPE_THETA = 500000.0
ROPE_DIM = B_HEAD_DIM // 4
ROPE_HALF = ROPE_DIM // 2
SWA_BLOCK = 128
N_MEM = 256
MEM_HEADS = 4
MEM_HEAD_DIM = 128
MEM_WIDTH = MEM_HEADS * MEM_HEAD_DIM
N_GROUPS = 4
EXPERTS_PER_GROUP = 8
N_EXPERTS = N_GROUPS * EXPERTS_PER_GROUP
TOP_K = 2
EXPERT_FF = 512
RMS_EPS = 1e-6
PAST_LEN = 8192

LANES = 128
VMEM_LIMIT = 56 * 1024 * 1024
RWKV_CHUNK = 64
MOE_ROWS = 256
NEG_INF = float("-inf")


def _dot(a, b, precision=None):
    return jnp.dot(a, b, preferred_element_type=F32, precision=precision)


def _dot_nt(a, b, precision=None):
    return lax.dot_general(a, b, (((1,), (1,)), ((), ())), preferred_element_type=F32, precision=precision)


def _dot_tn(a, b, precision=None):
    return lax.dot_general(a, b, (((0,), (0,)), ((), ())), preferred_element_type=F32, precision=precision)


def _bf16_round(t):
    return t.astype(BF16).astype(F32)


def _operand(t, exact):
    return t.astype(F32) if exact else t.astype(BF16)


def _precision(exact):
    return lax.Precision.HIGHEST if exact else None


def _contract_round(t, exact):
    return t if exact else _bf16_round(t)


def _rms(x, g):
    return x * lax.rsqrt(jnp.mean(x * x, axis=-1, keepdims=True) + RMS_EPS) * g


def _const_spec(shape):
    nd = len(shape)
    return pl.BlockSpec(shape, lambda *_: (0,) * nd, pipeline_mode=pl.Buffered(1))


def _params(n_axes):
    return pltpu.CompilerParams(dimension_semantics=("arbitrary",) * n_axes, vmem_limit_bytes=VMEM_LIMIT)


def _in_proj_kernel(x_ref, g_ref, w_ref, bg_ref, rc_ref, rs1_ref, rs2_ref, ua_ref, qkv_ref, gate_ref, *, exact):
    xn = _operand(_rms(x_ref[...], g_ref[...]), exact)
    mm = lambda w: _dot(xn, w, _precision(exact))
    ua_ref[...] = mm(w_ref[:, :A_PROJ])
    rc, rs1, rs2 = rc_ref[...], rs1_ref[...], rs2_ref[...]
    n_rot = 2 * B_WIDTH // LANES
    for j in range(B_PROJ // LANES):
        lo = A_PROJ + j * LANES
        s = mm(w_ref[:, lo:lo + LANES])
        if j < n_rot:
            s = s * rc + pltpu.roll(s, LANES - ROPE_HALF, 1) * rs1 + pltpu.roll(s, ROPE_HALF, 1) * rs2
        qkv_ref[:, j * LANES:(j + 1) * LANES] = s
    gate_ref[...] = jax.nn.sigmoid(mm(w_ref[:, A_PROJ + B_PROJ:]) + bg_ref[...])


def _rope_tables(pos):
    inv_freq = ROPE_THETA ** (-jnp.arange(ROPE_HALF, dtype=F32) * 2.0 / ROPE_DIM)
    ang = pos.astype(F32)[:, None] * inv_freq[None, :]
    cos, sin = jnp.cos(ang), jnp.sin(ang)
    n = pos.shape[0]
    rest = B_HEAD_DIM - ROPE_DIM
    c = jnp.concatenate([cos, cos, jnp.ones((n, rest), F32)], axis=1)
    s1 = jnp.concatenate([-sin, jnp.zeros((n, ROPE_HALF + rest), F32)], axis=1)
    s2 = jnp.concatenate([jnp.zeros((n, ROPE_HALF), F32), sin, jnp.zeros((n, rest), F32)], axis=1)
    rep = LANES // B_HEAD_DIM
    return tuple(jnp.tile(t, (1, rep)) for t in (c, s1, s2))


def _in_proj(x, g, w, b_gate, tables, tm):
    n = x.shape[0]
    p_rows = tables[0].shape[0]
    t_tiles = p_rows // tm
    in_proj_w = w.shape[1]
    tab_spec = pl.BlockSpec((tm, LANES), lambda i: (i % t_tiles, 0))
    return pl.pallas_call(
        functools.partial(_in_proj_kernel, exact=w.dtype == F32),
        grid=(n // tm,),
        in_specs=[
            pl.BlockSpec((tm, D_MODEL), lambda i: (i, 0)),
            _const_spec((1, D_MODEL)),
            _const_spec((D_MODEL, in_proj_w)),
            _const_spec((1, 2 * D_MODEL)),
            tab_spec, tab_spec, tab_spec,
        ],
        out_specs=[
            pl.BlockSpec((tm, A_PROJ), lambda i: (i, 0)),
            pl.BlockSpec((tm, B_PROJ), lambda i: (i, 0)),
            pl.BlockSpec((tm, 2 * D_MODEL), lambda i: (i, 0)),
        ],
        out_shape=[
            jax.ShapeDtypeStruct((n, A_PROJ), F32),
            jax.ShapeDtypeStruct((n, B_PROJ), F32),
            jax.ShapeDtypeStruct((n, 2 * D_MODEL), F32),
        ],
        compiler_params=_params(1),
        name="in_proj",
    )(x, g.reshape(1, -1), w, b_gate.reshape(1, -1), *tables)


def _norm_matmul_kernel(x_ref, g_ref, w_ref, o_ref):
    o_ref[...] = _dot(_rms(x_ref[...], g_ref[...]).astype(BF16), w_ref[...])


def _norm_matmul(x, g, w_bf16, tm):
    n, d = x.shape
    dout = w_bf16.shape[1]
    return pl.pallas_call(
        _norm_matmul_kernel,
        grid=(n // tm,),
        in_specs=[pl.BlockSpec((tm, d), lambda i: (i, 0)), _const_spec((1, d)), _const_spec((d, dout))],
        out_specs=pl.BlockSpec((tm, dout), lambda i: (i, 0)),
        out_shape=jax.ShapeDtypeStruct((n, dout), F32),
        compiler_params=_params(1),
        name="norm_matmul",
    )(x, g.reshape(1, -1), w_bf16)


def _rwkv_features(u, u_first_prev, w_refs, bd, hp, exact=False):
    mu_ref, w0_ref, w2_ref, a0_ref, a2_ref, g2_ref, kk_ref, ka_ref = w_refs
    row = lax.broadcasted_iota(jnp.int32, (u.shape[0], 1), 0)
    u_prev = jnp.where(row == 0, u_first_prev, pltpu.roll(u, 1, 0))
    um = u + (u_prev - u) * mu_ref[...]
    o1, o2, o3 = A_WIDTH, 2 * A_WIDTH, 3 * A_WIDTH
    o4 = o3 + A_DECAY_LORA
    o5 = o4 + A_ICLR_LORA
    r, k, v = um[:, :o1], um[:, o1:o2], um[:, o2:o3]
    xw, xa, xg = um[:, o3:o4], um[:, o4:o5], um[:, o5:]
    lora = lambda t, w_ref: _dot(_operand(t, exact), _operand(w_ref[...], exact), _precision(exact))
    w = -jax.nn.softplus(-(w0_ref[...] + lora(jnp.tanh(xw), w2_ref))) - 0.5
    e = jnp.exp(w)
    a = jax.nn.sigmoid(a0_ref[...] + lora(xa, a2_ref))
    g = lora(jax.nn.sigmoid(xg), g2_ref)
    kk = k * kk_ref[...]
    kkn = kk / jnp.maximum(jnp.sqrt(_dot(kk * kk, bd, hp)), 1e-12)
    k2 = k * (1.0 + (a - 1.0) * ka_ref[...])
    return r, k2, v, e, a, g, kkn


def _rwkv_output(y, r, k2, v, g, rk_ref, lng_ref, lnb_ref, bd, hp):
    inv_n = 1.0 / A_HEAD_DIM
    mean = _dot(y, bd, hp) * inv_n
    yc = y - mean
    var = _dot(yc * yc, bd, hp) * inv_n
    yn = yc * lax.rsqrt(var + A_LNX_EPS) * lng_ref[...] + lnb_ref[...]
    bonus = _dot(r * k2 * rk_ref[...], bd, hp) * v
    return (yn + bonus) * g


def _rwkv_kernel(u_ref, sh0_ref, s0_ref, mu_ref, w0_ref, w2_ref, a0_ref, a2_ref, g2_ref, kk_ref, ka_ref,
                 rk_ref, lng_ref, lnb_ref, bd_ref, y_ref, sfin_ref, shout_ref, st_scr, prev_scr,
                 *, chunk, n_chunks, t_valid, precision):
    c = pl.program_id(1)
    hp = precision

    @pl.when(c == 0)
    def _():
        st_scr[...] = s0_ref[0]
        prev_scr[...] = sh0_ref[0]

    u = u_ref[0]
    row = lax.broadcasted_iota(jnp.int32, (chunk, 1), 0)
    bd = bd_ref[...]
    r, k2, v, e, a, g, kkn = _rwkv_features(
        u, prev_scr[...], (mu_ref, w0_ref, w2_ref, a0_ref, a2_ref, g2_ref, kk_ref, ka_ref), bd, hp)
    last = (t_valid - 1) % chunk
    prev_scr[...] = u[last:last + 1, :]
    b = kkn * a
    if t_valid % chunk:
        live = row < t_valid
        e = jnp.where(live, e, 0.0)
        kkn = jnp.where(live, kkn, 0.0)
        b = jnp.where(live, b, 0.0)
        k2 = jnp.where(live, k2, 0.0)
        v = jnp.where(live, v, 0.0)

    ri = lax.broadcasted_iota(jnp.int32, (chunk, chunk), 0)
    ci = lax.broadcasted_iota(jnp.int32, (chunk, chunk), 1)
    incl = ri >= ci
    strict = ri > ci
    cum = _dot(incl.astype(F32), e, hp)
    cum_excl = cum - e
    cum_end = cum[chunk - 1:chunk, :]
    at = -kkn * jnp.exp(-cum_excl)
    rt = r * jnp.exp(-cum)
    grow = jnp.exp(cum)
    bt = b * grow
    kt = k2 * grow
    to_end = jnp.exp(cum - cum_end)
    bh = b * to_end
    kh = k2 * to_end
    dec_end = jnp.exp(-cum_end)
    eye = (ri == ci).astype(F32)
    n_sq = int(math.log2(chunk)) - 1

    hd = A_HEAD_DIM
    e_r = lax.broadcasted_iota(jnp.int32, (hd, hd), 0)
    e_c = lax.broadcasted_iota(jnp.int32, (hd, hd), 1)
    eye_h = (e_r == e_c).astype(F32)

    ys = []
    for h in range(A_HEADS):
        sl = slice(h * hd, (h + 1) * hd)
        at_h, rt_h, vh = at[:, sl], rt[:, sl], v[:, sl]
        m = _dot_nt(jnp.concatenate([at_h, rt_h], axis=0), jnp.concatenate([bt[:, sl], kt[:, sl]], axis=0), hp)
        a_ab = jnp.where(strict, m[:chunk, :chunk], 0.0)
        a_ak = jnp.where(strict, m[:chunk, chunk:], 0.0)
        m_rb = jnp.where(incl, m[chunk:, :chunk], 0.0)
        m_rk = jnp.where(incl, m[chunk:, chunk:], 0.0)
        tinv = eye + a_ab
        pw = a_ab
        for _ in range(n_sq):
            pw = _dot(pw, pw, hp)
            tinv = tinv + _dot(tinv, pw, hp)
        w1 = _dot(tinv, at_h, hp)
        w2v = _dot(tinv, _dot(a_ak, vh, hp), hp)
        st = st_scr[h]
        uu = _dot(w1, st, hp) + w2v
        ys.append(_dot(rt_h, st, hp) + _dot(m_rb, uu, hp) + _dot(m_rk, vh, hp))
        st_scr[h] = (_dot(eye_h * dec_end[:, sl], st, hp) + _dot_tn(bh[:, sl], uu, hp) + _dot_tn(kh[:, sl], vh, hp))
    y = jnp.concatenate(ys, axis=1)
    y_ref[0] = _rwkv_output(y, r, k2, v, g, rk_ref, lng_ref, lnb_ref, bd, hp)

    @pl.when(c == n_chunks - 1)
    def _():
        sfin_ref[0] = st_scr[...]
        shout_ref[0] = u[last:last + 1, :]


def _rwkv_step_kernel(u_ref, sh0_ref, s0_ref, mu_ref, w0_ref, w2_ref, a0_ref, a2_ref, g2_ref, kk_ref, ka_ref,
                      rk_ref, lng_ref, lnb_ref, bd_ref, y_ref, sfin_ref, shout_ref, *, t_valid, precision, exact):
    hp = precision
    hd = A_HEAD_DIM
    u = u_ref[0]
    n_rows = u.shape[0]
    bd = bd_ref[...]
    r, k2, v, e, a, g, kkn = _rwkv_features(
        u, sh0_ref[0], (mu_ref, w0_ref, w2_ref, a0_ref, a2_ref, g2_ref, kk_ref, ka_ref), bd, hp, exact)
    rnd = lambda t: _contract_round(t, exact)
    decay = jnp.exp(-e)
    b = kkn * a
    eye = (lax.broadcasted_iota(jnp.int32, (hd, hd), 0) == lax.broadcasted_iota(jnp.int32, (hd, hd), 1)).astype(F32)
    to_col = lambda t: jnp.sum(eye * t, axis=1, keepdims=True)
    to_row = lambda t: jnp.sum(eye * t, axis=0, keepdims=True)
    y_heads = []
    for h in range(A_HEADS):
        sl = slice(h * hd, (h + 1) * hd)
        s = s0_ref[0, h]
        y_rows = []
        for t in range(t_valid):
            tt = slice(t, t + 1)
            sa = jnp.sum(rnd(s) * rnd(-kkn[tt, sl]), axis=1, keepdims=True)
            s = s * decay[tt, sl] + sa * b[tt, sl] + to_col(v[tt, sl]) * k2[tt, sl]
            y_rows.append(to_row(jnp.sum(rnd(s) * rnd(r[tt, sl]), axis=1, keepdims=True)))
        sfin_ref[0, h] = s
        y_rows.append(jnp.zeros((n_rows - t_valid, hd), F32))
        y_heads.append(jnp.concatenate(y_rows, axis=0))
    y = jnp.concatenate(y_heads, axis=1)
    y_ref[0] = _rwkv_output(y, r, k2, v, g, rk_ref, lng_ref, lnb_ref, bd, hp)
    shout_ref[0] = u[t_valid - 1:t_valid, :]


def _rwkv(u_a, shift0, s0, p, chunk, t_valid, precision):
    bsz, t_len, _ = u_a.shape
    hd = A_HEAD_DIM
    bd = jnp.asarray(np.kron(np.eye(A_HEADS, dtype=np.float32), np.ones((hd, hd), np.float32)))
    row = lambda t: t.reshape(1, -1)
    if chunk:
        n_chunks = t_len // chunk
        assert (n_chunks - 1) * chunk < t_valid <= t_len
        kern = functools.partial(_rwkv_kernel, chunk=chunk, n_chunks=n_chunks, t_valid=t_valid, precision=precision)
        scratch = [pltpu.VMEM((A_HEADS, hd, hd), F32), pltpu.VMEM((1, A_PROJ), F32)]
    else:
        n_chunks, chunk = 1, t_len
        kern = functools.partial(_rwkv_step_kernel, t_valid=t_valid, precision=precision, exact=True)
        scratch = []
    state_spec = pl.BlockSpec((1, A_HEADS, hd, hd), lambda b, c: (b, 0, 0, 0))
    shift_spec = pl.BlockSpec((1, 1, A_PROJ), lambda b, c: (b, 0, 0))
    return pl.pallas_call(
        kern,
        grid=(bsz, n_chunks),
        in_specs=[
            pl.BlockSpec((1, chunk, A_PROJ), lambda b, c: (b, c, 0)),
            shift_spec, state_spec,
            _const_spec((1, A_PROJ)), _const_spec((1, A_WIDTH)), _const_spec((A_DECAY_LORA, A_WIDTH)),
            _const_spec((1, A_WIDTH)), _const_spec((A_ICLR_LORA, A_WIDTH)), _const_spec((A_GATE_LORA, A_WIDTH)),
            _const_spec((1, A_WIDTH)), _const_spec((1, A_WIDTH)), _const_spec((1, A_WIDTH)),
            _const_spec((1, A_WIDTH)), _const_spec((1, A_WIDTH)), _const_spec((A_WIDTH, A_WIDTH)),
        ],
        out_specs=[pl.BlockSpec((1, chunk, A_WIDTH), lambda b, c: (b, c, 0)), state_spec, shift_spec],
        out_shape=[
            jax.ShapeDtypeStruct((bsz, t_len, A_WIDTH), F32),
            jax.ShapeDtypeStruct((bsz, A_HEADS, hd, hd), F32),
            jax.ShapeDtypeStruct((bsz, 1, A_PROJ), F32),
        ],
        scratch_shapes=scratch,
        compiler_params=_params(2),
        name="rwkv7",
    )(u_a, shift0.reshape(bsz, 1, A_PROJ), s0, row(p['rwkv_mu']), row(p['rwkv_w0']), p['rwkv_w2'],
      row(p['rwkv_a0']), p['rwkv_a2'], p['rwkv_g2'], row(p['rwkv_k_k']), row(p['rwkv_k_a']),
      row(p['rwkv_r_k']), row(p['rwkv_lnx_g']), row(p['rwkv_lnx_b']), bd)


def _swa_prompt_kernel(q_ref, k_ref, v_ref, o_ref, lse_ref, *, n_seq, n_blk):
    blk = SWA_BLOCK
    hd = B_HEAD_DIM
    qi = lax.broadcasted_iota(jnp.int32, (blk, 2 * blk), 0) + blk
    ki = lax.broadcasted_iota(jnp.int32, (blk, 2 * blk), 1)
    dist = qi - ki
    band = (dist >= 0) & (dist <= blk)
    scale = hd ** -0.5

    def body(it, carry):
        s_idx = it // n_blk
        n = it % n_blk
        cur = pl.multiple_of(n * blk, blk)
        prv = pl.multiple_of(jnp.maximum(n - 1, 0) * blk, blk)
        valid = band & (ki >= jnp.where(n > 0, 0, blk))
        q = q_ref[s_idx, pl.ds(cur, blk), :]
        kb = jnp.concatenate([k_ref[s_idx, pl.ds(prv, blk), :], k_ref[s_idx, pl.ds(cur, blk), :]], axis=0).astype(BF16)
        vb = jnp.concatenate([v_ref[s_idx, pl.ds(prv, blk), :], v_ref[s_idx, pl.ds(cur, blk), :]], axis=0).astype(BF16)
        outs, lses = [], []
        for h in range(B_HEADS_PER_GROUP):
            sl = slice(h * hd, (h + 1) * hd)
            s = _dot_nt(q[:, sl].astype(BF16), kb[:, sl]) * scale
            s = jnp.where(valid, s, NEG_INF)
            m = jnp.max(s, axis=-1, keepdims=True)
            p = jnp.exp(s - m)
            den = jnp.sum(p, axis=-1, keepdims=True)
            outs.append(_dot((p / den).astype(BF16), vb[:, sl]))
            lses.append(jnp.broadcast_to(m + jnp.log(den), (blk, hd)))
        o_ref[s_idx, pl.ds(cur, blk), :] = jnp.concatenate(outs, axis=1)
        lse_ref[s_idx, pl.ds(cur, blk), :] = jnp.concatenate(lses, axis=1)
        return carry

    lax.fori_loop(0, n_seq * n_blk, body, 0)


def _swa_prompt(q, k, v, seqs_per_step):
    n_seq, length, width = q.shape
    n_blk = length // SWA_BLOCK
    spec = pl.BlockSpec((seqs_per_step, length, width), lambda i: (i, 0, 0))
    shp = jax.ShapeDtypeStruct((n_seq, length, width), F32)
    return pl.pallas_call(
        functools.partial(_swa_prompt_kernel, n_seq=seqs_per_step, n_blk=n_blk),
        grid=(n_seq // seqs_per_step,),
        in_specs=[spec, spec, spec],
        out_specs=[spec, spec],
        out_shape=[shp, shp],
        compiler_params=_params(1),
        name="swa_prompt",
    )(q, k, v)


def _swa_sample_kernel(qkv_ref, c0_ref, c1_ref, c2_ref, o_ref, lse_ref, n0_ref, n1_ref, n2_ref, *, t_len, exact):
    hd = B_HEAD_DIM
    nh = B_HEADS_PER_GROUP
    gw = B_GROUP_WIDTH
    rows = t_len * nh
    qkv = qkv_ref[0]
    scale = hd ** -0.5
    r_i = lax.broadcasted_iota(jnp.int32, (rows, gw), 0)
    l_i = lax.broadcasted_iota(jnp.int32, (rows, gw), 1)
    head_lanes = (l_i // hd) == (r_i % nh)
    t_of_row = lax.broadcasted_iota(jnp.int32, (rows, 1), 0) // nh
    outs, lses = [], []
    for gi, ((window, dil), c_ref, n_ref) in enumerate(zip(B_GROUPS, (c0_ref, c1_ref, c2_ref), (n0_ref, n1_ref, n2_ref))):
        buf_len = c_ref.shape[2]
        q = qkv[:, gi * gw:(gi + 1) * gw]
        k_new = qkv[:, B_WIDTH + gi * gw:B_WIDTH + (gi + 1) * gw]
        v_new = qkv[:, 2 * B_WIDTH + gi * gw:2 * B_WIDTH + (gi + 1) * gw]
        kc = c_ref[0, 0]
        vc = c_ref[0, 1]
        qx = jnp.zeros((rows, gw), F32)
        for t in range(t_len):
            qx = jnp.where(head_lanes & (t_of_row == t), q[t:t + 1, :], qx)
        qx = _contract_round(qx, exact)
        k_new_r = _contract_round(k_new, exact)
        v_new_r = _contract_round(v_new, exact)
        s_c = _dot_nt(_operand(qx, exact), _operand(kc, exact), _precision(exact)) * scale
        t_c = lax.broadcasted_iota(jnp.int32, (rows, buf_len), 0) // nh
        j_c = lax.broadcasted_iota(jnp.int32, (rows, buf_len), 1)
        ok_c = (j_c >= t_c) & (((j_c - t_c) & (dil - 1)) == 0)
        s_c = jnp.where(ok_c, s_c, NEG_INF)
        s_n = []
        for i in range(t_len):
            s_i = jnp.sum(qx * k_new_r[i:i + 1, :], axis=-1, keepdims=True) * scale
            ok_i = (t_of_row >= i) & (((t_of_row - i) & (dil - 1)) == 0)
            s_n.append(jnp.where(ok_i, s_i, NEG_INF))
        m = functools.reduce(jnp.maximum, s_n, jnp.max(s_c, axis=-1, keepdims=True))
        p_c = jnp.exp(s_c - m)
        p_n = [jnp.exp(s_i - m) for s_i in s_n]
        den = functools.reduce(lambda a, b: a + b, p_n, jnp.sum(p_c, axis=-1, keepdims=True))
        o = _dot(_operand(p_c / den, exact), _operand(vc, exact), _precision(exact))
        for i in range(t_len):
            o = o + _contract_round(p_n[i] / den, exact) * v_new_r[i:i + 1, :]
        o = jnp.where(head_lanes, o, 0.0)
        lse = jnp.where(head_lanes, m + jnp.log(den), 0.0)
        outs.append(jnp.concatenate(
            [jnp.sum(o[t * nh:(t + 1) * nh, :], axis=0, keepdims=True) for t in range(t_len)], axis=0))
        lses.append(jnp.concatenate(
            [jnp.sum(lse[t * nh:(t + 1) * nh, :], axis=0, keepdims=True) for t in range(t_len)], axis=0))
        for j, new in ((0, k_new), (1, v_new)):
            n_ref[0, j, :buf_len - t_len, :] = c_ref[0, j, t_len:, :]
            n_ref[0, j, buf_len - t_len:, :] = new
    o_ref[0] = jnp.concatenate(outs, axis=1)
    lse_ref[0] = jnp.concatenate(lses, axis=1)


def _swa_sample(qkv, caches):
    bsz, t_len, _ = qkv.shape
    cache_specs = [pl.BlockSpec((1, 2) + c.shape[2:], lambda b: (b, 0, 0, 0)) for c in caches]
    row_spec = pl.BlockSpec((1, t_len, B_WIDTH), lambda b: (b, 0, 0))
    return pl.pallas_call(
        functools.partial(_swa_sample_kernel, t_len=t_len, exact=True),
        grid=(bsz,),
        in_specs=[pl.BlockSpec((1, t_len, B_PROJ), lambda b: (b, 0, 0))] + cache_specs,
        out_specs=[row_spec, row_spec] + cache_specs,
        out_shape=[jax.ShapeDtypeStruct((bsz, t_len, B_WIDTH), F32)] * 2
        + [jax.ShapeDtypeStruct(c.shape, F32) for c in caches],
        compiler_params=_params(1),
        name="swa_sample",
    )(qkv, *caches)


def _mid_kernel(x_ref, ya_ref, ob_ref, lse_ref, gate_ref, mk_ref, mv_ref, wa_ref, wb_ref, wo_ref, gm_ref,
                wq_ref, wmo_ref, gf_ref, wr_ref, br_ref, xn_all_ref, x2_ref, xn_ref, route_ref, *, rows_per_batch, exact):
    del xn_all_ref
    tm = x_ref.shape[0]
    gw = B_GROUP_WIDTH
    lses = [lse_ref[:, g * gw:(g + 1) * gw] for g in range(len(B_GROUPS))]
    m = functools.reduce(jnp.maximum, lses)
    es = [jnp.exp(l - m) for l in lses]
    den = functools.reduce(lambda p, q: p + q, es)
    rnd = lambda t: _contract_round(t, exact)
    op = lambda t: _operand(t, exact)
    prec = _precision(exact)
    yb = functools.reduce(lambda p, q: p + q, [rnd(es[g] / den) * rnd(ob_ref[:, g * gw:(g + 1) * gw])
                                               for g in range(len(B_GROUPS))])
    gates = gate_ref[...]
    merged = (gates[:, :D_MODEL] * _dot(op(ya_ref[...]), wa_ref[...], prec)
              + gates[:, D_MODEL:] * _dot(op(yb), wb_ref[...], prec))
    x1 = x_ref[...] + _dot(op(merged), wo_ref[...], prec)

    q = _dot(op(_rms(x1, gm_ref[...])), wq_ref[...], prec)
    n_b = mk_ref.shape[0]
    mk = op(mk_ref[...].reshape(n_b * N_MEM, MEM_WIDTH))
    mv = op(mv_ref[...].reshape(n_b * N_MEM, MEM_WIDTH))
    if n_b > 1:
        rb = lax.broadcasted_iota(jnp.int32, (tm, n_b * N_MEM), 0) // rows_per_batch
        cb = lax.broadcasted_iota(jnp.int32, (tm, n_b * N_MEM), 1) // N_MEM
        same = rb == cb
    heads = []
    for h in range(MEM_HEADS):
        sl = slice(h * MEM_HEAD_DIM, (h + 1) * MEM_HEAD_DIM)
        s = _dot_nt(op(q[:, sl]), mk[:, sl], prec) * (MEM_HEAD_DIM ** -0.5)
        if n_b > 1:
            s = jnp.where(same, s, NEG_INF)
        s = s - jnp.max(s, axis=-1, keepdims=True)
        p = jnp.exp(s)
        heads.append(_dot(op(p / jnp.sum(p, axis=-1, keepdims=True)), mv[:, sl], prec))
    x2 = x1 + _dot(op(jnp.concatenate(heads, axis=1)), wmo_ref[...], prec)
    x2_ref[...] = x2

    xn = _rms(x2, gf_ref[...])
    xn_ref[...] = xn
    logits = _dot(op(xn), wr_ref[...], prec) + br_ref[...]
    lane = lax.broadcasted_iota(jnp.int32, (tm, LANES), 1)
    gl = jnp.where(lane < N_GROUPS, logits, NEG_INF)
    gmax = jnp.max(gl, axis=-1, keepdims=True)
    grp = jnp.min(jnp.where(gl == gmax, lane, LANES), axis=-1, keepdims=True)
    w_grp = 1.0 / jnp.sum(jnp.exp(gl - gmax), axis=-1, keepdims=True)
    first = N_GROUPS + grp * EXPERTS_PER_GROUP
    el = jnp.where((lane >= first) & (lane < first + EXPERTS_PER_GROUP), logits, NEG_INF)
    m1 = jnp.max(el, axis=-1, keepdims=True)
    i1 = jnp.min(jnp.where(el == m1, lane, LANES), axis=-1, keepdims=True)
    el2 = jnp.where(lane == i1, NEG_INF, el)
    m2 = jnp.max(el2, axis=-1, keepdims=True)
    i2 = jnp.min(jnp.where(el2 == m2, lane, LANES), axis=-1, keepdims=True)
    e2 = jnp.exp(m2 - m1)
    g1 = w_grp / (1.0 + e2)
    g2 = w_grp * e2 / (1.0 + e2)
    route = jnp.where(lane == 0, (i1 - N_GROUPS).astype(F32), 0.0)
    route = jnp.where(lane == 1, (i2 - N_GROUPS).astype(F32), route)
    route = jnp.where(lane == 2, g1, route)
    route_ref[...] = jnp.where(lane == 3, g2, route)


def _mid(x, y_a, o_b, lse_b, gates, mem_k, mem_v, mk_map, mv_map, n_b, rows_per_batch, w, tm, xn_all, row0):
    n = x.shape[0]
    row = lambda width: pl.BlockSpec((tm, width), lambda i: (i, 0))
    in_specs = [
        row(D_MODEL), row(A_WIDTH), row(B_WIDTH), row(B_WIDTH), row(2 * D_MODEL),
        pl.BlockSpec((n_b, N_MEM, MEM_WIDTH), mk_map), pl.BlockSpec((n_b, N_MEM, MEM_WIDTH), mv_map),
        _const_spec((A_WIDTH, D_MODEL)), _const_spec((B_GROUP_WIDTH, D_MODEL)), _const_spec((D_MODEL, D_MODEL)),
        _const_spec((1, D_MODEL)), _const_spec((D_MODEL, MEM_WIDTH)), _const_spec((MEM_WIDTH, D_MODEL)),
        _const_spec((1, D_MODEL)), _const_spec((D_MODEL, LANES)), _const_spec((1, LANES)),
        pl.BlockSpec(memory_space=pl.ANY),
    ]
    args = [x, y_a, o_b, lse_b, gates, mem_k, mem_v, w['wa'], w['wb'], w['wo'], w['gm'], w['wq'], w['wmo'],
            w['gf'], w['wr'], w['br'], xn_all]
    blk0 = row0 // tm
    return pl.pallas_call(
        functools.partial(_mid_kernel, rows_per_batch=rows_per_batch, exact=w['wa'].dtype == F32),
        grid=(n // tm,),
        in_specs=in_specs,
        out_specs=[row(D_MODEL), pl.BlockSpec((tm, D_MODEL), lambda i: (i + blk0, 0)), row(LANES)],
        out_shape=[
            jax.ShapeDtypeStruct((n, D_MODEL), F32),
            jax.ShapeDtypeStruct(xn_all.shape, F32),
            jax.ShapeDtypeStruct((n, LANES), F32),
        ],
        input_output_aliases={len(args) - 1: 1},
        compiler_params=_params(1),
        name="mid",
    )(*args)


def _row_copy(src_hbm, idx, dst_buf, slot, j, sem):
    return pltpu.make_async_copy(src_hbm.at[pl.ds(idx, 1), :], dst_buf.at[slot, pl.ds(j, 1), :], sem.at[slot])


def _gather_start(idx_ref, src_hbm, dst_buf, slot, sem, n_rows):
    for j in range(n_rows):
        _row_copy(src_hbm, idx_ref[0, 0, j], dst_buf, slot, j, sem).start()


def _gather_wait(src_hbm, dst_buf, slot, sem, n_rows):
    for j in range(n_rows):
        _row_copy(src_hbm, 0, dst_buf, slot, j, sem).wait()


def _experts_kernel(meta_ref, be_ref, idx_ref, idx_next_ref, x_hbm, wg_ref, wu_ref, wd_ref, o_ref, xbuf, sem):
    i = pl.program_id(0)
    n_used = meta_ref[0]
    slot = i % 2

    @pl.when(i == 0)
    def _():
        _gather_start(idx_ref, x_hbm, xbuf, 0, sem, MOE_ROWS)

    @pl.when(i + 1 < n_used)
    def _():
        _gather_start(idx_next_ref, x_hbm, xbuf, 1 - slot, sem, MOE_ROWS)

    @pl.when(i < n_used)
    def _():
        _gather_wait(x_hbm, xbuf, slot, sem, MOE_ROWS)
        xb = xbuf[slot].astype(BF16)
        hg = _dot(xb, wg_ref[0].astype(BF16))
        hu = _dot(xb, wu_ref[0].astype(BF16))
        hh = (jax.nn.silu(hg) * hu).astype(BF16)
        o_ref[...] = _dot(hh, wd_ref[0].astype(BF16))

    @pl.when(i >= n_used)
    def _():
        o_ref[...] = jnp.zeros_like(o_ref)


def _experts(xn_all, row_tok, block_e, n_used, w_gate, w_up, w_down):
    n_blocks = block_e.shape[0]
    idx3 = row_tok.reshape(n_blocks, 1, MOE_ROWS)
    idx_spec = lambda f: pl.BlockSpec((1, 1, MOE_ROWS), f, memory_space=pltpu.SMEM)
    grid_spec = pltpu.PrefetchScalarGridSpec(
        num_scalar_prefetch=2,
        grid=(n_blocks,),
        in_specs=[
            idx_spec(lambda i, meta, be: (i, 0, 0)),
            idx_spec(lambda i, meta, be: (jnp.minimum(i + 1, n_blocks - 1), 0, 0)),
            pl.BlockSpec(memory_space=pl.ANY),
            pl.BlockSpec((1, D_MODEL, EXPERT_FF), lambda i, meta, be: (be[i], 0, 0)),
            pl.BlockSpec((1, D_MODEL, EXPERT_FF), lambda i, meta, be: (be[i], 0, 0)),
            pl.BlockSpec((1, EXPERT_FF, D_MODEL), lambda i, meta, be: (be[i], 0, 0)),
        ],
        out_specs=pl.BlockSpec((MOE_ROWS, D_MODEL), lambda i, meta, be: (i, 0)),
        scratch_shapes=[pltpu.VMEM((2, MOE_ROWS, D_MODEL), F32), pltpu.SemaphoreType.DMA((2,))],
    )
    return pl.pallas_call(
        _experts_kernel,
        grid_spec=grid_spec,
        out_shape=jax.ShapeDtypeStruct((n_blocks * MOE_ROWS, D_MODEL), F32),
        compiler_params=_params(1),
        name="experts",
    )(n_used.reshape(1), block_e, idx3, idx3, xn_all, w_gate, w_up, w_down)


def _combine_kernel(pos_ref, pos_next_ref, x_ref, route_ref, yb_hbm, g_ref, o_ref, ybuf, sem, *, n_tiles):
    i = pl.program_id(0)
    tm = x_ref.shape[0]
    slot = i % 2

    @pl.when(i == 0)
    def _():
        _gather_start(pos_ref, yb_hbm, ybuf, 0, sem, 2 * tm)

    if n_tiles > 1:
        @pl.when(i + 1 < n_tiles)
        def _():
            _gather_start(pos_next_ref, yb_hbm, ybuf, 1 - slot, sem, 2 * tm)

    _gather_wait(yb_hbm, ybuf, slot, sem, 2 * tm)
    route = route_ref[...]
    y = x_ref[...] + (route[:, 2:3] * ybuf[slot, :tm, :] + route[:, 3:4] * ybuf[slot, tm:, :])
    o_ref[...] = _rms(y, g_ref[...])


def _combine(x2, route, pos, yb, g_final, tm):
    n = x2.shape[0]
    n_tiles = n // tm
    pos_spec = lambda f: pl.BlockSpec((1, 1, 2 * tm), f, memory_space=pltpu.SMEM)
    return pl.pallas_call(
        functools.partial(_combine_kernel, n_tiles=n_tiles),
        grid=(n_tiles,),
        in_specs=[
            pos_spec(lambda i: (i, 0, 0)),
            pos_spec(lambda i: (jnp.minimum(i + 1, n_tiles - 1), 0, 0)),
            pl.BlockSpec((tm, D_MODEL), lambda i: (i, 0)),
            pl.BlockSpec((tm, LANES), lambda i: (i, 0)),
            pl.BlockSpec(memory_space=pl.ANY),
            _const_spec((1, D_MODEL)),
        ],
        out_specs=pl.BlockSpec((tm, D_MODEL), lambda i: (i, 0)),
        out_shape=jax.ShapeDtypeStruct((n, D_MODEL), F32),
        scratch_shapes=[pltpu.VMEM((2, 2 * tm, D_MODEL), F32), pltpu.SemaphoreType.DMA((2,))],
        compiler_params=_params(1),
        name="combine",
    )(pos, pos, x2, route, yb, g_final.reshape(1, -1))


def _dispatch(eid):
    n_tok = eid.shape[0]
    n_rows = n_tok * TOP_K
    n_blocks = n_rows // MOE_ROWS + N_EXPERTS
    flat_e = eid.reshape(-1)
    onehot = (flat_e[:, None] == jnp.arange(N_EXPERTS, dtype=jnp.int32)[None, :]).astype(jnp.int32)
    csum = jnp.cumsum(onehot, axis=0)
    counts = csum[-1]
    rank = jnp.sum((csum - onehot) * onehot, axis=1)
    padded = (counts + MOE_ROWS - 1) // MOE_ROWS * MOE_ROWS
    pad_end = jnp.cumsum(padded)
    pad_start = pad_end - padded
    dest = pad_start[flat_e] + rank
    flat_tok = jnp.arange(n_rows, dtype=jnp.int32) // TOP_K
    row_tok = jnp.zeros((n_blocks * MOE_ROWS,), jnp.int32).at[dest].set(flat_tok)
    block_start = jnp.arange(n_blocks, dtype=jnp.int32) * MOE_ROWS
    block_e = jnp.minimum(jnp.sum((pad_end[None, :] <= block_start[:, None]).astype(jnp.int32), axis=1), N_EXPERTS - 1)
    n_used = (pad_end[-1] // MOE_ROWS).astype(jnp.int32)
    return row_tok, block_e, n_used, dest.reshape(n_tok, TOP_K).astype(jnp.int32)


def _tile_pos(pos, tm):
    n = pos.shape[0]
    return pos.reshape(n // tm, tm, TOP_K).transpose(0, 2, 1).reshape(n // tm, 1, TOP_K * tm)


def _deinterleave(t, dil):
    bsz, s_len, width = t.shape
    return t.reshape(bsz, s_len // dil, dil, width).transpose(0, 2, 1, 3).reshape(bsz * dil, s_len // dil, width)


def _interleave(t, bsz, dil):
    _, n_cls, width = t.shape
    return t.reshape(bsz, dil, n_cls, width).transpose(0, 2, 1, 3).reshape(bsz, n_cls * dil, width)


def kernel(x_prompt, x_sample, state_rwkv, state_shift, cache_swa_w128, cache_swa_w512, cache_swa_w2048,
           cache_mem_kv, mem_prompt, norm_mix_g, w_in, b_gate, rwkv_mu, rwkv_w0, rwkv_w2, rwkv_a0, rwkv_a2,
           rwkv_g2, rwkv_k_k, rwkv_k_a, rwkv_r_k, rwkv_lnx_g, rwkv_lnx_b, w_branch_a, w_branch_b, w_out,
           norm_mem_g, norm_memkv_g, w_mem_q, w_mem_kv, w_mem_out, norm_ffn_g, w_router_group, b_router_group,
           w_router_expert, b_router_expert, w_exp_gate, w_exp_up, w_exp_down, norm_final_g):
    bsz, s_len, _ = x_prompt.shape
    dbs, t_len, _ = x_sample.shape
    n_p, n_s = bsz * s_len, dbs * t_len
    hp = lax.Precision.HIGHEST
    rw = dict(rwkv_mu=rwkv_mu, rwkv_w0=rwkv_w0, rwkv_w2=rwkv_w2, rwkv_a0=rwkv_a0, rwkv_a2=rwkv_a2, rwkv_g2=rwkv_g2,
              rwkv_k_k=rwkv_k_k, rwkv_k_a=rwkv_k_a, rwkv_r_k=rwkv_r_k, rwkv_lnx_g=rwkv_lnx_g, rwkv_lnx_b=rwkv_lnx_b)
    pad = LANES - N_GROUPS - N_EXPERTS
    mid_w_s = dict(
        wa=w_branch_a, wb=w_branch_b, wo=w_out, gm=norm_mem_g.reshape(1, -1), wq=w_mem_q, wmo=w_mem_out,
        gf=norm_ffn_g.reshape(1, -1),
        wr=jnp.concatenate([w_router_group, w_router_expert, jnp.zeros((D_MODEL, pad), F32)], axis=1),
        br=jnp.concatenate([b_router_group, b_router_expert, jnp.zeros((pad,), F32)]).reshape(1, -1))
    mid_w_p = {k: (v.astype(BF16) if k.startswith('w') else v) for k, v in mid_w_s.items()}
    w_in_b = w_in.astype(BF16)

    xp = x_prompt.reshape(n_p, D_MODEL)
    tm = 256
    ua_p, qkv_p, gates_p = _in_proj(xp, norm_mix_g, w_in_b, b_gate, _rope_tables(jnp.arange(s_len)), tm)
    s0 = jnp.zeros((bsz, A_HEADS, A_HEAD_DIM, A_HEAD_DIM), F32)
    ya_p, st_p, shift_p = _rwkv(ua_p.reshape(bsz, s_len, A_PROJ), jnp.zeros((bsz, A_PROJ), F32), s0, rw,
                                RWKV_CHUNK, s_len, hp)
    qkv3 = qkv_p.reshape(bsz, s_len, 3, len(B_GROUPS), B_GROUP_WIDTH)
    outs, lses, p_bufs = [], [], []
    for gi, (window, dil) in enumerate(B_GROUPS):
        q, k, v = (_deinterleave(qkv3[:, :, j, gi], dil) for j in range(3))
        n_cls = s_len // dil
        o, lse = _swa_prompt(q, k, v, max(1, 2048 // n_cls))
        outs.append(_interleave(o, bsz, dil))
        lses.append(_interleave(lse, bsz, dil))
        keep = min(window, s_len)
        p_bufs.append(jnp.stack([qkv3[:, s_len - keep:, 1, gi], qkv3[:, s_len - keep:, 2, gi]], axis=1)
                      .reshape(bsz, 2, keep, B_HEADS_PER_GROUP, B_HEAD_DIM))
    ob_p = jnp.concatenate(outs, axis=-1).reshape(n_p, B_WIDTH)
    lse_p = jnp.concatenate(lses, axis=-1).reshape(n_p, B_WIDTH)

    memkv = _norm_matmul(mem_prompt.reshape(bsz * N_MEM, D_MODEL), norm_memkv_g, w_mem_kv.astype(BF16), 256)
    memkv3 = memkv.reshape(bsz, N_MEM, 2 * MEM_WIDTH)
    mem_kv_prompt = memkv3.reshape(bsz, N_MEM, 2, MEM_HEADS, MEM_HEAD_DIM).transpose(0, 2, 1, 3, 4)
    tiles_per_batch = s_len // tm
    xn_all = jnp.zeros((n_p + n_s, D_MODEL), F32)
    x2_p, xn_all, route_p = _mid(
        xp, ya_p.reshape(n_p, A_WIDTH), ob_p, lse_p, gates_p, memkv3, memkv3,
        lambda i: (i // tiles_per_batch, 0, 0), lambda i: (i // tiles_per_batch, 0, 1),
        1, s_len, mid_w_p, tm, xn_all, 0)

    xs = x_sample.reshape(n_s, D_MODEL)
    pos_s = PAST_LEN + (jnp.arange(n_s) % t_len)
    ua_s, qkv_s, gates_s = _in_proj(xs, norm_mix_g, w_in, b_gate, _rope_tables(pos_s), n_s)
    t_pad = 8
    ua_s3 = jnp.pad(ua_s.reshape(dbs, t_len, A_PROJ), ((0, 0), (0, t_pad - t_len), (0, 0)))
    ya_s, st_s, shift_s = _rwkv(ua_s3, state_shift, state_rwkv, rw, 0, t_len, hp)
    ya_s = ya_s[:, :t_len].reshape(n_s, A_WIDTH)
    caches = [c.reshape(c.shape[0], 2, c.shape[2], B_GROUP_WIDTH) for c in (cache_swa_w128, cache_swa_w512, cache_swa_w2048)]
    ob_s, lse_s, nb0, nb1, nb2 = _swa_sample(qkv_s.reshape(dbs, t_len, B_PROJ), caches)
    s_bufs = [nb.reshape(nb.shape[0], 2, nb.shape[2], B_HEADS_PER_GROUP, B_HEAD_DIM) for nb in (nb0, nb1, nb2)]
    mem_s = cache_mem_kv.reshape(dbs, 2 * N_MEM, MEM_WIDTH)
    tm_s = 32
    x2_s, xn_all, route_s = _mid(
        xs, ya_s, ob_s.reshape(n_s, B_WIDTH), lse_s.reshape(n_s, B_WIDTH), gates_s, mem_s, mem_s,
        lambda i: (i, 0, 0), lambda i: (i, 1, 0), tm_s // t_len, t_len, mid_w_s, tm_s, xn_all, n_p)

    route = jnp.concatenate([route_p, route_s], axis=0)
    eid = route[:, :TOP_K].astype(jnp.int32)
    row_tok, block_e, n_used, pos = _dispatch(eid)
    yb = _experts(xn_all, row_tok, block_e, n_used, w_exp_gate, w_exp_up, w_exp_down)
    tm_c = 128
    y_p = _combine(x2_p, route_p, _tile_pos(pos[:n_p], tm_c), yb, norm_final_g, tm_c)
    y_s = _combine(x2_s, route_s, _tile_pos(pos[n_p:], tm_c), yb, norm_final_g, tm_c)

    return (y_p.reshape(bsz, s_len, D_MODEL), y_s.reshape(dbs, t_len, D_MODEL),
            jnp.swapaxes(st_p, 2, 3), shift_p.reshape(bsz, A_PROJ), p_bufs[0], p_bufs[1], p_bufs[2], mem_kv_prompt,
            st_s, shift_s.reshape(dbs, A_PROJ), s_bufs[0], s_bufs[1], s_bufs[2])
```

```python
import functools
import math

import jax
import jax.numpy as jnp
import numpy as np
from jax import lax
from jax.experimental import pallas as pl
from jax.experimental.pallas import tpu as pltpu

F32 = jnp.float32
BF16 = jnp.bfloat16

D_MODEL = 1024
A_HEADS = 8
A_HEAD_DIM = 64
A_WIDTH = A_HEADS * A_HEAD_DIM
A_DECAY_LORA = 64
A_ICLR_LORA = 64
A_GATE_LORA = 128
A_PROJ = 3 * A_WIDTH + A_DECAY_LORA + A_ICLR_LORA + A_GATE_LORA
A_LNX_EPS = 64e-5
B_GROUPS = ((128, 1), (512, 4), (2048, 16))
B_HEADS_PER_GROUP = 4
B_HEAD_DIM = 64
B_GROUP_WIDTH = B_HEADS_PER_GROUP * B_HEAD_DIM
B_WIDTH = B_GROUP_WIDTH * len(B_GROUPS)
B_PROJ = 3 * B_WIDTH
ROPE_THETA = 500000.0
ROPE_DIM = B_HEAD_DIM // 4
ROPE_HALF = ROPE_DIM // 2
SWA_BLOCK = 128
N_MEM = 256
MEM_HEADS = 4
MEM_HEAD_DIM = 128
MEM_WIDTH = MEM_HEADS * MEM_HEAD_DIM
N_GROUPS = 4
EXPERTS_PER_GROUP = 8
N_EXPERTS = N_GROUPS * EXPERTS_PER_GROUP
TOP_K = 2
EXPERT_FF = 512
RMS_EPS = 1e-6
PAST_LEN = 8192

LANES = 128
VMEM_LIMIT = 56 * 1024 * 1024
RWKV_CHUNK = 64
RWKV_SUB = 2
MOE_ROWS = 256
NEG_INF = float("-inf")


def _dot(a, b, precision=None):
    return jnp.dot(a, b, preferred_element_type=F32, precision=precision)


def _dot_nt(a, b, precision=None):
    return lax.dot_general(a, b, (((1,), (1,)), ((), ())), preferred_element_type=F32, precision=precision)


def _dot_tn(a, b, precision=None):
    return lax.dot_general(a, b, (((0,), (0,)), ((), ())), preferred_element_type=F32, precision=precision)


def _bf16_round(t):
    return t.astype(BF16).astype(F32)


def _operand(t, exact):
    return t.astype(F32) if exact else t.astype(BF16)


def _precision(exact):
    return lax.Precision.HIGHEST if exact else None


def _contract_round(t, exact):
    return t if exact else _bf16_round(t)


def _rms(x, g):
    return x * lax.rsqrt(jnp.mean(x * x, axis=-1, keepdims=True) + RMS_EPS) * g


def _const_spec(shape):
    nd = len(shape)
    return pl.BlockSpec(shape, lambda *_: (0,) * nd, pipeline_mode=pl.Buffered(1))


def _params(n_axes):
    return pltpu.CompilerParams(dimension_semantics=("arbitrary",) * n_axes, vmem_limit_bytes=VMEM_LIMIT)


def _in_proj_kernel(x_ref, g_ref, w_ref, bg_ref, rc_ref, rs1_ref, rs2_ref, ua_ref, qkv_ref, gate_ref, *, exact):
    xn = _operand(_rms(x_ref[...], g_ref[...]), exact)
    mm = lambda w: _dot(xn, w, _precision(exact))
    ua_ref[...] = mm(w_ref[:, :A_PROJ])
    rc, rs1, rs2 = rc_ref[...], rs1_ref[...], rs2_ref[...]
    n_rot = 2 * B_WIDTH // LANES
    for j in range(B_PROJ // LANES):
        lo = A_PROJ + j * LANES
        s = mm(w_ref[:, lo:lo + LANES])
        if j < n_rot:
            s = s * rc + pltpu.roll(s, LANES - ROPE_HALF, 1) * rs1 + pltpu.roll(s, ROPE_HALF, 1) * rs2
        qkv_ref[:, j * LANES:(j + 1) * LANES] = s
    gate_ref[...] = jax.nn.sigmoid(mm(w_ref[:, A_PROJ + B_PROJ:]) + bg_ref[...])


def _rope_tables(pos):
    inv_freq = ROPE_THETA ** (-jnp.arange(ROPE_HALF, dtype=F32) * 2.0 / ROPE_DIM)
    ang = pos.astype(F32)[:, None] * inv_freq[None, :]
    cos, sin = jnp.cos(ang), jnp.sin(ang)
    n = pos.shape[0]
    rest = B_HEAD_DIM - ROPE_DIM
    c = jnp.concatenate([cos, cos, jnp.ones((n, rest), F32)], axis=1)
    s1 = jnp.concatenate([-sin, jnp.zeros((n, ROPE_HALF + rest), F32)], axis=1)
    s2 = jnp.concatenate([jnp.zeros((n, ROPE_HALF), F32), sin, jnp.zeros((n, rest), F32)], axis=1)
    rep = LANES // B_HEAD_DIM
    return tuple(jnp.tile(t, (1, rep)) for t in (c, s1, s2))


def _in_proj(x, g, w, b_gate, tables, tm):
    n = x.shape[0]
    p_rows = tables[0].shape[0]
    t_tiles = p_rows // tm
    in_proj_w = w.shape[1]
    tab_spec = pl.BlockSpec((tm, LANES), lambda i: (i % t_tiles, 0))
    return pl.pallas_call(
        functools.partial(_in_proj_kernel, exact=w.dtype == F32),
        grid=(n // tm,),
        in_specs=[
            pl.BlockSpec((tm, D_MODEL), lambda i: (i, 0)),
            _const_spec((1, D_MODEL)),
            _const_spec((D_MODEL, in_proj_w)),
            _const_spec((1, 2 * D_MODEL)),
            tab_spec, tab_spec, tab_spec,
        ],
        out_specs=[
            pl.BlockSpec((tm, A_PROJ), lambda i: (i, 0)),
            pl.BlockSpec((tm, B_PROJ), lambda i: (i, 0)),
            pl.BlockSpec((tm, 2 * D_MODEL), lambda i: (i, 0)),
        ],
        out_shape=[
            jax.ShapeDtypeStruct((n, A_PROJ), F32),
            jax.ShapeDtypeStruct((n, B_PROJ), F32),
            jax.ShapeDtypeStruct((n, 2 * D_MODEL), F32),
        ],
        compiler_params=_params(1),
        name="in_proj",
    )(x, g.reshape(1, -1), w, b_gate.reshape(1, -1), *tables)


def _norm_matmul_kernel(x_ref, g_ref, w_ref, o_ref):
    o_ref[...] = _dot(_rms(x_ref[...], g_ref[...]).astype(BF16), w_ref[...])


def _norm_matmul(x, g, w_bf16, tm):
    n, d = x.shape
    dout = w_bf16.shape[1]
    return pl.pallas_call(
        _norm_matmul_kernel,
        grid=(n // tm,),
        in_specs=[pl.BlockSpec((tm, d), lambda i: (i, 0)), _const_spec((1, d)), _const_spec((d, dout))],
        out_specs=pl.BlockSpec((tm, dout), lambda i: (i, 0)),
        out_shape=jax.ShapeDtypeStruct((n, dout), F32),
        compiler_params=_params(1),
        name="norm_matmul",
    )(x, g.reshape(1, -1), w_bf16)


def _rwkv_features(u, u_first_prev, w_refs, bd, hp, exact=False):
    mu_ref, w0_ref, w2_ref, a0_ref, a2_ref, g2_ref, kk_ref, ka_ref = w_refs
    row = lax.broadcasted_iota(jnp.int32, (u.shape[0], 1), 0)
    u_prev = jnp.where(row == 0, u_first_prev, pltpu.roll(u, 1, 0))
    um = u + (u_prev - u) * mu_ref[...]
    o1, o2, o3 = A_WIDTH, 2 * A_WIDTH, 3 * A_WIDTH
    o4 = o3 + A_DECAY_LORA
    o5 = o4 + A_ICLR_LORA
    r, k, v = um[:, :o1], um[:, o1:o2], um[:, o2:o3]
    xw, xa, xg = um[:, o3:o4], um[:, o4:o5], um[:, o5:]
    lora = lambda t, w_ref: _dot(_operand(t, exact), _operand(w_ref[...], exact), _precision(exact))
    w = -jax.nn.softplus(-(w0_ref[...] + lora(jnp.tanh(xw), w2_ref))) - 0.5
    e = jnp.exp(w)
    a = jax.nn.sigmoid(a0_ref[...] + lora(xa, a2_ref))
    g = lora(jax.nn.sigmoid(xg), g2_ref)
    kk = k * kk_ref[...]
    kkn = kk / jnp.maximum(jnp.sqrt(_dot(kk * kk, bd, hp)), 1e-12)
    k2 = k * (1.0 + (a - 1.0) * ka_ref[...])
    return r, k2, v, e, a, g, kkn


def _rwkv_output(y, r, k2, v, g, rk_ref, lng_ref, lnb_ref, bd, hp):
    inv_n = 1.0 / A_HEAD_DIM
    mean = _dot(y, bd, hp) * inv_n
    yc = y - mean
    var = _dot(yc * yc, bd, hp) * inv_n
    yn = yc * lax.rsqrt(var + A_LNX_EPS) * lng_ref[...] + lnb_ref[...]
    bonus = _dot(r * k2 * rk_ref[...], bd, hp) * v
    return (yn + bonus) * g


def _rwkv_kernel(u_ref, sh0_ref, s0_ref, mu_ref, w0_ref, w2_ref, a0_ref, a2_ref, g2_ref, kk_ref, ka_ref,
                 rk_ref, lng_ref, lnb_ref, bd_ref, y_ref, sfin_ref, shout_ref, st_scr, prev_scr,
                 *, chunk, n_sub, n_steps, precision):
    c = pl.program_id(1)
    hp = precision
    hd = A_HEAD_DIM
    rows = chunk * n_sub
    bf = lambda t: t.astype(BF16)

    @pl.when(c == 0)
    def _():
        st_scr[...] = s0_ref[0]
        prev_scr[...] = sh0_ref[0]

    u = u_ref[0]
    bd = bd_ref[...]
    r, k2, v, e, a, g, kkn = _rwkv_features(
        u, prev_scr[...], (mu_ref, w0_ref, w2_ref, a0_ref, a2_ref, g2_ref, kk_ref, ka_ref), bd, hp)
    prev_scr[...] = u[rows - 1:rows, :]
    b = kkn * a

    ri = lax.broadcasted_iota(jnp.int32, (rows, rows), 0)
    ci = lax.broadcasted_iota(jnp.int32, (rows, rows), 1)
    same = (ri // chunk) == (ci // chunk)
    cum = _dot((same & (ri >= ci)).astype(F32), e, hp)
    cum_end = _dot(same.astype(F32), e, hp)
    grow = jnp.exp(cum)
    to_end = jnp.exp(cum - cum_end)
    at = bf(-kkn * jnp.exp(e - cum))
    rt = bf(r * jnp.exp(-cum))
    bt = bf(b * grow)
    kt = bf(k2 * grow)
    bh = bf(b * to_end)
    kh = bf(k2 * to_end)
    vb = bf(v)
    dec_end = jnp.exp(-cum_end)

    li = lax.broadcasted_iota(jnp.int32, (chunk, chunk), 0)
    lj = lax.broadcasted_iota(jnp.int32, (chunk, chunk), 1)
    incl = li >= lj
    strict = li > lj
    eye = (li == lj).astype(F32)
    n_sq = int(math.log2(chunk)) - 1

    units = [(slice(s * chunk, (s + 1) * chunk), slice(h * hd, (h + 1) * hd)) for s in range(n_sub) for h in range(A_HEADS)]
    at_u = [at[rs, sl] for rs, sl in units]
    v_u = [vb[rs, sl] for rs, sl in units]
    m_u = [_dot_nt(jnp.concatenate([at[rs, sl], rt[rs, sl]], axis=0), jnp.concatenate([bt[rs, sl], kt[rs, sl]], axis=0))
           for rs, sl in units]
    a_ab = [jnp.where(strict, m[:chunk, :chunk], 0.0) for m in m_u]
    akv = [_dot(bf(jnp.where(strict, m[:chunk, chunk:], 0.0)), vh) for m, vh in zip(m_u, v_u)]
    m_r = [bf(jnp.concatenate([jnp.where(incl, m[chunk:, :chunk], 0.0), jnp.where(incl, m[chunk:, chunk:], 0.0)], axis=1))
           for m in m_u]
    tinv = [eye + a for a in a_ab]
    pw = a_ab
    for _ in range(n_sq):
        pw = [_dot(bf(p), bf(p)) for p in pw]
        tinv = [t + _dot(bf(t), bf(p)) for t, p in zip(tinv, pw)]
    w12 = [_dot(bf(t), jnp.concatenate([a_h, bf(x)], axis=1)) for t, a_h, x in zip(tinv, at_u, akv)]
    lhs_s = [jnp.concatenate([bf(w[:, :hd]), rt[rs, sl]], axis=0) for w, (rs, sl) in zip(w12, units)]
    bk_u = [jnp.concatenate([bh[rs, sl], kh[rs, sl]], axis=0) for rs, sl in units]

    y_rows = []
    for s in range(n_sub):
        idx = range(s * A_HEADS, (s + 1) * A_HEADS)
        st = [st_scr[h] for h in range(A_HEADS)]
        x = [_dot_nt(lhs_s[i], bf(st[h])) for h, i in enumerate(idx)]
        uv = [jnp.concatenate([bf(x[h][:chunk] + w12[i][:, hd:]), v_u[i]], axis=0) for h, i in enumerate(idx)]
        ys = [x[h][chunk:] + _dot(m_r[i], uv[h]) for h, i in enumerate(idx)]
        for h, i in enumerate(idx):
            d_h = dec_end[s * chunk:s * chunk + 1, h * hd:(h + 1) * hd]
            st_scr[h] = st[h] * d_h + _dot_tn(uv[h], bk_u[i])
        y_rows.append(jnp.concatenate(ys, axis=1))
    y = jnp.concatenate(y_rows, axis=0)
    y_ref[0] = _rwkv_output(y, r, k2, v, g, rk_ref, lng_ref, lnb_ref, bd, hp)

    @pl.when(c == n_steps - 1)
    def _():
        sfin_ref[0] = st_scr[...]
        shout_ref[0] = u[rows - 1:rows, :]


def _rwkv_step_kernel(u_ref, sh0_ref, s0_ref, mu_ref, w0_ref, w2_ref, a0_ref, a2_ref, g2_ref, kk_ref, ka_ref,
                      rk_ref, lng_ref, lnb_ref, bd_ref, y_ref, sfin_ref, shout_ref, *, t_valid, precision, exact):
    hp = precision
    hd = A_HEAD_DIM
    u = u_ref[0]
    n_rows = u.shape[0]
    bd = bd_ref[...]
    r, k2, v, e, a, g, kkn = _rwkv_features(
        u, sh0_ref[0], (mu_ref, w0_ref, w2_ref, a0_ref, a2_ref, g2_ref, kk_ref, ka_ref), bd, hp, exact)
    rnd = lambda t: _contract_round(t, exact)
    decay = jnp.exp(-e)
    b = kkn * a
    eye = (lax.broadcasted_iota(jnp.int32, (hd, hd), 0) == lax.broadcasted_iota(jnp.int32, (hd, hd), 1)).astype(F32)
    to_col = lambda t: jnp.sum(eye * t, axis=1, keepdims=True)
    to_row = lambda t: jnp.sum(eye * t, axis=0, keepdims=True)
    y_heads = []
    for h in range(A_HEADS):
        sl = slice(h * hd, (h + 1) * hd)
        s = s0_ref[0, h]
        y_rows = []
        for t in range(t_valid):
            tt = slice(t, t + 1)
            sa = jnp.sum(rnd(s) * rnd(-kkn[tt, sl]), axis=1, keepdims=True)
            s = s * decay[tt, sl] + sa * b[tt, sl] + to_col(v[tt, sl]) * k2[tt, sl]
            y_rows.append(to_row(jnp.sum(rnd(s) * rnd(r[tt, sl]), axis=1, keepdims=True)))
        sfin_ref[0, h] = s
        y_rows.append(jnp.zeros((n_rows - t_valid, hd), F32))
        y_heads.append(jnp.concatenate(y_rows, axis=0))
    y = jnp.concatenate(y_heads, axis=1)
    y_ref[0] = _rwkv_output(y, r, k2, v, g, rk_ref, lng_ref, lnb_ref, bd, hp)
    shout_ref[0] = u[t_valid - 1:t_valid, :]


def _rwkv(u_a, shift0, s0, p, chunk, t_valid, precision):
    bsz, t_len, _ = u_a.shape
    hd = A_HEAD_DIM
    bd = jnp.asarray(np.kron(np.eye(A_HEADS, dtype=np.float32), np.ones((hd, hd), np.float32)))
    row = lambda t: t.reshape(1, -1)
    if chunk:
        assert t_valid == t_len
        n_chunks = t_len // (chunk * RWKV_SUB)
        kern = functools.partial(_rwkv_kernel, chunk=chunk, n_sub=RWKV_SUB, n_steps=n_chunks, precision=precision)
        scratch = [pltpu.VMEM((A_HEADS, hd, hd), F32), pltpu.VMEM((1, A_PROJ), F32)]
        chunk = chunk * RWKV_SUB
    else:
        n_chunks, chunk = 1, t_len
        kern = functools.partial(_rwkv_step_kernel, t_valid=t_valid, precision=precision, exact=True)
        scratch = []
    state_spec = pl.BlockSpec((1, A_HEADS, hd, hd), lambda b, c: (b, 0, 0, 0))
    shift_spec = pl.BlockSpec((1, 1, A_PROJ), lambda b, c: (b, 0, 0))
    return pl.pallas_call(
        kern,
        grid=(bsz, n_chunks),
        in_specs=[
            pl.BlockSpec((1, chunk, A_PROJ), lambda b, c: (b, c, 0)),
            shift_spec, state_spec,
            _const_spec((1, A_PROJ)), _const_spec((1, A_WIDTH)), _const_spec((A_DECAY_LORA, A_WIDTH)),
            _const_spec((1, A_WIDTH)), _const_spec((A_ICLR_LORA, A_WIDTH)), _const_spec((A_GATE_LORA, A_WIDTH)),
            _const_spec((1, A_WIDTH)), _const_spec((1, A_WIDTH)), _const_spec((1, A_WIDTH)),
            _const_spec((1, A_WIDTH)), _const_spec((1, A_WIDTH)), _const_spec((A_WIDTH, A_WIDTH)),
        ],
        out_specs=[pl.BlockSpec((1, chunk, A_WIDTH), lambda b, c: (b, c, 0)), state_spec, shift_spec],
        out_shape=[
            jax.ShapeDtypeStruct((bsz, t_len, A_WIDTH), F32),
            jax.ShapeDtypeStruct((bsz, A_HEADS, hd, hd), F32),
            jax.ShapeDtypeStruct((bsz, 1, A_PROJ), F32),
        ],
        scratch_shapes=scratch,
        compiler_params=_params(2),
        name="rwkv7",
    )(u_a, shift0.reshape(bsz, 1, A_PROJ), s0, row(p['rwkv_mu']), row(p['rwkv_w0']), p['rwkv_w2'],
      row(p['rwkv_a0']), p['rwkv_a2'], p['rwkv_g2'], row(p['rwkv_k_k']), row(p['rwkv_k_a']),
      row(p['rwkv_r_k']), row(p['rwkv_lnx_g']), row(p['rwkv_lnx_b']), bd)


def _swa_prompt_kernel(q_ref, k_ref, v_ref, o_ref, lse_ref, *, n_seq, n_blk):
    blk = SWA_BLOCK
    hd = B_HEAD_DIM
    qi = lax.broadcasted_iota(jnp.int32, (blk, 2 * blk), 0) + blk
    ki = lax.broadcasted_iota(jnp.int32, (blk, 2 * blk), 1)
    dist = qi - ki
    band = (dist >= 0) & (dist <= blk)
    scale = hd ** -0.5

    def body(it, carry):
        s_idx = it // n_blk
        n = it % n_blk
        cur = pl.multiple_of(n * blk, blk)
        prv = pl.multiple_of(jnp.maximum(n - 1, 0) * blk, blk)
        valid = band & (ki >= jnp.where(n > 0, 0, blk))
        q = q_ref[s_idx, pl.ds(cur, blk), :]
        kb = jnp.concatenate([k_ref[s_idx, pl.ds(prv, blk), :], k_ref[s_idx, pl.ds(cur, blk), :]], axis=0).astype(BF16)
        vb = jnp.concatenate([v_ref[s_idx, pl.ds(prv, blk), :], v_ref[s_idx, pl.ds(cur, blk), :]], axis=0).astype(BF16)
        outs, lses = [], []
        for h in range(B_HEADS_PER_GROUP):
            sl = slice(h * hd, (h + 1) * hd)
            s = _dot_nt(q[:, sl].astype(BF16), kb[:, sl]) * scale
            s = jnp.where(valid, s, NEG_INF)
            m = jnp.max(s, axis=-1, keepdims=True)
            p = jnp.exp(s - m)
            den = jnp.sum(p, axis=-1, keepdims=True)
            outs.append(_dot((p / den).astype(BF16), vb[:, sl]))
            lses.append(jnp.broadcast_to(m + jnp.log(den), (blk, hd)))
        o_ref[s_idx, pl.ds(cur, blk), :] = jnp.concatenate(outs, axis=1)
        lse_ref[s_idx, pl.ds(cur, blk), :] = jnp.concatenate(lses, axis=1)
        return carry

    lax.fori_loop(0, n_seq * n_blk, body, 0)


def _swa_prompt(q, k, v, seqs_per_step):
    n_seq, length, width = q.shape
    n_blk = length // SWA_BLOCK
    spec = pl.BlockSpec((seqs_per_step, length, width), lambda i: (i, 0, 0))
    shp = jax.ShapeDtypeStruct((n_seq, length, width), F32)
    return pl.pallas_call(
        functools.partial(_swa_prompt_kernel, n_seq=seqs_per_step, n_blk=n_blk),
        grid=(n_seq // seqs_per_step,),
        in_specs=[spec, spec, spec],
        out_specs=[spec, spec],
        out_shape=[shp, shp],
        compiler_params=_params(1),
        name="swa_prompt",
    )(q, k, v)


def _swa_sample_kernel(qkv_ref, c0_ref, c1_ref, c2_ref, o_ref, lse_ref, n0_ref, n1_ref, n2_ref, *, t_len, exact):
    hd = B_HEAD_DIM
    nh = B_HEADS_PER_GROUP
    gw = B_GROUP_WIDTH
    rows = t_len * nh
    qkv = qkv_ref[0]
    scale = hd ** -0.5
    r_i = lax.broadcasted_iota(jnp.int32, (rows, gw), 0)
    l_i = lax.broadcasted_iota(jnp.int32, (rows, gw), 1)
    head_lanes = (l_i // hd) == (r_i % nh)
    t_of_row = lax.broadcasted_iota(jnp.int32, (rows, 1), 0) // nh
    outs, lses = [], []
    for gi, ((window, dil), c_ref, n_ref) in enumerate(zip(B_GROUPS, (c0_ref, c1_ref, c2_ref), (n0_ref, n1_ref, n2_ref))):
        buf_len = c_ref.shape[2]
        q = qkv[:, gi * gw:(gi + 1) * gw]
        k_new = qkv[:, B_WIDTH + gi * gw:B_WIDTH + (gi + 1) * gw]
        v_new = qkv[:, 2 * B_WIDTH + gi * gw:2 * B_WIDTH + (gi + 1) * gw]
        kc = c_ref[0, 0]
        vc = c_ref[0, 1]
        qx = jnp.zeros((rows, gw), F32)
        for t in range(t_len):
            qx = jnp.where(head_lanes & (t_of_row == t), q[t:t + 1, :], qx)
        qx = _contract_round(qx, exact)
        k_new_r = _contract_round(k_new, exact)
        v_new_r = _contract_round(v_new, exact)
        s_c = _dot_nt(_operand(qx, exact), _operand(kc, exact), _precision(exact)) * scale
        t_c = lax.broadcasted_iota(jnp.int32, (rows, buf_len), 0) // nh
        j_c = lax.broadcasted_iota(jnp.int32, (rows, buf_len), 1)
        ok_c = (j_c >= t_c) & (((j_c - t_c) & (dil - 1)) == 0)
        s_c = jnp.where(ok_c, s_c, NEG_INF)
        s_n = []
        for i in range(t_len):
            s_i = jnp.sum(qx * k_new_r[i:i + 1, :], axis=-1, keepdims=True) * scale
            ok_i = (t_of_row >= i) & (((t_of_row - i) & (dil - 1)) == 0)
            s_n.append(jnp.where(ok_i, s_i, NEG_INF))
        m = functools.reduce(jnp.maximum, s_n, jnp.max(s_c, axis=-1, keepdims=True))
        p_c = jnp.exp(s_c - m)
        p_n = [jnp.exp(s_i - m) for s_i in s_n]
        den = functools.reduce(lambda a, b: a + b, p_n, jnp.sum(p_c, axis=-1, keepdims=True))
        o = _dot(_operand(p_c / den, exact), _operand(vc, exact), _precision(exact))
        for i in range(t_len):
            o = o + _contract_round(p_n[i] / den, exact) * v_new_r[i:i + 1, :]
        o = jnp.where(head_lanes, o, 0.0)
        lse = jnp.where(head_lanes, m + jnp.log(den), 0.0)
        outs.append(jnp.concatenate(
            [jnp.sum(o[t * nh:(t + 1) * nh, :], axis=0, keepdims=True) for t in range(t_len)], axis=0))
        lses.append(jnp.concatenate(
            [jnp.sum(lse[t * nh:(t + 1) * nh, :], axis=0, keepdims=True) for t in range(t_len)], axis=0))
        for j, new in ((0, k_new), (1, v_new)):
            n_ref[0, j, :buf_len - t_len, :] = c_ref[0, j, t_len:, :]
            n_ref[0, j, buf_len - t_len:, :] = new
    o_ref[0] = jnp.concatenate(outs, axis=1)
    lse_ref[0] = jnp.concatenate(lses, axis=1)


def _swa_sample(qkv, caches):
    bsz, t_len, _ = qkv.shape
    cache_specs = [pl.BlockSpec((1, 2) + c.shape[2:], lambda b: (b, 0, 0, 0)) for c in caches]
    row_spec = pl.BlockSpec((1, t_len, B_WIDTH), lambda b: (b, 0, 0))
    return pl.pallas_call(
        functools.partial(_swa_sample_kernel, t_len=t_len, exact=True),
        grid=(bsz,),
        in_specs=[pl.BlockSpec((1, t_len, B_PROJ), lambda b: (b, 0, 0))] + cache_specs,
        out_specs=[row_spec, row_spec] + cache_specs,
        out_shape=[jax.ShapeDtypeStruct((bsz, t_len, B_WIDTH), F32)] * 2
        + [jax.ShapeDtypeStruct(c.shape, F32) for c in caches],
        compiler_params=_params(1),
        name="swa_sample",
    )(qkv, *caches)


def _mid_kernel(x_ref, ya_ref, ob_ref, lse_ref, gate_ref, mk_ref, mv_ref, wa_ref, wb_ref, wo_ref, gm_ref,
                wq_ref, wmo_ref, gf_ref, wr_ref, br_ref, xn_all_ref, x2_ref, xn_ref, route_ref, *, rows_per_batch, exact):
    del xn_all_ref
    tm = x_ref.shape[0]
    gw = B_GROUP_WIDTH
    lses = [lse_ref[:, g * gw:(g + 1) * gw] for g in range(len(B_GROUPS))]
    m = functools.reduce(jnp.maximum, lses)
    es = [jnp.exp(l - m) for l in lses]
    den = functools.reduce(lambda p, q: p + q, es)
    rnd = lambda t: _contract_round(t, exact)
    op = lambda t: _operand(t, exact)
    prec = _precision(exact)
    yb = functools.reduce(lambda p, q: p + q, [rnd(es[g] / den) * rnd(ob_ref[:, g * gw:(g + 1) * gw])
                                               for g in range(len(B_GROUPS))])
    gates = gate_ref[...]
    merged = (gates[:, :D_MODEL] * _dot(op(ya_ref[...]), wa_ref[...], prec)
              + gates[:, D_MODEL:] * _dot(op(yb), wb_ref[...], prec))
    x1 = x_ref[...] + _dot(op(merged), wo_ref[...], prec)

    q = _dot(op(_rms(x1, gm_ref[...])), wq_ref[...], prec)
    n_b = mk_ref.shape[0]
    mk = op(mk_ref[...].reshape(n_b * N_MEM, MEM_WIDTH))
    mv = op(mv_ref[...].reshape(n_b * N_MEM, MEM_WIDTH))
    if n_b > 1:
        rb = lax.broadcasted_iota(jnp.int32, (tm, n_b * N_MEM), 0) // rows_per_batch
        cb = lax.broadcasted_iota(jnp.int32, (tm, n_b * N_MEM), 1) // N_MEM
        same = rb == cb
    heads = []
    for h in range(MEM_HEADS):
        sl = slice(h * MEM_HEAD_DIM, (h + 1) * MEM_HEAD_DIM)
        s = _dot_nt(op(q[:, sl]), mk[:, sl], prec) * (MEM_HEAD_DIM ** -0.5)
        if n_b > 1:
            s = jnp.where(same, s, NEG_INF)
        s = s - jnp.max(s, axis=-1, keepdims=True)
        p = jnp.exp(s)
        heads.append(_dot(op(p / jnp.sum(p, axis=-1, keepdims=True)), mv[:, sl], prec))
    x2 = x1 + _dot(op(jnp.concatenate(heads, axis=1)), wmo_ref[...], prec)
    x2_ref[...] = x2

    xn = _rms(x2, gf_ref[...])
    xn_ref[...] = xn
    logits = _dot(op(xn), wr_ref[...], prec) + br_ref[...]
    lane = lax.broadcasted_iota(jnp.int32, (tm, LANES), 1)
    gl = jnp.where(lane < N_GROUPS, logits, NEG_INF)
    gmax = jnp.max(gl, axis=-1, keepdims=True)
    grp = jnp.min(jnp.where(gl == gmax, lane, LANES), axis=-1, keepdims=True)
    w_grp = 1.0 / jnp.sum(jnp.exp(gl - gmax), axis=-1, keepdims=True)
    first = N_GROUPS + grp * EXPERTS_PER_GROUP
    el = jnp.where((lane >= first) & (lane < first + EXPERTS_PER_GROUP), logits, NEG_INF)
    m1 = jnp.max(el, axis=-1, keepdims=True)
    i1 = jnp.min(jnp.where(el == m1, lane, LANES), axis=-1, keepdims=True)
    el2 = jnp.where(lane == i1, NEG_INF, el)
    m2 = jnp.max(el2, axis=-1, keepdims=True)
    i2 = jnp.min(jnp.where(el2 == m2, lane, LANES), axis=-1, keepdims=True)
    e2 = jnp.exp(m2 - m1)
    g1 = w_grp / (1.0 + e2)
    g2 = w_grp * e2 / (1.0 + e2)
    route = jnp.where(lane == 0, (i1 - N_GROUPS).astype(F32), 0.0)
    route = jnp.where(lane == 1, (i2 - N_GROUPS).astype(F32), route)
    route = jnp.where(lane == 2, g1, route)
    route_ref[...] = jnp.where(lane == 3, g2, route)


def _mid(x, y_a, o_b, lse_b, gates, mem_k, mem_v, mk_map, mv_map, n_b, rows_per_batch, w, tm, xn_all, row0):
    n = x.shape[0]
    row = lambda width: pl.BlockSpec((tm, width), lambda i: (i, 0))
    in_specs = [
        row(D_MODEL), row(A_WIDTH), row(B_WIDTH), row(B_WIDTH), row(2 * D_MODEL),
        pl.BlockSpec((n_b, N_MEM, MEM_WIDTH), mk_map), pl.BlockSpec((n_b, N_MEM, MEM_WIDTH), mv_map),
        _const_spec((A_WIDTH, D_MODEL)), _const_spec((B_GROUP_WIDTH, D_MODEL)), _const_spec((D_MODEL, D_MODEL)),
        _const_spec((1, D_MODEL)), _const_spec((D_MODEL, MEM_WIDTH)), _const_spec((MEM_WIDTH, D_MODEL)),
        _const_spec((1, D_MODEL)), _const_spec((D_MODEL, LANES)), _const_spec((1, LANES)),
        pl.BlockSpec(memory_space=pl.ANY),
    ]
    args = [x, y_a, o_b, lse_b, gates, mem_k, mem_v, w['wa'], w['wb'], w['wo'], w['gm'], w['wq'], w['wmo'],
            w['gf'], w['wr'], w['br'], xn_all]
    blk0 = row0 // tm
    return pl.pallas_call(
        functools.partial(_mid_kernel, rows_per_batch=rows_per_batch, exact=w['wa'].dtype == F32),
        grid=(n // tm,),
        in_specs=in_specs,
        out_specs=[row(D_MODEL), pl.BlockSpec((tm, D_MODEL), lambda i: (i + blk0, 0)), row(LANES)],
        out_shape=[
            jax.ShapeDtypeStruct((n, D_MODEL), F32),
            jax.ShapeDtypeStruct(xn_all.shape, F32),
            jax.ShapeDtypeStruct((n, LANES), F32),
        ],
        input_output_aliases={len(args) - 1: 1},
        compiler_params=_params(1),
        name="mid",
    )(*args)


def _row_copy(src_hbm, idx, dst_buf, slot, j, sem):
    return pltpu.make_async_copy(src_hbm.at[pl.ds(idx, 1), :], dst_buf.at[slot, pl.ds(j, 1), :], sem.at[slot])


def _gather_start(idx_ref, src_hbm, dst_buf, slot, sem, n_rows):
    for j in range(n_rows):
        _row_copy(src_hbm, idx_ref[0, 0, j], dst_buf, slot, j, sem).start()


def _gather_wait(src_hbm, dst_buf, slot, sem, n_rows):
    for j in range(n_rows):
        _row_copy(src_hbm, 0, dst_buf, slot, j, sem).wait()


def _experts_kernel(meta_ref, be_ref, idx_ref, idx_next_ref, x_hbm, wg_ref, wu_ref, wd_ref, o_ref, xbuf, sem):
    i = pl.program_id(0)
    n_used = meta_ref[0]
    slot = i % 2

    @pl.when(i == 0)
    def _():
        _gather_start(idx_ref, x_hbm, xbuf, 0, sem, MOE_ROWS)

    @pl.when(i + 1 < n_used)
    def _():
        _gather_start(idx_next_ref, x_hbm, xbuf, 1 - slot, sem, MOE_ROWS)

    @pl.when(i < n_used)
    def _():
        _gather_wait(x_hbm, xbuf, slot, sem, MOE_ROWS)
        xb = xbuf[slot].astype(BF16)
        hg = _dot(xb, wg_ref[0].astype(BF16))
        hu = _dot(xb, wu_ref[0].astype(BF16))
        hh = (jax.nn.silu(hg) * hu).astype(BF16)
        o_ref[...] = _dot(hh, wd_ref[0].astype(BF16))

    @pl.when(i >= n_used)
    def _():
        o_ref[...] = jnp.zeros_like(o_ref)


def _experts(xn_all, row_tok, block_e, n_used, w_gate, w_up, w_down):
    n_blocks = block_e.shape[0]
    idx3 = row_tok.reshape(n_blocks, 1, MOE_ROWS)
    idx_spec = lambda f: pl.BlockSpec((1, 1, MOE_ROWS), f, memory_space=pltpu.SMEM)
    grid_spec = pltpu.PrefetchScalarGridSpec(
        num_scalar_prefetch=2,
        grid=(n_blocks,),
        in_specs=[
            idx_spec(lambda i, meta, be: (i, 0, 0)),
            idx_spec(lambda i, meta, be: (jnp.minimum(i + 1, n_blocks - 1), 0, 0)),
            pl.BlockSpec(memory_space=pl.ANY),
            pl.BlockSpec((1, D_MODEL, EXPERT_FF), lambda i, meta, be: (be[i], 0, 0)),
            pl.BlockSpec((1, D_MODEL, EXPERT_FF), lambda i, meta, be: (be[i], 0, 0)),
            pl.BlockSpec((1, EXPERT_FF, D_MODEL), lambda i, meta, be: (be[i], 0, 0)),
        ],
        out_specs=pl.BlockSpec((MOE_ROWS, D_MODEL), lambda i, meta, be: (i, 0)),
        scratch_shapes=[pltpu.VMEM((2, MOE_ROWS, D_MODEL), F32), pltpu.SemaphoreType.DMA((2,))],
    )
    return pl.pallas_call(
        _experts_kernel,
        grid_spec=grid_spec,
        out_shape=jax.ShapeDtypeStruct((n_blocks * MOE_ROWS, D_MODEL), F32),
        compiler_params=_params(1),
        name="experts",
    )(n_used.reshape(1), block_e, idx3, idx3, xn_all, w_gate, w_up, w_down)


def _combine_kernel(pos_ref, pos_next_ref, x_ref, route_ref, yb_hbm, g_ref, o_ref, ybuf, sem, *, n_tiles):
    i = pl.program_id(0)
    tm = x_ref.shape[0]
    slot = i % 2

    @pl.when(i == 0)
    def _():
        _gather_start(pos_ref, yb_hbm, ybuf, 0, sem, 2 * tm)

    if n_tiles > 1:
        @pl.when(i + 1 < n_tiles)
        def _():
            _gather_start(pos_next_ref, yb_hbm, ybuf, 1 - slot, sem, 2 * tm)

    _gather_wait(yb_hbm, ybuf, slot, sem, 2 * tm)
    route = route_ref[...]
    y = x_ref[...] + (route[:, 2:3] * ybuf[slot, :tm, :] + route[:, 3:4] * ybuf[slot, tm:, :])
    o_ref[...] = _rms(y, g_ref[...])


def _combine(x2, route, pos, yb, g_final, tm):
    n = x2.shape[0]
    n_tiles = n // tm
    pos_spec = lambda f: pl.BlockSpec((1, 1, 2 * tm), f, memory_space=pltpu.SMEM)
    return pl.pallas_call(
        functools.partial(_combine_kernel, n_tiles=n_tiles),
        grid=(n_tiles,),
        in_specs=[
            pos_spec(lambda i: (i, 0, 0)),
            pos_spec(lambda i: (jnp.minimum(i + 1, n_tiles - 1), 0, 0)),
            pl.BlockSpec((tm, D_MODEL), lambda i: (i, 0)),
            pl.BlockSpec((tm, LANES), lambda i: (i, 0)),
            pl.BlockSpec(memory_space=pl.ANY),
            _const_spec((1, D_MODEL)),
        ],
        out_specs=pl.BlockSpec((tm, D_MODEL), lambda i: (i, 0)),
        out_shape=jax.ShapeDtypeStruct((n, D_MODEL), F32),
        scratch_shapes=[pltpu.VMEM((2, 2 * tm, D_MODEL), F32), pltpu.SemaphoreType.DMA((2,))],
        compiler_params=_params(1),
        name="combine",
    )(pos, pos, x2, route, yb, g_final.reshape(1, -1))


def _dispatch(eid):
    n_tok = eid.shape[0]
    n_rows = n_tok * TOP_K
    n_blocks = n_rows // MOE_ROWS + N_EXPERTS
    flat_e = eid.reshape(-1)
    onehot = (flat_e[:, None] == jnp.arange(N_EXPERTS, dtype=jnp.int32)[None, :]).astype(jnp.int32)
    csum = jnp.cumsum(onehot, axis=0)
    counts = csum[-1]
    rank = jnp.sum((csum - onehot) * onehot, axis=1)
    padded = (counts + MOE_ROWS - 1) // MOE_ROWS * MOE_ROWS
    pad_end = jnp.cumsum(padded)
    pad_start = pad_end - padded
    dest = pad_start[flat_e] + rank
    flat_tok = jnp.arange(n_rows, dtype=jnp.int32) // TOP_K
    row_tok = jnp.zeros((n_blocks * MOE_ROWS,), jnp.int32).at[dest].set(flat_tok)
    block_start = jnp.arange(n_blocks, dtype=jnp.int32) * MOE_ROWS
    block_e = jnp.minimum(jnp.sum((pad_end[None, :] <= block_start[:, None]).astype(jnp.int32), axis=1), N_EXPERTS - 1)
    n_used = (pad_end[-1] // MOE_ROWS).astype(jnp.int32)
    return row_tok, block_e, n_used, dest.reshape(n_tok, TOP_K).astype(jnp.int32)


def _tile_pos(pos, tm):
    n = pos.shape[0]
    return pos.reshape(n // tm, tm, TOP_K).transpose(0, 2, 1).reshape(n // tm, 1, TOP_K * tm)


def _deinterleave(t, dil):
    bsz, s_len, width = t.shape
    return t.reshape(bsz, s_len // dil, dil, width).transpose(0, 2, 1, 3).reshape(bsz * dil, s_len // dil, width)


def _interleave(t, bsz, dil):
    _, n_cls, width = t.shape
    return t.reshape(bsz, dil, n_cls, width).transpose(0, 2, 1, 3).reshape(bsz, n_cls * dil, width)


def kernel(x_prompt, x_sample, state_rwkv, state_shift, cache_swa_w128, cache_swa_w512, cache_swa_w2048,
           cache_mem_kv, mem_prompt, norm_mix_g, w_in, b_gate, rwkv_mu, rwkv_w0, rwkv_w2, rwkv_a0, rwkv_a2,
           rwkv_g2, rwkv_k_k, rwkv_k_a, rwkv_r_k, rwkv_lnx_g, rwkv_lnx_b, w_branch_a, w_branch_b, w_out,
           norm_mem_g, norm_memkv_g, w_mem_q, w_mem_kv, w_mem_out, norm_ffn_g, w_router_group, b_router_group,
           w_router_expert, b_router_expert, w_exp_gate, w_exp_up, w_exp_down, norm_final_g):
    bsz, s_len, _ = x_prompt.shape
    dbs, t_len, _ = x_sample.shape
    n_p, n_s = bsz * s_len, dbs * t_len
    hp = lax.Precision.HIGHEST
    rw = dict(rwkv_mu=rwkv_mu, rwkv_w0=rwkv_w0, rwkv_w2=rwkv_w2, rwkv_a0=rwkv_a0, rwkv_a2=rwkv_a2, rwkv_g2=rwkv_g2,
              rwkv_k_k=rwkv_k_k, rwkv_k_a=rwkv_k_a, rwkv_r_k=rwkv_r_k, rwkv_lnx_g=rwkv_lnx_g, rwkv_lnx_b=rwkv_lnx_b)
    pad = LANES - N_GROUPS - N_EXPERTS
    mid_w_s = dict(
        wa=w_branch_a, wb=w_branch_b, wo=w_out, gm=norm_mem_g.reshape(1, -1), wq=w_mem_q, wmo=w_mem_out,
        gf=norm_ffn_g.reshape(1, -1),
        wr=jnp.concatenate([w_router_group, w_router_expert, jnp.zeros((D_MODEL, pad), F32)], axis=1),
        br=jnp.concatenate([b_router_group, b_router_expert, jnp.zeros((pad,), F32)]).reshape(1, -1))
    mid_w_p = {k: (v.astype(BF16) if k.startswith('w') else v) for k, v in mid_w_s.items()}
    w_in_b = w_in.astype(BF16)

    xp = x_prompt.reshape(n_p, D_MODEL)
    tm = 256
    ua_p, qkv_p, gates_p = _in_proj(xp, norm_mix_g, w_in_b, b_gate, _rope_tables(jnp.arange(s_len)), tm)
    s0 = jnp.zeros((bsz, A_HEADS, A_HEAD_DIM, A_HEAD_DIM), F32)
    ya_p, st_p, shift_p = _rwkv(ua_p.reshape(bsz, s_len, A_PROJ), jnp.zeros((bsz, A_PROJ), F32), s0, rw,
                                RWKV_CHUNK, s_len, hp)
    qkv3 = qkv_p.reshape(bsz, s_len, 3, len(B_GROUPS), B_GROUP_WIDTH)
    outs, lses, p_bufs = [], [], []
    for gi, (window, dil) in enumerate(B_GROUPS):
        q, k, v = (_deinterleave(qkv3[:, :, j, gi], dil) for j in range(3))
        n_cls = s_len // dil
        o, lse = _swa_prompt(q, k, v, max(1, 2048 // n_cls))
        outs.append(_interleave(o, bsz, dil))
        lses.append(_interleave(lse, bsz, dil))
        keep = min(window, s_len)
        p_bufs.append(jnp.stack([qkv3[:, s_len - keep:, 1, gi], qkv3[:, s_len - keep:, 2, gi]], axis=1)
                      .reshape(bsz, 2, keep, B_HEADS_PER_GROUP, B_HEAD_DIM))
    ob_p = jnp.concatenate(outs, axis=-1).reshape(n_p, B_WIDTH)
    lse_p = jnp.concatenate(lses, axis=-1).reshape(n_p, B_WIDTH)

    memkv = _norm_matmul(mem_prompt.reshape(bsz * N_MEM, D_MODEL), norm_memkv_g, w_mem_kv.astype(BF16), 256)
    memkv3 = memkv.reshape(bsz, N_MEM, 2 * MEM_WIDTH)
    mem_kv_prompt = memkv3.reshape(bsz, N_MEM, 2, MEM_HEADS, MEM_HEAD_DIM).transpose(0, 2, 1, 3, 4)
    tiles_per_batch = s_len // tm
    xn_all = jnp.zeros((n_p + n_s, D_MODEL), F32)
    x2_p, xn_all, route_p = _mid(
        xp, ya_p.reshape(n_p, A_WIDTH), ob_p, lse_p, gates_p, memkv3, memkv3,
        lambda i: (i // tiles_per_batch, 0, 0), lambda i: (i // tiles_per_batch, 0, 1),
        1, s_len, mid_w_p, tm, xn_all, 0)

    xs = x_sample.reshape(n_s, D_MODEL)
    pos_s = PAST_LEN + (jnp.arange(n_s) % t_len)
    ua_s, qkv_s, gates_s = _in_proj(xs, norm_mix_g, w_in, b_gate, _rope_tables(pos_s), n_s)
    t_pad = 8
    ua_s3 = jnp.pad(ua_s.reshape(dbs, t_len, A_PROJ), ((0, 0), (0, t_pad - t_len), (0, 0)))
    ya_s, st_s, shift_s = _rwkv(ua_s3, state_shift, state_rwkv, rw, 0, t_len, hp)
    ya_s = ya_s[:, :t_len].reshape(n_s, A_WIDTH)
    caches = [c.reshape(c.shape[0], 2, c.shape[2], B_GROUP_WIDTH) for c in (cache_swa_w128, cache_swa_w512, cache_swa_w2048)]
    ob_s, lse_s, nb0, nb1, nb2 = _swa_sample(qkv_s.reshape(dbs, t_len, B_PROJ), caches)
    s_bufs = [nb.reshape(nb.shape[0], 2, nb.shape[2], B_HEADS_PER_GROUP, B_HEAD_DIM) for nb in (nb0, nb1, nb2)]
    mem_s = cache_mem_kv.reshape(dbs, 2 * N_MEM, MEM_WIDTH)
    tm_s = 32
    x2_s, xn_all, route_s = _mid(
        xs, ya_s, ob_s.reshape(n_s, B_WIDTH), lse_s.reshape(n_s, B_WIDTH), gates_s, mem_s, mem_s,
        lambda i: (i, 0, 0), lambda i: (i, 1, 0), tm_s // t_len, t_len, mid_w_s, tm_s, xn_all, n_p)

    route = jnp.concatenate([route_p, route_s], axis=0)
    eid = route[:, :TOP_K].astype(jnp.int32)
    row_tok, block_e, n_used, pos = _dispatch(eid)
    yb = _experts(xn_all, row_tok, block_e, n_used, w_exp_gate, w_exp_up, w_exp_down)
    tm_c = 128
    y_p = _combine(x2_p, route_p, _tile_pos(pos[:n_p], tm_c), yb, norm_final_g, tm_c)
    y_s = _combine(x2_s, route_s, _tile_pos(pos[n_p:], tm_c), yb, norm_final_g, tm_c)

    return (y_p.reshape(bsz, s_len, D_MODEL), y_s.reshape(dbs, t_len, D_MODEL),
            st_p, shift_p.reshape(bsz, A_PROJ), p_bufs[0], p_bufs[1], p_bufs[2], mem_kv_prompt,
            st_s, shift_s.reshape(dbs, A_PROJ), s_bufs[0], s_bufs[1], s_bufs[2])
```

```python
import functools
import math

import jax
import jax.numpy as jnp
import numpy as np
from jax import lax
from jax.experimental import pallas as pl
from jax.experimental.pallas import tpu as pltpu

F32 = jnp.float32
BF16 = jnp.bfloat16

D_MODEL = 1024
A_HEADS = 8
A_HEAD_DIM = 64
A_WIDTH = A_HEADS * A_HEAD_DIM
A_DECAY_LORA = 64
A_ICLR_LORA = 64
A_GATE_LORA = 128
A_PROJ = 3 * A_WIDTH + A_DECAY_LORA + A_ICLR_LORA + A_GATE_LORA
A_LNX_EPS = 64e-5
B_GROUPS = ((128, 1), (512, 4), (2048, 16))
B_HEADS_PER_GROUP = 4
B_HEAD_DIM = 64
B_GROUP_WIDTH = B_HEADS_PER_GROUP * B_HEAD_DIM
B_WIDTH = B_GROUP_WIDTH * len(B_GROUPS)
B_PROJ = 3 * B_WIDTH
ROPE_THETA = 500000.0
ROPE_DIM = B_HEAD_DIM // 4
ROPE_HALF = ROPE_DIM // 2
SWA_BLOCK = 128
N_MEM = 256
MEM_HEADS = 4
MEM_HEAD_DIM = 128
MEM_WIDTH = MEM_HEADS * MEM_HEAD_DIM
N_GROUPS = 4
EXPERTS_PER_GROUP = 8
N_EXPERTS = N_GROUPS * EXPERTS_PER_GROUP
TOP_K = 2
EXPERT_FF = 512
RMS_EPS = 1e-6
PAST_LEN = 8192

LANES = 128
VMEM_LIMIT = 56 * 1024 * 1024
RWKV_CHUNK = 64
MOE_ROWS = 256
NEG_INF = float("-inf")


def _dot(a, b, precision=None):
    return jnp.dot(a, b, preferred_element_type=F32, precision=precision)


def _dot_nt(a, b, precision=None):
    return lax.dot_general(a, b, (((1,), (1,)), ((), ())), preferred_element_type=F32, precision=precision)


def _dot_tn(a, b, precision=None):
    return lax.dot_general(a, b, (((0,), (0,)), ((), ())), preferred_element_type=F32, precision=precision)


def _bf16_round(t):
    return t.astype(BF16).astype(F32)


def _operand(t, exact):
    return t.astype(F32) if exact else t.astype(BF16)


def _precision(exact):
    return lax.Precision.HIGHEST if exact else None


def _contract_round(t, exact):
    return t if exact else _bf16_round(t)


def _rms(x, g):
    return x * lax.rsqrt(jnp.mean(x * x, axis=-1, keepdims=True) + RMS_EPS) * g


def _const_spec(shape):
    nd = len(shape)
    return pl.BlockSpec(shape, lambda *_: (0,) * nd, pipeline_mode=pl.Buffered(1))


def _params(n_axes):
    return pltpu.CompilerParams(dimension_semantics=("arbitrary",) * n_axes, vmem_limit_bytes=VMEM_LIMIT)


def _in_proj_kernel(x_ref, g_ref, w_ref, bg_ref, rc_ref, rs1_ref, rs2_ref, ua_ref, qkv_ref, gate_ref, *, exact):
    xn = _operand(_rms(x_ref[...], g_ref[...]), exact)
    mm = lambda w: _dot(xn, w, _precision(exact))
    ua_ref[...] = mm(w_ref[:, :A_PROJ])
    rc, rs1, rs2 = rc_ref[...], rs1_ref[...], rs2_ref[...]
    n_rot = 2 * B_WIDTH // LANES
    for j in range(B_PROJ // LANES):
        lo = A_PROJ + j * LANES
        s = mm(w_ref[:, lo:lo + LANES])
        if j < n_rot:
            s = s * rc + pltpu.roll(s, LANES - ROPE_HALF, 1) * rs1 + pltpu.roll(s, ROPE_HALF, 1) * rs2
        qkv_ref[:, j * LANES:(j + 1) * LANES] = s
    gate_ref[...] = jax.nn.sigmoid(mm(w_ref[:, A_PROJ + B_PROJ:]) + bg_ref[...])


def _rope_tables(pos):
    inv_freq = ROPE_THETA ** (-jnp.arange(ROPE_HALF, dtype=F32) * 2.0 / ROPE_DIM)
    ang = pos.astype(F32)[:, None] * inv_freq[None, :]
    cos, sin = jnp.cos(ang), jnp.sin(ang)
    n = pos.shape[0]
    rest = B_HEAD_DIM - ROPE_DIM
    c = jnp.concatenate([cos, cos, jnp.ones((n, rest), F32)], axis=1)
    s1 = jnp.concatenate([-sin, jnp.zeros((n, ROPE_HALF + rest), F32)], axis=1)
    s2 = jnp.concatenate([jnp.zeros((n, ROPE_HALF), F32), sin, jnp.zeros((n, rest), F32)], axis=1)
    rep = LANES // B_HEAD_DIM
    return tuple(jnp.tile(t, (1, rep)) for t in (c, s1, s2))


def _in_proj(x, g, w, b_gate, tables, tm):
    n = x.shape[0]
    p_rows = tables[0].shape[0]
    t_tiles = p_rows // tm
    in_proj_w = w.shape[1]
    tab_spec = pl.BlockSpec((tm, LANES), lambda i: (i % t_tiles, 0))
    return pl.pallas_call(
        functools.partial(_in_proj_kernel, exact=w.dtype == F32),
        grid=(n // tm,),
        in_specs=[
            pl.BlockSpec((tm, D_MODEL), lambda i: (i, 0)),
            _const_spec((1, D_MODEL)),
            _const_spec((D_MODEL, in_proj_w)),
            _const_spec((1, 2 * D_MODEL)),
            tab_spec, tab_spec, tab_spec,
        ],
        out_specs=[
            pl.BlockSpec((tm, A_PROJ), lambda i: (i, 0)),
            pl.BlockSpec((tm, B_PROJ), lambda i: (i, 0)),
            pl.BlockSpec((tm, 2 * D_MODEL), lambda i: (i, 0)),
        ],
        out_shape=[
            jax.ShapeDtypeStruct((n, A_PROJ), F32),
            jax.ShapeDtypeStruct((n, B_PROJ), F32),
            jax.ShapeDtypeStruct((n, 2 * D_MODEL), F32),
        ],
        compiler_params=_params(1),
        name="in_proj",
    )(x, g.reshape(1, -1), w, b_gate.reshape(1, -1), *tables)


def _norm_matmul_kernel(x_ref, g_ref, w_ref, o_ref):
    o_ref[...] = _dot(_rms(x_ref[...], g_ref[...]).astype(BF16), w_ref[...])


def _norm_matmul(x, g, w_bf16, tm):
    n, d = x.shape
    dout = w_bf16.shape[1]
    return pl.pallas_call(
        _norm_matmul_kernel,
        grid=(n // tm,),
        in_specs=[pl.BlockSpec((tm, d), lambda i: (i, 0)), _const_spec((1, d)), _const_spec((d, dout))],
        out_specs=pl.BlockSpec((tm, dout), lambda i: (i, 0)),
        out_shape=jax.ShapeDtypeStruct((n, dout), F32),
        compiler_params=_params(1),
        name="norm_matmul",
    )(x, g.reshape(1, -1), w_bf16)


RWKV_HEADS_PER_PACK = 4
RWKV_PACK_WIDTH = RWKV_HEADS_PER_PACK * A_HEAD_DIM
RWKV_PACKS = A_HEADS // RWKV_HEADS_PER_PACK


def _shift_rows(u, first_prev):
    row = lax.broadcasted_iota(jnp.int32, (u.shape[0], 1), 0)
    return jnp.where(row == 0, first_prev, pltpu.roll(u, 1, 0))


def _head_sum(x, bd, exact, pieces=1):
    if exact:
        return _dot(x, bd, lax.Precision.HIGHEST)
    ones = bd.astype(BF16)
    hi = x.astype(BF16)
    if pieces == 1:
        return _dot(hi, ones)
    return _dot(hi, ones) + _dot((x - hi.astype(F32)).astype(BF16), ones)


def _rwkv_features(u, u_prev, w_refs, bd, exact):
    mu_ref, w0_ref, w2_ref, a0_ref, a2_ref, g2_ref, kk_ref, ka_ref = w_refs
    um = u + (u_prev - u) * mu_ref[...]
    o1, o2, o3 = A_WIDTH, 2 * A_WIDTH, 3 * A_WIDTH
    o4 = o3 + A_DECAY_LORA
    o5 = o4 + A_ICLR_LORA
    r, k, v = um[:, :o1], um[:, o1:o2], um[:, o2:o3]
    xw, xa, xg = um[:, o3:o4], um[:, o4:o5], um[:, o5:]
    lora = lambda t, w_ref: _dot(_operand(t, exact), _operand(w_ref[...], exact), _precision(exact))
    w = -jax.nn.softplus(-(w0_ref[...] + lora(jnp.tanh(xw), w2_ref))) - 0.5
    e = jnp.exp(w)
    a = jax.nn.sigmoid(a0_ref[...] + lora(xa, a2_ref))
    g = lora(jax.nn.sigmoid(xg), g2_ref)
    kk = k * kk_ref[...]
    kkn = kk / jnp.maximum(jnp.sqrt(_head_sum(kk * kk, bd, exact, pieces=2)), 1e-12)
    k2 = k * (1.0 + (a - 1.0) * ka_ref[...])
    return r, k2, v, e, a, g, kkn


def _rwkv_output(y, r, k2, v, g, rk_ref, lng_ref, lnb_ref, bd, exact):
    inv_n = 1.0 / A_HEAD_DIM
    mean = _head_sum(y, bd, exact) * inv_n
    yc = y - mean
    var = _head_sum(yc * yc, bd, exact) * inv_n
    yn = yc * lax.rsqrt(var + A_LNX_EPS) * lng_ref[...] + lnb_ref[...]
    bonus = _head_sum(r * k2 * rk_ref[...], bd, exact) * v
    return (yn + bonus) * g


def _rwkv_chunk_kernel(u_ref, sh0_ref, s0_ref, mu_ref, w0_ref, w2_ref, a0_ref, a2_ref, g2_ref, kk_ref, ka_ref,
                       rk_ref, lng_ref, lnb_ref, bd_ref, y_ref, sfin_ref, shout_ref, st_scr, prev_scr, *, n_steps):
    c = pl.program_id(0)
    n_b, chunk, _ = u_ref.shape
    hd, hpp, pw_ = A_HEAD_DIM, RWKV_HEADS_PER_PACK, RWKV_PACK_WIDTH
    bf = lambda t: t.astype(BF16)

    @pl.when(c == 0)
    def _():
        for b in range(n_b):
            prev_scr[b:b + 1, :] = sh0_ref[b]
            for p in range(RWKV_PACKS):
                st_scr[b, p] = jnp.concatenate([s0_ref[b, p * hpp + h] for h in range(hpp)], axis=1)

    u_b = [u_ref[b] for b in range(n_b)]
    u = jnp.concatenate(u_b, axis=0)
    u_prev = jnp.concatenate([_shift_rows(u_b[b], prev_scr[b:b + 1, :]) for b in range(n_b)], axis=0)
    for b in range(n_b):
        prev_scr[b:b + 1, :] = u_b[b][chunk - 1:chunk, :]
    bd = bd_ref[...]
    r, k2, v, e, a, g, kkn = _rwkv_features(
        u, u_prev, (mu_ref, w0_ref, w2_ref, a0_ref, a2_ref, g2_ref, kk_ref, ka_ref), bd, False)
    b_ = kkn * a

    li = lax.broadcasted_iota(jnp.int32, (chunk, chunk), 0)
    lj = lax.broadcasted_iota(jnp.int32, (chunk, chunk), 1)
    tri = bf((li >= lj).astype(F32))
    e_hi = bf(e)
    e_rest = e - e_hi.astype(F32)
    e_mid = bf(e_rest)
    e_lo = bf(e_rest - e_mid.astype(F32))
    cums, ends = [], []
    for b in range(n_b):
        rs = slice(b * chunk, (b + 1) * chunk)
        cb = _dot(tri, e_hi[rs]) + (_dot(tri, e_mid[rs]) + _dot(tri, e_lo[rs]))
        cums.append(cb)
        ends.append(jnp.broadcast_to(cb[chunk - 1:chunk, :], (chunk, A_WIDTH)))
    cum = jnp.concatenate(cums, axis=0)
    cum_end = jnp.concatenate(ends, axis=0)
    grow = jnp.exp(cum)
    to_end = jnp.exp(cum - cum_end)
    at = bf(-kkn * jnp.exp(e - cum))
    rt = bf(r * jnp.exp(-cum))
    bt = bf(b_ * grow)
    kt = bf(k2 * grow)
    bh = bf(b_ * to_end)
    kh = bf(k2 * to_end)
    vb = bf(v)
    dec_end = jnp.exp(-cum_end)

    lane_head = lax.broadcasted_iota(jnp.int32, (1, pw_), 1) // hd
    head_mask = [lane_head == h for h in range(hpp)]

    def block_diag(x):
        return jnp.concatenate([jnp.where(head_mask[h], x, jnp.zeros_like(x)) for h in range(hpp)], axis=0)

    assert chunk == hd
    ti = lax.broadcasted_iota(jnp.int32, (chunk, pw_), 0)
    tj = lax.broadcasted_iota(jnp.int32, (chunk, pw_), 1) % chunk
    strict = ti > tj
    incl = ti >= tj
    eye = (ti == tj).astype(F32)
    n_sq = int(math.log2(chunk)) - 1

    streams = [(b, p, slice(b * chunk, (b + 1) * chunk), slice(p * pw_, (p + 1) * pw_))
               for b in range(n_b) for p in range(RWKV_PACKS)]
    at_s = [at[rs, cs] for _, _, rs, cs in streams]
    rt_s = [rt[rs, cs] for _, _, rs, cs in streams]
    v_s = [vb[rs, cs] for _, _, rs, cs in streams]
    m = [_dot_nt(jnp.concatenate([a_, r_], axis=0),
                 jnp.concatenate([block_diag(bt[rs, cs]), block_diag(kt[rs, cs])], axis=0))
         for a_, r_, (_, _, rs, cs) in zip(at_s, rt_s, streams)]
    a_ab = [jnp.where(strict, x[:chunk, :pw_], 0.0) for x in m]
    akv = [_dot(bf(jnp.where(strict, x[:chunk, pw_:], 0.0)), block_diag(vs)) for x, vs in zip(m, v_s)]
    m_r = [bf(jnp.concatenate([jnp.where(incl, x[chunk:, :pw_], 0.0), jnp.where(incl, x[chunk:, pw_:], 0.0)], axis=1))
           for x in m]
    tinv = [eye + x for x in a_ab]
    pw = a_ab
    for _ in range(n_sq):
        pw = [_dot(bf(x), block_diag(bf(x))) for x in pw]
        tinv = [t + _dot(bf(t), block_diag(bf(x))) for t, x in zip(tinv, pw)]
    w12 = [_dot(bf(t), jnp.concatenate([block_diag(a_), block_diag(bf(x))], axis=1))
           for t, a_, x in zip(tinv, at_s, akv)]

    st = [st_scr[b, p] for b, p, _, _ in streams]
    x = [_dot_nt(jnp.concatenate([bf(w[:, :pw_]), r_], axis=0), block_diag(bf(s))) for w, r_, s in zip(w12, rt_s, st)]
    uu = [bf(xi[:chunk] + w[:, pw_:]) for xi, w in zip(x, w12)]
    ys = [xi[chunk:] + _dot(mr, jnp.concatenate([block_diag(ui), block_diag(vs)], axis=0))
          for xi, mr, ui, vs in zip(x, m_r, uu, v_s)]
    for i, (b, p, rs, cs) in enumerate(streams):
        upd = _dot_tn(jnp.concatenate([uu[i], v_s[i]], axis=0), jnp.concatenate([bh[rs, cs], kh[rs, cs]], axis=0))
        diag = functools.reduce(lambda s_, t_: s_ + t_,
                                [jnp.where(head_mask[h], upd[h * hd:(h + 1) * hd, :], 0.0) for h in range(hpp)])
        st_scr[b, p] = st[i] * dec_end[rs.start:rs.start + 1, cs] + diag
    y = jnp.concatenate([jnp.concatenate(ys[b * RWKV_PACKS:(b + 1) * RWKV_PACKS], axis=1) for b in range(n_b)], axis=0)
    out = _rwkv_output(y, r, k2, v, g, rk_ref, lng_ref, lnb_ref, bd, False)
    for b in range(n_b):
        y_ref[b] = out[b * chunk:(b + 1) * chunk]

    @pl.when(c == n_steps - 1)
    def _():
        for b in range(n_b):
            shout_ref[b] = u_b[b][chunk - 1:chunk, :]
            for p in range(RWKV_PACKS):
                for h in range(hpp):
                    sfin_ref[b, p * hpp + h] = st_scr[b, p][:, h * hd:(h + 1) * hd]


def _rwkv_step_kernel(u_ref, sh0_ref, s0_ref, mu_ref, w0_ref, w2_ref, a0_ref, a2_ref, g2_ref, kk_ref, ka_ref,
                      rk_ref, lng_ref, lnb_ref, bd_ref, y_ref, sfin_ref, shout_ref, *, t_valid, exact):
    hd = A_HEAD_DIM
    u = u_ref[0]
    n_rows = u.shape[0]
    bd = bd_ref[...]
    r, k2, v, e, a, g, kkn = _rwkv_features(
        u, _shift_rows(u, sh0_ref[0]), (mu_ref, w0_ref, w2_ref, a0_ref, a2_ref, g2_ref, kk_ref, ka_ref), bd, exact)
    rnd = lambda t: _contract_round(t, exact)
    decay = jnp.exp(-e)
    b = kkn * a
    eye = (lax.broadcasted_iota(jnp.int32, (hd, hd), 0) == lax.broadcasted_iota(jnp.int32, (hd, hd), 1)).astype(F32)
    to_col = lambda t: jnp.sum(eye * t, axis=1, keepdims=True)
    to_row = lambda t: jnp.sum(eye * t, axis=0, keepdims=True)
    y_heads = []
    for h in range(A_HEADS):
        sl = slice(h * hd, (h + 1) * hd)
        s = s0_ref[0, h]
        y_rows = []
        for t in range(t_valid):
            tt = slice(t, t + 1)
            sa = jnp.sum(rnd(s) * rnd(-kkn[tt, sl]), axis=1, keepdims=True)
            s = s * decay[tt, sl] + sa * b[tt, sl] + to_col(v[tt, sl]) * k2[tt, sl]
            y_rows.append(to_row(jnp.sum(rnd(s) * rnd(r[tt, sl]), axis=1, keepdims=True)))
        sfin_ref[0, h] = s
        y_rows.append(jnp.zeros((n_rows - t_valid, hd), F32))
        y_heads.append(jnp.concatenate(y_rows, axis=0))
    y = jnp.concatenate(y_heads, axis=1)
    y_ref[0] = _rwkv_output(y, r, k2, v, g, rk_ref, lng_ref, lnb_ref, bd, exact)
    shout_ref[0] = u[t_valid - 1:t_valid, :]


def _rwkv(u_a, shift0, s0, p, chunk, t_valid):
    bsz, t_len, _ = u_a.shape
    hd = A_HEAD_DIM
    bd = jnp.asarray(np.kron(np.eye(A_HEADS, dtype=np.float32), np.ones((hd, hd), np.float32)))
    row = lambda t: t.reshape(1, -1)
    if chunk:
        assert t_valid == t_len and t_len % chunk == 0
        grid = (t_len // chunk,)
        kern = functools.partial(_rwkv_chunk_kernel, n_steps=grid[0])
        scratch = [pltpu.VMEM((bsz, RWKV_PACKS, hd, RWKV_PACK_WIDTH), F32), pltpu.VMEM((bsz, A_PROJ), F32)]
        u_spec = pl.BlockSpec((bsz, chunk, A_PROJ), lambda c: (0, c, 0))
        y_spec = pl.BlockSpec((bsz, chunk, A_WIDTH), lambda c: (0, c, 0))
        state_spec = pl.BlockSpec((bsz, A_HEADS, hd, hd), lambda c: (0, 0, 0, 0))
        shift_spec = pl.BlockSpec((bsz, 1, A_PROJ), lambda c: (0, 0, 0))
    else:
        grid = (bsz,)
        kern = functools.partial(_rwkv_step_kernel, t_valid=t_valid, exact=True)
        scratch = []
        u_spec = pl.BlockSpec((1, t_len, A_PROJ), lambda b: (b, 0, 0))
        y_spec = pl.BlockSpec((1, t_len, A_WIDTH), lambda b: (b, 0, 0))
        state_spec = pl.BlockSpec((1, A_HEADS, hd, hd), lambda b: (b, 0, 0, 0))
        shift_spec = pl.BlockSpec((1, 1, A_PROJ), lambda b: (b, 0, 0))
    return pl.pallas_call(
        kern,
        grid=grid,
        in_specs=[
            u_spec, shift_spec, state_spec,
            _const_spec((1, A_PROJ)), _const_spec((1, A_WIDTH)), _const_spec((A_DECAY_LORA, A_WIDTH)),
            _const_spec((1, A_WIDTH)), _const_spec((A_ICLR_LORA, A_WIDTH)), _const_spec((A_GATE_LORA, A_WIDTH)),
            _const_spec((1, A_WIDTH)), _const_spec((1, A_WIDTH)), _const_spec((1, A_WIDTH)),
            _const_spec((1, A_WIDTH)), _const_spec((1, A_WIDTH)), _const_spec((A_WIDTH, A_WIDTH)),
        ],
        out_specs=[y_spec, state_spec, shift_spec],
        out_shape=[
            jax.ShapeDtypeStruct((bsz, t_len, A_WIDTH), F32),
            jax.ShapeDtypeStruct((bsz, A_HEADS, hd, hd), F32),
            jax.ShapeDtypeStruct((bsz, 1, A_PROJ), F32),
        ],
        scratch_shapes=scratch,
        compiler_params=_params(1),
        name="rwkv7",
    )(u_a, shift0.reshape(bsz, 1, A_PROJ), s0, row(p['rwkv_mu']), row(p['rwkv_w0']), p['rwkv_w2'],
      row(p['rwkv_a0']), p['rwkv_a2'], p['rwkv_g2'], row(p['rwkv_k_k']), row(p['rwkv_k_a']),
      row(p['rwkv_r_k']), row(p['rwkv_lnx_g']), row(p['rwkv_lnx_b']), bd)


def _swa_prompt_kernel(q_ref, k_ref, v_ref, o_ref, lse_ref, *, n_seq, n_blk):
    blk = SWA_BLOCK
    hd = B_HEAD_DIM
    qi = lax.broadcasted_iota(jnp.int32, (blk, 2 * blk), 0) + blk
    ki = lax.broadcasted_iota(jnp.int32, (blk, 2 * blk), 1)
    dist = qi - ki
    band = (dist >= 0) & (dist <= blk)
    scale = hd ** -0.5

    def body(it, carry):
        s_idx = it // n_blk
        n = it % n_blk
        cur = pl.multiple_of(n * blk, blk)
        prv = pl.multiple_of(jnp.maximum(n - 1, 0) * blk, blk)
        valid = band & (ki >= jnp.where(n > 0, 0, blk))
        q = q_ref[s_idx, pl.ds(cur, blk), :]
        kb = jnp.concatenate([k_ref[s_idx, pl.ds(prv, blk), :], k_ref[s_idx, pl.ds(cur, blk), :]], axis=0).astype(BF16)
        vb = jnp.concatenate([v_ref[s_idx, pl.ds(prv, blk), :], v_ref[s_idx, pl.ds(cur, blk), :]], axis=0).astype(BF16)
        outs, lses = [], []
        for h in range(B_HEADS_PER_GROUP):
            sl = slice(h * hd, (h + 1) * hd)
            s = _dot_nt(q[:, sl].astype(BF16), kb[:, sl]) * scale
            s = jnp.where(valid, s, NEG_INF)
            m = jnp.max(s, axis=-1, keepdims=True)
            p = jnp.exp(s - m)
            den = jnp.sum(p, axis=-1, keepdims=True)
            outs.append(_dot((p / den).astype(BF16), vb[:, sl]))
            lses.append(jnp.broadcast_to(m + jnp.log(den), (blk, hd)))
        o_ref[s_idx, pl.ds(cur, blk), :] = jnp.concatenate(outs, axis=1)
        lse_ref[s_idx, pl.ds(cur, blk), :] = jnp.concatenate(lses, axis=1)
        return carry

    lax.fori_loop(0, n_seq * n_blk, body, 0)


def _swa_prompt(q, k, v, seqs_per_step):
    n_seq, length, width = q.shape
    n_blk = length // SWA_BLOCK
    spec = pl.BlockSpec((seqs_per_step, length, width), lambda i: (i, 0, 0))
    shp = jax.ShapeDtypeStruct((n_seq, length, width), F32)
    return pl.pallas_call(
        functools.partial(_swa_prompt_kernel, n_seq=seqs_per_step, n_blk=n_blk),
        grid=(n_seq // seqs_per_step,),
        in_specs=[spec, spec, spec],
        out_specs=[spec, spec],
        out_shape=[shp, shp],
        compiler_params=_params(1),
        name="swa_prompt",
    )(q, k, v)


def _swa_sample_kernel(qkv_ref, c0_ref, c1_ref, c2_ref, o_ref, lse_ref, n0_ref, n1_ref, n2_ref, *, t_len, exact):
    hd = B_HEAD_DIM
    nh = B_HEADS_PER_GROUP
    gw = B_GROUP_WIDTH
    rows = t_len * nh
    qkv = qkv_ref[0]
    scale = hd ** -0.5
    r_i = lax.broadcasted_iota(jnp.int32, (rows, gw), 0)
    l_i = lax.broadcasted_iota(jnp.int32, (rows, gw), 1)
    head_lanes = (l_i // hd) == (r_i % nh)
    t_of_row = lax.broadcasted_iota(jnp.int32, (rows, 1), 0) // nh
    outs, lses = [], []
    for gi, ((window, dil), c_ref, n_ref) in enumerate(zip(B_GROUPS, (c0_ref, c1_ref, c2_ref), (n0_ref, n1_ref, n2_ref))):
        buf_len = c_ref.shape[2]
        q = qkv[:, gi * gw:(gi + 1) * gw]
        k_new = qkv[:, B_WIDTH + gi * gw:B_WIDTH + (gi + 1) * gw]
        v_new = qkv[:, 2 * B_WIDTH + gi * gw:2 * B_WIDTH + (gi + 1) * gw]
        kc = c_ref[0, 0]
        vc = c_ref[0, 1]
        qx = jnp.zeros((rows, gw), F32)
        for t in range(t_len):
            qx = jnp.where(head_lanes & (t_of_row == t), q[t:t + 1, :], qx)
        qx = _contract_round(qx, exact)
        k_new_r = _contract_round(k_new, exact)
        v_new_r = _contract_round(v_new, exact)
        s_c = _dot_nt(_operand(qx, exact), _operand(kc, exact), _precision(exact)) * scale
        t_c = lax.broadcasted_iota(jnp.int32, (rows, buf_len), 0) // nh
        j_c = lax.broadcasted_iota(jnp.int32, (rows, buf_len), 1)
        ok_c = (j_c >= t_c) & (((j_c - t_c) & (dil - 1)) == 0)
        s_c = jnp.where(ok_c, s_c, NEG_INF)
        s_n = []
        for i in range(t_len):
            s_i = jnp.sum(qx * k_new_r[i:i + 1, :], axis=-1, keepdims=True) * scale
            ok_i = (t_of_row >= i) & (((t_of_row - i) & (dil - 1)) == 0)
            s_n.append(jnp.where(ok_i, s_i, NEG_INF))
        m = functools.reduce(jnp.maximum, s_n, jnp.max(s_c, axis=-1, keepdims=True))
        p_c = jnp.exp(s_c - m)
        p_n = [jnp.exp(s_i - m) for s_i in s_n]
        den = functools.reduce(lambda a, b: a + b, p_n, jnp.sum(p_c, axis=-1, keepdims=True))
        o = _dot(_operand(p_c / den, exact), _operand(vc, exact), _precision(exact))
        for i in range(t_len):
            o = o + _contract_round(p_n[i] / den, exact) * v_new_r[i:i + 1, :]
        o = jnp.where(head_lanes, o, 0.0)
        lse = jnp.where(head_lanes, m + jnp.log(den), 0.0)
        outs.append(jnp.concatenate(
            [jnp.sum(o[t * nh:(t + 1) * nh, :], axis=0, keepdims=True) for t in range(t_len)], axis=0))
        lses.append(jnp.concatenate(
            [jnp.sum(lse[t * nh:(t + 1) * nh, :], axis=0, keepdims=True) for t in range(t_len)], axis=0))
        for j, new in ((0, k_new), (1, v_new)):
            n_ref[0, j, :buf_len - t_len, :] = c_ref[0, j, t_len:, :]
            n_ref[0, j, buf_len - t_len:, :] = new
    o_ref[0] = jnp.concatenate(outs, axis=1)
    lse_ref[0] = jnp.concatenate(lses, axis=1)


def _swa_sample(qkv, caches):
    bsz, t_len, _ = qkv.shape
    cache_specs = [pl.BlockSpec((1, 2) + c.shape[2:], lambda b: (b, 0, 0, 0)) for c in caches]
    row_spec = pl.BlockSpec((1, t_len, B_WIDTH), lambda b: (b, 0, 0))
    return pl.pallas_call(
        functools.partial(_swa_sample_kernel, t_len=t_len, exact=True),
        grid=(bsz,),
        in_specs=[pl.BlockSpec((1, t_len, B_PROJ), lambda b: (b, 0, 0))] + cache_specs,
        out_specs=[row_spec, row_spec] + cache_specs,
        out_shape=[jax.ShapeDtypeStruct((bsz, t_len, B_WIDTH), F32)] * 2
        + [jax.ShapeDtypeStruct(c.shape, F32) for c in caches],
        compiler_params=_params(1),
        name="swa_sample",
    )(qkv, *caches)


def _mid_kernel(x_ref, ya_ref, ob_ref, lse_ref, gate_ref, mk_ref, mv_ref, wa_ref, wb_ref, wo_ref, gm_ref,
                wq_ref, wmo_ref, gf_ref, wr_ref, br_ref, xn_all_ref, x2_ref, xn_ref, route_ref, *, rows_per_batch, exact):
    del xn_all_ref
    tm = x_ref.shape[0]
    gw = B_GROUP_WIDTH
    lses = [lse_ref[:, g * gw:(g + 1) * gw] for g in range(len(B_GROUPS))]
    m = functools.reduce(jnp.maximum, lses)
    es = [jnp.exp(l - m) for l in lses]
    den = functools.reduce(lambda p, q: p + q, es)
    rnd = lambda t: _contract_round(t, exact)
    op = lambda t: _operand(t, exact)
    prec = _precision(exact)
    yb = functools.reduce(lambda p, q: p + q, [rnd(es[g] / den) * rnd(ob_ref[:, g * gw:(g + 1) * gw])
                                               for g in range(len(B_GROUPS))])
    gates = gate_ref[...]
    merged = (gates[:, :D_MODEL] * _dot(op(ya_ref[...]), wa_ref[...], prec)
              + gates[:, D_MODEL:] * _dot(op(yb), wb_ref[...], prec))
    x1 = x_ref[...] + _dot(op(merged), wo_ref[...], prec)

    q = _dot(op(_rms(x1, gm_ref[...])), wq_ref[...], prec)
    n_b = mk_ref.shape[0]
    mk = op(mk_ref[...].reshape(n_b * N_MEM, MEM_WIDTH))
    mv = op(mv_ref[...].reshape(n_b * N_MEM, MEM_WIDTH))
    if n_b > 1:
        rb = lax.broadcasted_iota(jnp.int32, (tm, n_b * N_MEM), 0) // rows_per_batch
        cb = lax.broadcasted_iota(jnp.int32, (tm, n_b * N_MEM), 1) // N_MEM
        same = rb == cb
    heads = []
    for h in range(MEM_HEADS):
        sl = slice(h * MEM_HEAD_DIM, (h + 1) * MEM_HEAD_DIM)
        s = _dot_nt(op(q[:, sl]), mk[:, sl], prec) * (MEM_HEAD_DIM ** -0.5)
        if n_b > 1:
            s = jnp.where(same, s, NEG_INF)
        s = s - jnp.max(s, axis=-1, keepdims=True)
        p = jnp.exp(s)
        heads.append(_dot(op(p / jnp.sum(p, axis=-1, keepdims=True)), mv[:, sl], prec))
    x2 = x1 + _dot(op(jnp.concatenate(heads, axis=1)), wmo_ref[...], prec)
    x2_ref[...] = x2

    xn = _rms(x2, gf_ref[...])
    xn_ref[...] = xn
    logits = _dot(op(xn), wr_ref[...], prec) + br_ref[...]
    lane = lax.broadcasted_iota(jnp.int32, (tm, LANES), 1)
    gl = jnp.where(lane < N_GROUPS, logits, NEG_INF)
    gmax = jnp.max(gl, axis=-1, keepdims=True)
    grp = jnp.min(jnp.where(gl == gmax, lane, LANES), axis=-1, keepdims=True)
    w_grp = 1.0 / jnp.sum(jnp.exp(gl - gmax), axis=-1, keepdims=True)
    first = N_GROUPS + grp * EXPERTS_PER_GROUP
    el = jnp.where((lane >= first) & (lane < first + EXPERTS_PER_GROUP), logits, NEG_INF)
    m1 = jnp.max(el, axis=-1, keepdims=True)
    i1 = jnp.min(jnp.where(el == m1, lane, LANES), axis=-1, keepdims=True)
    el2 = jnp.where(lane == i1, NEG_INF, el)
    m2 = jnp.max(el2, axis=-1, keepdims=True)
    i2 = jnp.min(jnp.where(el2 == m2, lane, LANES), axis=-1, keepdims=True)
    e2 = jnp.exp(m2 - m1)
    g1 = w_grp / (1.0 + e2)
    g2 = w_grp * e2 / (1.0 + e2)
    route = jnp.where(lane == 0, (i1 - N_GROUPS).astype(F32), 0.0)
    route = jnp.where(lane == 1, (i2 - N_GROUPS).astype(F32), route)
    route = jnp.where(lane == 2, g1, route)
    route_ref[...] = jnp.where(lane == 3, g2, route)


def _mid(x, y_a, o_b, lse_b, gates, mem_k, mem_v, mk_map, mv_map, n_b, rows_per_batch, w, tm, xn_all, row0):
    n = x.shape[0]
    row = lambda width: pl.BlockSpec((tm, width), lambda i: (i, 0))
    in_specs = [
        row(D_MODEL), row(A_WIDTH), row(B_WIDTH), row(B_WIDTH), row(2 * D_MODEL),
        pl.BlockSpec((n_b, N_MEM, MEM_WIDTH), mk_map), pl.BlockSpec((n_b, N_MEM, MEM_WIDTH), mv_map),
        _const_spec((A_WIDTH, D_MODEL)), _const_spec((B_GROUP_WIDTH, D_MODEL)), _const_spec((D_MODEL, D_MODEL)),
        _const_spec((1, D_MODEL)), _const_spec((D_MODEL, MEM_WIDTH)), _const_spec((MEM_WIDTH, D_MODEL)),
        _const_spec((1, D_MODEL)), _const_spec((D_MODEL, LANES)), _const_spec((1, LANES)),
        pl.BlockSpec(memory_space=pl.ANY),
    ]
    args = [x, y_a, o_b, lse_b, gates, mem_k, mem_v, w['wa'], w['wb'], w['wo'], w['gm'], w['wq'], w['wmo'],
            w['gf'], w['wr'], w['br'], xn_all]
    blk0 = row0 // tm
    return pl.pallas_call(
        functools.partial(_mid_kernel, rows_per_batch=rows_per_batch, exact=w['wa'].dtype == F32),
        grid=(n // tm,),
        in_specs=in_specs,
        out_specs=[row(D_MODEL), pl.BlockSpec((tm, D_MODEL), lambda i: (i + blk0, 0)), row(LANES)],
        out_shape=[
            jax.ShapeDtypeStruct((n, D_MODEL), F32),
            jax.ShapeDtypeStruct(xn_all.shape, F32),
            jax.ShapeDtypeStruct((n, LANES), F32),
        ],
        input_output_aliases={len(args) - 1: 1},
        compiler_params=_params(1),
        name="mid",
    )(*args)


def _row_copy(src_hbm, idx, dst_buf, slot, j, sem):
    return pltpu.make_async_copy(src_hbm.at[pl.ds(idx, 1), :], dst_buf.at[slot, pl.ds(j, 1), :], sem.at[slot])


def _gather_start(idx_ref, src_hbm, dst_buf, slot, sem, n_rows):
    for j in range(n_rows):
        _row_copy(src_hbm, idx_ref[0, 0, j], dst_buf, slot, j, sem).start()


def _gather_wait(src_hbm, dst_buf, slot, sem, n_rows):
    for j in range(n_rows):
        _row_copy(src_hbm, 0, dst_buf, slot, j, sem).wait()


def _experts_kernel(meta_ref, be_ref, idx_ref, idx_next_ref, x_hbm, wg_ref, wu_ref, wd_ref, o_ref, xbuf, sem):
    i = pl.program_id(0)
    n_used = meta_ref[0]
    slot = i % 2

    @pl.when(i == 0)
    def _():
        _gather_start(idx_ref, x_hbm, xbuf, 0, sem, MOE_ROWS)

    @pl.when(i + 1 < n_used)
    def _():
        _gather_start(idx_next_ref, x_hbm, xbuf, 1 - slot, sem, MOE_ROWS)

    @pl.when(i < n_used)
    def _():
        _gather_wait(x_hbm, xbuf, slot, sem, MOE_ROWS)
        xb = xbuf[slot].astype(BF16)
        hg = _dot(xb, wg_ref[0].astype(BF16))
        hu = _dot(xb, wu_ref[0].astype(BF16))
        hh = (jax.nn.silu(hg) * hu).astype(BF16)
        o_ref[...] = _dot(hh, wd_ref[0].astype(BF16))

    @pl.when(i >= n_used)
    def _():
        o_ref[...] = jnp.zeros_like(o_ref)


def _experts(xn_all, row_tok, block_e, n_used, w_gate, w_up, w_down):
    n_blocks = block_e.shape[0]
    idx3 = row_tok.reshape(n_blocks, 1, MOE_ROWS)
    idx_spec = lambda f: pl.BlockSpec((1, 1, MOE_ROWS), f, memory_space=pltpu.SMEM)
    grid_spec = pltpu.PrefetchScalarGridSpec(
        num_scalar_prefetch=2,
        grid=(n_blocks,),
        in_specs=[
            idx_spec(lambda i, meta, be: (i, 0, 0)),
            idx_spec(lambda i, meta, be: (jnp.minimum(i + 1, n_blocks - 1), 0, 0)),
            pl.BlockSpec(memory_space=pl.ANY),
            pl.BlockSpec((1, D_MODEL, EXPERT_FF), lambda i, meta, be: (be[i], 0, 0)),
            pl.BlockSpec((1, D_MODEL, EXPERT_FF), lambda i, meta, be: (be[i], 0, 0)),
            pl.BlockSpec((1, EXPERT_FF, D_MODEL), lambda i, meta, be: (be[i], 0, 0)),
        ],
        out_specs=pl.BlockSpec((MOE_ROWS, D_MODEL), lambda i, meta, be: (i, 0)),
        scratch_shapes=[pltpu.VMEM((2, MOE_ROWS, D_MODEL), F32), pltpu.SemaphoreType.DMA((2,))],
    )
    return pl.pallas_call(
        _experts_kernel,
        grid_spec=grid_spec,
        out_shape=jax.ShapeDtypeStruct((n_blocks * MOE_ROWS, D_MODEL), F32),
        compiler_params=_params(1),
        name="experts",
    )(n_used.reshape(1), block_e, idx3, idx3, xn_all, w_gate, w_up, w_down)


def _combine_kernel(pos_ref, pos_next_ref, x_ref, route_ref, yb_hbm, g_ref, o_ref, ybuf, sem, *, n_tiles):
    i = pl.program_id(0)
    tm = x_ref.shape[0]
    slot = i % 2

    @pl.when(i == 0)
    def _():
        _gather_start(pos_ref, yb_hbm, ybuf, 0, sem, 2 * tm)

    if n_tiles > 1:
        @pl.when(i + 1 < n_tiles)
        def _():
            _gather_start(pos_next_ref, yb_hbm, ybuf, 1 - slot, sem, 2 * tm)

    _gather_wait(yb_hbm, ybuf, slot, sem, 2 * tm)
    route = route_ref[...]
    y = x_ref[...] + (route[:, 2:3] * ybuf[slot, :tm, :] + route[:, 3:4] * ybuf[slot, tm:, :])
    o_ref[...] = _rms(y, g_ref[...])


def _combine(x2, route, pos, yb, g_final, tm):
    n = x2.shape[0]
    n_tiles = n // tm
    pos_spec = lambda f: pl.BlockSpec((1, 1, 2 * tm), f, memory_space=pltpu.SMEM)
    return pl.pallas_call(
        functools.partial(_combine_kernel, n_tiles=n_tiles),
        grid=(n_tiles,),
        in_specs=[
            pos_spec(lambda i: (i, 0, 0)),
            pos_spec(lambda i: (jnp.minimum(i + 1, n_tiles - 1), 0, 0)),
            pl.BlockSpec((tm, D_MODEL), lambda i: (i, 0)),
            pl.BlockSpec((tm, LANES), lambda i: (i, 0)),
            pl.BlockSpec(memory_space=pl.ANY),
            _const_spec((1, D_MODEL)),
        ],
        out_specs=pl.BlockSpec((tm, D_MODEL), lambda i: (i, 0)),
        out_shape=jax.ShapeDtypeStruct((n, D_MODEL), F32),
        scratch_shapes=[pltpu.VMEM((2, 2 * tm, D_MODEL), F32), pltpu.SemaphoreType.DMA((2,))],
        compiler_params=_params(1),
        name="combine",
    )(pos, pos, x2, route, yb, g_final.reshape(1, -1))


def _dispatch(eid):
    n_tok = eid.shape[0]
    n_rows = n_tok * TOP_K
    n_blocks = n_rows // MOE_ROWS + N_EXPERTS
    flat_e = eid.reshape(-1)
    onehot = (flat_e[:, None] == jnp.arange(N_EXPERTS, dtype=jnp.int32)[None, :]).astype(jnp.int32)
    csum = jnp.cumsum(onehot, axis=0)
    counts = csum[-1]
    rank = jnp.sum((csum - onehot) * onehot, axis=1)
    padded = (counts + MOE_ROWS - 1) // MOE_ROWS * MOE_ROWS
    pad_end = jnp.cumsum(padded)
    pad_start = pad_end - padded
    dest = pad_start[flat_e] + rank
    flat_tok = jnp.arange(n_rows, dtype=jnp.int32) // TOP_K
    row_tok = jnp.zeros((n_blocks * MOE_ROWS,), jnp.int32).at[dest].set(flat_tok)
    block_start = jnp.arange(n_blocks, dtype=jnp.int32) * MOE_ROWS
    block_e = jnp.minimum(jnp.sum((pad_end[None, :] <= block_start[:, None]).astype(jnp.int32), axis=1), N_EXPERTS - 1)
    n_used = (pad_end[-1] // MOE_ROWS).astype(jnp.int32)
    return row_tok, block_e, n_used, dest.reshape(n_tok, TOP_K).astype(jnp.int32)


def _tile_pos(pos, tm):
    n = pos.shape[0]
    return pos.reshape(n // tm, tm, TOP_K).transpose(0, 2, 1).reshape(n // tm, 1, TOP_K * tm)


def _deinterleave(t, dil):
    bsz, s_len, width = t.shape
    return t.reshape(bsz, s_len // dil, dil, width).transpose(0, 2, 1, 3).reshape(bsz * dil, s_len // dil, width)


def _interleave(t, bsz, dil):
    _, n_cls, width = t.shape
    return t.reshape(bsz, dil, n_cls, width).transpose(0, 2, 1, 3).reshape(bsz, n_cls * dil, width)


def kernel(x_prompt, x_sample, state_rwkv, state_shift, cache_swa_w128, cache_swa_w512, cache_swa_w2048,
           cache_mem_kv, mem_prompt, norm_mix_g, w_in, b_gate, rwkv_mu, rwkv_w0, rwkv_w2, rwkv_a0, rwkv_a2,
           rwkv_g2, rwkv_k_k, rwkv_k_a, rwkv_r_k, rwkv_lnx_g, rwkv_lnx_b, w_branch_a, w_branch_b, w_out,
           norm_mem_g, norm_memkv_g, w_mem_q, w_mem_kv, w_mem_out, norm_ffn_g, w_router_group, b_router_group,
           w_router_expert, b_router_expert, w_exp_gate, w_exp_up, w_exp_down, norm_final_g):
    bsz, s_len, _ = x_prompt.shape
    dbs, t_len, _ = x_sample.shape
    n_p, n_s = bsz * s_len, dbs * t_len
    rw = dict(rwkv_mu=rwkv_mu, rwkv_w0=rwkv_w0, rwkv_w2=rwkv_w2, rwkv_a0=rwkv_a0, rwkv_a2=rwkv_a2, rwkv_g2=rwkv_g2,
              rwkv_k_k=rwkv_k_k, rwkv_k_a=rwkv_k_a, rwkv_r_k=rwkv_r_k, rwkv_lnx_g=rwkv_lnx_g, rwkv_lnx_b=rwkv_lnx_b)
    pad = LANES - N_GROUPS - N_EXPERTS
    mid_w_s = dict(
        wa=w_branch_a, wb=w_branch_b, wo=w_out, gm=norm_mem_g.reshape(1, -1), wq=w_mem_q, wmo=w_mem_out,
        gf=norm_ffn_g.reshape(1, -1),
        wr=jnp.concatenate([w_router_group, w_router_expert, jnp.zeros((D_MODEL, pad), F32)], axis=1),
        br=jnp.concatenate([b_router_group, b_router_expert, jnp.zeros((pad,), F32)]).reshape(1, -1))
    mid_w_p = {k: (v.astype(BF16) if k.startswith('w') else v) for k, v in mid_w_s.items()}
    w_in_b = w_in.astype(BF16)

    xp = x_prompt.reshape(n_p, D_MODEL)
    tm = 256
    ua_p, qkv_p, gates_p = _in_proj(xp, norm_mix_g, w_in_b, b_gate, _rope_tables(jnp.arange(s_len)), tm)
    s0 = jnp.zeros((bsz, A_HEADS, A_HEAD_DIM, A_HEAD_DIM), F32)
    ya_p, st_p, shift_p = _rwkv(ua_p.reshape(bsz, s_len, A_PROJ), jnp.zeros((bsz, A_PROJ), F32), s0, rw,
                                RWKV_CHUNK, s_len)
    qkv3 = qkv_p.reshape(bsz, s_len, 3, len(B_GROUPS), B_GROUP_WIDTH)
    outs, lses, p_bufs = [], [], []
    for gi, (window, dil) in enumerate(B_GROUPS):
        q, k, v = (_deinterleave(qkv3[:, :, j, gi], dil) for j in range(3))
        n_cls = s_len // dil
        o, lse = _swa_prompt(q, k, v, max(1, 2048 // n_cls))
        outs.append(_interleave(o, bsz, dil))
        lses.append(_interleave(lse, bsz, dil))
        keep = min(window, s_len)
        p_bufs.append(jnp.stack([qkv3[:, s_len - keep:, 1, gi], qkv3[:, s_len - keep:, 2, gi]], axis=1)
                      .reshape(bsz, 2, keep, B_HEADS_PER_GROUP, B_HEAD_DIM))
    ob_p = jnp.concatenate(outs, axis=-1).reshape(n_p, B_WIDTH)
    lse_p = jnp.concatenate(lses, axis=-1).reshape(n_p, B_WIDTH)

    memkv = _norm_matmul(mem_prompt.reshape(bsz * N_MEM, D_MODEL), norm_memkv_g, w_mem_kv.astype(BF16), 256)
    memkv3 = memkv.reshape(bsz, N_MEM, 2 * MEM_WIDTH)
    mem_kv_prompt = memkv3.reshape(bsz, N_MEM, 2, MEM_HEADS, MEM_HEAD_DIM).transpose(0, 2, 1, 3, 4)
    tiles_per_batch = s_len // tm
    xn_all = jnp.zeros((n_p + n_s, D_MODEL), F32)
    x2_p, xn_all, route_p = _mid(
        xp, ya_p.reshape(n_p, A_WIDTH), ob_p, lse_p, gates_p, memkv3, memkv3,
        lambda i: (i // tiles_per_batch, 0, 0), lambda i: (i // tiles_per_batch, 0, 1),
        1, s_len, mid_w_p, tm, xn_all, 0)

    xs = x_sample.reshape(n_s, D_MODEL)
    pos_s = PAST_LEN + (jnp.arange(n_s) % t_len)
    ua_s, qkv_s, gates_s = _in_proj(xs, norm_mix_g, w_in, b_gate, _rope_tables(pos_s), n_s)
    t_pad = 8
    ua_s3 = jnp.pad(ua_s.reshape(dbs, t_len, A_PROJ), ((0, 0), (0, t_pad - t_len), (0, 0)))
    ya_s, st_s, shift_s = _rwkv(ua_s3, state_shift, state_rwkv, rw, 0, t_len)
    ya_s = ya_s[:, :t_len].reshape(n_s, A_WIDTH)
    caches = [c.reshape(c.shape[0], 2, c.shape[2], B_GROUP_WIDTH) for c in (cache_swa_w128, cache_swa_w512, cache_swa_w2048)]
    ob_s, lse_s, nb0, nb1, nb2 = _swa_sample(qkv_s.reshape(dbs, t_len, B_PROJ), caches)
    s_bufs = [nb.reshape(nb.shape[0], 2, nb.shape[2], B_HEADS_PER_GROUP, B_HEAD_DIM) for nb in (nb0, nb1, nb2)]
    mem_s = cache_mem_kv.reshape(dbs, 2 * N_MEM, MEM_WIDTH)
    tm_s = 32
    x2_s, xn_all, route_s = _mid(
        xs, ya_s, ob_s.reshape(n_s, B_WIDTH), lse_s.reshape(n_s, B_WIDTH), gates_s, mem_s, mem_s,
        lambda i: (i, 0, 0), lambda i: (i, 1, 0), tm_s // t_len, t_len, mid_w_s, tm_s, xn_all, n_p)

    route = jnp.concatenate([route_p, route_s], axis=0)
    eid = route[:, :TOP_K].astype(jnp.int32)
    row_tok, block_e, n_used, pos = _dispatch(eid)
    yb = _experts(xn_all, row_tok, block_e, n_used, w_exp_gate, w_exp_up, w_exp_down)
    tm_c = 128
    y_p = _combine(x2_p, route_p, _tile_pos(pos[:n_p], tm_c), yb, norm_final_g, tm_c)
    y_s = _combine(x2_s, route_s, _tile_pos(pos[n_p:], tm_c), yb, norm_final_g, tm_c)

    return (y_p.reshape(bsz, s_len, D_MODEL), y_s.reshape(dbs, t_len, D_MODEL),
            st_p, shift_p.reshape(bsz, A_PROJ), p_bufs[0], p_bufs[1], p_bufs[2], mem_kv_prompt,
            st_s, shift_s.reshape(dbs, A_PROJ), s_bufs[0], s_bufs[1], s_bufs[2])
```

```python
import functools
import math

import jax
import jax.numpy as jnp
import numpy as np
from jax import lax
from jax.experimental import pallas as pl
from jax.experimental.pallas import tpu as pltpu

F32 = jnp.float32
BF16 = jnp.bfloat16

D_MODEL = 1024
A_HEADS = 8
A_HEAD_DIM = 64
A_WIDTH = A_HEADS * A_HEAD_DIM
A_DECAY_LORA = 64
A_ICLR_LORA = 64
A_GATE_LORA = 128
A_PROJ = 3 * A_WIDTH + A_DECAY_LORA + A_ICLR_LORA + A_GATE_LORA
A_LNX_EPS = 64e-5
B_GROUPS = ((128, 1), (512, 4), (2048, 16))
B_HEADS_PER_GROUP = 4
B_HEAD_DIM = 64
B_GROUP_WIDTH = B_HEADS_PER_GROUP * B_HEAD_DIM
B_WIDTH = B_GROUP_WIDTH * len(B_GROUPS)
B_PROJ = 3 * B_WIDTH
ROPE_THETA = 500000.0
ROPE_DIM = B_HEAD_DIM // 4
ROPE_HALF = ROPE_DIM // 2
SWA_BLOCK = 128
N_MEM = 256
MEM_HEADS = 4
MEM_HEAD_DIM = 128
MEM_WIDTH = MEM_HEADS * MEM_HEAD_DIM
N_GROUPS = 4
EXPERTS_PER_GROUP = 8
N_EXPERTS = N_GROUPS * EXPERTS_PER_GROUP
TOP_K = 2
EXPERT_FF = 512
RMS_EPS = 1e-6
PAST_LEN = 8192

LANES = 128
SWA_SLABS = B_PROJ // LANES
SWA_PAIR = LANES // B_HEAD_DIM
SWA_PAIRS = B_GROUP_WIDTH // LANES
SWA_QUAD = 4
VMEM_LIMIT = 56 * 1024 * 1024
RWKV_CHUNK = 64
MOE_ROWS = 256
NEG_INF = float("-inf")


def _dot(a, b, precision=None):
    return jnp.dot(a, b, preferred_element_type=F32, precision=precision)


def _dot_nt(a, b, precision=None):
    return lax.dot_general(a, b, (((1,), (1,)), ((), ())), preferred_element_type=F32, precision=precision)


def _dot_tn(a, b, precision=None):
    return lax.dot_general(a, b, (((0,), (0,)), ((), ())), preferred_element_type=F32, precision=precision)


def _bf16_round(t):
    return t.astype(BF16).astype(F32)


def _operand(t, exact):
    return t.astype(F32) if exact else t.astype(BF16)


def _precision(exact):
    return lax.Precision.HIGHEST if exact else None


def _contract_round(t, exact):
    return t if exact else _bf16_round(t)


def _rms(x, g):
    return x * lax.rsqrt(jnp.mean(x * x, axis=-1, keepdims=True) + RMS_EPS) * g


def _const_spec(shape):
    nd = len(shape)
    return pl.BlockSpec(shape, lambda *_: (0,) * nd, pipeline_mode=pl.Buffered(1))


def _params(n_axes):
    return pltpu.CompilerParams(dimension_semantics=("arbitrary",) * n_axes, vmem_limit_bytes=VMEM_LIMIT)


def _slab_position(j):
    kind, hs = divmod(j, SWA_SLABS // 3)
    g, p = divmod(hs, SWA_PAIRS)
    return p * (3 * len(B_GROUPS)) + kind * len(B_GROUPS) + g


def _in_proj_kernel(x_ref, g_ref, w_ref, bg_ref, rc_ref, rs1_ref, rs2_ref, ua_ref, qkv_ref, gate_ref, *cache_refs, exact):
    tm = x_ref.shape[0]
    xn = _operand(_rms(x_ref[...], g_ref[...]), exact)
    mm = lambda w: _dot(xn, w, _precision(exact))
    ua_ref[...] = mm(w_ref[:, :A_PROJ])
    rc, rs1, rs2 = rc_ref[...], rs1_ref[...], rs2_ref[...]
    n_rot = 2 * B_WIDTH // LANES
    for j in range(SWA_SLABS):
        lo = A_PROJ + j * LANES
        s = mm(w_ref[:, lo:lo + LANES])
        if j < n_rot:
            s = s * rc + pltpu.roll(s, LANES - ROPE_HALF, 1) * rs1 + pltpu.roll(s, ROPE_HALF, 1) * rs2
        qkv_ref[_slab_position(j)] = s
        kind, hs = divmod(j, SWA_SLABS // 3)
        if cache_refs and kind > 0:
            g, p = divmod(hs, SWA_PAIRS)
            rows = cache_refs[g].shape[2]
            cache_refs[g][0, kind - 1, :, p * LANES:(p + 1) * LANES] = s[tm - rows:, :]
    gate_ref[...] = jax.nn.sigmoid(mm(w_ref[:, A_PROJ + B_PROJ:]) + bg_ref[...])


def _rope_tables(pos):
    inv_freq = ROPE_THETA ** (-jnp.arange(ROPE_HALF, dtype=F32) * 2.0 / ROPE_DIM)
    ang = pos.astype(F32)[:, None] * inv_freq[None, :]
    cos, sin = jnp.cos(ang), jnp.sin(ang)
    n = pos.shape[0]
    rest = B_HEAD_DIM - ROPE_DIM
    c = jnp.concatenate([cos, cos, jnp.ones((n, rest), F32)], axis=1)
    s1 = jnp.concatenate([-sin, jnp.zeros((n, ROPE_HALF + rest), F32)], axis=1)
    s2 = jnp.concatenate([jnp.zeros((n, ROPE_HALF), F32), sin, jnp.zeros((n, rest), F32)], axis=1)
    rep = LANES // B_HEAD_DIM
    return tuple(jnp.tile(t, (1, rep)) for t in (c, s1, s2))


def _in_proj(x, g, w, b_gate, tables, tm, cache_seq=None):
    n = x.shape[0]
    p_rows = tables[0].shape[0]
    t_tiles = p_rows // tm
    in_proj_w = w.shape[1]
    tab_spec = pl.BlockSpec((tm, LANES), lambda i: (i % t_tiles, 0))
    cache_specs, cache_shapes = [], []
    if cache_seq is not None:
        n_seq, s_len = cache_seq
        tiles = s_len // tm
        for window, _ in B_GROUPS:
            keep = min(window, s_len)
            rows = min(tm, keep)
            first_tile = tiles - keep // rows
            cache_specs.append(pl.BlockSpec(
                (1, 2, rows, B_GROUP_WIDTH),
                lambda i, tiles=tiles, first_tile=first_tile: (i // tiles, 0, jnp.maximum(i % tiles - first_tile, 0), 0)))
            cache_shapes.append(jax.ShapeDtypeStruct((n_seq, 2, keep, B_GROUP_WIDTH), F32))
    return pl.pallas_call(
        functools.partial(_in_proj_kernel, exact=w.dtype == F32),
        grid=(n // tm,),
        in_specs=[
            pl.BlockSpec((tm, D_MODEL), lambda i: (i, 0)),
            _const_spec((1, D_MODEL)),
            _const_spec((D_MODEL, in_proj_w)),
            _const_spec((1, 2 * D_MODEL)),
            tab_spec, tab_spec, tab_spec,
        ],
        out_specs=[
            pl.BlockSpec((tm, A_PROJ), lambda i: (i, 0)),
            pl.BlockSpec((SWA_SLABS, tm, LANES), lambda i: (0, i, 0)),
            pl.BlockSpec((tm, 2 * D_MODEL), lambda i: (i, 0)),
        ] + cache_specs,
        out_shape=[
            jax.ShapeDtypeStruct((n, A_PROJ), F32),
            jax.ShapeDtypeStruct((SWA_SLABS, n, LANES), F32),
            jax.ShapeDtypeStruct((n, 2 * D_MODEL), F32),
        ] + cache_shapes,
        compiler_params=_params(1),
        name="in_proj",
    )(x, g.reshape(1, -1), w, b_gate.reshape(1, -1), *tables)


def _norm_matmul_kernel(x_ref, g_ref, w_ref, o_ref):
    o_ref[...] = _dot(_rms(x_ref[...], g_ref[...]).astype(BF16), w_ref[...])


def _norm_matmul(x, g, w_bf16, tm):
    n, d = x.shape
    dout = w_bf16.shape[1]
    return pl.pallas_call(
        _norm_matmul_kernel,
        grid=(n // tm,),
        in_specs=[pl.BlockSpec((tm, d), lambda i: (i, 0)), _const_spec((1, d)), _const_spec((d, dout))],
        out_specs=pl.BlockSpec((tm, dout), lambda i: (i, 0)),
        out_shape=jax.ShapeDtypeStruct((n, dout), F32),
        compiler_params=_params(1),
        name="norm_matmul",
    )(x, g.reshape(1, -1), w_bf16)


RWKV_HEADS_PER_PACK = 4
RWKV_PACK_WIDTH = RWKV_HEADS_PER_PACK * A_HEAD_DIM
RWKV_PACKS = A_HEADS // RWKV_HEADS_PER_PACK


def _shift_rows(u, first_prev):
    row = lax.broadcasted_iota(jnp.int32, (u.shape[0], 1), 0)
    return jnp.where(row == 0, first_prev, pltpu.roll(u, 1, 0))


def _head_sum(x, bd, exact, pieces=1):
    if exact:
        return _dot(x, bd, lax.Precision.HIGHEST)
    ones = bd.astype(BF16)
    hi = x.astype(BF16)
    if pieces == 1:
        return _dot(hi, ones)
    return _dot(hi, ones) + _dot((x - hi.astype(F32)).astype(BF16), ones)


def _rwkv_features(u, u_prev, w_refs, bd, exact):
    mu_ref, w0_ref, w2_ref, a0_ref, a2_ref, g2_ref, kk_ref, ka_ref = w_refs
    um = u + (u_prev - u) * mu_ref[...]
    o1, o2, o3 = A_WIDTH, 2 * A_WIDTH, 3 * A_WIDTH
    o4 = o3 + A_DECAY_LORA
    o5 = o4 + A_ICLR_LORA
    r, k, v = um[:, :o1], um[:, o1:o2], um[:, o2:o3]
    xw, xa, xg = um[:, o3:o4], um[:, o4:o5], um[:, o5:]
    lora = lambda t, w_ref: _dot(_operand(t, exact), _operand(w_ref[...], exact), _precision(exact))
    w = -jax.nn.softplus(-(w0_ref[...] + lora(jnp.tanh(xw), w2_ref))) - 0.5
    e = jnp.exp(w)
    a = jax.nn.sigmoid(a0_ref[...] + lora(xa, a2_ref))
    g = lora(jax.nn.sigmoid(xg), g2_ref)
    kk = k * kk_ref[...]
    kkn = kk / jnp.maximum(jnp.sqrt(_head_sum(kk * kk, bd, exact, pieces=2)), 1e-12)
    k2 = k * (1.0 + (a - 1.0) * ka_ref[...])
    return r, k2, v, e, a, g, kkn


def _rwkv_output(y, r, k2, v, g, rk_ref, lng_ref, lnb_ref, bd, exact):
    inv_n = 1.0 / A_HEAD_DIM
    mean = _head_sum(y, bd, exact) * inv_n
    yc = y - mean
    var = _head_sum(yc * yc, bd, exact) * inv_n
    yn = yc * lax.rsqrt(var + A_LNX_EPS) * lng_ref[...] + lnb_ref[...]
    bonus = _head_sum(r * k2 * rk_ref[...], bd, exact) * v
    return (yn + bonus) * g


def _rwkv_chunk_kernel(u_ref, sh0_ref, s0_ref, mu_ref, w0_ref, w2_ref, a0_ref, a2_ref, g2_ref, kk_ref, ka_ref,
                       rk_ref, lng_ref, lnb_ref, bd_ref, y_ref, sfin_ref, shout_ref, st_scr, prev_scr, *, n_steps):
    c = pl.program_id(0)
    n_b, chunk, _ = u_ref.shape
    hd, hpp, pw_ = A_HEAD_DIM, RWKV_HEADS_PER_PACK, RWKV_PACK_WIDTH
    bf = lambda t: t.astype(BF16)

    @pl.when(c == 0)
    def _():
        for b in range(n_b):
            prev_scr[b:b + 1, :] = sh0_ref[b]
            for p in range(RWKV_PACKS):
                st_scr[b, p] = jnp.concatenate([s0_ref[b, p * hpp + h] for h in range(hpp)], axis=1)

    u_b = [u_ref[b] for b in range(n_b)]
    u = jnp.concatenate(u_b, axis=0)
    u_prev = jnp.concatenate([_shift_rows(u_b[b], prev_scr[b:b + 1, :]) for b in range(n_b)], axis=0)
    for b in range(n_b):
        prev_scr[b:b + 1, :] = u_b[b][chunk - 1:chunk, :]
    bd = bd_ref[...]
    r, k2, v, e, a, g, kkn = _rwkv_features(
        u, u_prev, (mu_ref, w0_ref, w2_ref, a0_ref, a2_ref, g2_ref, kk_ref, ka_ref), bd, False)
    b_ = kkn * a

    li = lax.broadcasted_iota(jnp.int32, (chunk, chunk), 0)
    lj = lax.broadcasted_iota(jnp.int32, (chunk, chunk), 1)
    tri = bf((li >= lj).astype(F32))
    e_hi = bf(e)
    e_rest = e - e_hi.astype(F32)
    e_mid = bf(e_rest)
    e_lo = bf(e_rest - e_mid.astype(F32))
    cums, ends = [], []
    for b in range(n_b):
        rs = slice(b * chunk, (b + 1) * chunk)
        cb = _dot(tri, e_hi[rs]) + (_dot(tri, e_mid[rs]) + _dot(tri, e_lo[rs]))
        cums.append(cb)
        ends.append(jnp.broadcast_to(cb[chunk - 1:chunk, :], (chunk, A_WIDTH)))
    cum = jnp.concatenate(cums, axis=0)
    cum_end = jnp.concatenate(ends, axis=0)
    grow = jnp.exp(cum)
    to_end = jnp.exp(cum - cum_end)
    at = bf(-kkn * jnp.exp(e - cum))
    rt = bf(r * jnp.exp(-cum))
    bt = bf(b_ * grow)
    kt = bf(k2 * grow)
    bh = bf(b_ * to_end)
    kh = bf(k2 * to_end)
    vb = bf(v)
    dec_end = jnp.exp(-cum_end)

    lane_head = lax.broadcasted_iota(jnp.int32, (1, pw_), 1) // hd
    head_mask = [lane_head == h for h in range(hpp)]

    def block_diag(x):
        return jnp.concatenate([jnp.where(head_mask[h], x, jnp.zeros_like(x)) for h in range(hpp)], axis=0)

    assert chunk == hd
    ti = lax.broadcasted_iota(jnp.int32, (chunk, pw_), 0)
    tj = lax.broadcasted_iota(jnp.int32, (chunk, pw_), 1) % chunk
    strict = ti > tj
    incl = ti >= tj
    eye = (ti == tj).astype(F32)
    n_sq = int(math.log2(chunk)) - 1

    streams = [(b, p, slice(b * chunk, (b + 1) * chunk), slice(p * pw_, (p + 1) * pw_))
               for b in range(n_b) for p in range(RWKV_PACKS)]
    at_s = [at[rs, cs] for _, _, rs, cs in streams]
    rt_s = [rt[rs, cs] for _, _, rs, cs in streams]
    v_s = [vb[rs, cs] for _, _, rs, cs in streams]
    m = [_dot_nt(jnp.concatenate([a_, r_], axis=0),
                 jnp.concatenate([block_diag(bt[rs, cs]), block_diag(kt[rs, cs])], axis=0))
         for a_, r_, (_, _, rs, cs) in zip(at_s, rt_s, streams)]
    a_ab = [jnp.where(strict, x[:chunk, :pw_], 0.0) for x in m]
    akv = [_dot(bf(jnp.where(strict, x[:chunk, pw_:], 0.0)), block_diag(vs)) for x, vs in zip(m, v_s)]
    m_r = [bf(jnp.concatenate([jnp.where(incl, x[chunk:, :pw_], 0.0), jnp.where(incl, x[chunk:, pw_:], 0.0)], axis=1))
           for x in m]
    tinv = [eye + x for x in a_ab]
    pw = a_ab
    for _ in range(n_sq):
        pw = [_dot(bf(x), block_diag(bf(x))) for x in pw]
        tinv = [t + _dot(bf(t), block_diag(bf(x))) for t, x in zip(tinv, pw)]
    w12 = [_dot(bf(t), jnp.concatenate([block_diag(a_), block_diag(bf(x))], axis=1))
           for t, a_, x in zip(tinv, at_s, akv)]

    st = [st_scr[b, p] for b, p, _, _ in streams]
    x = [_dot_nt(jnp.concatenate([bf(w[:, :pw_]), r_], axis=0), block_diag(bf(s))) for w, r_, s in zip(w12, rt_s, st)]
    uu = [bf(xi[:chunk] + w[:, pw_:]) for xi, w in zip(x, w12)]
    ys = [xi[chunk:] + _dot(mr, jnp.concatenate([block_diag(ui), block_diag(vs)], axis=0))
          for xi, mr, ui, vs in zip(x, m_r, uu, v_s)]
    for i, (b, p, rs, cs) in enumerate(streams):
        upd = _dot_tn(jnp.concatenate([uu[i], v_s[i]], axis=0), jnp.concatenate([bh[rs, cs], kh[rs, cs]], axis=0))
        diag = functools.reduce(lambda s_, t_: s_ + t_,
                                [jnp.where(head_mask[h], upd[h * hd:(h + 1) * hd, :], 0.0) for h in range(hpp)])
        st_scr[b, p] = st[i] * dec_end[rs.start:rs.start + 1, cs] + diag
    y = jnp.concatenate([jnp.concatenate(ys[b * RWKV_PACKS:(b + 1) * RWKV_PACKS], axis=1) for b in range(n_b)], axis=0)
    out = _rwkv_output(y, r, k2, v, g, rk_ref, lng_ref, lnb_ref, bd, False)
    for b in range(n_b):
        y_ref[b] = out[b * chunk:(b + 1) * chunk]

    @pl.when(c == n_steps - 1)
    def _():
        for b in range(n_b):
            shout_ref[b] = u_b[b][chunk - 1:chunk, :]
            for p in range(RWKV_PACKS):
                for h in range(hpp):
                    sfin_ref[b, p * hpp + h] = st_scr[b, p][:, h * hd:(h + 1) * hd]


def _rwkv_step_kernel(u_ref, sh0_ref, s0_ref, mu_ref, w0_ref, w2_ref, a0_ref, a2_ref, g2_ref, kk_ref, ka_ref,
                      rk_ref, lng_ref, lnb_ref, bd_ref, y_ref, sfin_ref, shout_ref, *, t_valid, exact):
    hd = A_HEAD_DIM
    u = u_ref[0]
    n_rows = u.shape[0]
    bd = bd_ref[...]
    r, k2, v, e, a, g, kkn = _rwkv_features(
        u, _shift_rows(u, sh0_ref[0]), (mu_ref, w0_ref, w2_ref, a0_ref, a2_ref, g2_ref, kk_ref, ka_ref), bd, exact)
    rnd = lambda t: _contract_round(t, exact)
    decay = jnp.exp(-e)
    b = kkn * a
    eye = (lax.broadcasted_iota(jnp.int32, (hd, hd), 0) == lax.broadcasted_iota(jnp.int32, (hd, hd), 1)).astype(F32)
    to_col = lambda t: jnp.sum(eye * t, axis=1, keepdims=True)
    to_row = lambda t: jnp.sum(eye * t, axis=0, keepdims=True)
    y_heads = []
    for h in range(A_HEADS):
        sl = slice(h * hd, (h + 1) * hd)
        s = s0_ref[0, h]
        y_rows = []
        for t in range(t_valid):
            tt = slice(t, t + 1)
            sa = jnp.sum(rnd(s) * rnd(-kkn[tt, sl]), axis=1, keepdims=True)
            s = s * decay[tt, sl] + sa * b[tt, sl] + to_col(v[tt, sl]) * k2[tt, sl]
            y_rows.append(to_row(jnp.sum(rnd(s) * rnd(r[tt, sl]), axis=1, keepdims=True)))
        sfin_ref[0, h] = s
        y_rows.append(jnp.zeros((n_rows - t_valid, hd), F32))
        y_heads.append(jnp.concatenate(y_rows, axis=0))
    y = jnp.concatenate(y_heads, axis=1)
    y_ref[0] = _rwkv_output(y, r, k2, v, g, rk_ref, lng_ref, lnb_ref, bd, exact)
    shout_ref[0] = u[t_valid - 1:t_valid, :]


def _rwkv(u_a, shift0, s0, p, chunk, t_valid):
    bsz, t_len, _ = u_a.shape
    hd = A_HEAD_DIM
    bd = jnp.asarray(np.kron(np.eye(A_HEADS, dtype=np.float32), np.ones((hd, hd), np.float32)))
    row = lambda t: t.reshape(1, -1)
    if chunk:
        assert t_valid == t_len and t_len % chunk == 0
        grid = (t_len // chunk,)
        kern = functools.partial(_rwkv_chunk_kernel, n_steps=grid[0])
        scratch = [pltpu.VMEM((bsz, RWKV_PACKS, hd, RWKV_PACK_WIDTH), F32), pltpu.VMEM((bsz, A_PROJ), F32)]
        u_spec = pl.BlockSpec((bsz, chunk, A_PROJ), lambda c: (0, c, 0))
        y_spec = pl.BlockSpec((bsz, chunk, A_WIDTH), lambda c: (0, c, 0))
        state_spec = pl.BlockSpec((bsz, A_HEADS, hd, hd), lambda c: (0, 0, 0, 0))
        shift_spec = pl.BlockSpec((bsz, 1, A_PROJ), lambda c: (0, 0, 0))
    else:
        grid = (bsz,)
        kern = functools.partial(_rwkv_step_kernel, t_valid=t_valid, exact=True)
        scratch = []
        u_spec = pl.BlockSpec((1, t_len, A_PROJ), lambda b: (b, 0, 0))
        y_spec = pl.BlockSpec((1, t_len, A_WIDTH), lambda b: (b, 0, 0))
        state_spec = pl.BlockSpec((1, A_HEADS, hd, hd), lambda b: (b, 0, 0, 0))
        shift_spec = pl.BlockSpec((1, 1, A_PROJ), lambda b: (b, 0, 0))
    return pl.pallas_call(
        kern,
        grid=grid,
        in_specs=[
            u_spec, shift_spec, state_spec,
            _const_spec((1, A_PROJ)), _const_spec((1, A_WIDTH)), _const_spec((A_DECAY_LORA, A_WIDTH)),
            _const_spec((1, A_WIDTH)), _const_spec((A_ICLR_LORA, A_WIDTH)), _const_spec((A_GATE_LORA, A_WIDTH)),
            _const_spec((1, A_WIDTH)), _const_spec((1, A_WIDTH)), _const_spec((1, A_WIDTH)),
            _const_spec((1, A_WIDTH)), _const_spec((1, A_WIDTH)), _const_spec((A_WIDTH, A_WIDTH)),
        ],
        out_specs=[y_spec, state_spec, shift_spec],
        out_shape=[
            jax.ShapeDtypeStruct((bsz, t_len, A_WIDTH), F32),
            jax.ShapeDtypeStruct((bsz, A_HEADS, hd, hd), F32),
            jax.ShapeDtypeStruct((bsz, 1, A_PROJ), F32),
        ],
        scratch_shapes=scratch,
        compiler_params=_params(1),
        name="rwkv7",
    )(u_a, shift0.reshape(bsz, 1, A_PROJ), s0, row(p['rwkv_mu']), row(p['rwkv_w0']), p['rwkv_w2'],
      row(p['rwkv_a0']), p['rwkv_a2'], p['rwkv_g2'], row(p['rwkv_k_k']), row(p['rwkv_k_a']),
      row(p['rwkv_r_k']), row(p['rwkv_lnx_g']), row(p['rwkv_lnx_b']), bd)


def _swa_prompt_kernel(qkv_ref, o_ref, lse_ref):
    blk, hd, n_g = SWA_BLOCK, B_HEAD_DIM, len(B_GROUPS)
    s_len = qkv_ref.shape[1]
    scale = hd ** -0.5
    rows2 = SWA_PAIR * blk
    r_i = lax.broadcasted_iota(jnp.int32, (rows2, 1), 0)
    qi = r_i % blk
    own_lanes = (lax.broadcasted_iota(jnp.int32, (1, LANES), 1) // hd) == (r_i // blk)
    head0_lanes = own_lanes[:blk]
    ki2 = lax.broadcasted_iota(jnp.int32, (rows2, 2 * blk), 1)
    band2 = (qi + blk - ki2 >= 0) & (qi - ki2 <= 0)
    causal1 = lax.broadcasted_iota(jnp.int32, (rows2, blk), 1) <= qi

    def attend(g, specs):
        dil = B_GROUPS[g][1]

        def rows(kind, st):
            idx = pl.ds(st, blk, stride=dil) if dil > 1 else pl.ds(st, blk)
            return qkv_ref[kind * n_g + g, idx, :]

        qs, kbs, vbs, masks = [], [], [], []
        for st, prev, first in specs:
            q = rows(0, st)
            qs.append(jnp.where(own_lanes, jnp.concatenate([q] * SWA_PAIR, axis=0), 0.0).astype(BF16))
            if prev is None:
                kbs.append(rows(1, st).astype(BF16))
                vbs.append(rows(2, st).astype(BF16))
                masks.append(causal1)
            else:
                kbs.append(jnp.concatenate([rows(1, prev), rows(1, st)], axis=0).astype(BF16))
                vbs.append(jnp.concatenate([rows(2, prev), rows(2, st)], axis=0).astype(BF16))
                masks.append(band2 & (ki2 >= jnp.where(first, blk, 0)))
        scores = [_dot_nt(q, kb) * scale for q, kb in zip(qs, kbs)]
        probs, lses = [], []
        for sc, mk in zip(scores, masks):
            sc = jnp.where(mk, sc, NEG_INF)
            m = jnp.max(sc, axis=-1, keepdims=True)
            p = jnp.exp(sc - m)
            den = jnp.sum(p, axis=-1, keepdims=True)
            probs.append((p / den).astype(BF16))
            lses.append(m + jnp.log(den))
        outs = [_dot(p, vb) for p, vb in zip(probs, vbs)]
        for (st, _, _), o2, l2 in zip(specs, outs, lses):
            idx = pl.ds(st, blk, stride=dil) if dil > 1 else pl.ds(st, blk)
            o_ref[g, idx, :] = jnp.where(head0_lanes, o2[:blk], o2[blk:])
            lse_ref[g, idx, :] = jnp.where(head0_lanes, l2[:blk], l2[blk:])

    for g, (_, dil) in enumerate(B_GROUPS):
        n_blk = s_len // dil // blk
        n_quads = dil * n_blk // SWA_QUAD

        def quad(it, carry, g=g, dil=dil, n_blk=n_blk):
            specs = []
            for j in range(SWA_QUAD):
                if n_blk >= SWA_QUAD:
                    e = it * SWA_QUAD + j
                    r, n = e // n_blk, e % n_blk
                    specs.append((n * (blk * dil) + r, jnp.maximum(n - 1, 0) * (blk * dil) + r, n == 0))
                else:
                    r = it * (SWA_QUAD // n_blk) + j // n_blk
                    n = j % n_blk
                    specs.append((n * (blk * dil) + r, None if n == 0 else (n - 1) * (blk * dil) + r, False))
            attend(g, specs)
            return carry

        lax.fori_loop(0, n_quads, quad, 0)


def _swa_prompt(qkv_slabs, n_seq, s_len):
    n_g = len(B_GROUPS)
    n = n_seq * s_len
    out_spec = pl.BlockSpec((n_g, s_len, LANES), lambda b, p: (p, b, 0))
    shp = jax.ShapeDtypeStruct((SWA_PAIRS * n_g, n, LANES), F32)
    return pl.pallas_call(
        _swa_prompt_kernel,
        grid=(n_seq, SWA_PAIRS),
        in_specs=[pl.BlockSpec((3 * n_g, s_len, LANES), lambda b, p: (p, b, 0))],
        out_specs=[out_spec, out_spec],
        out_shape=[shp, shp],
        compiler_params=_params(2),
        name="swa_prompt",
    )(qkv_slabs)


def _swa_sample_kernel(qkv_ref, c0_ref, c1_ref, c2_ref, o_ref, lse_ref, n0_ref, n1_ref, n2_ref, *, t_len, exact):
    hd = B_HEAD_DIM
    nh = B_HEADS_PER_GROUP
    gw = B_GROUP_WIDTH
    rows = t_len * nh
    qkv = qkv_ref[0]
    scale = hd ** -0.5
    r_i = lax.broadcasted_iota(jnp.int32, (rows, gw), 0)
    l_i = lax.broadcasted_iota(jnp.int32, (rows, gw), 1)
    head_lanes = (l_i // hd) == (r_i % nh)
    t_of_row = lax.broadcasted_iota(jnp.int32, (rows, 1), 0) // nh
    outs, lses = [], []
    for gi, ((window, dil), c_ref, n_ref) in enumerate(zip(B_GROUPS, (c0_ref, c1_ref, c2_ref), (n0_ref, n1_ref, n2_ref))):
        buf_len = c_ref.shape[2]
        q = qkv[:, gi * gw:(gi + 1) * gw]
        k_new = qkv[:, B_WIDTH + gi * gw:B_WIDTH + (gi + 1) * gw]
        v_new = qkv[:, 2 * B_WIDTH + gi * gw:2 * B_WIDTH + (gi + 1) * gw]
        kc = c_ref[0, 0]
        vc = c_ref[0, 1]
        qx = jnp.zeros((rows, gw), F32)
        for t in range(t_len):
            qx = jnp.where(head_lanes & (t_of_row == t), q[t:t + 1, :], qx)
        qx = _contract_round(qx, exact)
        k_new_r = _contract_round(k_new, exact)
        v_new_r = _contract_round(v_new, exact)
        s_c = _dot_nt(_operand(qx, exact), _operand(kc, exact), _precision(exact)) * scale
        t_c = lax.broadcasted_iota(jnp.int32, (rows, buf_len), 0) // nh
        j_c = lax.broadcasted_iota(jnp.int32, (rows, buf_len), 1)
        ok_c = (j_c >= t_c) & (((j_c - t_c) & (dil - 1)) == 0)
        s_c = jnp.where(ok_c, s_c, NEG_INF)
        s_n = []
        for i in range(t_len):
            s_i = jnp.sum(qx * k_new_r[i:i + 1, :], axis=-1, keepdims=True) * scale
            ok_i = (t_of_row >= i) & (((t_of_row - i) & (dil - 1)) == 0)
            s_n.append(jnp.where(ok_i, s_i, NEG_INF))
        m = functools.reduce(jnp.maximum, s_n, jnp.max(s_c, axis=-1, keepdims=True))
        p_c = jnp.exp(s_c - m)
        p_n = [jnp.exp(s_i - m) for s_i in s_n]
        den = functools.reduce(lambda a, b: a + b, p_n, jnp.sum(p_c, axis=-1, keepdims=True))
        o = _dot(_operand(p_c / den, exact), _operand(vc, exact), _precision(exact))
        for i in range(t_len):
            o = o + _contract_round(p_n[i] / den, exact) * v_new_r[i:i + 1, :]
        o = jnp.where(head_lanes, o, 0.0)
        lse = jnp.where(head_lanes, m + jnp.log(den), 0.0)
        outs.append(jnp.concatenate(
            [jnp.sum(o[t * nh:(t + 1) * nh, :], axis=0, keepdims=True) for t in range(t_len)], axis=0))
        lses.append(jnp.concatenate(
            [jnp.sum(lse[t * nh:(t + 1) * nh, :], axis=0, keepdims=True) for t in range(t_len)], axis=0))
        for j, new in ((0, k_new), (1, v_new)):
            n_ref[0, j, :buf_len - t_len, :] = c_ref[0, j, t_len:, :]
            n_ref[0, j, buf_len - t_len:, :] = new
    o_ref[0] = jnp.concatenate(outs, axis=1)
    lse_ref[0] = jnp.concatenate(lses, axis=1)


def _swa_sample(qkv, caches):
    bsz, t_len, _ = qkv.shape
    cache_specs = [pl.BlockSpec((1, 2) + c.shape[2:], lambda b: (b, 0, 0, 0)) for c in caches]
    row_spec = pl.BlockSpec((1, t_len, B_WIDTH), lambda b: (b, 0, 0))
    return pl.pallas_call(
        functools.partial(_swa_sample_kernel, t_len=t_len, exact=True),
        grid=(bsz,),
        in_specs=[pl.BlockSpec((1, t_len, B_PROJ), lambda b: (b, 0, 0))] + cache_specs,
        out_specs=[row_spec, row_spec] + cache_specs,
        out_shape=[jax.ShapeDtypeStruct((bsz, t_len, B_WIDTH), F32)] * 2
        + [jax.ShapeDtypeStruct(c.shape, F32) for c in caches],
        compiler_params=_params(1),
        name="swa_sample",
    )(qkv, *caches)


def _mid_kernel(x_ref, ya_ref, ob_ref, lse_ref, gate_ref, mk_ref, mv_ref, wa_ref, wb_ref, wo_ref, gm_ref,
                wq_ref, wmo_ref, gf_ref, wr_ref, br_ref, xn_all_ref, x2_ref, xn_ref, route_ref, *, rows_per_batch, exact):
    del xn_all_ref
    tm = x_ref.shape[0]
    n_g = len(B_GROUPS)
    rnd = lambda t: _contract_round(t, exact)
    op = lambda t: _operand(t, exact)
    prec = _precision(exact)
    yb_pairs = []
    for p in range(SWA_PAIRS):
        lses = [lse_ref[p * n_g + g] for g in range(n_g)]
        m = functools.reduce(jnp.maximum, lses)
        es = [jnp.exp(l - m) for l in lses]
        den = functools.reduce(lambda a, b: a + b, es)
        yb_pairs.append(functools.reduce(lambda a, b: a + b,
                                         [rnd(es[g] / den) * rnd(ob_ref[p * n_g + g]) for g in range(n_g)]))
    yb = jnp.concatenate(yb_pairs, axis=1)
    gates = gate_ref[...]
    merged = (gates[:, :D_MODEL] * _dot(op(ya_ref[...]), wa_ref[...], prec)
              + gates[:, D_MODEL:] * _dot(op(yb), wb_ref[...], prec))
    x1 = x_ref[...] + _dot(op(merged), wo_ref[...], prec)

    q = _dot(op(_rms(x1, gm_ref[...])), wq_ref[...], prec)
    n_b = mk_ref.shape[0]
    mk = op(mk_ref[...].reshape(n_b * N_MEM, MEM_WIDTH))
    mv = op(mv_ref[...].reshape(n_b * N_MEM, MEM_WIDTH))
    if n_b > 1:
        rb = lax.broadcasted_iota(jnp.int32, (tm, n_b * N_MEM), 0) // rows_per_batch
        cb = lax.broadcasted_iota(jnp.int32, (tm, n_b * N_MEM), 1) // N_MEM
        same = rb == cb
    heads = []
    for h in range(MEM_HEADS):
        sl = slice(h * MEM_HEAD_DIM, (h + 1) * MEM_HEAD_DIM)
        s = _dot_nt(op(q[:, sl]), mk[:, sl], prec) * (MEM_HEAD_DIM ** -0.5)
        if n_b > 1:
            s = jnp.where(same, s, NEG_INF)
        s = s - jnp.max(s, axis=-1, keepdims=True)
        p = jnp.exp(s)
        heads.append(_dot(op(p / jnp.sum(p, axis=-1, keepdims=True)), mv[:, sl], prec))
    x2 = x1 + _dot(op(jnp.concatenate(heads, axis=1)), wmo_ref[...], prec)
    x2_ref[...] = x2

    xn = _rms(x2, gf_ref[...])
    xn_ref[...] = xn
    logits = _dot(op(xn), wr_ref[...], prec) + br_ref[...]
    lane = lax.broadcasted_iota(jnp.int32, (tm, LANES), 1)
    gl = jnp.where(lane < N_GROUPS, logits, NEG_INF)
    gmax = jnp.max(gl, axis=-1, keepdims=True)
    grp = jnp.min(jnp.where(gl == gmax, lane, LANES), axis=-1, keepdims=True)
    w_grp = 1.0 / jnp.sum(jnp.exp(gl - gmax), axis=-1, keepdims=True)
    first = N_GROUPS + grp * EXPERTS_PER_GROUP
    el = jnp.where((lane >= first) & (lane < first + EXPERTS_PER_GROUP), logits, NEG_INF)
    m1 = jnp.max(el, axis=-1, keepdims=True)
    i1 = jnp.min(jnp.where(el == m1, lane, LANES), axis=-1, keepdims=True)
    el2 = jnp.where(lane == i1, NEG_INF, el)
    m2 = jnp.max(el2, axis=-1, keepdims=True)
    i2 = jnp.min(jnp.where(el2 == m2, lane, LANES), axis=-1, keepdims=True)
    e2 = jnp.exp(m2 - m1)
    g1 = w_grp / (1.0 + e2)
    g2 = w_grp * e2 / (1.0 + e2)
    route = jnp.where(lane == 0, (i1 - N_GROUPS).astype(F32), 0.0)
    route = jnp.where(lane == 1, (i2 - N_GROUPS).astype(F32), route)
    route = jnp.where(lane == 2, g1, route)
    route_ref[...] = jnp.where(lane == 3, g2, route)


def _mid(x, y_a, o_b, lse_b, gates, mem_k, mem_v, mk_map, mv_map, n_b, rows_per_batch, w, tm, xn_all, row0):
    n = x.shape[0]
    row = lambda width: pl.BlockSpec((tm, width), lambda i: (i, 0))
    slabs = pl.BlockSpec((SWA_PAIRS * len(B_GROUPS), tm, LANES), lambda i: (0, i, 0))
    in_specs = [
        row(D_MODEL), row(A_WIDTH), slabs, slabs, row(2 * D_MODEL),
        pl.BlockSpec((n_b, N_MEM, MEM_WIDTH), mk_map), pl.BlockSpec((n_b, N_MEM, MEM_WIDTH), mv_map),
        _const_spec((A_WIDTH, D_MODEL)), _const_spec((B_GROUP_WIDTH, D_MODEL)), _const_spec((D_MODEL, D_MODEL)),
        _const_spec((1, D_MODEL)), _const_spec((D_MODEL, MEM_WIDTH)), _const_spec((MEM_WIDTH, D_MODEL)),
        _const_spec((1, D_MODEL)), _const_spec((D_MODEL, LANES)), _const_spec((1, LANES)),
        pl.BlockSpec(memory_space=pl.ANY),
    ]
    args = [x, y_a, o_b, lse_b, gates, mem_k, mem_v, w['wa'], w['wb'], w['wo'], w['gm'], w['wq'], w['wmo'],
            w['gf'], w['wr'], w['br'], xn_all]
    blk0 = row0 // tm
    return pl.pallas_call(
        functools.partial(_mid_kernel, rows_per_batch=rows_per_batch, exact=w['wa'].dtype == F32),
        grid=(n // tm,),
        in_specs=in_specs,
        out_specs=[row(D_MODEL), pl.BlockSpec((tm, D_MODEL), lambda i: (i + blk0, 0)), row(LANES)],
        out_shape=[
            jax.ShapeDtypeStruct((n, D_MODEL), F32),
            jax.ShapeDtypeStruct(xn_all.shape, F32),
            jax.ShapeDtypeStruct((n, LANES), F32),
        ],
        input_output_aliases={len(args) - 1: 1},
        compiler_params=_params(1),
        name="mid",
    )(*args)


def _row_copy(src_hbm, idx, dst_buf, slot, j, sem):
    return pltpu.make_async_copy(src_hbm.at[pl.ds(idx, 1), :], dst_buf.at[slot, pl.ds(j, 1), :], sem.at[slot])


def _gather_start(idx_ref, src_hbm, dst_buf, slot, sem, n_rows):
    for j in range(n_rows):
        _row_copy(src_hbm, idx_ref[0, 0, j], dst_buf, slot, j, sem).start()


def _gather_wait(src_hbm, dst_buf, slot, sem, n_rows):
    for j in range(n_rows):
        _row_copy(src_hbm, 0, dst_buf, slot, j, sem).wait()


def _experts_kernel(meta_ref, be_ref, idx_ref, idx_next_ref, x_hbm, wg_ref, wu_ref, wd_ref, o_ref, xbuf, sem):
    i = pl.program_id(0)
    n_used = meta_ref[0]
    slot = i % 2

    @pl.when(i == 0)
    def _():
        _gather_start(idx_ref, x_hbm, xbuf, 0, sem, MOE_ROWS)

    @pl.when(i + 1 < n_used)
    def _():
        _gather_start(idx_next_ref, x_hbm, xbuf, 1 - slot, sem, MOE_ROWS)

    @pl.when(i < n_used)
    def _():
        _gather_wait(x_hbm, xbuf, slot, sem, MOE_ROWS)
        xb = xbuf[slot].astype(BF16)
        hg = _dot(xb, wg_ref[0].astype(BF16))
        hu = _dot(xb, wu_ref[0].astype(BF16))
        hh = (jax.nn.silu(hg) * hu).astype(BF16)
        o_ref[...] = _dot(hh, wd_ref[0].astype(BF16))

    @pl.when(i >= n_used)
    def _():
        o_ref[...] = jnp.zeros_like(o_ref)


def _experts(xn_all, row_tok, block_e, n_used, w_gate, w_up, w_down):
    n_blocks = block_e.shape[0]
    idx3 = row_tok.reshape(n_blocks, 1, MOE_ROWS)
    idx_spec = lambda f: pl.BlockSpec((1, 1, MOE_ROWS), f, memory_space=pltpu.SMEM)
    grid_spec = pltpu.PrefetchScalarGridSpec(
        num_scalar_prefetch=2,
        grid=(n_blocks,),
        in_specs=[
            idx_spec(lambda i, meta, be: (i, 0, 0)),
            idx_spec(lambda i, meta, be: (jnp.minimum(i + 1, n_blocks - 1), 0, 0)),
            pl.BlockSpec(memory_space=pl.ANY),
            pl.BlockSpec((1, D_MODEL, EXPERT_FF), lambda i, meta, be: (be[i], 0, 0)),
            pl.BlockSpec((1, D_MODEL, EXPERT_FF), lambda i, meta, be: (be[i], 0, 0)),
            pl.BlockSpec((1, EXPERT_FF, D_MODEL), lambda i, meta, be: (be[i], 0, 0)),
        ],
        out_specs=pl.BlockSpec((MOE_ROWS, D_MODEL), lambda i, meta, be: (i, 0)),
        scratch_shapes=[pltpu.VMEM((2, MOE_ROWS, D_MODEL), F32), pltpu.SemaphoreType.DMA((2,))],
    )
    return pl.pallas_call(
        _experts_kernel,
        grid_spec=grid_spec,
        out_shape=jax.ShapeDtypeStruct((n_blocks * MOE_ROWS, D_MODEL), F32),
        compiler_params=_params(1),
        name="experts",
    )(n_used.reshape(1), block_e, idx3, idx3, xn_all, w_gate, w_up, w_down)


def _combine_kernel(pos_ref, pos_next_ref, x_ref, route_ref, yb_hbm, g_ref, o_ref, ybuf, sem, *, n_tiles):
    i = pl.program_id(0)
    tm = x_ref.shape[0]
    slot = i % 2

    @pl.when(i == 0)
    def _():
        _gather_start(pos_ref, yb_hbm, ybuf, 0, sem, 2 * tm)

    if n_tiles > 1:
        @pl.when(i + 1 < n_tiles)
        def _():
            _gather_start(pos_next_ref, yb_hbm, ybuf, 1 - slot, sem, 2 * tm)

    _gather_wait(yb_hbm, ybuf, slot, sem, 2 * tm)
    route = route_ref[...]
    y = x_ref[...] + (route[:, 2:3] * ybuf[slot, :tm, :] + route[:, 3:4] * ybuf[slot, tm:, :])
    o_ref[...] = _rms(y, g_ref[...])


def _combine(x2, route, pos, yb, g_final, tm):
    n = x2.shape[0]
    n_tiles = n // tm
    pos_spec = lambda f: pl.BlockSpec((1, 1, 2 * tm), f, memory_space=pltpu.SMEM)
    return pl.pallas_call(
        functools.partial(_combine_kernel, n_tiles=n_tiles),
        grid=(n_tiles,),
        in_specs=[
            pos_spec(lambda i: (i, 0, 0)),
            pos_spec(lambda i: (jnp.minimum(i + 1, n_tiles - 1), 0, 0)),
            pl.BlockSpec((tm, D_MODEL), lambda i: (i, 0)),
            pl.BlockSpec((tm, LANES), lambda i: (i, 0)),
            pl.BlockSpec(memory_space=pl.ANY),
            _const_spec((1, D_MODEL)),
        ],
        out_specs=pl.BlockSpec((tm, D_MODEL), lambda i: (i, 0)),
        out_shape=jax.ShapeDtypeStruct((n, D_MODEL), F32),
        scratch_shapes=[pltpu.VMEM((2, 2 * tm, D_MODEL), F32), pltpu.SemaphoreType.DMA((2,))],
        compiler_params=_params(1),
        name="combine",
    )(pos, pos, x2, route, yb, g_final.reshape(1, -1))


def _dispatch(eid):
    n_tok = eid.shape[0]
    n_rows = n_tok * TOP_K
    n_blocks = n_rows // MOE_ROWS + N_EXPERTS
    flat_e = eid.reshape(-1)
    onehot = (flat_e[:, None] == jnp.arange(N_EXPERTS, dtype=jnp.int32)[None, :]).astype(jnp.int32)
    csum = jnp.cumsum(onehot, axis=0)
    counts = csum[-1]
    rank = jnp.sum((csum - onehot) * onehot, axis=1)
    padded = (counts + MOE_ROWS - 1) // MOE_ROWS * MOE_ROWS
    pad_end = jnp.cumsum(padded)
    pad_start = pad_end - padded
    dest = pad_start[flat_e] + rank
    flat_tok = jnp.arange(n_rows, dtype=jnp.int32) // TOP_K
    row_tok = jnp.zeros((n_blocks * MOE_ROWS,), jnp.int32).at[dest].set(flat_tok)
    block_start = jnp.arange(n_blocks, dtype=jnp.int32) * MOE_ROWS
    block_e = jnp.minimum(jnp.sum((pad_end[None, :] <= block_start[:, None]).astype(jnp.int32), axis=1), N_EXPERTS - 1)
    n_used = (pad_end[-1] // MOE_ROWS).astype(jnp.int32)
    return row_tok, block_e, n_used, dest.reshape(n_tok, TOP_K).astype(jnp.int32)


def _tile_pos(pos, tm):
    n = pos.shape[0]
    return pos.reshape(n // tm, tm, TOP_K).transpose(0, 2, 1).reshape(n // tm, 1, TOP_K * tm)


def _slabs_to_rows(qkv_slabs):
    order = np.array([_slab_position(j) for j in range(SWA_SLABS)])
    return jnp.transpose(qkv_slabs[order], (1, 0, 2)).reshape(qkv_slabs.shape[1], B_PROJ)


def _rows_to_pair_slabs(t):
    n = t.shape[0]
    return jnp.transpose(t.reshape(n, len(B_GROUPS), SWA_PAIRS, LANES), (2, 1, 0, 3)).reshape(-1, n, LANES)


def kernel(x_prompt, x_sample, state_rwkv, state_shift, cache_swa_w128, cache_swa_w512, cache_swa_w2048,
           cache_mem_kv, mem_prompt, norm_mix_g, w_in, b_gate, rwkv_mu, rwkv_w0, rwkv_w2, rwkv_a0, rwkv_a2,
           rwkv_g2, rwkv_k_k, rwkv_k_a, rwkv_r_k, rwkv_lnx_g, rwkv_lnx_b, w_branch_a, w_branch_b, w_out,
           norm_mem_g, norm_memkv_g, w_mem_q, w_mem_kv, w_mem_out, norm_ffn_g, w_router_group, b_router_group,
           w_router_expert, b_router_expert, w_exp_gate, w_exp_up, w_exp_down, norm_final_g):
    bsz, s_len, _ = x_prompt.shape
    dbs, t_len, _ = x_sample.shape
    n_p, n_s = bsz * s_len, dbs * t_len
    rw = dict(rwkv_mu=rwkv_mu, rwkv_w0=rwkv_w0, rwkv_w2=rwkv_w2, rwkv_a0=rwkv_a0, rwkv_a2=rwkv_a2, rwkv_g2=rwkv_g2,
              rwkv_k_k=rwkv_k_k, rwkv_k_a=rwkv_k_a, rwkv_r_k=rwkv_r_k, rwkv_lnx_g=rwkv_lnx_g, rwkv_lnx_b=rwkv_lnx_b)
    pad = LANES - N_GROUPS - N_EXPERTS
    mid_w_s = dict(
        wa=w_branch_a, wb=w_branch_b, wo=w_out, gm=norm_mem_g.reshape(1, -1), wq=w_mem_q, wmo=w_mem_out,
        gf=norm_ffn_g.reshape(1, -1),
        wr=jnp.concatenate([w_router_group, w_router_expert, jnp.zeros((D_MODEL, pad), F32)], axis=1),
        br=jnp.concatenate([b_router_group, b_router_expert, jnp.zeros((pad,), F32)]).reshape(1, -1))
    mid_w_p = {k: (v.astype(BF16) if k.startswith('w') else v) for k, v in mid_w_s.items()}
    w_in_b = w_in.astype(BF16)

    xp = x_prompt.reshape(n_p, D_MODEL)
    tm = 256
    ua_p, qkv_p, gates_p, *p_caches = _in_proj(xp, norm_mix_g, w_in_b, b_gate, _rope_tables(jnp.arange(s_len)), tm,
                                               cache_seq=(bsz, s_len))
    s0 = jnp.zeros((bsz, A_HEADS, A_HEAD_DIM, A_HEAD_DIM), F32)
    ya_p, st_p, shift_p = _rwkv(ua_p.reshape(bsz, s_len, A_PROJ), jnp.zeros((bsz, A_PROJ), F32), s0, rw,
                                RWKV_CHUNK, s_len)
    ob_p, lse_p = _swa_prompt(qkv_p, bsz, s_len)
    p_bufs = [c.reshape(c.shape[:3] + (B_HEADS_PER_GROUP, B_HEAD_DIM)) for c in p_caches]

    memkv = _norm_matmul(mem_prompt.reshape(bsz * N_MEM, D_MODEL), norm_memkv_g, w_mem_kv.astype(BF16), 256)
    memkv3 = memkv.reshape(bsz, N_MEM, 2 * MEM_WIDTH)
    mem_kv_prompt = memkv3.reshape(bsz, N_MEM, 2, MEM_HEADS, MEM_HEAD_DIM).transpose(0, 2, 1, 3, 4)
    tiles_per_batch = s_len // tm
    xn_all = jnp.zeros((n_p + n_s, D_MODEL), F32)
    x2_p, xn_all, route_p = _mid(
        xp, ya_p.reshape(n_p, A_WIDTH), ob_p, lse_p, gates_p, memkv3, memkv3,
        lambda i: (i // tiles_per_batch, 0, 0), lambda i: (i // tiles_per_batch, 0, 1),
        1, s_len, mid_w_p, tm, xn_all, 0)

    xs = x_sample.reshape(n_s, D_MODEL)
    pos_s = PAST_LEN + (jnp.arange(n_s) % t_len)
    ua_s, qkv_s, gates_s = _in_proj(xs, norm_mix_g, w_in, b_gate, _rope_tables(pos_s), n_s)
    t_pad = 8
    ua_s3 = jnp.pad(ua_s.reshape(dbs, t_len, A_PROJ), ((0, 0), (0, t_pad - t_len), (0, 0)))
    ya_s, st_s, shift_s = _rwkv(ua_s3, state_shift, state_rwkv, rw, 0, t_len)
    ya_s = ya_s[:, :t_len].reshape(n_s, A_WIDTH)
    caches = [c.reshape(c.shape[0], 2, c.shape[2], B_GROUP_WIDTH) for c in (cache_swa_w128, cache_swa_w512, cache_swa_w2048)]
    ob_s, lse_s, nb0, nb1, nb2 = _swa_sample(_slabs_to_rows(qkv_s).reshape(dbs, t_len, B_PROJ), caches)
    s_bufs = [nb.reshape(nb.shape[0], 2, nb.shape[2], B_HEADS_PER_GROUP, B_HEAD_DIM) for nb in (nb0, nb1, nb2)]
    mem_s = cache_mem_kv.reshape(dbs, 2 * N_MEM, MEM_WIDTH)
    tm_s = 32
    x2_s, xn_all, route_s = _mid(
        xs, ya_s, _rows_to_pair_slabs(ob_s.reshape(n_s, B_WIDTH)), _rows_to_pair_slabs(lse_s.reshape(n_s, B_WIDTH)),
        gates_s, mem_s, mem_s,
        lambda i: (i, 0, 0), lambda i: (i, 1, 0), tm_s // t_len, t_len, mid_w_s, tm_s, xn_all, n_p)

    route = jnp.concatenate([route_p, route_s], axis=0)
    eid = route[:, :TOP_K].astype(jnp.int32)
    row_tok, block_e, n_used, pos = _dispatch(eid)
    yb = _experts(xn_all, row_tok, block_e, n_used, w_exp_gate, w_exp_up, w_exp_down)
    tm_c = 128
    y_p = _combine(x2_p, route_p, _tile_pos(pos[:n_p], tm_c), yb, norm_final_g, tm_c)
    y_s = _combine(x2_s, route_s, _tile_pos(pos[n_p:], tm_c), yb, norm_final_g, tm_c)

    return (y_p.reshape(bsz, s_len, D_MODEL), y_s.reshape(dbs, t_len, D_MODEL),
            st_p, shift_p.reshape(bsz, A_PROJ), p_bufs[0], p_bufs[1], p_bufs[2], mem_kv_prompt,
            st_s, shift_s.reshape(dbs, A_PROJ), s_bufs[0], s_bufs[1], s_bufs[2])
```

```python
import functools
import math

import jax
import jax.numpy as jnp
import numpy as np
from jax import lax
from jax.experimental import pallas as pl
from jax.experimental.pallas import tpu as pltpu

F32 = jnp.float32
BF16 = jnp.bfloat16

D_MODEL = 1024
A_HEADS = 8
A_HEAD_DIM = 64
A_WIDTH = A_HEADS * A_HEAD_DIM
A_DECAY_LORA = 64
A_ICLR_LORA = 64
A_GATE_LORA = 128
A_PROJ = 3 * A_WIDTH + A_DECAY_LORA + A_ICLR_LORA + A_GATE_LORA
A_LNX_EPS = 64e-5
B_GROUPS = ((128, 1), (512, 4), (2048, 16))
B_HEADS_PER_GROUP = 4
B_HEAD_DIM = 64
B_GROUP_WIDTH = B_HEADS_PER_GROUP * B_HEAD_DIM
B_WIDTH = B_GROUP_WIDTH * len(B_GROUPS)
B_PROJ = 3 * B_WIDTH
ROPE_THETA = 500000.0
ROPE_DIM = B_HEAD_DIM // 4
ROPE_HALF = ROPE_DIM // 2
SWA_BLOCK = 128
N_MEM = 256
MEM_HEADS = 4
MEM_HEAD_DIM = 128
MEM_WIDTH = MEM_HEADS * MEM_HEAD_DIM
N_GROUPS = 4
EXPERTS_PER_GROUP = 8
N_EXPERTS = N_GROUPS * EXPERTS_PER_GROUP
TOP_K = 2
EXPERT_FF = 512
RMS_EPS = 1e-6
PAST_LEN = 8192

LANES = 128
SWA_SLABS = B_PROJ // LANES
SWA_PAIR = LANES // B_HEAD_DIM
SWA_PAIRS = B_GROUP_WIDTH // LANES
SWA_QUAD = 4
VMEM_LIMIT = 56 * 1024 * 1024
RWKV_CHUNK = 64
MOE_ROWS = 256
NEG_INF = float("-inf")


def _dot(a, b, precision=None):
    return jnp.dot(a, b, preferred_element_type=F32, precision=precision)


def _dot_nt(a, b, precision=None):
    return lax.dot_general(a, b, (((1,), (1,)), ((), ())), preferred_element_type=F32, precision=precision)


def _dot_tn(a, b, precision=None):
    return lax.dot_general(a, b, (((0,), (0,)), ((), ())), preferred_element_type=F32, precision=precision)


def _bf16_round(t):
    return t.astype(BF16).astype(F32)


_NN = (((1,), (0,)), ((), ()))
_NT = (((1,), (1,)), ((), ()))


def _mm(a, b, exact, dims=_NN):
    dg = lambda x, y: lax.dot_general(x, y, dims, preferred_element_type=F32)
    if not exact:
        return dg(a.astype(BF16), b.astype(BF16))
    a, b = a.astype(F32), b.astype(F32)
    a_hi, b_hi = a.astype(BF16), b.astype(BF16)
    a_lo = (a - a_hi.astype(F32)).astype(BF16)
    b_lo = (b - b_hi.astype(F32)).astype(BF16)
    return dg(a_hi, b_hi) + (dg(a_hi, b_lo) + dg(a_lo, b_hi))


def _contract_round(t, exact):
    return t if exact else _bf16_round(t)


def _rms(x, g):
    return x * lax.rsqrt(jnp.mean(x * x, axis=-1, keepdims=True) + RMS_EPS) * g


def _const_spec(shape):
    nd = len(shape)
    return pl.BlockSpec(shape, lambda *_: (0,) * nd, pipeline_mode=pl.Buffered(1))


def _params(n_axes):
    return pltpu.CompilerParams(dimension_semantics=("arbitrary",) * n_axes, vmem_limit_bytes=VMEM_LIMIT)


def _slab_position(j):
    kind, hs = divmod(j, SWA_SLABS // 3)
    g, p = divmod(hs, SWA_PAIRS)
    return p * (3 * len(B_GROUPS)) + kind * len(B_GROUPS) + g


def _in_proj_kernel(x_ref, g_ref, w_ref, bg_ref, rc_ref, rs1_ref, rs2_ref, ua_ref, qkv_ref, gate_ref, *cache_refs, exact):
    tm = x_ref.shape[0]
    xn = _rms(x_ref[...], g_ref[...])
    xn = xn if exact else xn.astype(BF16)
    mm = lambda w: _mm(xn, w, exact)
    ua_ref[...] = mm(w_ref[:, :A_PROJ])
    rc, rs1, rs2 = rc_ref[...], rs1_ref[...], rs2_ref[...]
    n_rot = 2 * B_WIDTH // LANES
    for j in range(SWA_SLABS):
        lo = A_PROJ + j * LANES
        s = mm(w_ref[:, lo:lo + LANES])
        if j < n_rot:
            s = s * rc + pltpu.roll(s, LANES - ROPE_HALF, 1) * rs1 + pltpu.roll(s, ROPE_HALF, 1) * rs2
        qkv_ref[_slab_position(j)] = s
        kind, hs = divmod(j, SWA_SLABS // 3)
        if cache_refs and kind > 0:
            g, p = divmod(hs, SWA_PAIRS)
            rows = cache_refs[g].shape[2]
            cache_refs[g][0, kind - 1, :, p * LANES:(p + 1) * LANES] = s[tm - rows:, :]
    gate_ref[...] = jax.nn.sigmoid(mm(w_ref[:, A_PROJ + B_PROJ:]) + bg_ref[...])


def _rope_tables(pos):
    inv_freq = ROPE_THETA ** (-jnp.arange(ROPE_HALF, dtype=F32) * 2.0 / ROPE_DIM)
    ang = pos.astype(F32)[:, None] * inv_freq[None, :]
    cos, sin = jnp.cos(ang), jnp.sin(ang)
    n = pos.shape[0]
    rest = B_HEAD_DIM - ROPE_DIM
    c = jnp.concatenate([cos, cos, jnp.ones((n, rest), F32)], axis=1)
    s1 = jnp.concatenate([-sin, jnp.zeros((n, ROPE_HALF + rest), F32)], axis=1)
    s2 = jnp.concatenate([jnp.zeros((n, ROPE_HALF), F32), sin, jnp.zeros((n, rest), F32)], axis=1)
    rep = LANES // B_HEAD_DIM
    return tuple(jnp.tile(t, (1, rep)) for t in (c, s1, s2))


def _in_proj(x, g, w, b_gate, tables, tm, cache_seq=None):
    n = x.shape[0]
    p_rows = tables[0].shape[0]
    t_tiles = p_rows // tm
    in_proj_w = w.shape[1]
    tab_spec = pl.BlockSpec((tm, LANES), lambda i: (i % t_tiles, 0))
    cache_specs, cache_shapes = [], []
    if cache_seq is not None:
        n_seq, s_len = cache_seq
        tiles = s_len // tm
        for window, _ in B_GROUPS:
            keep = min(window, s_len)
            rows = min(tm, keep)
            first_tile = tiles - keep // rows
            cache_specs.append(pl.BlockSpec(
                (1, 2, rows, B_GROUP_WIDTH),
                lambda i, tiles=tiles, first_tile=first_tile: (i // tiles, 0, jnp.maximum(i % tiles - first_tile, 0), 0)))
            cache_shapes.append(jax.ShapeDtypeStruct((n_seq, 2, keep, B_GROUP_WIDTH), F32))
    return pl.pallas_call(
        functools.partial(_in_proj_kernel, exact=w.dtype == F32),
        grid=(n // tm,),
        in_specs=[
            pl.BlockSpec((tm, D_MODEL), lambda i: (i, 0)),
            _const_spec((1, D_MODEL)),
            _const_spec((D_MODEL, in_proj_w)),
            _const_spec((1, 2 * D_MODEL)),
            tab_spec, tab_spec, tab_spec,
        ],
        out_specs=[
            pl.BlockSpec((tm, A_PROJ), lambda i: (i, 0)),
            pl.BlockSpec((SWA_SLABS, tm, LANES), lambda i: (0, i, 0)),
            pl.BlockSpec((tm, 2 * D_MODEL), lambda i: (i, 0)),
        ] + cache_specs,
        out_shape=[
            jax.ShapeDtypeStruct((n, A_PROJ), F32),
            jax.ShapeDtypeStruct((SWA_SLABS, n, LANES), F32),
            jax.ShapeDtypeStruct((n, 2 * D_MODEL), F32),
        ] + cache_shapes,
        compiler_params=_params(1),
        name="in_proj",
    )(x, g.reshape(1, -1), w, b_gate.reshape(1, -1), *tables)


def _norm_matmul_kernel(x_ref, g_ref, w_ref, o_ref):
    o_ref[...] = _dot(_rms(x_ref[...], g_ref[...]).astype(BF16), w_ref[...])


def _norm_matmul(x, g, w_bf16, tm):
    n, d = x.shape
    dout = w_bf16.shape[1]
    return pl.pallas_call(
        _norm_matmul_kernel,
        grid=(n // tm,),
        in_specs=[pl.BlockSpec((tm, d), lambda i: (i, 0)), _const_spec((1, d)), _const_spec((d, dout))],
        out_specs=pl.BlockSpec((tm, dout), lambda i: (i, 0)),
        out_shape=jax.ShapeDtypeStruct((n, dout), F32),
        compiler_params=_params(1),
        name="norm_matmul",
    )(x, g.reshape(1, -1), w_bf16)


RWKV_HEADS_PER_PACK = 4
RWKV_PACK_WIDTH = RWKV_HEADS_PER_PACK * A_HEAD_DIM
RWKV_PACKS = A_HEADS // RWKV_HEADS_PER_PACK
RWKV_STEP_SEQS = 1


def _shift_rows(u, first_prev):
    row = lax.broadcasted_iota(jnp.int32, (u.shape[0], 1), 0)
    return jnp.where(row == 0, first_prev, pltpu.roll(u, 1, 0))


def _head_sum(x, bd, exact, pieces=1):
    ones = bd.astype(BF16)
    total, rest = None, x
    for _ in range(3 if exact else pieces):
        piece = rest.astype(BF16)
        rest = rest - piece.astype(F32)
        part = _dot(piece, ones)
        total = part if total is None else total + part
    return total


def _rwkv_features(u, u_prev, w_refs, bd, exact):
    mu_ref, w0_ref, w2_ref, a0_ref, a2_ref, g2_ref, kk_ref, ka_ref = w_refs
    um = u + (u_prev - u) * mu_ref[...]
    o1, o2, o3 = A_WIDTH, 2 * A_WIDTH, 3 * A_WIDTH
    o4 = o3 + A_DECAY_LORA
    o5 = o4 + A_ICLR_LORA
    r, k, v = um[:, :o1], um[:, o1:o2], um[:, o2:o3]
    xw, xa, xg = um[:, o3:o4], um[:, o4:o5], um[:, o5:]
    lora = lambda t, w_ref: _mm(t, w_ref[...], exact)
    w = -jax.nn.softplus(-(w0_ref[...] + lora(jnp.tanh(xw), w2_ref))) - 0.5
    e = jnp.exp(w)
    a = jax.nn.sigmoid(a0_ref[...] + lora(xa, a2_ref))
    g = lora(jax.nn.sigmoid(xg), g2_ref)
    kk = k * kk_ref[...]
    kkn = kk / jnp.maximum(jnp.sqrt(_head_sum(kk * kk, bd, exact, pieces=2)), 1e-12)
    k2 = k * (1.0 + (a - 1.0) * ka_ref[...])
    return r, k2, v, e, a, g, kkn


def _rwkv_output(y, r, k2, v, g, rk_ref, lng_ref, lnb_ref, bd, exact):
    inv_n = 1.0 / A_HEAD_DIM
    mean = _head_sum(y, bd, exact) * inv_n
    yc = y - mean
    var = _head_sum(yc * yc, bd, exact) * inv_n
    yn = yc * lax.rsqrt(var + A_LNX_EPS) * lng_ref[...] + lnb_ref[...]
    bonus = _head_sum(r * k2 * rk_ref[...], bd, exact) * v
    return (yn + bonus) * g


def _rwkv_chunk_kernel(u_ref, sh0_ref, s0_ref, mu_ref, w0_ref, w2_ref, a0_ref, a2_ref, g2_ref, kk_ref, ka_ref,
                       rk_ref, lng_ref, lnb_ref, bd_ref, y_ref, sfin_ref, shout_ref, st_scr, prev_scr, *, n_steps):
    c = pl.program_id(0)
    n_b, chunk, _ = u_ref.shape
    hd, hpp, pw_ = A_HEAD_DIM, RWKV_HEADS_PER_PACK, RWKV_PACK_WIDTH
    bf = lambda t: t.astype(BF16)

    @pl.when(c == 0)
    def _():
        for b in range(n_b):
            prev_scr[b:b + 1, :] = sh0_ref[b]
            for p in range(RWKV_PACKS):
                st_scr[b, p] = jnp.concatenate([s0_ref[b, p * hpp + h] for h in range(hpp)], axis=1)

    u_b = [u_ref[b] for b in range(n_b)]
    u = jnp.concatenate(u_b, axis=0)
    u_prev = jnp.concatenate([_shift_rows(u_b[b], prev_scr[b:b + 1, :]) for b in range(n_b)], axis=0)
    for b in range(n_b):
        prev_scr[b:b + 1, :] = u_b[b][chunk - 1:chunk, :]
    bd = bd_ref[...]
    r, k2, v, e, a, g, kkn = _rwkv_features(
        u, u_prev, (mu_ref, w0_ref, w2_ref, a0_ref, a2_ref, g2_ref, kk_ref, ka_ref), bd, False)
    b_ = kkn * a

    li = lax.broadcasted_iota(jnp.int32, (chunk, chunk), 0)
    lj = lax.broadcasted_iota(jnp.int32, (chunk, chunk), 1)
    tri = bf((li >= lj).astype(F32))
    e_hi = bf(e)
    e_rest = e - e_hi.astype(F32)
    e_mid = bf(e_rest)
    e_lo = bf(e_rest - e_mid.astype(F32))
    cums, ends = [], []
    for b in range(n_b):
        rs = slice(b * chunk, (b + 1) * chunk)
        cb = _dot(tri, e_hi[rs]) + (_dot(tri, e_mid[rs]) + _dot(tri, e_lo[rs]))
        cums.append(cb)
        ends.append(jnp.broadcast_to(cb[chunk - 1:chunk, :], (chunk, A_WIDTH)))
    cum = jnp.concatenate(cums, axis=0)
    cum_end = jnp.concatenate(ends, axis=0)
    grow = jnp.exp(cum)
    to_end = jnp.exp(cum - cum_end)
    at = bf(-kkn * jnp.exp(e - cum))
    rt = bf(r * jnp.exp(-cum))
    bt = bf(b_ * grow)
    kt = bf(k2 * grow)
    bh = bf(b_ * to_end)
    kh = bf(k2 * to_end)
    vb = bf(v)
    dec_end = jnp.exp(-cum_end)

    lane_head = lax.broadcasted_iota(jnp.int32, (1, pw_), 1) // hd
    head_mask = [lane_head == h for h in range(hpp)]

    def block_diag(x):
        return jnp.concatenate([jnp.where(head_mask[h], x, jnp.zeros_like(x)) for h in range(hpp)], axis=0)

    assert chunk == hd
    ti = lax.broadcasted_iota(jnp.int32, (chunk, pw_), 0)
    tj = lax.broadcasted_iota(jnp.int32, (chunk, pw_), 1) % chunk
    strict = ti > tj
    incl = ti >= tj
    eye = (ti == tj).astype(F32)
    n_sq = int(math.log2(chunk)) - 1

    streams = [(b, p, slice(b * chunk, (b + 1) * chunk), slice(p * pw_, (p + 1) * pw_))
               for b in range(n_b) for p in range(RWKV_PACKS)]
    at_s = [at[rs, cs] for _, _, rs, cs in streams]
    rt_s = [rt[rs, cs] for _, _, rs, cs in streams]
    v_s = [vb[rs, cs] for _, _, rs, cs in streams]
    m = [_dot_nt(jnp.concatenate([a_, r_], axis=0),
                 jnp.concatenate([block_diag(bt[rs, cs]), block_diag(kt[rs, cs])], axis=0))
         for a_, r_, (_, _, rs, cs) in zip(at_s, rt_s, streams)]
    a_ab = [jnp.where(strict, x[:chunk, :pw_], 0.0) for x in m]
    akv = [_dot(bf(jnp.where(strict, x[:chunk, pw_:], 0.0)), block_diag(vs)) for x, vs in zip(m, v_s)]
    m_r = [bf(jnp.concatenate([jnp.where(incl, x[chunk:, :pw_], 0.0), jnp.where(incl, x[chunk:, pw_:], 0.0)], axis=1))
           for x in m]
    tinv = [eye + x for x in a_ab]
    pw = a_ab
    for _ in range(n_sq):
        pw = [_dot(bf(x), block_diag(bf(x))) for x in pw]
        tinv = [t + _dot(bf(t), block_diag(bf(x))) for t, x in zip(tinv, pw)]
    w12 = [_dot(bf(t), jnp.concatenate([block_diag(a_), block_diag(bf(x))], axis=1))
           for t, a_, x in zip(tinv, at_s, akv)]

    st = [st_scr[b, p] for b, p, _, _ in streams]
    x = [_dot_nt(jnp.concatenate([bf(w[:, :pw_]), r_], axis=0), block_diag(bf(s))) for w, r_, s in zip(w12, rt_s, st)]
    uu = [bf(xi[:chunk] + w[:, pw_:]) for xi, w in zip(x, w12)]
    ys = [xi[chunk:] + _dot(mr, jnp.concatenate([block_diag(ui), block_diag(vs)], axis=0))
          for xi, mr, ui, vs in zip(x, m_r, uu, v_s)]
    for i, (b, p, rs, cs) in enumerate(streams):
        upd = _dot_tn(jnp.concatenate([uu[i], v_s[i]], axis=0), jnp.concatenate([bh[rs, cs], kh[rs, cs]], axis=0))
        diag = functools.reduce(lambda s_, t_: s_ + t_,
                                [jnp.where(head_mask[h], upd[h * hd:(h + 1) * hd, :], 0.0) for h in range(hpp)])
        st_scr[b, p] = st[i] * dec_end[rs.start:rs.start + 1, cs] + diag
    y = jnp.concatenate([jnp.concatenate(ys[b * RWKV_PACKS:(b + 1) * RWKV_PACKS], axis=1) for b in range(n_b)], axis=0)
    out = _rwkv_output(y, r, k2, v, g, rk_ref, lng_ref, lnb_ref, bd, False)
    for b in range(n_b):
        y_ref[b] = out[b * chunk:(b + 1) * chunk]

    @pl.when(c == n_steps - 1)
    def _():
        for b in range(n_b):
            shout_ref[b] = u_b[b][chunk - 1:chunk, :]
            for p in range(RWKV_PACKS):
                for h in range(hpp):
                    sfin_ref[b, p * hpp + h] = st_scr[b, p][:, h * hd:(h + 1) * hd]


def _rwkv_step_kernel(u_ref, sh0_ref, s0_ref, mu_ref, w0_ref, w2_ref, a0_ref, a2_ref, g2_ref, kk_ref, ka_ref,
                      rk_ref, lng_ref, lnb_ref, bd_ref, y_ref, sfin_ref, shout_ref, *, t_valid, exact):
    hd = A_HEAD_DIM
    n_b, n_rows, _ = u_ref.shape
    u_b = [u_ref[b] for b in range(n_b)]
    u = jnp.concatenate(u_b, axis=0)
    u_prev = jnp.concatenate([_shift_rows(u_b[b], sh0_ref[b]) for b in range(n_b)], axis=0)
    bd = bd_ref[...]
    r, k2, v, e, a, g, kkn = _rwkv_features(
        u, u_prev, (mu_ref, w0_ref, w2_ref, a0_ref, a2_ref, g2_ref, kk_ref, ka_ref), bd, exact)
    rnd = lambda t: _contract_round(t, exact)
    decay = jnp.exp(-e)
    b_ = kkn * a
    eye = (lax.broadcasted_iota(jnp.int32, (hd, hd), 0) == lax.broadcasted_iota(jnp.int32, (hd, hd), 1)).astype(F32)
    to_col = lambda t: jnp.sum(eye * t, axis=1, keepdims=True)
    to_row = lambda t: jnp.sum(eye * t, axis=0, keepdims=True)
    y_seq = []
    for b in range(n_b):
        y_heads = []
        for h in range(A_HEADS):
            sl = slice(h * hd, (h + 1) * hd)
            s = s0_ref[b, h]
            y_rows = []
            for t in range(t_valid):
                tt = slice(b * n_rows + t, b * n_rows + t + 1)
                sa = jnp.sum(rnd(s) * rnd(-kkn[tt, sl]), axis=1, keepdims=True)
                s = s * decay[tt, sl] + sa * b_[tt, sl] + to_col(v[tt, sl]) * k2[tt, sl]
                y_rows.append(to_row(jnp.sum(rnd(s) * rnd(r[tt, sl]), axis=1, keepdims=True)))
            sfin_ref[b, h] = s
            y_rows.append(jnp.zeros((n_rows - t_valid, hd), F32))
            y_heads.append(jnp.concatenate(y_rows, axis=0))
        y_seq.append(jnp.concatenate(y_heads, axis=1))
    y = jnp.concatenate(y_seq, axis=0)
    out = _rwkv_output(y, r, k2, v, g, rk_ref, lng_ref, lnb_ref, bd, exact)
    for b in range(n_b):
        y_ref[b] = out[b * n_rows:(b + 1) * n_rows]
        shout_ref[b] = u_b[b][t_valid - 1:t_valid, :]


def _rwkv(u_a, shift0, s0, p, chunk, t_valid):
    bsz, t_len, _ = u_a.shape
    hd = A_HEAD_DIM
    bd = jnp.asarray(np.kron(np.eye(A_HEADS, dtype=np.float32), np.ones((hd, hd), np.float32)))
    row = lambda t: t.reshape(1, -1)
    if chunk:
        assert t_valid == t_len and t_len % chunk == 0
        grid = (t_len // chunk,)
        kern = functools.partial(_rwkv_chunk_kernel, n_steps=grid[0])
        scratch = [pltpu.VMEM((bsz, RWKV_PACKS, hd, RWKV_PACK_WIDTH), F32), pltpu.VMEM((bsz, A_PROJ), F32)]
        u_spec = pl.BlockSpec((bsz, chunk, A_PROJ), lambda c: (0, c, 0))
        y_spec = pl.BlockSpec((bsz, chunk, A_WIDTH), lambda c: (0, c, 0))
        state_spec = pl.BlockSpec((bsz, A_HEADS, hd, hd), lambda c: (0, 0, 0, 0))
        shift_spec = pl.BlockSpec((bsz, 1, A_PROJ), lambda c: (0, 0, 0))
    else:
        n_b = math.gcd(bsz, RWKV_STEP_SEQS)
        grid = (bsz // n_b,)
        kern = functools.partial(_rwkv_step_kernel, t_valid=t_valid, exact=True)
        scratch = []
        u_spec = pl.BlockSpec((n_b, t_len, A_PROJ), lambda b: (b, 0, 0))
        y_spec = pl.BlockSpec((n_b, t_len, A_WIDTH), lambda b: (b, 0, 0))
        state_spec = pl.BlockSpec((n_b, A_HEADS, hd, hd), lambda b: (b, 0, 0, 0))
        shift_spec = pl.BlockSpec((n_b, 1, A_PROJ), lambda b: (b, 0, 0))
    return pl.pallas_call(
        kern,
        grid=grid,
        in_specs=[
            u_spec, shift_spec, state_spec,
            _const_spec((1, A_PROJ)), _const_spec((1, A_WIDTH)), _const_spec((A_DECAY_LORA, A_WIDTH)),
            _const_spec((1, A_WIDTH)), _const_spec((A_ICLR_LORA, A_WIDTH)), _const_spec((A_GATE_LORA, A_WIDTH)),
            _const_spec((1, A_WIDTH)), _const_spec((1, A_WIDTH)), _const_spec((1, A_WIDTH)),
            _const_spec((1, A_WIDTH)), _const_spec((1, A_WIDTH)), _const_spec((A_WIDTH, A_WIDTH)),
        ],
        out_specs=[y_spec, state_spec, shift_spec],
        out_shape=[
            jax.ShapeDtypeStruct((bsz, t_len, A_WIDTH), F32),
            jax.ShapeDtypeStruct((bsz, A_HEADS, hd, hd), F32),
            jax.ShapeDtypeStruct((bsz, 1, A_PROJ), F32),
        ],
        scratch_shapes=scratch,
        compiler_params=_params(1),
        name="rwkv7",
    )(u_a, shift0.reshape(bsz, 1, A_PROJ), s0, row(p['rwkv_mu']), row(p['rwkv_w0']), p['rwkv_w2'],
      row(p['rwkv_a0']), p['rwkv_a2'], p['rwkv_g2'], row(p['rwkv_k_k']), row(p['rwkv_k_a']),
      row(p['rwkv_r_k']), row(p['rwkv_lnx_g']), row(p['rwkv_lnx_b']), bd)


def _swa_prompt_kernel(qkv_ref, o_ref, lse_ref):
    blk, hd, n_g = SWA_BLOCK, B_HEAD_DIM, len(B_GROUPS)
    s_len = qkv_ref.shape[1]
    scale = hd ** -0.5
    rows2 = SWA_PAIR * blk
    r_i = lax.broadcasted_iota(jnp.int32, (rows2, 1), 0)
    qi = r_i % blk
    own_lanes = (lax.broadcasted_iota(jnp.int32, (1, LANES), 1) // hd) == (r_i // blk)
    head0_lanes = own_lanes[:blk]
    ki2 = lax.broadcasted_iota(jnp.int32, (rows2, 2 * blk), 1)
    band2 = (qi + blk - ki2 >= 0) & (qi - ki2 <= 0)
    causal1 = lax.broadcasted_iota(jnp.int32, (rows2, blk), 1) <= qi

    def attend(g, specs):
        dil = B_GROUPS[g][1]

        def rows(kind, st):
            idx = pl.ds(st, blk, stride=dil) if dil > 1 else pl.ds(st, blk)
            return qkv_ref[kind * n_g + g, idx, :]

        qs, kbs, vbs, masks = [], [], [], []
        for st, prev, first in specs:
            q = rows(0, st)
            qs.append(jnp.where(own_lanes, jnp.concatenate([q] * SWA_PAIR, axis=0), 0.0).astype(BF16))
            if prev is None:
                kbs.append(rows(1, st).astype(BF16))
                vbs.append(rows(2, st).astype(BF16))
                masks.append(causal1)
            else:
                kbs.append(jnp.concatenate([rows(1, prev), rows(1, st)], axis=0).astype(BF16))
                vbs.append(jnp.concatenate([rows(2, prev), rows(2, st)], axis=0).astype(BF16))
                masks.append(band2 & (ki2 >= jnp.where(first, blk, 0)))
        scores = [_dot_nt(q, kb) * scale for q, kb in zip(qs, kbs)]
        probs, lses = [], []
        for sc, mk in zip(scores, masks):
            sc = jnp.where(mk, sc, NEG_INF)
            m = jnp.max(sc, axis=-1, keepdims=True)
            p = jnp.exp(sc - m)
            den = jnp.sum(p, axis=-1, keepdims=True)
            probs.append((p / den).astype(BF16))
            lses.append(m + jnp.log(den))
        outs = [_dot(p, vb) for p, vb in zip(probs, vbs)]
        for (st, _, _), o2, l2 in zip(specs, outs, lses):
            idx = pl.ds(st, blk, stride=dil) if dil > 1 else pl.ds(st, blk)
            o_ref[g, idx, :] = jnp.where(head0_lanes, o2[:blk], o2[blk:])
            lse_ref[g, idx, :] = jnp.where(head0_lanes, l2[:blk], l2[blk:])

    for g, (_, dil) in enumerate(B_GROUPS):
        n_blk = s_len // dil // blk
        n_quads = dil * n_blk // SWA_QUAD

        def quad(it, carry, g=g, dil=dil, n_blk=n_blk):
            specs = []
            for j in range(SWA_QUAD):
                if n_blk >= SWA_QUAD:
                    e = it * SWA_QUAD + j
                    r, n = e // n_blk, e % n_blk
                    specs.append((n * (blk * dil) + r, jnp.maximum(n - 1, 0) * (blk * dil) + r, n == 0))
                else:
                    r = it * (SWA_QUAD // n_blk) + j // n_blk
                    n = j % n_blk
                    specs.append((n * (blk * dil) + r, None if n == 0 else (n - 1) * (blk * dil) + r, False))
            attend(g, specs)
            return carry

        lax.fori_loop(0, n_quads, quad, 0)


def _swa_prompt(qkv_slabs, n_seq, s_len):
    n_g = len(B_GROUPS)
    n = n_seq * s_len
    out_spec = pl.BlockSpec((n_g, s_len, LANES), lambda b, p: (p, b, 0))
    shp = jax.ShapeDtypeStruct((SWA_PAIRS * n_g, n, LANES), F32)
    return pl.pallas_call(
        _swa_prompt_kernel,
        grid=(n_seq, SWA_PAIRS),
        in_specs=[pl.BlockSpec((3 * n_g, s_len, LANES), lambda b, p: (p, b, 0))],
        out_specs=[out_spec, out_spec],
        out_shape=[shp, shp],
        compiler_params=_params(2),
        name="swa_prompt",
    )(qkv_slabs)


def _swa_sample_kernel(qkv_ref, c0_ref, c1_ref, c2_ref, o_ref, lse_ref, n0_ref, n1_ref, n2_ref, *, t_len, exact):
    hd = B_HEAD_DIM
    nh = B_HEADS_PER_GROUP
    gw = B_GROUP_WIDTH
    rows = t_len * nh
    qkv = qkv_ref[0]
    scale = hd ** -0.5
    r_i = lax.broadcasted_iota(jnp.int32, (rows, gw), 0)
    l_i = lax.broadcasted_iota(jnp.int32, (rows, gw), 1)
    head_lanes = (l_i // hd) == (r_i % nh)
    t_of_row = lax.broadcasted_iota(jnp.int32, (rows, 1), 0) // nh
    outs, lses = [], []
    for gi, ((window, dil), c_ref, n_ref) in enumerate(zip(B_GROUPS, (c0_ref, c1_ref, c2_ref), (n0_ref, n1_ref, n2_ref))):
        buf_len = c_ref.shape[2]
        q = qkv[:, gi * gw:(gi + 1) * gw]
        k_new = qkv[:, B_WIDTH + gi * gw:B_WIDTH + (gi + 1) * gw]
        v_new = qkv[:, 2 * B_WIDTH + gi * gw:2 * B_WIDTH + (gi + 1) * gw]
        kc = c_ref[0, 0]
        vc = c_ref[0, 1]
        qx = jnp.zeros((rows, gw), F32)
        for t in range(t_len):
            qx = jnp.where(head_lanes & (t_of_row == t), q[t:t + 1, :], qx)
        qx = _contract_round(qx, exact)
        k_new_r = _contract_round(k_new, exact)
        v_new_r = _contract_round(v_new, exact)
        s_c = _mm(qx, kc, exact, _NT) * scale
        t_c = lax.broadcasted_iota(jnp.int32, (rows, buf_len), 0) // nh
        j_c = lax.broadcasted_iota(jnp.int32, (rows, buf_len), 1)
        ok_c = (j_c >= t_c) & (((j_c - t_c) & (dil - 1)) == 0)
        s_c = jnp.where(ok_c, s_c, NEG_INF)
        s_n = []
        for i in range(t_len):
            s_i = jnp.sum(qx * k_new_r[i:i + 1, :], axis=-1, keepdims=True) * scale
            ok_i = (t_of_row >= i) & (((t_of_row - i) & (dil - 1)) == 0)
            s_n.append(jnp.where(ok_i, s_i, NEG_INF))
        m = functools.reduce(jnp.maximum, s_n, jnp.max(s_c, axis=-1, keepdims=True))
        p_c = jnp.exp(s_c - m)
        p_n = [jnp.exp(s_i - m) for s_i in s_n]
        den = functools.reduce(lambda a, b: a + b, p_n, jnp.sum(p_c, axis=-1, keepdims=True))
        o = _mm(p_c / den, vc, exact)
        for i in range(t_len):
            o = o + _contract_round(p_n[i] / den, exact) * v_new_r[i:i + 1, :]
        o = jnp.where(head_lanes, o, 0.0)
        lse = jnp.where(head_lanes, m + jnp.log(den), 0.0)
        outs.append(jnp.concatenate(
            [jnp.sum(o[t * nh:(t + 1) * nh, :], axis=0, keepdims=True) for t in range(t_len)], axis=0))
        lses.append(jnp.concatenate(
            [jnp.sum(lse[t * nh:(t + 1) * nh, :], axis=0, keepdims=True) for t in range(t_len)], axis=0))
        for j, new in ((0, k_new), (1, v_new)):
            n_ref[0, j, :buf_len - t_len, :] = c_ref[0, j, t_len:, :]
            n_ref[0, j, buf_len - t_len:, :] = new
    o_ref[0] = jnp.concatenate(outs, axis=1)
    lse_ref[0] = jnp.concatenate(lses, axis=1)


def _swa_sample(qkv, caches):
    bsz, t_len, _ = qkv.shape
    cache_specs = [pl.BlockSpec((1, 2) + c.shape[2:], lambda b: (b, 0, 0, 0)) for c in caches]
    row_spec = pl.BlockSpec((1, t_len, B_WIDTH), lambda b: (b, 0, 0))
    return pl.pallas_call(
        functools.partial(_swa_sample_kernel, t_len=t_len, exact=True),
        grid=(bsz,),
        in_specs=[pl.BlockSpec((1, t_len, B_PROJ), lambda b: (b, 0, 0))] + cache_specs,
        out_specs=[row_spec, row_spec] + cache_specs,
        out_shape=[jax.ShapeDtypeStruct((bsz, t_len, B_WIDTH), F32)] * 2
        + [jax.ShapeDtypeStruct(c.shape, F32) for c in caches],
        compiler_params=_params(1),
        name="swa_sample",
    )(qkv, *caches)


def _mid_kernel(x_ref, ya_ref, ob_ref, lse_ref, gate_ref, mk_ref, mv_ref, wa_ref, wb_ref, wo_ref, gm_ref,
                wq_ref, wmo_ref, gf_ref, wr_ref, br_ref, xn_all_ref, x2_ref, xn_ref, route_ref, *, rows_per_batch, exact):
    del xn_all_ref
    tm = x_ref.shape[0]
    n_g = len(B_GROUPS)
    rnd = lambda t: _contract_round(t, exact)
    mm = lambda a, b, dims=_NN: _mm(a, b, exact, dims)
    yb_pairs = []
    for p in range(SWA_PAIRS):
        lses = [lse_ref[p * n_g + g] for g in range(n_g)]
        m = functools.reduce(jnp.maximum, lses)
        es = [jnp.exp(l - m) for l in lses]
        den = functools.reduce(lambda a, b: a + b, es)
        yb_pairs.append(functools.reduce(lambda a, b: a + b,
                                         [rnd(es[g] / den) * rnd(ob_ref[p * n_g + g]) for g in range(n_g)]))
    yb = jnp.concatenate(yb_pairs, axis=1)
    gates = gate_ref[...]
    merged = (gates[:, :D_MODEL] * mm(ya_ref[...], wa_ref[...])
              + gates[:, D_MODEL:] * mm(yb, wb_ref[...]))
    x1 = x_ref[...] + mm(merged, wo_ref[...])

    q = mm(_rms(x1, gm_ref[...]), wq_ref[...])
    n_b = mk_ref.shape[0]
    mk = mk_ref[...].reshape(n_b * N_MEM, MEM_WIDTH)
    mv = mv_ref[...].reshape(n_b * N_MEM, MEM_WIDTH)
    if not exact:
        mk, mv = mk.astype(BF16), mv.astype(BF16)
    if n_b > 1:
        rb = lax.broadcasted_iota(jnp.int32, (tm, n_b * N_MEM), 0) // rows_per_batch
        cb = lax.broadcasted_iota(jnp.int32, (tm, n_b * N_MEM), 1) // N_MEM
        same = rb == cb
    heads = []
    for h in range(MEM_HEADS):
        sl = slice(h * MEM_HEAD_DIM, (h + 1) * MEM_HEAD_DIM)
        s = mm(q[:, sl], mk[:, sl], _NT) * (MEM_HEAD_DIM ** -0.5)
        if n_b > 1:
            s = jnp.where(same, s, NEG_INF)
        s = s - jnp.max(s, axis=-1, keepdims=True)
        p = jnp.exp(s)
        heads.append(mm(p / jnp.sum(p, axis=-1, keepdims=True), mv[:, sl]))
    x2 = x1 + mm(jnp.concatenate(heads, axis=1), wmo_ref[...])
    x2_ref[...] = x2

    xn = _rms(x2, gf_ref[...])
    xn_ref[...] = xn
    logits = mm(xn, wr_ref[...]) + br_ref[...]
    lane = lax.broadcasted_iota(jnp.int32, (tm, LANES), 1)
    gl = jnp.where(lane < N_GROUPS, logits, NEG_INF)
    gmax = jnp.max(gl, axis=-1, keepdims=True)
    grp = jnp.min(jnp.where(gl == gmax, lane, LANES), axis=-1, keepdims=True)
    w_grp = 1.0 / jnp.sum(jnp.exp(gl - gmax), axis=-1, keepdims=True)
    first = N_GROUPS + grp * EXPERTS_PER_GROUP
    el = jnp.where((lane >= first) & (lane < first + EXPERTS_PER_GROUP), logits, NEG_INF)
    m1 = jnp.max(el, axis=-1, keepdims=True)
    i1 = jnp.min(jnp.where(el == m1, lane, LANES), axis=-1, keepdims=True)
    el2 = jnp.where(lane == i1, NEG_INF, el)
    m2 = jnp.max(el2, axis=-1, keepdims=True)
    i2 = jnp.min(jnp.where(el2 == m2, lane, LANES), axis=-1, keepdims=True)
    e2 = jnp.exp(m2 - m1)
    g1 = w_grp / (1.0 + e2)
    g2 = w_grp * e2 / (1.0 + e2)
    route = jnp.where(lane == 0, (i1 - N_GROUPS).astype(F32), 0.0)
    route = jnp.where(lane == 1, (i2 - N_GROUPS).astype(F32), route)
    route = jnp.where(lane == 2, g1, route)
    route_ref[...] = jnp.where(lane == 3, g2, route)


def _mid(x, y_a, o_b, lse_b, gates, mem_k, mem_v, mk_map, mv_map, n_b, rows_per_batch, w, tm, xn_all, row0):
    n = x.shape[0]
    row = lambda width: pl.BlockSpec((tm, width), lambda i: (i, 0))
    slabs = pl.BlockSpec((SWA_PAIRS * len(B_GROUPS), tm, LANES), lambda i: (0, i, 0))
    in_specs = [
        row(D_MODEL), row(A_WIDTH), slabs, slabs, row(2 * D_MODEL),
        pl.BlockSpec((n_b, N_MEM, MEM_WIDTH), mk_map), pl.BlockSpec((n_b, N_MEM, MEM_WIDTH), mv_map),
        _const_spec((A_WIDTH, D_MODEL)), _const_spec((B_GROUP_WIDTH, D_MODEL)), _const_spec((D_MODEL, D_MODEL)),
        _const_spec((1, D_MODEL)), _const_spec((D_MODEL, MEM_WIDTH)), _const_spec((MEM_WIDTH, D_MODEL)),
        _const_spec((1, D_MODEL)), _const_spec((D_MODEL, LANES)), _const_spec((1, LANES)),
        pl.BlockSpec(memory_space=pl.ANY),
    ]
    args = [x, y_a, o_b, lse_b, gates, mem_k, mem_v, w['wa'], w['wb'], w['wo'], w['gm'], w['wq'], w['wmo'],
            w['gf'], w['wr'], w['br'], xn_all]
    blk0 = row0 // tm
    return pl.pallas_call(
        functools.partial(_mid_kernel, rows_per_batch=rows_per_batch, exact=w['wa'].dtype == F32),
        grid=(n // tm,),
        in_specs=in_specs,
        out_specs=[row(D_MODEL), pl.BlockSpec((tm, D_MODEL), lambda i: (i + blk0, 0)), row(LANES)],
        out_shape=[
            jax.ShapeDtypeStruct((n, D_MODEL), F32),
            jax.ShapeDtypeStruct(xn_all.shape, F32),
            jax.ShapeDtypeStruct((n, LANES), F32),
        ],
        input_output_aliases={len(args) - 1: 1},
        compiler_params=_params(1),
        name="mid",
    )(*args)


def _row_copy(src_hbm, idx, dst_buf, slot, j, sem):
    return pltpu.make_async_copy(src_hbm.at[pl.ds(idx, 1), :], dst_buf.at[slot, pl.ds(j, 1), :], sem.at[slot])


def _gather_start(idx_ref, src_hbm, dst_buf, slot, sem, n_rows):
    for j in range(n_rows):
        _row_copy(src_hbm, idx_ref[0, 0, j], dst_buf, slot, j, sem).start()


def _gather_wait(src_hbm, dst_buf, slot, sem, n_rows):
    for j in range(n_rows):
        _row_copy(src_hbm, 0, dst_buf, slot, j, sem).wait()


def _experts_kernel(meta_ref, be_ref, idx_ref, idx_next_ref, x_hbm, wg_ref, wu_ref, wd_ref, o_ref, xbuf, sem):
    i = pl.program_id(0)
    n_used = meta_ref[0]
    slot = i % 2

    @pl.when(i == 0)
    def _():
        _gather_start(idx_ref, x_hbm, xbuf, 0, sem, MOE_ROWS)

    @pl.when(i + 1 < n_used)
    def _():
        _gather_start(idx_next_ref, x_hbm, xbuf, 1 - slot, sem, MOE_ROWS)

    @pl.when(i < n_used)
    def _():
        _gather_wait(x_hbm, xbuf, slot, sem, MOE_ROWS)
        xb = xbuf[slot].astype(BF16)
        hg = _dot(xb, wg_ref[0].astype(BF16))
        hu = _dot(xb, wu_ref[0].astype(BF16))
        hh = (jax.nn.silu(hg) * hu).astype(BF16)
        o_ref[...] = _dot(hh, wd_ref[0].astype(BF16))

    @pl.when(i >= n_used)
    def _():
        o_ref[...] = jnp.zeros_like(o_ref)


def _experts(xn_all, row_tok, block_e, n_used, w_gate, w_up, w_down):
    n_blocks = block_e.shape[0]
    idx3 = row_tok.reshape(n_blocks, 1, MOE_ROWS)
    idx_spec = lambda f: pl.BlockSpec((1, 1, MOE_ROWS), f, memory_space=pltpu.SMEM)
    grid_spec = pltpu.PrefetchScalarGridSpec(
        num_scalar_prefetch=2,
        grid=(n_blocks,),
        in_specs=[
            idx_spec(lambda i, meta, be: (i, 0, 0)),
            idx_spec(lambda i, meta, be: (jnp.minimum(i + 1, n_blocks - 1), 0, 0)),
            pl.BlockSpec(memory_space=pl.ANY),
            pl.BlockSpec((1, D_MODEL, EXPERT_FF), lambda i, meta, be: (be[i], 0, 0)),
            pl.BlockSpec((1, D_MODEL, EXPERT_FF), lambda i, meta, be: (be[i], 0, 0)),
            pl.BlockSpec((1, EXPERT_FF, D_MODEL), lambda i, meta, be: (be[i], 0, 0)),
        ],
        out_specs=pl.BlockSpec((MOE_ROWS, D_MODEL), lambda i, meta, be: (i, 0)),
        scratch_shapes=[pltpu.VMEM((2, MOE_ROWS, D_MODEL), F32), pltpu.SemaphoreType.DMA((2,))],
    )
    return pl.pallas_call(
        _experts_kernel,
        grid_spec=grid_spec,
        out_shape=jax.ShapeDtypeStruct((n_blocks * MOE_ROWS, D_MODEL), F32),
        compiler_params=_params(1),
        name="experts",
    )(n_used.reshape(1), block_e, idx3, idx3, xn_all, w_gate, w_up, w_down)


def _combine_kernel(pos_ref, pos_next_ref, x_ref, route_ref, yb_hbm, g_ref, o_ref, ybuf, sem, *, n_tiles):
    i = pl.program_id(0)
    tm = x_ref.shape[0]
    slot = i % 2

    @pl.when(i == 0)
    def _():
        _gather_start(pos_ref, yb_hbm, ybuf, 0, sem, 2 * tm)

    if n_tiles > 1:
        @pl.when(i + 1 < n_tiles)
        def _():
            _gather_start(pos_next_ref, yb_hbm, ybuf, 1 - slot, sem, 2 * tm)

    _gather_wait(yb_hbm, ybuf, slot, sem, 2 * tm)
    route = route_ref[...]
    y = x_ref[...] + (route[:, 2:3] * ybuf[slot, :tm, :] + route[:, 3:4] * ybuf[slot, tm:, :])
    o_ref[...] = _rms(y, g_ref[...])


def _combine(x2, route, pos, yb, g_final, tm):
    n = x2.shape[0]
    n_tiles = n // tm
    pos_spec = lambda f: pl.BlockSpec((1, 1, 2 * tm), f, memory_space=pltpu.SMEM)
    return pl.pallas_call(
        functools.partial(_combine_kernel, n_tiles=n_tiles),
        grid=(n_tiles,),
        in_specs=[
            pos_spec(lambda i: (i, 0, 0)),
            pos_spec(lambda i: (jnp.minimum(i + 1, n_tiles - 1), 0, 0)),
            pl.BlockSpec((tm, D_MODEL), lambda i: (i, 0)),
            pl.BlockSpec((tm, LANES), lambda i: (i, 0)),
            pl.BlockSpec(memory_space=pl.ANY),
            _const_spec((1, D_MODEL)),
        ],
        out_specs=pl.BlockSpec((tm, D_MODEL), lambda i: (i, 0)),
        out_shape=jax.ShapeDtypeStruct((n, D_MODEL), F32),
        scratch_shapes=[pltpu.VMEM((2, 2 * tm, D_MODEL), F32), pltpu.SemaphoreType.DMA((2,))],
        compiler_params=_params(1),
        name="combine",
    )(pos, pos, x2, route, yb, g_final.reshape(1, -1))


def _dispatch(eid):
    n_tok = eid.shape[0]
    n_rows = n_tok * TOP_K
    n_blocks = n_rows // MOE_ROWS + N_EXPERTS
    flat_e = eid.reshape(-1)
    onehot = (flat_e[:, None] == jnp.arange(N_EXPERTS, dtype=jnp.int32)[None, :]).astype(jnp.int32)
    csum = jnp.cumsum(onehot, axis=0)
    counts = csum[-1]
    rank = jnp.sum((csum - onehot) * onehot, axis=1)
    padded = (counts + MOE_ROWS - 1) // MOE_ROWS * MOE_ROWS
    pad_end = jnp.cumsum(padded)
    pad_start = pad_end - padded
    dest = pad_start[flat_e] + rank
    flat_tok = jnp.arange(n_rows, dtype=jnp.int32) // TOP_K
    row_tok = jnp.zeros((n_blocks * MOE_ROWS,), jnp.int32).at[dest].set(flat_tok, unique_indices=True)
    block_start = jnp.arange(n_blocks, dtype=jnp.int32) * MOE_ROWS
    block_e = jnp.minimum(jnp.sum((pad_end[None, :] <= block_start[:, None]).astype(jnp.int32), axis=1), N_EXPERTS - 1)
    n_used = (pad_end[-1] // MOE_ROWS).astype(jnp.int32)
    return row_tok, block_e, n_used, dest.reshape(n_tok, TOP_K).astype(jnp.int32)


def _tile_pos(pos, tm):
    n = pos.shape[0]
    return pos.reshape(n // tm, tm, TOP_K).transpose(0, 2, 1).reshape(n // tm, 1, TOP_K * tm)


def _slabs_to_rows(qkv_slabs):
    order = np.array([_slab_position(j) for j in range(SWA_SLABS)])
    return jnp.transpose(qkv_slabs[order], (1, 0, 2)).reshape(qkv_slabs.shape[1], B_PROJ)


def _rows_to_pair_slabs(t):
    n = t.shape[0]
    return jnp.transpose(t.reshape(n, len(B_GROUPS), SWA_PAIRS, LANES), (2, 1, 0, 3)).reshape(-1, n, LANES)


def kernel(x_prompt, x_sample, state_rwkv, state_shift, cache_swa_w128, cache_swa_w512, cache_swa_w2048,
           cache_mem_kv, mem_prompt, norm_mix_g, w_in, b_gate, rwkv_mu, rwkv_w0, rwkv_w2, rwkv_a0, rwkv_a2,
           rwkv_g2, rwkv_k_k, rwkv_k_a, rwkv_r_k, rwkv_lnx_g, rwkv_lnx_b, w_branch_a, w_branch_b, w_out,
           norm_mem_g, norm_memkv_g, w_mem_q, w_mem_kv, w_mem_out, norm_ffn_g, w_router_group, b_router_group,
           w_router_expert, b_router_expert, w_exp_gate, w_exp_up, w_exp_down, norm_final_g):
    bsz, s_len, _ = x_prompt.shape
    dbs, t_len, _ = x_sample.shape
    n_p, n_s = bsz * s_len, dbs * t_len
    rw = dict(rwkv_mu=rwkv_mu, rwkv_w0=rwkv_w0, rwkv_w2=rwkv_w2, rwkv_a0=rwkv_a0, rwkv_a2=rwkv_a2, rwkv_g2=rwkv_g2,
              rwkv_k_k=rwkv_k_k, rwkv_k_a=rwkv_k_a, rwkv_r_k=rwkv_r_k, rwkv_lnx_g=rwkv_lnx_g, rwkv_lnx_b=rwkv_lnx_b)
    pad = LANES - N_GROUPS - N_EXPERTS
    mid_w_s = dict(
        wa=w_branch_a, wb=w_branch_b, wo=w_out, gm=norm_mem_g.reshape(1, -1), wq=w_mem_q, wmo=w_mem_out,
        gf=norm_ffn_g.reshape(1, -1),
        wr=jnp.concatenate([w_router_group, w_router_expert, jnp.zeros((D_MODEL, pad), F32)], axis=1),
        br=jnp.concatenate([b_router_group, b_router_expert, jnp.zeros((pad,), F32)]).reshape(1, -1))
    mid_w_p = {k: (v.astype(BF16) if k.startswith('w') else v) for k, v in mid_w_s.items()}
    w_in_b = w_in.astype(BF16)

    xp = x_prompt.reshape(n_p, D_MODEL)
    tm = 512
    ua_p, qkv_p, gates_p, *p_caches = _in_proj(xp, norm_mix_g, w_in_b, b_gate, _rope_tables(jnp.arange(s_len)), tm,
                                               cache_seq=(bsz, s_len))
    s0 = jnp.zeros((bsz, A_HEADS, A_HEAD_DIM, A_HEAD_DIM), F32)
    ya_p, st_p, shift_p = _rwkv(ua_p.reshape(bsz, s_len, A_PROJ), jnp.zeros((bsz, A_PROJ), F32), s0, rw,
                                RWKV_CHUNK, s_len)
    ob_p, lse_p = _swa_prompt(qkv_p, bsz, s_len)
    p_bufs = [c.reshape(c.shape[:3] + (B_HEADS_PER_GROUP, B_HEAD_DIM)) for c in p_caches]

    memkv = _norm_matmul(mem_prompt.reshape(bsz * N_MEM, D_MODEL), norm_memkv_g, w_mem_kv.astype(BF16), 256)
    memkv3 = memkv.reshape(bsz, N_MEM, 2 * MEM_WIDTH)
    mem_kv_prompt = memkv3.reshape(bsz, N_MEM, 2, MEM_HEADS, MEM_HEAD_DIM).transpose(0, 2, 1, 3, 4)
    tiles_per_batch = s_len // tm
    xn_all = jnp.zeros((n_p + n_s, D_MODEL), F32)
    x2_p, xn_all, route_p = _mid(
        xp, ya_p.reshape(n_p, A_WIDTH), ob_p, lse_p, gates_p, memkv3, memkv3,
        lambda i: (i // tiles_per_batch, 0, 0), lambda i: (i // tiles_per_batch, 0, 1),
        1, s_len, mid_w_p, tm, xn_all, 0)

    xs = x_sample.reshape(n_s, D_MODEL)
    pos_s = PAST_LEN + (jnp.arange(n_s) % t_len)
    ua_s, qkv_s, gates_s = _in_proj(xs, norm_mix_g, w_in, b_gate, _rope_tables(pos_s), n_s)
    t_pad = 8
    ua_s3 = jnp.pad(ua_s.reshape(dbs, t_len, A_PROJ), ((0, 0), (0, t_pad - t_len), (0, 0)))
    ya_s, st_s, shift_s = _rwkv(ua_s3, state_shift, state_rwkv, rw, 0, t_len)
    ya_s = ya_s[:, :t_len].reshape(n_s, A_WIDTH)
    caches = [c.reshape(c.shape[0], 2, c.shape[2], B_GROUP_WIDTH) for c in (cache_swa_w128, cache_swa_w512, cache_swa_w2048)]
    ob_s, lse_s, nb0, nb1, nb2 = _swa_sample(_slabs_to_rows(qkv_s).reshape(dbs, t_len, B_PROJ), caches)
    s_bufs = [nb.reshape(nb.shape[0], 2, nb.shape[2], B_HEADS_PER_GROUP, B_HEAD_DIM) for nb in (nb0, nb1, nb2)]
    mem_s = cache_mem_kv.reshape(dbs, 2 * N_MEM, MEM_WIDTH)
    tm_s = 32
    x2_s, xn_all, route_s = _mid(
        xs, ya_s, _rows_to_pair_slabs(ob_s.reshape(n_s, B_WIDTH)), _rows_to_pair_slabs(lse_s.reshape(n_s, B_WIDTH)),
        gates_s, mem_s, mem_s,
        lambda i: (i, 0, 0), lambda i: (i, 1, 0), tm_s // t_len, t_len, mid_w_s, tm_s, xn_all, n_p)

    route = jnp.concatenate([route_p, route_s], axis=0)
    eid = route[:, :TOP_K].astype(jnp.int32)
    row_tok, block_e, n_used, pos = _dispatch(eid)
    yb = _experts(xn_all, row_tok, block_e, n_used, w_exp_gate, w_exp_up, w_exp_down)
    tm_c = 128
    y_p = _combine(x2_p, route_p, _tile_pos(pos[:n_p], tm_c), yb, norm_final_g, tm_c)
    y_s = _combine(x2_s, route_s, _tile_pos(pos[n_p:], tm_c), yb, norm_final_g, tm_c)

    return (y_p.reshape(bsz, s_len, D_MODEL), y_s.reshape(dbs, t_len, D_MODEL),
            st_p, shift_p.reshape(bsz, A_PROJ), p_bufs[0], p_bufs[1], p_bufs[2], mem_kv_prompt,
            st_s, shift_s.reshape(dbs, A_PROJ), s_bufs[0], s_bufs[1], s_bufs[2])
```

```python
import functools
import math

import jax
import jax.numpy as jnp
import numpy as np
from jax import lax
from jax.experimental import pallas as pl
from jax.experimental.pallas import tpu as pltpu

F32 = jnp.float32
BF16 = jnp.bfloat16

D_MODEL = 1024
A_HEADS = 8
A_HEAD_DIM = 64
A_WIDTH = A_HEADS * A_HEAD_DIM
A_DECAY_LORA = 64
A_ICLR_LORA = 64
A_GATE_LORA = 128
A_PROJ = 3 * A_WIDTH + A_DECAY_LORA + A_ICLR_LORA + A_GATE_LORA
A_LNX_EPS = 64e-5
B_GROUPS = ((128, 1), (512, 4), (2048, 16))
B_HEADS_PER_GROUP = 4
B_HEAD_DIM = 64
B_GROUP_WIDTH = B_HEADS_PER_GROUP * B_HEAD_DIM
B_WIDTH = B_GROUP_WIDTH * len(B_GROUPS)
B_PROJ = 3 * B_WIDTH
ROPE_THETA = 500000.0
ROPE_DIM = B_HEAD_DIM // 4
ROPE_HALF = ROPE_DIM // 2
SWA_BLOCK = 128
N_MEM = 256
MEM_HEADS = 4
MEM_HEAD_DIM = 128
MEM_WIDTH = MEM_HEADS * MEM_HEAD_DIM
N_GROUPS = 4
EXPERTS_PER_GROUP = 8
N_EXPERTS = N_GROUPS * EXPERTS_PER_GROUP
TOP_K = 2
EXPERT_FF = 512
RMS_EPS = 1e-6
PAST_LEN = 8192

LANES = 128
SWA_SLABS = B_PROJ // LANES
SWA_PAIR = LANES // B_HEAD_DIM
SWA_PAIRS = B_GROUP_WIDTH // LANES
SWA_QUAD = 4
VMEM_LIMIT = 56 * 1024 * 1024
RWKV_CHUNK = 64
MOE_ROWS = 256
NEG_INF = float("-inf")


def _dot(a, b, precision=None):
    return jnp.dot(a, b, preferred_element_type=F32, precision=precision)


def _dot_nt(a, b, precision=None):
    return lax.dot_general(a, b, (((1,), (1,)), ((), ())), preferred_element_type=F32, precision=precision)


def _dot_tn(a, b, precision=None):
    return lax.dot_general(a, b, (((0,), (0,)), ((), ())), preferred_element_type=F32, precision=precision)


def _bf16_round(t):
    return t.astype(BF16).astype(F32)


_NN = (((1,), (0,)), ((), ()))
_NT = (((1,), (1,)), ((), ()))


def _mm(a, b, exact, dims=_NN):
    dg = lambda x, y: lax.dot_general(x, y, dims, preferred_element_type=F32)
    if not exact:
        return dg(a.astype(BF16), b.astype(BF16))
    a, b = a.astype(F32), b.astype(F32)
    a_hi, b_hi = a.astype(BF16), b.astype(BF16)
    a_lo = (a - a_hi.astype(F32)).astype(BF16)
    b_lo = (b - b_hi.astype(F32)).astype(BF16)
    return dg(a_hi, b_hi) + (dg(a_hi, b_lo) + dg(a_lo, b_hi))


def _contract_round(t, exact):
    return t if exact else _bf16_round(t)


def _rms(x, g):
    return x * lax.rsqrt(jnp.mean(x * x, axis=-1, keepdims=True) + RMS_EPS) * g


def _const_spec(shape):
    nd = len(shape)
    return pl.BlockSpec(shape, lambda *_: (0,) * nd, pipeline_mode=pl.Buffered(1))


def _params(n_axes):
    return pltpu.CompilerParams(dimension_semantics=("arbitrary",) * n_axes, vmem_limit_bytes=VMEM_LIMIT)


def _slab_position(j):
    kind, hs = divmod(j, SWA_SLABS // 3)
    g, p = divmod(hs, SWA_PAIRS)
    return p * (3 * len(B_GROUPS)) + kind * len(B_GROUPS) + g


def _in_proj_kernel(x_ref, g_ref, w_ref, bg_ref, rc_ref, rs1_ref, rs2_ref, ua_ref, qkv_ref, gate_ref, *cache_refs, exact):
    tm = x_ref.shape[0]
    xn = _rms(x_ref[...], g_ref[...])
    xn = xn if exact else xn.astype(BF16)
    mm = lambda w: _mm(xn, w, exact)
    ua_ref[...] = mm(w_ref[:, :A_PROJ])
    rc, rs1, rs2 = rc_ref[...], rs1_ref[...], rs2_ref[...]
    n_rot = 2 * B_WIDTH // LANES
    for j in range(SWA_SLABS):
        lo = A_PROJ + j * LANES
        s = mm(w_ref[:, lo:lo + LANES])
        if j < n_rot:
            s = s * rc + pltpu.roll(s, LANES - ROPE_HALF, 1) * rs1 + pltpu.roll(s, ROPE_HALF, 1) * rs2
        qkv_ref[_slab_position(j)] = s
        kind, hs = divmod(j, SWA_SLABS // 3)
        if cache_refs and kind > 0:
            g, p = divmod(hs, SWA_PAIRS)
            rows = cache_refs[g].shape[2]
            cache_refs[g][0, kind - 1, :, p * LANES:(p + 1) * LANES] = s[tm - rows:, :]
    gate_ref[...] = jax.nn.sigmoid(mm(w_ref[:, A_PROJ + B_PROJ:]) + bg_ref[...])


def _rope_tables(pos):
    inv_freq = ROPE_THETA ** (-jnp.arange(ROPE_HALF, dtype=F32) * 2.0 / ROPE_DIM)
    ang = pos.astype(F32)[:, None] * inv_freq[None, :]
    cos, sin = jnp.cos(ang), jnp.sin(ang)
    n = pos.shape[0]
    rest = B_HEAD_DIM - ROPE_DIM
    c = jnp.concatenate([cos, cos, jnp.ones((n, rest), F32)], axis=1)
    s1 = jnp.concatenate([-sin, jnp.zeros((n, ROPE_HALF + rest), F32)], axis=1)
    s2 = jnp.concatenate([jnp.zeros((n, ROPE_HALF), F32), sin, jnp.zeros((n, rest), F32)], axis=1)
    rep = LANES // B_HEAD_DIM
    return tuple(jnp.tile(t, (1, rep)) for t in (c, s1, s2))


def _in_proj(x, g, w, b_gate, tables, tm, cache_seq=None):
    n = x.shape[0]
    p_rows = tables[0].shape[0]
    t_tiles = p_rows // tm
    in_proj_w = w.shape[1]
    tab_spec = pl.BlockSpec((tm, LANES), lambda i: (i % t_tiles, 0))
    cache_specs, cache_shapes = [], []
    if cache_seq is not None:
        n_seq, s_len = cache_seq
        tiles = s_len // tm
        for window, _ in B_GROUPS:
            keep = min(window, s_len)
            rows = min(tm, keep)
            first_tile = tiles - keep // rows
            cache_specs.append(pl.BlockSpec(
                (1, 2, rows, B_GROUP_WIDTH),
                lambda i, tiles=tiles, first_tile=first_tile: (i // tiles, 0, jnp.maximum(i % tiles - first_tile, 0), 0)))
            cache_shapes.append(jax.ShapeDtypeStruct((n_seq, 2, keep, B_GROUP_WIDTH), F32))
    return pl.pallas_call(
        functools.partial(_in_proj_kernel, exact=w.dtype == F32),
        grid=(n // tm,),
        in_specs=[
            pl.BlockSpec((tm, D_MODEL), lambda i: (i, 0)),
            _const_spec((1, D_MODEL)),
            _const_spec((D_MODEL, in_proj_w)),
            _const_spec((1, 2 * D_MODEL)),
            tab_spec, tab_spec, tab_spec,
        ],
        out_specs=[
            pl.BlockSpec((tm, A_PROJ), lambda i: (i, 0)),
            pl.BlockSpec((SWA_SLABS, tm, LANES), lambda i: (0, i, 0)),
            pl.BlockSpec((tm, 2 * D_MODEL), lambda i: (i, 0)),
        ] + cache_specs,
        out_shape=[
            jax.ShapeDtypeStruct((n, A_PROJ), F32),
            jax.ShapeDtypeStruct((SWA_SLABS, n, LANES), F32),
            jax.ShapeDtypeStruct((n, 2 * D_MODEL), F32),
        ] + cache_shapes,
        compiler_params=_params(1),
        name="in_proj",
    )(x, g.reshape(1, -1), w, b_gate.reshape(1, -1), *tables)


def _norm_matmul_kernel(x_ref, g_ref, w_ref, o_ref):
    o_ref[...] = _dot(_rms(x_ref[...], g_ref[...]).astype(BF16), w_ref[...])


def _norm_matmul(x, g, w_bf16, tm):
    n, d = x.shape
    dout = w_bf16.shape[1]
    return pl.pallas_call(
        _norm_matmul_kernel,
        grid=(n // tm,),
        in_specs=[pl.BlockSpec((tm, d), lambda i: (i, 0)), _const_spec((1, d)), _const_spec((d, dout))],
        out_specs=pl.BlockSpec((tm, dout), lambda i: (i, 0)),
        out_shape=jax.ShapeDtypeStruct((n, dout), F32),
        compiler_params=_params(1),
        name="norm_matmul",
    )(x, g.reshape(1, -1), w_bf16)


RWKV_HEADS_PER_PACK = 4
RWKV_PACK_WIDTH = RWKV_HEADS_PER_PACK * A_HEAD_DIM
RWKV_PACKS = A_HEADS // RWKV_HEADS_PER_PACK
RWKV_STEP_SEQS = 1


def _shift_rows(u, first_prev):
    row = lax.broadcasted_iota(jnp.int32, (u.shape[0], 1), 0)
    return jnp.where(row == 0, first_prev, pltpu.roll(u, 1, 0))


def _head_sum(x, bd, exact, pieces=1):
    ones = bd.astype(BF16)
    total, rest = None, x
    for _ in range(3 if exact else pieces):
        piece = rest.astype(BF16)
        rest = rest - piece.astype(F32)
        part = _dot(piece, ones)
        total = part if total is None else total + part
    return total


def _rwkv_features(u, u_prev, w_refs, bd, exact):
    mu_ref, w0_ref, w2_ref, a0_ref, a2_ref, g2_ref, kk_ref, ka_ref = w_refs
    um = u + (u_prev - u) * mu_ref[...]
    o1, o2, o3 = A_WIDTH, 2 * A_WIDTH, 3 * A_WIDTH
    o4 = o3 + A_DECAY_LORA
    o5 = o4 + A_ICLR_LORA
    r, k, v = um[:, :o1], um[:, o1:o2], um[:, o2:o3]
    xw, xa, xg = um[:, o3:o4], um[:, o4:o5], um[:, o5:]
    lora = lambda t, w_ref: _mm(t, w_ref[...], exact)
    w = -jax.nn.softplus(-(w0_ref[...] + lora(jnp.tanh(xw), w2_ref))) - 0.5
    e = jnp.exp(w)
    a = jax.nn.sigmoid(a0_ref[...] + lora(xa, a2_ref))
    g = lora(jax.nn.sigmoid(xg), g2_ref)
    kk = k * kk_ref[...]
    kkn = kk / jnp.maximum(jnp.sqrt(_head_sum(kk * kk, bd, exact, pieces=2)), 1e-12)
    k2 = k * (1.0 + (a - 1.0) * ka_ref[...])
    return r, k2, v, e, a, g, kkn


def _rwkv_output(y, r, k2, v, g, rk_ref, lng_ref, lnb_ref, bd, exact):
    inv_n = 1.0 / A_HEAD_DIM
    mean = _head_sum(y, bd, exact) * inv_n
    yc = y - mean
    var = _head_sum(yc * yc, bd, exact) * inv_n
    yn = yc * lax.rsqrt(var + A_LNX_EPS) * lng_ref[...] + lnb_ref[...]
    bonus = _head_sum(r * k2 * rk_ref[...], bd, exact) * v
    return (yn + bonus) * g


def _rwkv_chunk_kernel(u_ref, sh0_ref, s0_ref, mu_ref, w0_ref, w2_ref, a0_ref, a2_ref, g2_ref, kk_ref, ka_ref,
                       rk_ref, lng_ref, lnb_ref, bd_ref, y_ref, sfin_ref, shout_ref, st_scr, prev_scr, *, n_steps):
    c = pl.program_id(0)
    n_b, chunk, _ = u_ref.shape
    hd, hpp, pw_ = A_HEAD_DIM, RWKV_HEADS_PER_PACK, RWKV_PACK_WIDTH
    bf = lambda t: t.astype(BF16)

    @pl.when(c == 0)
    def _():
        for b in range(n_b):
            prev_scr[b:b + 1, :] = sh0_ref[b]
            for p in range(RWKV_PACKS):
                st_scr[b, p] = jnp.concatenate([s0_ref[b, p * hpp + h] for h in range(hpp)], axis=1)

    u_b = [u_ref[b] for b in range(n_b)]
    u = jnp.concatenate(u_b, axis=0)
    u_prev = jnp.concatenate([_shift_rows(u_b[b], prev_scr[b:b + 1, :]) for b in range(n_b)], axis=0)
    for b in range(n_b):
        prev_scr[b:b + 1, :] = u_b[b][chunk - 1:chunk, :]
    bd = bd_ref[...]
    r, k2, v, e, a, g, kkn = _rwkv_features(
        u, u_prev, (mu_ref, w0_ref, w2_ref, a0_ref, a2_ref, g2_ref, kk_ref, ka_ref), bd, False)
    b_ = kkn * a

    li = lax.broadcasted_iota(jnp.int32, (chunk, chunk), 0)
    lj = lax.broadcasted_iota(jnp.int32, (chunk, chunk), 1)
    tri = bf((li >= lj).astype(F32))
    e_hi = bf(e)
    e_rest = e - e_hi.astype(F32)
    e_mid = bf(e_rest)
    e_lo = bf(e_rest - e_mid.astype(F32))
    cums, ends = [], []
    for b in range(n_b):
        rs = slice(b * chunk, (b + 1) * chunk)
        cb = _dot(tri, e_hi[rs]) + (_dot(tri, e_mid[rs]) + _dot(tri, e_lo[rs]))
        cums.append(cb)
        ends.append(jnp.broadcast_to(cb[chunk - 1:chunk, :], (chunk, A_WIDTH)))
    cum = jnp.concatenate(cums, axis=0)
    cum_end = jnp.concatenate(ends, axis=0)
    grow = jnp.exp(cum)
    to_end = jnp.exp(cum - cum_end)
    at = bf(-kkn * jnp.exp(e - cum))
    rt = bf(r * jnp.exp(-cum))
    bt = bf(b_ * grow)
    kt = bf(k2 * grow)
    bh = bf(b_ * to_end)
    kh = bf(k2 * to_end)
    vb = bf(v)
    dec_end = jnp.exp(-cum_end)

    lane_head = lax.broadcasted_iota(jnp.int32, (1, pw_), 1) // hd
    head_mask = [lane_head == h for h in range(hpp)]

    def block_diag(x):
        return jnp.concatenate([jnp.where(head_mask[h], x, jnp.zeros_like(x)) for h in range(hpp)], axis=0)

    assert chunk == hd
    ti = lax.broadcasted_iota(jnp.int32, (chunk, pw_), 0)
    tj = lax.broadcasted_iota(jnp.int32, (chunk, pw_), 1) % chunk
    strict = ti > tj
    incl = ti >= tj
    eye = (ti == tj).astype(F32)
    n_sq = int(math.log2(chunk)) - 1

    streams = [(b, p, slice(b * chunk, (b + 1) * chunk), slice(p * pw_, (p + 1) * pw_))
               for b in range(n_b) for p in range(RWKV_PACKS)]
    at_s = [at[rs, cs] for _, _, rs, cs in streams]
    rt_s = [rt[rs, cs] for _, _, rs, cs in streams]
    v_s = [vb[rs, cs] for _, _, rs, cs in streams]
    m = [_dot_nt(jnp.concatenate([a_, r_], axis=0),
                 jnp.concatenate([block_diag(bt[rs, cs]), block_diag(kt[rs, cs])], axis=0))
         for a_, r_, (_, _, rs, cs) in zip(at_s, rt_s, streams)]
    a_ab = [jnp.where(strict, x[:chunk, :pw_], 0.0) for x in m]
    akv = [_dot(bf(jnp.where(strict, x[:chunk, pw_:], 0.0)), block_diag(vs)) for x, vs in zip(m, v_s)]
    m_r = [bf(jnp.concatenate([jnp.where(incl, x[chunk:, :pw_], 0.0), jnp.where(incl, x[chunk:, pw_:], 0.0)], axis=1))
           for x in m]
    tinv = [eye + x for x in a_ab]
    pw = a_ab
    for _ in range(n_sq):
        pw = [_dot(bf(x), block_diag(bf(x))) for x in pw]
        tinv = [t + _dot(bf(t), block_diag(bf(x))) for t, x in zip(tinv, pw)]
    w12 = [_dot(bf(t), jnp.concatenate([block_diag(a_), block_diag(bf(x))], axis=1))
           for t, a_, x in zip(tinv, at_s, akv)]

    st = [st_scr[b, p] for b, p, _, _ in streams]
    x = [_dot_nt(jnp.concatenate([bf(w[:, :pw_]), r_], axis=0), block_diag(bf(s))) for w, r_, s in zip(w12, rt_s, st)]
    uu = [bf(xi[:chunk] + w[:, pw_:]) for xi, w in zip(x, w12)]
    ys = [xi[chunk:] + _dot(mr, jnp.concatenate([block_diag(ui), block_diag(vs)], axis=0))
          for xi, mr, ui, vs in zip(x, m_r, uu, v_s)]
    for i, (b, p, rs, cs) in enumerate(streams):
        upd = _dot_tn(jnp.concatenate([uu[i], v_s[i]], axis=0), jnp.concatenate([bh[rs, cs], kh[rs, cs]], axis=0))
        diag = functools.reduce(lambda s_, t_: s_ + t_,
                                [jnp.where(head_mask[h], upd[h * hd:(h + 1) * hd, :], 0.0) for h in range(hpp)])
        st_scr[b, p] = st[i] * dec_end[rs.start:rs.start + 1, cs] + diag
    y = jnp.concatenate([jnp.concatenate(ys[b * RWKV_PACKS:(b + 1) * RWKV_PACKS], axis=1) for b in range(n_b)], axis=0)
    out = _rwkv_output(y, r, k2, v, g, rk_ref, lng_ref, lnb_ref, bd, False)
    for b in range(n_b):
        y_ref[b] = out[b * chunk:(b + 1) * chunk]

    @pl.when(c == n_steps - 1)
    def _():
        for b in range(n_b):
            shout_ref[b] = u_b[b][chunk - 1:chunk, :]
            for p in range(RWKV_PACKS):
                for h in range(hpp):
                    sfin_ref[b, p * hpp + h] = st_scr[b, p][:, h * hd:(h + 1) * hd]


def _rwkv_step_kernel(u_ref, sh0_ref, s0_ref, mu_ref, w0_ref, w2_ref, a0_ref, a2_ref, g2_ref, kk_ref, ka_ref,
                      rk_ref, lng_ref, lnb_ref, bd_ref, y_ref, sfin_ref, shout_ref, *, t_valid, exact):
    hd = A_HEAD_DIM
    n_b, n_rows, _ = u_ref.shape
    u_b = [u_ref[b] for b in range(n_b)]
    u = jnp.concatenate(u_b, axis=0)
    u_prev = jnp.concatenate([_shift_rows(u_b[b], sh0_ref[b]) for b in range(n_b)], axis=0)
    bd = bd_ref[...]
    r, k2, v, e, a, g, kkn = _rwkv_features(
        u, u_prev, (mu_ref, w0_ref, w2_ref, a0_ref, a2_ref, g2_ref, kk_ref, ka_ref), bd, exact)
    rnd = lambda t: _contract_round(t, exact)
    decay = jnp.exp(-e)
    b_ = kkn * a
    eye = (lax.broadcasted_iota(jnp.int32, (hd, hd), 0) == lax.broadcasted_iota(jnp.int32, (hd, hd), 1)).astype(F32)
    to_col = lambda t: jnp.sum(eye * t, axis=1, keepdims=True)
    to_row = lambda t: jnp.sum(eye * t, axis=0, keepdims=True)
    y_seq = []
    for b in range(n_b):
        y_heads = []
        for h in range(A_HEADS):
            sl = slice(h * hd, (h + 1) * hd)
            s = s0_ref[b, h]
            y_rows = []
            for t in range(t_valid):
                tt = slice(b * n_rows + t, b * n_rows + t + 1)
                sa = jnp.sum(rnd(s) * rnd(-kkn[tt, sl]), axis=1, keepdims=True)
                s = s * decay[tt, sl] + sa * b_[tt, sl] + to_col(v[tt, sl]) * k2[tt, sl]
                y_rows.append(to_row(jnp.sum(rnd(s) * rnd(r[tt, sl]), axis=1, keepdims=True)))
            sfin_ref[b, h] = s
            y_rows.append(jnp.zeros((n_rows - t_valid, hd), F32))
            y_heads.append(jnp.concatenate(y_rows, axis=0))
        y_seq.append(jnp.concatenate(y_heads, axis=1))
    y = jnp.concatenate(y_seq, axis=0)
    out = _rwkv_output(y, r, k2, v, g, rk_ref, lng_ref, lnb_ref, bd, exact)
    for b in range(n_b):
        y_ref[b] = out[b * n_rows:(b + 1) * n_rows]
        shout_ref[b] = u_b[b][t_valid - 1:t_valid, :]


def _rwkv(u_a, shift0, s0, p, chunk, t_valid):
    bsz, t_len, _ = u_a.shape
    hd = A_HEAD_DIM
    bd = jnp.asarray(np.kron(np.eye(A_HEADS, dtype=np.float32), np.ones((hd, hd), np.float32)))
    row = lambda t: t.reshape(1, -1)
    if chunk:
        assert t_valid == t_len and t_len % chunk == 0
        grid = (t_len // chunk,)
        kern = functools.partial(_rwkv_chunk_kernel, n_steps=grid[0])
        scratch = [pltpu.VMEM((bsz, RWKV_PACKS, hd, RWKV_PACK_WIDTH), F32), pltpu.VMEM((bsz, A_PROJ), F32)]
        u_spec = pl.BlockSpec((bsz, chunk, A_PROJ), lambda c: (0, c, 0))
        y_spec = pl.BlockSpec((bsz, chunk, A_WIDTH), lambda c: (0, c, 0))
        state_spec = pl.BlockSpec((bsz, A_HEADS, hd, hd), lambda c: (0, 0, 0, 0))
        shift_spec = pl.BlockSpec((bsz, 1, A_PROJ), lambda c: (0, 0, 0))
    else:
        n_b = math.gcd(bsz, RWKV_STEP_SEQS)
        grid = (bsz // n_b,)
        kern = functools.partial(_rwkv_step_kernel, t_valid=t_valid, exact=True)
        scratch = []
        u_spec = pl.BlockSpec((n_b, t_len, A_PROJ), lambda b: (b, 0, 0))
        y_spec = pl.BlockSpec((n_b, t_len, A_WIDTH), lambda b: (b, 0, 0))
        state_spec = pl.BlockSpec((n_b, A_HEADS, hd, hd), lambda b: (b, 0, 0, 0))
        shift_spec = pl.BlockSpec((n_b, 1, A_PROJ), lambda b: (b, 0, 0))
    return pl.pallas_call(
        kern,
        grid=grid,
        in_specs=[
            u_spec, shift_spec, state_spec,
            _const_spec((1, A_PROJ)), _const_spec((1, A_WIDTH)), _const_spec((A_DECAY_LORA, A_WIDTH)),
            _const_spec((1, A_WIDTH)), _const_spec((A_ICLR_LORA, A_WIDTH)), _const_spec((A_GATE_LORA, A_WIDTH)),
            _const_spec((1, A_WIDTH)), _const_spec((1, A_WIDTH)), _const_spec((1, A_WIDTH)),
            _const_spec((1, A_WIDTH)), _const_spec((1, A_WIDTH)), _const_spec((A_WIDTH, A_WIDTH)),
        ],
        out_specs=[y_spec, state_spec, shift_spec],
        out_shape=[
            jax.ShapeDtypeStruct((bsz, t_len, A_WIDTH), F32),
            jax.ShapeDtypeStruct((bsz, A_HEADS, hd, hd), F32),
            jax.ShapeDtypeStruct((bsz, 1, A_PROJ), F32),
        ],
        scratch_shapes=scratch,
        compiler_params=_params(1),
        name="rwkv7",
    )(u_a, shift0.reshape(bsz, 1, A_PROJ), s0, row(p['rwkv_mu']), row(p['rwkv_w0']), p['rwkv_w2'],
      row(p['rwkv_a0']), p['rwkv_a2'], p['rwkv_g2'], row(p['rwkv_k_k']), row(p['rwkv_k_a']),
      row(p['rwkv_r_k']), row(p['rwkv_lnx_g']), row(p['rwkv_lnx_b']), bd)


def _swa_prompt_kernel(qkv_ref, o_ref, lse_ref):
    blk, hd, n_g = SWA_BLOCK, B_HEAD_DIM, len(B_GROUPS)
    s_len = qkv_ref.shape[1]
    scale = hd ** -0.5
    rows2 = SWA_PAIR * blk
    r_i = lax.broadcasted_iota(jnp.int32, (rows2, 1), 0)
    qi = r_i % blk
    own_lanes = (lax.broadcasted_iota(jnp.int32, (1, LANES), 1) // hd) == (r_i // blk)
    head0_lanes = own_lanes[:blk]
    ki2 = lax.broadcasted_iota(jnp.int32, (rows2, 2 * blk), 1)
    band2 = (qi + blk - ki2 >= 0) & (qi - ki2 <= 0)
    causal1 = lax.broadcasted_iota(jnp.int32, (rows2, blk), 1) <= qi

    def attend(g, specs):
        dil = B_GROUPS[g][1]

        def rows(kind, st):
            idx = pl.ds(st, blk, stride=dil) if dil > 1 else pl.ds(st, blk)
            return qkv_ref[kind * n_g + g, idx, :]

        qs, kbs, vbs, masks = [], [], [], []
        for st, prev, first in specs:
            q = rows(0, st)
            qs.append(jnp.where(own_lanes, jnp.concatenate([q] * SWA_PAIR, axis=0), 0.0).astype(BF16))
            if prev is None:
                kbs.append(rows(1, st).astype(BF16))
                vbs.append(rows(2, st).astype(BF16))
                masks.append(causal1)
            else:
                kbs.append(jnp.concatenate([rows(1, prev), rows(1, st)], axis=0).astype(BF16))
                vbs.append(jnp.concatenate([rows(2, prev), rows(2, st)], axis=0).astype(BF16))
                masks.append(band2 & (ki2 >= jnp.where(first, blk, 0)))
        scores = [_dot_nt(q, kb) * scale for q, kb in zip(qs, kbs)]
        probs, lses = [], []
        for sc, mk in zip(scores, masks):
            sc = jnp.where(mk, sc, NEG_INF)
            m = jnp.max(sc, axis=-1, keepdims=True)
            p = jnp.exp(sc - m)
            den = jnp.sum(p, axis=-1, keepdims=True)
            probs.append((p / den).astype(BF16))
            lses.append(m + jnp.log(den))
        outs = [_dot(p, vb) for p, vb in zip(probs, vbs)]
        for (st, _, _), o2, l2 in zip(specs, outs, lses):
            idx = pl.ds(st, blk, stride=dil) if dil > 1 else pl.ds(st, blk)
            o_ref[g, idx, :] = jnp.where(head0_lanes, o2[:blk], o2[blk:])
            lse_ref[g, idx, :] = jnp.where(head0_lanes, l2[:blk], l2[blk:])

    for g, (_, dil) in enumerate(B_GROUPS):
        n_blk = s_len // dil // blk
        n_quads = dil * n_blk // SWA_QUAD

        def quad(it, carry, g=g, dil=dil, n_blk=n_blk):
            specs = []
            for j in range(SWA_QUAD):
                if n_blk >= SWA_QUAD:
                    e = it * SWA_QUAD + j
                    r, n = e // n_blk, e % n_blk
                    specs.append((n * (blk * dil) + r, jnp.maximum(n - 1, 0) * (blk * dil) + r, n == 0))
                else:
                    r = it * (SWA_QUAD // n_blk) + j // n_blk
                    n = j % n_blk
                    specs.append((n * (blk * dil) + r, None if n == 0 else (n - 1) * (blk * dil) + r, False))
            attend(g, specs)
            return carry

        lax.fori_loop(0, n_quads, quad, 0)


def _swa_prompt(qkv_slabs, n_seq, s_len):
    n_g = len(B_GROUPS)
    n = n_seq * s_len
    out_spec = pl.BlockSpec((n_g, s_len, LANES), lambda b, p: (p, b, 0))
    shp = jax.ShapeDtypeStruct((SWA_PAIRS * n_g, n, LANES), F32)
    return pl.pallas_call(
        _swa_prompt_kernel,
        grid=(n_seq, SWA_PAIRS),
        in_specs=[pl.BlockSpec((3 * n_g, s_len, LANES), lambda b, p: (p, b, 0))],
        out_specs=[out_spec, out_spec],
        out_shape=[shp, shp],
        compiler_params=_params(2),
        name="swa_prompt",
    )(qkv_slabs)


def _swa_sample_kernel(qkv_ref, c0_ref, c1_ref, c2_ref, o_ref, lse_ref, n0_ref, n1_ref, n2_ref, *, t_len, exact):
    hd = B_HEAD_DIM
    nh = B_HEADS_PER_GROUP
    gw = B_GROUP_WIDTH
    qkv = qkv_ref[0]
    scale = hd ** -0.5
    rnd = lambda t: _contract_round(t, exact)
    t_col = lax.broadcasted_iota(jnp.int32, (t_len, 1), 0)
    i_n = lax.broadcasted_iota(jnp.int32, (t_len, t_len), 1)
    units = []
    for gi, ((window, dil), c_ref, n_ref) in enumerate(zip(B_GROUPS, (c0_ref, c1_ref, c2_ref), (n0_ref, n1_ref, n2_ref))):
        buf_len = c_ref.shape[4]
        j_c = lax.broadcasted_iota(jnp.int32, (t_len, buf_len), 1)
        ok_c = (j_c >= t_col) & (((j_c - t_col) & (dil - 1)) == 0)
        ok_n = (i_n <= t_col) & (((t_col - i_n) & (dil - 1)) == 0)
        for h in range(nh):
            lo = gi * gw + h * hd
            units.append(dict(
                q=qkv[:, lo:lo + hd], k_new=qkv[:, B_WIDTH + lo:B_WIDTH + lo + hd],
                v_new=qkv[:, 2 * B_WIDTH + lo:2 * B_WIDTH + lo + hd],
                kt=c_ref[0, 0, h], vt=c_ref[0, 1, h], ok_c=ok_c, ok_n=ok_n))
    s_cs = [jnp.where(u["ok_c"], _mm(u["q"], u["kt"], exact) * scale, NEG_INF) for u in units]
    probs, p_news, lses = [], [], []
    for u, s_c in zip(units, s_cs):
        qr, kr = rnd(u["q"]), rnd(u["k_new"])
        s_n = jnp.zeros((t_len, t_len), F32)
        for i in range(t_len):
            s_n = jnp.where(i_n == i, jnp.sum(qr * kr[i:i + 1, :], axis=-1, keepdims=True) * scale, s_n)
        s_n = jnp.where(u["ok_n"], s_n, NEG_INF)
        m = jnp.maximum(jnp.max(s_c, axis=-1, keepdims=True), jnp.max(s_n, axis=-1, keepdims=True))
        p_c = jnp.exp(s_c - m)
        p_n = jnp.exp(s_n - m)
        den = jnp.sum(p_c, axis=-1, keepdims=True) + jnp.sum(p_n, axis=-1, keepdims=True)
        probs.append(p_c / den)
        p_news.append(rnd(p_n / den))
        lses.append(jnp.broadcast_to(m + jnp.log(den), (t_len, hd)))
    outs = [_mm(p, u["vt"], exact, _NT) for p, u in zip(probs, units)]
    for k, (u, p_n) in enumerate(zip(units, p_news)):
        vr = rnd(u["v_new"])
        for i in range(t_len):
            outs[k] = outs[k] + p_n[:, i:i + 1] * vr[i:i + 1, :]
    pad_rows = 8 - t_len
    place = (lax.broadcasted_iota(jnp.int32, (8, LANES), 0) + (LANES - t_len)
             == lax.broadcasted_iota(jnp.int32, (8, LANES), 1)).astype(BF16)
    tail_lane = lax.broadcasted_iota(jnp.int32, (1, LANES), 1) >= LANES - t_len
    for gi, (c_ref, n_ref) in enumerate(zip((c0_ref, c1_ref, c2_ref), (n0_ref, n1_ref, n2_ref))):
        buf_len = c_ref.shape[4]
        for j in range(2):
            new = jnp.pad(qkv[:, (j + 1) * B_WIDTH + gi * gw:(j + 1) * B_WIDTH + (gi + 1) * gw], ((0, pad_rows), (0, 0)))
            cols, rest = None, new
            for _ in range(3):
                piece = rest.astype(BF16)
                rest = rest - piece.astype(F32)
                part = _dot_tn(piece, place)
                cols = part if cols is None else cols + part
            nxt = pltpu.roll(c_ref[0, j].reshape(gw, buf_len), buf_len - t_len, 1)
            tail = jnp.where(tail_lane, cols, nxt[:, buf_len - LANES:])
            full = tail if buf_len == LANES else jnp.concatenate([nxt[:, :buf_len - LANES], tail], axis=1)
            n_ref[0, j] = full.reshape(nh, hd, buf_len)
    o_ref[0] = jnp.concatenate(outs, axis=1)
    lse_ref[0] = jnp.concatenate(lses, axis=1)


def _swa_sample(qkv, caches):
    bsz, t_len, _ = qkv.shape
    cache_specs = [pl.BlockSpec((1,) + c.shape[1:], lambda b: (b, 0, 0, 0, 0)) for c in caches]
    row_spec = pl.BlockSpec((1, t_len, B_WIDTH), lambda b: (b, 0, 0))
    return pl.pallas_call(
        functools.partial(_swa_sample_kernel, t_len=t_len, exact=True),
        grid=(bsz,),
        in_specs=[pl.BlockSpec((1, t_len, B_PROJ), lambda b: (b, 0, 0))] + cache_specs,
        out_specs=[row_spec, row_spec] + cache_specs,
        out_shape=[jax.ShapeDtypeStruct((bsz, t_len, B_WIDTH), F32)] * 2
        + [jax.ShapeDtypeStruct(c.shape, F32) for c in caches],
        compiler_params=_params(1),
        name="swa_sample",
    )(qkv, *caches)


def _mid_kernel(x_ref, ya_ref, ob_ref, lse_ref, gate_ref, mk_ref, mv_ref, wa_ref, wb_ref, wo_ref, gm_ref,
                wq_ref, wmo_ref, gf_ref, wr_ref, br_ref, xn_all_ref, x2_ref, xn_ref, route_ref, *, rows_per_batch, exact):
    del xn_all_ref
    tm = x_ref.shape[0]
    n_g = len(B_GROUPS)
    rnd = lambda t: _contract_round(t, exact)
    mm = lambda a, b, dims=_NN: _mm(a, b, exact, dims)
    yb_pairs = []
    for p in range(SWA_PAIRS):
        lses = [lse_ref[p * n_g + g] for g in range(n_g)]
        m = functools.reduce(jnp.maximum, lses)
        es = [jnp.exp(l - m) for l in lses]
        den = functools.reduce(lambda a, b: a + b, es)
        yb_pairs.append(functools.reduce(lambda a, b: a + b,
                                         [rnd(es[g] / den) * rnd(ob_ref[p * n_g + g]) for g in range(n_g)]))
    yb = jnp.concatenate(yb_pairs, axis=1)
    gates = gate_ref[...]
    merged = (gates[:, :D_MODEL] * mm(ya_ref[...], wa_ref[...])
              + gates[:, D_MODEL:] * mm(yb, wb_ref[...]))
    x1 = x_ref[...] + mm(merged, wo_ref[...])

    q = mm(_rms(x1, gm_ref[...]), wq_ref[...])
    n_b = mk_ref.shape[0]
    mk = mk_ref[...].reshape(n_b * N_MEM, MEM_WIDTH)
    mv = mv_ref[...].reshape(n_b * N_MEM, MEM_WIDTH)
    if not exact:
        mk, mv = mk.astype(BF16), mv.astype(BF16)
    if n_b > 1:
        rb = lax.broadcasted_iota(jnp.int32, (tm, n_b * N_MEM), 0) // rows_per_batch
        cb = lax.broadcasted_iota(jnp.int32, (tm, n_b * N_MEM), 1) // N_MEM
        same = rb == cb
    heads = []
    for h in range(MEM_HEADS):
        sl = slice(h * MEM_HEAD_DIM, (h + 1) * MEM_HEAD_DIM)
        s = mm(q[:, sl], mk[:, sl], _NT) * (MEM_HEAD_DIM ** -0.5)
        if n_b > 1:
            s = jnp.where(same, s, NEG_INF)
        s = s - jnp.max(s, axis=-1, keepdims=True)
        p = jnp.exp(s)
        heads.append(mm(p / jnp.sum(p, axis=-1, keepdims=True), mv[:, sl]))
    x2 = x1 + mm(jnp.concatenate(heads, axis=1), wmo_ref[...])
    x2_ref[...] = x2

    xn = _rms(x2, gf_ref[...])
    xn_ref[...] = xn
    logits = mm(xn, wr_ref[...]) + br_ref[...]
    lane = lax.broadcasted_iota(jnp.int32, (tm, LANES), 1)
    gl = jnp.where(lane < N_GROUPS, logits, NEG_INF)
    gmax = jnp.max(gl, axis=-1, keepdims=True)
    grp = jnp.min(jnp.where(gl == gmax, lane, LANES), axis=-1, keepdims=True)
    w_grp = 1.0 / jnp.sum(jnp.exp(gl - gmax), axis=-1, keepdims=True)
    first = N_GROUPS + grp * EXPERTS_PER_GROUP
    el = jnp.where((lane >= first) & (lane < first + EXPERTS_PER_GROUP), logits, NEG_INF)
    m1 = jnp.max(el, axis=-1, keepdims=True)
    i1 = jnp.min(jnp.where(el == m1, lane, LANES), axis=-1, keepdims=True)
    el2 = jnp.where(lane == i1, NEG_INF, el)
    m2 = jnp.max(el2, axis=-1, keepdims=True)
    i2 = jnp.min(jnp.where(el2 == m2, lane, LANES), axis=-1, keepdims=True)
    e2 = jnp.exp(m2 - m1)
    g1 = w_grp / (1.0 + e2)
    g2 = w_grp * e2 / (1.0 + e2)
    route = jnp.where(lane == 0, (i1 - N_GROUPS).astype(F32), 0.0)
    route = jnp.where(lane == 1, (i2 - N_GROUPS).astype(F32), route)
    route = jnp.where(lane == 2, g1, route)
    route_ref[...] = jnp.where(lane == 3, g2, route)


def _mid(x, y_a, o_b, lse_b, gates, mem_k, mem_v, mk_map, mv_map, n_b, rows_per_batch, w, tm, xn_all, row0):
    n = x.shape[0]
    row = lambda width: pl.BlockSpec((tm, width), lambda i: (i, 0))
    slabs = pl.BlockSpec((SWA_PAIRS * len(B_GROUPS), tm, LANES), lambda i: (0, i, 0))
    in_specs = [
        row(D_MODEL), row(A_WIDTH), slabs, slabs, row(2 * D_MODEL),
        pl.BlockSpec((n_b, N_MEM, MEM_WIDTH), mk_map), pl.BlockSpec((n_b, N_MEM, MEM_WIDTH), mv_map),
        _const_spec((A_WIDTH, D_MODEL)), _const_spec((B_GROUP_WIDTH, D_MODEL)), _const_spec((D_MODEL, D_MODEL)),
        _const_spec((1, D_MODEL)), _const_spec((D_MODEL, MEM_WIDTH)), _const_spec((MEM_WIDTH, D_MODEL)),
        _const_spec((1, D_MODEL)), _const_spec((D_MODEL, LANES)), _const_spec((1, LANES)),
        pl.BlockSpec(memory_space=pl.ANY),
    ]
    args = [x, y_a, o_b, lse_b, gates, mem_k, mem_v, w['wa'], w['wb'], w['wo'], w['gm'], w['wq'], w['wmo'],
            w['gf'], w['wr'], w['br'], xn_all]
    blk0 = row0 // tm
    return pl.pallas_call(
        functools.partial(_mid_kernel, rows_per_batch=rows_per_batch, exact=w['wa'].dtype == F32),
        grid=(n // tm,),
        in_specs=in_specs,
        out_specs=[row(D_MODEL), pl.BlockSpec((tm, D_MODEL), lambda i: (i + blk0, 0)), row(LANES)],
        out_shape=[
            jax.ShapeDtypeStruct((n, D_MODEL), F32),
            jax.ShapeDtypeStruct(xn_all.shape, F32),
            jax.ShapeDtypeStruct((n, LANES), F32),
        ],
        input_output_aliases={len(args) - 1: 1},
        compiler_params=_params(1),
        name="mid",
    )(*args)


def _row_copy(src_hbm, idx, dst_buf, slot, j, sem):
    return pltpu.make_async_copy(src_hbm.at[pl.ds(idx, 1), :], dst_buf.at[slot, pl.ds(j, 1), :], sem.at[slot])


def _gather_start(idx_ref, src_hbm, dst_buf, slot, sem, n_rows):
    for j in range(n_rows):
        _row_copy(src_hbm, idx_ref[0, 0, j], dst_buf, slot, j, sem).start()


def _gather_wait(src_hbm, dst_buf, slot, sem, n_rows):
    for j in range(n_rows):
        _row_copy(src_hbm, 0, dst_buf, slot, j, sem).wait()


def _experts_kernel(meta_ref, be_ref, idx_ref, idx_next_ref, x_hbm, wg_ref, wu_ref, wd_ref, o_ref, xbuf, sem):
    i = pl.program_id(0)
    n_used = meta_ref[0]
    slot = i % 2

    @pl.when(i == 0)
    def _():
        _gather_start(idx_ref, x_hbm, xbuf, 0, sem, MOE_ROWS)

    @pl.when(i + 1 < n_used)
    def _():
        _gather_start(idx_next_ref, x_hbm, xbuf, 1 - slot, sem, MOE_ROWS)

    @pl.when(i < n_used)
    def _():
        _gather_wait(x_hbm, xbuf, slot, sem, MOE_ROWS)
        xb = xbuf[slot].astype(BF16)
        hg = _dot(xb, wg_ref[0].astype(BF16))
        hu = _dot(xb, wu_ref[0].astype(BF16))
        hh = (jax.nn.silu(hg) * hu).astype(BF16)
        o_ref[...] = _dot(hh, wd_ref[0].astype(BF16))

    @pl.when(i >= n_used)
    def _():
        o_ref[...] = jnp.zeros_like(o_ref)


def _experts(xn_all, row_tok, block_e, n_used, w_gate, w_up, w_down):
    n_blocks = block_e.shape[0]
    idx3 = row_tok.reshape(n_blocks, 1, MOE_ROWS)
    idx_spec = lambda f: pl.BlockSpec((1, 1, MOE_ROWS), f, memory_space=pltpu.SMEM)
    grid_spec = pltpu.PrefetchScalarGridSpec(
        num_scalar_prefetch=2,
        grid=(n_blocks,),
        in_specs=[
            idx_spec(lambda i, meta, be: (i, 0, 0)),
            idx_spec(lambda i, meta, be: (jnp.minimum(i + 1, n_blocks - 1), 0, 0)),
            pl.BlockSpec(memory_space=pl.ANY),
            pl.BlockSpec((1, D_MODEL, EXPERT_FF), lambda i, meta, be: (be[i], 0, 0)),
            pl.BlockSpec((1, D_MODEL, EXPERT_FF), lambda i, meta, be: (be[i], 0, 0)),
            pl.BlockSpec((1, EXPERT_FF, D_MODEL), lambda i, meta, be: (be[i], 0, 0)),
        ],
        out_specs=pl.BlockSpec((MOE_ROWS, D_MODEL), lambda i, meta, be: (i, 0)),
        scratch_shapes=[pltpu.VMEM((2, MOE_ROWS, D_MODEL), F32), pltpu.SemaphoreType.DMA((2,))],
    )
    return pl.pallas_call(
        _experts_kernel,
        grid_spec=grid_spec,
        out_shape=jax.ShapeDtypeStruct((n_blocks * MOE_ROWS, D_MODEL), F32),
        compiler_params=_params(1),
        name="experts",
    )(n_used.reshape(1), block_e, idx3, idx3, xn_all, w_gate, w_up, w_down)


def _combine_kernel(pos_ref, pos_next_ref, x_ref, route_ref, yb_hbm, g_ref, o_ref, ybuf, sem, *, n_tiles):
    i = pl.program_id(0)
    tm = x_ref.shape[0]
    slot = i % 2

    @pl.when(i == 0)
    def _():
        _gather_start(pos_ref, yb_hbm, ybuf, 0, sem, 2 * tm)

    if n_tiles > 1:
        @pl.when(i + 1 < n_tiles)
        def _():
            _gather_start(pos_next_ref, yb_hbm, ybuf, 1 - slot, sem, 2 * tm)

    _gather_wait(yb_hbm, ybuf, slot, sem, 2 * tm)
    route = route_ref[...]
    y = x_ref[...] + (route[:, 2:3] * ybuf[slot, :tm, :] + route[:, 3:4] * ybuf[slot, tm:, :])
    o_ref[...] = _rms(y, g_ref[...])


def _combine(x2, route, pos, yb, g_final, tm):
    n = x2.shape[0]
    n_tiles = n // tm
    pos_spec = lambda f: pl.BlockSpec((1, 1, 2 * tm), f, memory_space=pltpu.SMEM)
    return pl.pallas_call(
        functools.partial(_combine_kernel, n_tiles=n_tiles),
        grid=(n_tiles,),
        in_specs=[
            pos_spec(lambda i: (i, 0, 0)),
            pos_spec(lambda i: (jnp.minimum(i + 1, n_tiles - 1), 0, 0)),
            pl.BlockSpec((tm, D_MODEL), lambda i: (i, 0)),
            pl.BlockSpec((tm, LANES), lambda i: (i, 0)),
            pl.BlockSpec(memory_space=pl.ANY),
            _const_spec((1, D_MODEL)),
        ],
        out_specs=pl.BlockSpec((tm, D_MODEL), lambda i: (i, 0)),
        out_shape=jax.ShapeDtypeStruct((n, D_MODEL), F32),
        scratch_shapes=[pltpu.VMEM((2, 2 * tm, D_MODEL), F32), pltpu.SemaphoreType.DMA((2,))],
        compiler_params=_params(1),
        name="combine",
    )(pos, pos, x2, route, yb, g_final.reshape(1, -1))


def _dispatch(eid):
    n_tok = eid.shape[0]
    n_rows = n_tok * TOP_K
    n_blocks = n_rows // MOE_ROWS + N_EXPERTS
    flat_e = eid.reshape(-1)
    onehot = (flat_e[:, None] == jnp.arange(N_EXPERTS, dtype=jnp.int32)[None, :]).astype(jnp.int32)
    csum = jnp.cumsum(onehot, axis=0)
    counts = csum[-1]
    rank = jnp.sum((csum - onehot) * onehot, axis=1)
    padded = (counts + MOE_ROWS - 1) // MOE_ROWS * MOE_ROWS
    pad_end = jnp.cumsum(padded)
    pad_start = pad_end - padded
    dest = pad_start[flat_e] + rank
    flat_tok = jnp.arange(n_rows, dtype=jnp.int32) // TOP_K
    row_tok = jnp.zeros((n_blocks * MOE_ROWS,), jnp.int32).at[dest].set(flat_tok, unique_indices=True)
    block_start = jnp.arange(n_blocks, dtype=jnp.int32) * MOE_ROWS
    block_e = jnp.minimum(jnp.sum((pad_end[None, :] <= block_start[:, None]).astype(jnp.int32), axis=1), N_EXPERTS - 1)
    n_used = (pad_end[-1] // MOE_ROWS).astype(jnp.int32)
    return row_tok, block_e, n_used, dest.reshape(n_tok, TOP_K).astype(jnp.int32)


def _tile_pos(pos, tm):
    n = pos.shape[0]
    return pos.reshape(n // tm, tm, TOP_K).transpose(0, 2, 1).reshape(n // tm, 1, TOP_K * tm)


def _slabs_to_rows(qkv_slabs):
    order = np.array([_slab_position(j) for j in range(SWA_SLABS)])
    return jnp.transpose(qkv_slabs[order], (1, 0, 2)).reshape(qkv_slabs.shape[1], B_PROJ)


def _rows_to_pair_slabs(t):
    n = t.shape[0]
    return jnp.transpose(t.reshape(n, len(B_GROUPS), SWA_PAIRS, LANES), (2, 1, 0, 3)).reshape(-1, n, LANES)


def kernel(x_prompt, x_sample, state_rwkv, state_shift, cache_swa_w128, cache_swa_w512, cache_swa_w2048,
           cache_mem_kv, mem_prompt, norm_mix_g, w_in, b_gate, rwkv_mu, rwkv_w0, rwkv_w2, rwkv_a0, rwkv_a2,
           rwkv_g2, rwkv_k_k, rwkv_k_a, rwkv_r_k, rwkv_lnx_g, rwkv_lnx_b, w_branch_a, w_branch_b, w_out,
           norm_mem_g, norm_memkv_g, w_mem_q, w_mem_kv, w_mem_out, norm_ffn_g, w_router_group, b_router_group,
           w_router_expert, b_router_expert, w_exp_gate, w_exp_up, w_exp_down, norm_final_g):
    bsz, s_len, _ = x_prompt.shape
    dbs, t_len, _ = x_sample.shape
    n_p, n_s = bsz * s_len, dbs * t_len
    rw = dict(rwkv_mu=rwkv_mu, rwkv_w0=rwkv_w0, rwkv_w2=rwkv_w2, rwkv_a0=rwkv_a0, rwkv_a2=rwkv_a2, rwkv_g2=rwkv_g2,
              rwkv_k_k=rwkv_k_k, rwkv_k_a=rwkv_k_a, rwkv_r_k=rwkv_r_k, rwkv_lnx_g=rwkv_lnx_g, rwkv_lnx_b=rwkv_lnx_b)
    pad = LANES - N_GROUPS - N_EXPERTS
    mid_w_s = dict(
        wa=w_branch_a, wb=w_branch_b, wo=w_out, gm=norm_mem_g.reshape(1, -1), wq=w_mem_q, wmo=w_mem_out,
        gf=norm_ffn_g.reshape(1, -1),
        wr=jnp.concatenate([w_router_group, w_router_expert, jnp.zeros((D_MODEL, pad), F32)], axis=1),
        br=jnp.concatenate([b_router_group, b_router_expert, jnp.zeros((pad,), F32)]).reshape(1, -1))
    mid_w_p = {k: (v.astype(BF16) if k.startswith('w') else v) for k, v in mid_w_s.items()}
    w_in_b = w_in.astype(BF16)

    xp = x_prompt.reshape(n_p, D_MODEL)
    tm = 512
    ua_p, qkv_p, gates_p, *p_caches = _in_proj(xp, norm_mix_g, w_in_b, b_gate, _rope_tables(jnp.arange(s_len)), tm,
                                               cache_seq=(bsz, s_len))
    s0 = jnp.zeros((bsz, A_HEADS, A_HEAD_DIM, A_HEAD_DIM), F32)
    ya_p, st_p, shift_p = _rwkv(ua_p.reshape(bsz, s_len, A_PROJ), jnp.zeros((bsz, A_PROJ), F32), s0, rw,
                                RWKV_CHUNK, s_len)
    ob_p, lse_p = _swa_prompt(qkv_p, bsz, s_len)
    p_bufs = [c.reshape(c.shape[:3] + (B_HEADS_PER_GROUP, B_HEAD_DIM)) for c in p_caches]

    memkv = _norm_matmul(mem_prompt.reshape(bsz * N_MEM, D_MODEL), norm_memkv_g, w_mem_kv.astype(BF16), 256)
    memkv3 = memkv.reshape(bsz, N_MEM, 2 * MEM_WIDTH)
    mem_kv_prompt = memkv3.reshape(bsz, N_MEM, 2, MEM_HEADS, MEM_HEAD_DIM).transpose(0, 2, 1, 3, 4)
    tiles_per_batch = s_len // tm
    xn_all = jnp.zeros((n_p + n_s, D_MODEL), F32)
    x2_p, xn_all, route_p = _mid(
        xp, ya_p.reshape(n_p, A_WIDTH), ob_p, lse_p, gates_p, memkv3, memkv3,
        lambda i: (i // tiles_per_batch, 0, 0), lambda i: (i // tiles_per_batch, 0, 1),
        1, s_len, mid_w_p, tm, xn_all, 0)

    xs = x_sample.reshape(n_s, D_MODEL)
    pos_s = PAST_LEN + (jnp.arange(n_s) % t_len)
    ua_s, qkv_s, gates_s = _in_proj(xs, norm_mix_g, w_in, b_gate, _rope_tables(pos_s), n_s)
    t_pad = 8
    ua_s3 = jnp.pad(ua_s.reshape(dbs, t_len, A_PROJ), ((0, 0), (0, t_pad - t_len), (0, 0)))
    ya_s, st_s, shift_s = _rwkv(ua_s3, state_shift, state_rwkv, rw, 0, t_len)
    ya_s = ya_s[:, :t_len].reshape(n_s, A_WIDTH)
    caches = [jnp.transpose(c, (0, 1, 3, 4, 2)) for c in (cache_swa_w128, cache_swa_w512, cache_swa_w2048)]
    ob_s, lse_s, nb0, nb1, nb2 = _swa_sample(_slabs_to_rows(qkv_s).reshape(dbs, t_len, B_PROJ), caches)
    s_bufs = [jnp.transpose(nb, (0, 1, 4, 2, 3)) for nb in (nb0, nb1, nb2)]
    mem_s = cache_mem_kv.reshape(dbs, 2 * N_MEM, MEM_WIDTH)
    tm_s = 32
    x2_s, xn_all, route_s = _mid(
        xs, ya_s, _rows_to_pair_slabs(ob_s.reshape(n_s, B_WIDTH)), _rows_to_pair_slabs(lse_s.reshape(n_s, B_WIDTH)),
        gates_s, mem_s, mem_s,
        lambda i: (i, 0, 0), lambda i: (i, 1, 0), tm_s // t_len, t_len, mid_w_s, tm_s, xn_all, n_p)

    route = jnp.concatenate([route_p, route_s], axis=0)
    eid = route[:, :TOP_K].astype(jnp.int32)
    row_tok, block_e, n_used, pos = _dispatch(eid)
    yb = _experts(xn_all, row_tok, block_e, n_used, w_exp_gate, w_exp_up, w_exp_down)
    tm_c = 128
    y_p = _combine(x2_p, route_p, _tile_pos(pos[:n_p], tm_c), yb, norm_final_g, tm_c)
    y_s = _combine(x2_s, route_s, _tile_pos(pos[n_p:], tm_c), yb, norm_final_g, tm_c)

    return (y_p.reshape(bsz, s_len, D_MODEL), y_s.reshape(dbs, t_len, D_MODEL),
            st_p, shift_p.reshape(bsz, A_PROJ), p_bufs[0], p_bufs[1], p_bufs[2], mem_kv_prompt,
            st_s, shift_s.reshape(dbs, A_PROJ), s_bufs[0], s_bufs[1], s_bufs[2])
```

```python
import functools
import math

import jax
import jax.numpy as jnp
import numpy as np
from jax import lax
from jax.experimental import pallas as pl
from jax.experimental.pallas import tpu as pltpu

F32 = jnp.float32
BF16 = jnp.bfloat16

D_MODEL = 1024
A_HEADS = 8
A_HEAD_DIM = 64
A_WIDTH = A_HEADS * A_HEAD_DIM
A_DECAY_LORA = 64
A_ICLR_LORA = 64
A_GATE_LORA = 128
A_PROJ = 3 * A_WIDTH + A_DECAY_LORA + A_ICLR_LORA + A_GATE_LORA
A_LNX_EPS = 64e-5
B_GROUPS = ((128, 1), (512, 4), (2048, 16))
B_HEADS_PER_GROUP = 4
B_HEAD_DIM = 64
B_GROUP_WIDTH = B_HEADS_PER_GROUP * B_HEAD_DIM
B_WIDTH = B_GROUP_WIDTH * len(B_GROUPS)
B_PROJ = 3 * B_WIDTH
ROPE_THETA = 500000.0
ROPE_DIM = B_HEAD_DIM // 4
ROPE_HALF = ROPE_DIM // 2
SWA_BLOCK = 128
N_MEM = 256
MEM_HEADS = 4
MEM_HEAD_DIM = 128
MEM_WIDTH = MEM_HEADS * MEM_HEAD_DIM
N_GROUPS = 4
EXPERTS_PER_GROUP = 8
N_EXPERTS = N_GROUPS * EXPERTS_PER_GROUP
TOP_K = 2
EXPERT_FF = 512
RMS_EPS = 1e-6
PAST_LEN = 8192

LANES = 128
SWA_SLABS = B_PROJ // LANES
SWA_PAIR = LANES // B_HEAD_DIM
SWA_PAIRS = B_GROUP_WIDTH // LANES
SWA_QUAD = 4
VMEM_LIMIT = 56 * 1024 * 1024
RWKV_CHUNK = 64
MOE_ROWS = 256
NEG_INF = float("-inf")


def _dot(a, b, precision=None):
    return jnp.dot(a, b, preferred_element_type=F32, precision=precision)


def _dot_nt(a, b, precision=None):
    return lax.dot_general(a, b, (((1,), (1,)), ((), ())), preferred_element_type=F32, precision=precision)


def _dot_tn(a, b, precision=None):
    return lax.dot_general(a, b, (((0,), (0,)), ((), ())), preferred_element_type=F32, precision=precision)


def _bf16_round(t):
    return t.astype(BF16).astype(F32)


_NN = (((1,), (0,)), ((), ()))
_NT = (((1,), (1,)), ((), ()))


def _mm(a, b, exact, dims=_NN):
    dg = lambda x, y: lax.dot_general(x, y, dims, preferred_element_type=F32)
    if not exact:
        return dg(a.astype(BF16), b.astype(BF16))
    a, b = a.astype(F32), b.astype(F32)
    a_hi, b_hi = a.astype(BF16), b.astype(BF16)
    a_lo = (a - a_hi.astype(F32)).astype(BF16)
    b_lo = (b - b_hi.astype(F32)).astype(BF16)
    return dg(a_hi, b_hi) + (dg(a_hi, b_lo) + dg(a_lo, b_hi))


def _contract_round(t, exact):
    return t if exact else _bf16_round(t)


def _rms(x, g):
    return x * lax.rsqrt(jnp.mean(x * x, axis=-1, keepdims=True) + RMS_EPS) * g


def _const_spec(shape):
    nd = len(shape)
    return pl.BlockSpec(shape, lambda *_: (0,) * nd, pipeline_mode=pl.Buffered(1))


def _params(n_axes):
    return pltpu.CompilerParams(dimension_semantics=("arbitrary",) * n_axes, vmem_limit_bytes=VMEM_LIMIT)


def _slab_position(j):
    kind, hs = divmod(j, SWA_SLABS // 3)
    g, p = divmod(hs, SWA_PAIRS)
    return p * (3 * len(B_GROUPS)) + kind * len(B_GROUPS) + g


def _in_proj_kernel(x_ref, g_ref, w_ref, bg_ref, rc_ref, rs1_ref, rs2_ref, ua_ref, qkv_ref, gate_ref, *cache_refs, exact):
    tm = x_ref.shape[0]
    xn = _rms(x_ref[...], g_ref[...])
    xn = xn if exact else xn.astype(BF16)
    mm = lambda w: _mm(xn, w, exact)
    ua_ref[...] = mm(w_ref[:, :A_PROJ])
    rc, rs1, rs2 = rc_ref[...], rs1_ref[...], rs2_ref[...]
    n_rot = 2 * B_WIDTH // LANES
    for j in range(SWA_SLABS):
        if j % 2 == 0:
            lo = A_PROJ + j * LANES
            s2 = mm(w_ref[:, lo:lo + 2 * LANES])
        s = s2[:, (j % 2) * LANES:(j % 2 + 1) * LANES]
        if j < n_rot:
            s = s * rc + pltpu.roll(s, LANES - ROPE_HALF, 1) * rs1 + pltpu.roll(s, ROPE_HALF, 1) * rs2
        qkv_ref[_slab_position(j)] = s
        kind, hs = divmod(j, SWA_SLABS // 3)
        if cache_refs and kind > 0:
            g, p = divmod(hs, SWA_PAIRS)
            rows = cache_refs[g].shape[2]
            cache_refs[g][0, kind - 1, :, p * LANES:(p + 1) * LANES] = s[tm - rows:, :]
    gate_ref[...] = jax.nn.sigmoid(mm(w_ref[:, A_PROJ + B_PROJ:]) + bg_ref[...])


def _rope_tables(pos):
    inv_freq = ROPE_THETA ** (-jnp.arange(ROPE_HALF, dtype=F32) * 2.0 / ROPE_DIM)
    ang = pos.astype(F32)[:, None] * inv_freq[None, :]
    cos, sin = jnp.cos(ang), jnp.sin(ang)
    n = pos.shape[0]
    rest = B_HEAD_DIM - ROPE_DIM
    c = jnp.concatenate([cos, cos, jnp.ones((n, rest), F32)], axis=1)
    s1 = jnp.concatenate([-sin, jnp.zeros((n, ROPE_HALF + rest), F32)], axis=1)
    s2 = jnp.concatenate([jnp.zeros((n, ROPE_HALF), F32), sin, jnp.zeros((n, rest), F32)], axis=1)
    rep = LANES // B_HEAD_DIM
    return tuple(jnp.tile(t, (1, rep)) for t in (c, s1, s2))


def _in_proj(x, g, w, b_gate, tables, tm, cache_seq=None):
    n = x.shape[0]
    p_rows = tables[0].shape[0]
    t_tiles = p_rows // tm
    in_proj_w = w.shape[1]
    tab_spec = pl.BlockSpec((tm, LANES), lambda i: (i % t_tiles, 0))
    cache_specs, cache_shapes = [], []
    if cache_seq is not None:
        n_seq, s_len = cache_seq
        tiles = s_len // tm
        for window, _ in B_GROUPS:
            keep = min(window, s_len)
            rows = min(tm, keep)
            first_tile = tiles - keep // rows
            cache_specs.append(pl.BlockSpec(
                (1, 2, rows, B_GROUP_WIDTH),
                lambda i, tiles=tiles, first_tile=first_tile: (i // tiles, 0, jnp.maximum(i % tiles - first_tile, 0), 0)))
            cache_shapes.append(jax.ShapeDtypeStruct((n_seq, 2, keep, B_GROUP_WIDTH), F32))
    return pl.pallas_call(
        functools.partial(_in_proj_kernel, exact=w.dtype == F32),
        grid=(n // tm,),
        in_specs=[
            pl.BlockSpec((tm, D_MODEL), lambda i: (i, 0)),
            _const_spec((1, D_MODEL)),
            _const_spec((D_MODEL, in_proj_w)),
            _const_spec((1, 2 * D_MODEL)),
            tab_spec, tab_spec, tab_spec,
        ],
        out_specs=[
            pl.BlockSpec((tm, A_PROJ), lambda i: (i, 0)),
            pl.BlockSpec((SWA_SLABS, tm, LANES), lambda i: (0, i, 0)),
            pl.BlockSpec((tm, 2 * D_MODEL), lambda i: (i, 0)),
        ] + cache_specs,
        out_shape=[
            jax.ShapeDtypeStruct((n, A_PROJ), F32),
            jax.ShapeDtypeStruct((SWA_SLABS, n, LANES), F32),
            jax.ShapeDtypeStruct((n, 2 * D_MODEL), F32),
        ] + cache_shapes,
        compiler_params=_params(1),
        name="in_proj",
    )(x, g.reshape(1, -1), w, b_gate.reshape(1, -1), *tables)


def _norm_matmul_kernel(x_ref, g_ref, w_ref, o_ref):
    o_ref[...] = _dot(_rms(x_ref[...], g_ref[...]).astype(BF16), w_ref[...])


def _norm_matmul(x, g, w_bf16, tm):
    n, d = x.shape
    dout = w_bf16.shape[1]
    return pl.pallas_call(
        _norm_matmul_kernel,
        grid=(n // tm,),
        in_specs=[pl.BlockSpec((tm, d), lambda i: (i, 0)), _const_spec((1, d)), _const_spec((d, dout))],
        out_specs=pl.BlockSpec((tm, dout), lambda i: (i, 0)),
        out_shape=jax.ShapeDtypeStruct((n, dout), F32),
        compiler_params=_params(1),
        name="norm_matmul",
    )(x, g.reshape(1, -1), w_bf16)


RWKV_HEADS_PER_PACK = 4
RWKV_PACK_WIDTH = RWKV_HEADS_PER_PACK * A_HEAD_DIM
RWKV_PACKS = A_HEADS // RWKV_HEADS_PER_PACK
RWKV_STEP_SEQS = 1


def _shift_rows(u, first_prev):
    row = lax.broadcasted_iota(jnp.int32, (u.shape[0], 1), 0)
    return jnp.where(row == 0, first_prev, pltpu.roll(u, 1, 0))


def _head_sum(x, bd, exact, pieces=1):
    ones = bd.astype(BF16)
    total, rest = None, x
    for _ in range(3 if exact else pieces):
        piece = rest.astype(BF16)
        rest = rest - piece.astype(F32)
        part = _dot(piece, ones)
        total = part if total is None else total + part
    return total


def _rwkv_features(u, u_prev, w_refs, bd, exact):
    mu_ref, w0_ref, w2_ref, a0_ref, a2_ref, g2_ref, kk_ref, ka_ref = w_refs
    um = u + (u_prev - u) * mu_ref[...]
    o1, o2, o3 = A_WIDTH, 2 * A_WIDTH, 3 * A_WIDTH
    o4 = o3 + A_DECAY_LORA
    o5 = o4 + A_ICLR_LORA
    r, k, v = um[:, :o1], um[:, o1:o2], um[:, o2:o3]
    xw, xa, xg = um[:, o3:o4], um[:, o4:o5], um[:, o5:]
    lora = lambda t, w_ref: _mm(t, w_ref[...], exact)
    w = -jax.nn.softplus(-(w0_ref[...] + lora(jnp.tanh(xw), w2_ref))) - 0.5
    e = jnp.exp(w)
    a = jax.nn.sigmoid(a0_ref[...] + lora(xa, a2_ref))
    g = lora(jax.nn.sigmoid(xg), g2_ref)
    kk = k * kk_ref[...]
    kkn = kk / jnp.maximum(jnp.sqrt(_head_sum(kk * kk, bd, exact, pieces=2)), 1e-12)
    k2 = k * (1.0 + (a - 1.0) * ka_ref[...])
    return r, k2, v, e, a, g, kkn


def _rwkv_output(y, r, k2, v, g, rk_ref, lng_ref, lnb_ref, bd, exact):
    inv_n = 1.0 / A_HEAD_DIM
    mean = _head_sum(y, bd, exact) * inv_n
    yc = y - mean
    var = _head_sum(yc * yc, bd, exact) * inv_n
    yn = yc * lax.rsqrt(var + A_LNX_EPS) * lng_ref[...] + lnb_ref[...]
    bonus = _head_sum(r * k2 * rk_ref[...], bd, exact) * v
    return (yn + bonus) * g


def _rwkv_chunk_kernel(u_ref, sh0_ref, s0_ref, mu_ref, w0_ref, w2_ref, a0_ref, a2_ref, g2_ref, kk_ref, ka_ref,
                       rk_ref, lng_ref, lnb_ref, bd_ref, y_ref, sfin_ref, shout_ref, st_scr, prev_scr, *, n_steps):
    c = pl.program_id(0)
    n_b, chunk, _ = u_ref.shape
    hd, hpp, pw_ = A_HEAD_DIM, RWKV_HEADS_PER_PACK, RWKV_PACK_WIDTH
    bf = lambda t: t.astype(BF16)

    @pl.when(c == 0)
    def _():
        for b in range(n_b):
            prev_scr[b:b + 1, :] = sh0_ref[b]
            for p in range(RWKV_PACKS):
                st_scr[b, p] = jnp.concatenate([s0_ref[b, p * hpp + h] for h in range(hpp)], axis=1)

    u_b = [u_ref[b] for b in range(n_b)]
    u = jnp.concatenate(u_b, axis=0)
    u_prev = jnp.concatenate([_shift_rows(u_b[b], prev_scr[b:b + 1, :]) for b in range(n_b)], axis=0)
    for b in range(n_b):
        prev_scr[b:b + 1, :] = u_b[b][chunk - 1:chunk, :]
    bd = bd_ref[...]
    r, k2, v, e, a, g, kkn = _rwkv_features(
        u, u_prev, (mu_ref, w0_ref, w2_ref, a0_ref, a2_ref, g2_ref, kk_ref, ka_ref), bd, False)
    b_ = kkn * a

    li = lax.broadcasted_iota(jnp.int32, (chunk, chunk), 0)
    lj = lax.broadcasted_iota(jnp.int32, (chunk, chunk), 1)
    tri = bf((li >= lj).astype(F32))
    e_hi = bf(e)
    e_rest = e - e_hi.astype(F32)
    e_mid = bf(e_rest)
    e_lo = bf(e_rest - e_mid.astype(F32))
    cums, ends = [], []
    for b in range(n_b):
        rs = slice(b * chunk, (b + 1) * chunk)
        cb = _dot(tri, e_hi[rs]) + (_dot(tri, e_mid[rs]) + _dot(tri, e_lo[rs]))
        cums.append(cb)
        ends.append(jnp.broadcast_to(cb[chunk - 1:chunk, :], (chunk, A_WIDTH)))
    cum = jnp.concatenate(cums, axis=0)
    cum_end = jnp.concatenate(ends, axis=0)
    grow = jnp.exp(cum)
    to_end = jnp.exp(cum - cum_end)
    at = bf(-kkn * jnp.exp(e - cum))
    rt = bf(r * jnp.exp(-cum))
    bt = bf(b_ * grow)
    kt = bf(k2 * grow)
    bh = bf(b_ * to_end)
    kh = bf(k2 * to_end)
    vb = bf(v)
    dec_end = jnp.exp(-cum_end)

    lane_head = lax.broadcasted_iota(jnp.int32, (1, pw_), 1) // hd
    head_mask = [lane_head == h for h in range(hpp)]

    def block_diag(x):
        return jnp.concatenate([jnp.where(head_mask[h], x, jnp.zeros_like(x)) for h in range(hpp)], axis=0)

    assert chunk == hd
    ti = lax.broadcasted_iota(jnp.int32, (chunk, pw_), 0)
    tj = lax.broadcasted_iota(jnp.int32, (chunk, pw_), 1) % chunk
    strict = ti > tj
    incl = ti >= tj
    eye = (ti == tj).astype(F32)
    n_sq = int(math.log2(chunk)) - 1

    streams = [(b, p, slice(b * chunk, (b + 1) * chunk), slice(p * pw_, (p + 1) * pw_))
               for b in range(n_b) for p in range(RWKV_PACKS)]
    at_s = [at[rs, cs] for _, _, rs, cs in streams]
    rt_s = [rt[rs, cs] for _, _, rs, cs in streams]
    v_s = [vb[rs, cs] for _, _, rs, cs in streams]
    m = [_dot_nt(jnp.concatenate([a_, r_], axis=0),
                 jnp.concatenate([block_diag(bt[rs, cs]), block_diag(kt[rs, cs])], axis=0))
         for a_, r_, (_, _, rs, cs) in zip(at_s, rt_s, streams)]
    a_ab = [jnp.where(strict, x[:chunk, :pw_], 0.0) for x in m]
    akv = [_dot(bf(jnp.where(strict, x[:chunk, pw_:], 0.0)), block_diag(vs)) for x, vs in zip(m, v_s)]
    m_r = [bf(jnp.concatenate([jnp.where(incl, x[chunk:, :pw_], 0.0), jnp.where(incl, x[chunk:, pw_:], 0.0)], axis=1))
           for x in m]
    tinv = [eye + x for x in a_ab]
    pw = a_ab
    for _ in range(n_sq):
        pw = [_dot(bf(x), block_diag(bf(x))) for x in pw]
        tinv = [t + _dot(bf(t), block_diag(bf(x))) for t, x in zip(tinv, pw)]
    w12 = [_dot(bf(t), jnp.concatenate([block_diag(a_), block_diag(bf(x))], axis=1))
           for t, a_, x in zip(tinv, at_s, akv)]

    st = [st_scr[b, p] for b, p, _, _ in streams]
    x = [_dot_nt(jnp.concatenate([bf(w[:, :pw_]), r_], axis=0), block_diag(bf(s))) for w, r_, s in zip(w12, rt_s, st)]
    uu = [bf(xi[:chunk] + w[:, pw_:]) for xi, w in zip(x, w12)]
    ys = [xi[chunk:] + _dot(mr, jnp.concatenate([block_diag(ui), block_diag(vs)], axis=0))
          for xi, mr, ui, vs in zip(x, m_r, uu, v_s)]
    for i, (b, p, rs, cs) in enumerate(streams):
        upd = _dot_tn(jnp.concatenate([uu[i], v_s[i]], axis=0), jnp.concatenate([bh[rs, cs], kh[rs, cs]], axis=0))
        diag = functools.reduce(lambda s_, t_: s_ + t_,
                                [jnp.where(head_mask[h], upd[h * hd:(h + 1) * hd, :], 0.0) for h in range(hpp)])
        st_scr[b, p] = st[i] * dec_end[rs.start:rs.start + 1, cs] + diag
    y = jnp.concatenate([jnp.concatenate(ys[b * RWKV_PACKS:(b + 1) * RWKV_PACKS], axis=1) for b in range(n_b)], axis=0)
    out = _rwkv_output(y, r, k2, v, g, rk_ref, lng_ref, lnb_ref, bd, False)
    for b in range(n_b):
        y_ref[b] = out[b * chunk:(b + 1) * chunk]

    @pl.when(c == n_steps - 1)
    def _():
        for b in range(n_b):
            shout_ref[b] = u_b[b][chunk - 1:chunk, :]
            for p in range(RWKV_PACKS):
                for h in range(hpp):
                    sfin_ref[b, p * hpp + h] = st_scr[b, p][:, h * hd:(h + 1) * hd]


def _rwkv_step_kernel(u_ref, sh0_ref, s0_ref, mu_ref, w0_ref, w2_ref, a0_ref, a2_ref, g2_ref, kk_ref, ka_ref,
                      rk_ref, lng_ref, lnb_ref, bd_ref, y_ref, sfin_ref, shout_ref, *, t_valid, exact):
    hd = A_HEAD_DIM
    n_b, n_rows, _ = u_ref.shape
    u_b = [u_ref[b] for b in range(n_b)]
    u = jnp.concatenate(u_b, axis=0)
    u_prev = jnp.concatenate([_shift_rows(u_b[b], sh0_ref[b]) for b in range(n_b)], axis=0)
    bd = bd_ref[...]
    r, k2, v, e, a, g, kkn = _rwkv_features(
        u, u_prev, (mu_ref, w0_ref, w2_ref, a0_ref, a2_ref, g2_ref, kk_ref, ka_ref), bd, exact)
    rnd = lambda t: _contract_round(t, exact)
    decay = jnp.exp(-e)
    b_ = kkn * a
    eye = (lax.broadcasted_iota(jnp.int32, (hd, hd), 0) == lax.broadcasted_iota(jnp.int32, (hd, hd), 1)).astype(F32)
    to_col = lambda t: jnp.sum(eye * t, axis=1, keepdims=True)
    to_row = lambda t: jnp.sum(eye * t, axis=0, keepdims=True)
    y_seq = []
    for b in range(n_b):
        y_heads = []
        for h in range(A_HEADS):
            sl = slice(h * hd, (h + 1) * hd)
            s = s0_ref[b, h]
            y_rows = []
            for t in range(t_valid):
                tt = slice(b * n_rows + t, b * n_rows + t + 1)
                sa = jnp.sum(rnd(s) * rnd(-kkn[tt, sl]), axis=1, keepdims=True)
                s = s * decay[tt, sl] + sa * b_[tt, sl] + to_col(v[tt, sl]) * k2[tt, sl]
                y_rows.append(to_row(jnp.sum(rnd(s) * rnd(r[tt, sl]), axis=1, keepdims=True)))
            sfin_ref[b, h] = s
            y_rows.append(jnp.zeros((n_rows - t_valid, hd), F32))
            y_heads.append(jnp.concatenate(y_rows, axis=0))
        y_seq.append(jnp.concatenate(y_heads, axis=1))
    y = jnp.concatenate(y_seq, axis=0)
    out = _rwkv_output(y, r, k2, v, g, rk_ref, lng_ref, lnb_ref, bd, exact)
    for b in range(n_b):
        y_ref[b] = out[b * n_rows:(b + 1) * n_rows]
        shout_ref[b] = u_b[b][t_valid - 1:t_valid, :]


def _rwkv(u_a, shift0, s0, p, chunk, t_valid):
    bsz, t_len, _ = u_a.shape
    hd = A_HEAD_DIM
    bd = jnp.asarray(np.kron(np.eye(A_HEADS, dtype=np.float32), np.ones((hd, hd), np.float32)))
    row = lambda t: t.reshape(1, -1)
    if chunk:
        assert t_valid == t_len and t_len % chunk == 0
        grid = (t_len // chunk,)
        kern = functools.partial(_rwkv_chunk_kernel, n_steps=grid[0])
        scratch = [pltpu.VMEM((bsz, RWKV_PACKS, hd, RWKV_PACK_WIDTH), F32), pltpu.VMEM((bsz, A_PROJ), F32)]
        u_spec = pl.BlockSpec((bsz, chunk, A_PROJ), lambda c: (0, c, 0))
        y_spec = pl.BlockSpec((bsz, chunk, A_WIDTH), lambda c: (0, c, 0))
        state_spec = pl.BlockSpec((bsz, A_HEADS, hd, hd), lambda c: (0, 0, 0, 0))
        shift_spec = pl.BlockSpec((bsz, 1, A_PROJ), lambda c: (0, 0, 0))
    else:
        n_b = math.gcd(bsz, RWKV_STEP_SEQS)
        grid = (bsz // n_b,)
        kern = functools.partial(_rwkv_step_kernel, t_valid=t_valid, exact=True)
        scratch = []
        u_spec = pl.BlockSpec((n_b, t_len, A_PROJ), lambda b: (b, 0, 0))
        y_spec = pl.BlockSpec((n_b, t_len, A_WIDTH), lambda b: (b, 0, 0))
        state_spec = pl.BlockSpec((n_b, A_HEADS, hd, hd), lambda b: (b, 0, 0, 0))
        shift_spec = pl.BlockSpec((n_b, 1, A_PROJ), lambda b: (b, 0, 0))
    return pl.pallas_call(
        kern,
        grid=grid,
        in_specs=[
            u_spec, shift_spec, state_spec,
            _const_spec((1, A_PROJ)), _const_spec((1, A_WIDTH)), _const_spec((A_DECAY_LORA, A_WIDTH)),
            _const_spec((1, A_WIDTH)), _const_spec((A_ICLR_LORA, A_WIDTH)), _const_spec((A_GATE_LORA, A_WIDTH)),
            _const_spec((1, A_WIDTH)), _const_spec((1, A_WIDTH)), _const_spec((1, A_WIDTH)),
            _const_spec((1, A_WIDTH)), _const_spec((1, A_WIDTH)), _const_spec((A_WIDTH, A_WIDTH)),
        ],
        out_specs=[y_spec, state_spec, shift_spec],
        out_shape=[
            jax.ShapeDtypeStruct((bsz, t_len, A_WIDTH), F32),
            jax.ShapeDtypeStruct((bsz, A_HEADS, hd, hd), F32),
            jax.ShapeDtypeStruct((bsz, 1, A_PROJ), F32),
        ],
        scratch_shapes=scratch,
        compiler_params=_params(1),
        name="rwkv7",
    )(u_a, shift0.reshape(bsz, 1, A_PROJ), s0, row(p['rwkv_mu']), row(p['rwkv_w0']), p['rwkv_w2'],
      row(p['rwkv_a0']), p['rwkv_a2'], p['rwkv_g2'], row(p['rwkv_k_k']), row(p['rwkv_k_a']),
      row(p['rwkv_r_k']), row(p['rwkv_lnx_g']), row(p['rwkv_lnx_b']), bd)


def _swa_prompt_kernel(qkv_ref, o_ref, lse_ref):
    blk, hd, n_g = SWA_BLOCK, B_HEAD_DIM, len(B_GROUPS)
    s_len = qkv_ref.shape[1]
    scale = hd ** -0.5
    rows2 = SWA_PAIR * blk
    r_i = lax.broadcasted_iota(jnp.int32, (rows2, 1), 0)
    qi = r_i % blk
    own_lanes = (lax.broadcasted_iota(jnp.int32, (1, LANES), 1) // hd) == (r_i // blk)
    head0_lanes = own_lanes[:blk]
    ki2 = lax.broadcasted_iota(jnp.int32, (rows2, 2 * blk), 1)
    band2 = (qi + blk - ki2 >= 0) & (qi - ki2 <= 0)
    causal1 = lax.broadcasted_iota(jnp.int32, (rows2, blk), 1) <= qi

    def attend(g, specs):
        dil = B_GROUPS[g][1]

        def rows(kind, st):
            idx = pl.ds(st, blk, stride=dil) if dil > 1 else pl.ds(st, blk)
            return qkv_ref[kind * n_g + g, idx, :]

        qs, kbs, vbs, masks = [], [], [], []
        for st, prev, first in specs:
            q = rows(0, st)
            qs.append(jnp.where(own_lanes, jnp.concatenate([q] * SWA_PAIR, axis=0), 0.0).astype(BF16))
            if prev is None:
                kbs.append(rows(1, st).astype(BF16))
                vbs.append(rows(2, st).astype(BF16))
                masks.append(causal1)
            else:
                kbs.append(jnp.concatenate([rows(1, prev), rows(1, st)], axis=0).astype(BF16))
                vbs.append(jnp.concatenate([rows(2, prev), rows(2, st)], axis=0).astype(BF16))
                masks.append(band2 & (ki2 >= jnp.where(first, blk, 0)))
        scores = [_dot_nt(q, kb) * scale for q, kb in zip(qs, kbs)]
        probs, lses = [], []
        for sc, mk in zip(scores, masks):
            sc = jnp.where(mk, sc, NEG_INF)
            m = jnp.max(sc, axis=-1, keepdims=True)
            p = jnp.exp(sc - m)
            den = jnp.sum(p, axis=-1, keepdims=True)
            probs.append((p / den).astype(BF16))
            lses.append(m + jnp.log(den))
        outs = [_dot(p, vb) for p, vb in zip(probs, vbs)]
        for (st, _, _), o2, l2 in zip(specs, outs, lses):
            idx = pl.ds(st, blk, stride=dil) if dil > 1 else pl.ds(st, blk)
            o_ref[g, idx, :] = jnp.where(head0_lanes, o2[:blk], o2[blk:])
            lse_ref[g, idx, :] = jnp.where(head0_lanes, l2[:blk], l2[blk:])

    for g, (_, dil) in enumerate(B_GROUPS):
        n_blk = s_len // dil // blk
        n_quads = dil * n_blk // SWA_QUAD

        def quad(it, carry, g=g, dil=dil, n_blk=n_blk):
            specs = []
            for j in range(SWA_QUAD):
                if n_blk >= SWA_QUAD:
                    e = it * SWA_QUAD + j
                    r, n = e // n_blk, e % n_blk
                    specs.append((n * (blk * dil) + r, jnp.maximum(n - 1, 0) * (blk * dil) + r, n == 0))
                else:
                    r = it * (SWA_QUAD // n_blk) + j // n_blk
                    n = j % n_blk
                    specs.append((n * (blk * dil) + r, None if n == 0 else (n - 1) * (blk * dil) + r, False))
            attend(g, specs)
            return carry

        lax.fori_loop(0, n_quads, quad, 0)


def _swa_prompt(qkv_slabs, n_seq, s_len):
    n_g = len(B_GROUPS)
    n = n_seq * s_len
    out_spec = pl.BlockSpec((n_g, s_len, LANES), lambda b, p: (p, b, 0))
    shp = jax.ShapeDtypeStruct((SWA_PAIRS * n_g, n, LANES), F32)
    return pl.pallas_call(
        _swa_prompt_kernel,
        grid=(n_seq, SWA_PAIRS),
        in_specs=[pl.BlockSpec((3 * n_g, s_len, LANES), lambda b, p: (p, b, 0))],
        out_specs=[out_spec, out_spec],
        out_shape=[shp, shp],
        compiler_params=_params(2),
        name="swa_prompt",
    )(qkv_slabs)


def _swa_sample_kernel(qkv_ref, c0_ref, c1_ref, c2_ref, o_ref, lse_ref, n0_ref, n1_ref, n2_ref, *, t_len, exact):
    hd = B_HEAD_DIM
    nh = B_HEADS_PER_GROUP
    gw = B_GROUP_WIDTH
    qkv = qkv_ref[0]
    scale = hd ** -0.5
    rnd = lambda t: _contract_round(t, exact)
    t_col = lax.broadcasted_iota(jnp.int32, (t_len, 1), 0)
    i_n = lax.broadcasted_iota(jnp.int32, (t_len, t_len), 1)
    units = []
    for gi, ((window, dil), c_ref, n_ref) in enumerate(zip(B_GROUPS, (c0_ref, c1_ref, c2_ref), (n0_ref, n1_ref, n2_ref))):
        buf_len = c_ref.shape[4]
        j_c = lax.broadcasted_iota(jnp.int32, (t_len, buf_len), 1)
        ok_c = (j_c >= t_col) & (((j_c - t_col) & (dil - 1)) == 0)
        ok_n = (i_n <= t_col) & (((t_col - i_n) & (dil - 1)) == 0)
        for h in range(nh):
            lo = gi * gw + h * hd
            units.append(dict(
                q=qkv[:, lo:lo + hd], k_new=qkv[:, B_WIDTH + lo:B_WIDTH + lo + hd],
                v_new=qkv[:, 2 * B_WIDTH + lo:2 * B_WIDTH + lo + hd],
                kt=c_ref[0, 0, h], vt=c_ref[0, 1, h], ok_c=ok_c, ok_n=ok_n))
    s_cs = [jnp.where(u["ok_c"], _mm(u["q"], u["kt"], exact) * scale, NEG_INF) for u in units]
    probs, p_news, lses = [], [], []
    for u, s_c in zip(units, s_cs):
        qr, kr = rnd(u["q"]), rnd(u["k_new"])
        s_n = jnp.zeros((t_len, t_len), F32)
        for i in range(t_len):
            s_n = jnp.where(i_n == i, jnp.sum(qr * kr[i:i + 1, :], axis=-1, keepdims=True) * scale, s_n)
        s_n = jnp.where(u["ok_n"], s_n, NEG_INF)
        m = jnp.maximum(jnp.max(s_c, axis=-1, keepdims=True), jnp.max(s_n, axis=-1, keepdims=True))
        p_c = jnp.exp(s_c - m)
        p_n = jnp.exp(s_n - m)
        den = jnp.sum(p_c, axis=-1, keepdims=True) + jnp.sum(p_n, axis=-1, keepdims=True)
        probs.append(p_c / den)
        p_news.append(rnd(p_n / den))
        lses.append(jnp.broadcast_to(m + jnp.log(den), (t_len, hd)))
    outs = [_mm(p, u["vt"], exact, _NT) for p, u in zip(probs, units)]
    for k, (u, p_n) in enumerate(zip(units, p_news)):
        vr = rnd(u["v_new"])
        for i in range(t_len):
            outs[k] = outs[k] + p_n[:, i:i + 1] * vr[i:i + 1, :]
    pad_rows = 8 - t_len
    place = (lax.broadcasted_iota(jnp.int32, (8, LANES), 0) + (LANES - t_len)
             == lax.broadcasted_iota(jnp.int32, (8, LANES), 1)).astype(BF16)
    tail_lane = lax.broadcasted_iota(jnp.int32, (1, LANES), 1) >= LANES - t_len
    for gi, (c_ref, n_ref) in enumerate(zip((c0_ref, c1_ref, c2_ref), (n0_ref, n1_ref, n2_ref))):
        buf_len = c_ref.shape[4]
        for j in range(2):
            new = jnp.pad(qkv[:, (j + 1) * B_WIDTH + gi * gw:(j + 1) * B_WIDTH + (gi + 1) * gw], ((0, pad_rows), (0, 0)))
            cols, rest = None, new
            for _ in range(3):
                piece = rest.astype(BF16)
                rest = rest - piece.astype(F32)
                part = _dot_tn(piece, place)
                cols = part if cols is None else cols + part
            nxt = pltpu.roll(c_ref[0, j].reshape(gw, buf_len), buf_len - t_len, 1)
            tail = jnp.where(tail_lane, cols, nxt[:, buf_len - LANES:])
            full = tail if buf_len == LANES else jnp.concatenate([nxt[:, :buf_len - LANES], tail], axis=1)
            n_ref[0, j] = full.reshape(nh, hd, buf_len)
    o_ref[0] = jnp.concatenate(outs, axis=1)
    lse_ref[0] = jnp.concatenate(lses, axis=1)


def _swa_sample(qkv, caches):
    bsz, t_len, _ = qkv.shape
    cache_specs = [pl.BlockSpec((1,) + c.shape[1:], lambda b: (b, 0, 0, 0, 0)) for c in caches]
    row_spec = pl.BlockSpec((1, t_len, B_WIDTH), lambda b: (b, 0, 0))
    return pl.pallas_call(
        functools.partial(_swa_sample_kernel, t_len=t_len, exact=True),
        grid=(bsz,),
        in_specs=[pl.BlockSpec((1, t_len, B_PROJ), lambda b: (b, 0, 0))] + cache_specs,
        out_specs=[row_spec, row_spec] + cache_specs,
        out_shape=[jax.ShapeDtypeStruct((bsz, t_len, B_WIDTH), F32)] * 2
        + [jax.ShapeDtypeStruct(c.shape, F32) for c in caches],
        compiler_params=_params(1),
        name="swa_sample",
    )(qkv, *caches)


def _mid_kernel(x_ref, ya_ref, ob_ref, lse_ref, gate_ref, mk_ref, mv_ref, wa_ref, wb_ref, wo_ref, gm_ref,
                wq_ref, wmo_ref, gf_ref, wr_ref, br_ref, xn_all_ref, x2_ref, xn_ref, route_ref, *, rows_per_batch, exact):
    del xn_all_ref
    tm = x_ref.shape[0]
    n_g = len(B_GROUPS)
    rnd = lambda t: _contract_round(t, exact)
    mm = lambda a, b, dims=_NN: _mm(a, b, exact, dims)
    yb_pairs = []
    for p in range(SWA_PAIRS):
        lses = [lse_ref[p * n_g + g] for g in range(n_g)]
        m = functools.reduce(jnp.maximum, lses)
        es = [jnp.exp(l - m) for l in lses]
        den = functools.reduce(lambda a, b: a + b, es)
        yb_pairs.append(functools.reduce(lambda a, b: a + b,
                                         [rnd(es[g] / den) * rnd(ob_ref[p * n_g + g]) for g in range(n_g)]))
    yb = jnp.concatenate(yb_pairs, axis=1)
    gates = gate_ref[...]
    merged = (gates[:, :D_MODEL] * mm(ya_ref[...], wa_ref[...])
              + gates[:, D_MODEL:] * mm(yb, wb_ref[...]))
    x1 = x_ref[...] + mm(merged, wo_ref[...])

    q = mm(_rms(x1, gm_ref[...]), wq_ref[...])
    n_b = mk_ref.shape[0]
    mk = mk_ref[...].reshape(n_b * N_MEM, MEM_WIDTH)
    mv = mv_ref[...].reshape(n_b * N_MEM, MEM_WIDTH)
    if not exact:
        mk, mv = mk.astype(BF16), mv.astype(BF16)
    if n_b > 1:
        rb = lax.broadcasted_iota(jnp.int32, (tm, n_b * N_MEM), 0) // rows_per_batch
        cb = lax.broadcasted_iota(jnp.int32, (tm, n_b * N_MEM), 1) // N_MEM
        same = rb == cb
    heads = []
    for h in range(MEM_HEADS):
        sl = slice(h * MEM_HEAD_DIM, (h + 1) * MEM_HEAD_DIM)
        s = mm(q[:, sl], mk[:, sl], _NT) * (MEM_HEAD_DIM ** -0.5)
        if n_b > 1:
            s = jnp.where(same, s, NEG_INF)
        s = s - jnp.max(s, axis=-1, keepdims=True)
        p = jnp.exp(s)
        heads.append(mm(p / jnp.sum(p, axis=-1, keepdims=True), mv[:, sl]))
    x2 = x1 + mm(jnp.concatenate(heads, axis=1), wmo_ref[...])
    x2_ref[...] = x2

    xn = _rms(x2, gf_ref[...])
    xn_ref[...] = xn
    logits = mm(xn, wr_ref[...]) + br_ref[...]
    lane = lax.broadcasted_iota(jnp.int32, (tm, LANES), 1)
    gl = jnp.where(lane < N_GROUPS, logits, NEG_INF)
    gmax = jnp.max(gl, axis=-1, keepdims=True)
    grp = jnp.min(jnp.where(gl == gmax, lane, LANES), axis=-1, keepdims=True)
    w_grp = 1.0 / jnp.sum(jnp.exp(gl - gmax), axis=-1, keepdims=True)
    first = N_GROUPS + grp * EXPERTS_PER_GROUP
    el = jnp.where((lane >= first) & (lane < first + EXPERTS_PER_GROUP), logits, NEG_INF)
    m1 = jnp.max(el, axis=-1, keepdims=True)
    i1 = jnp.min(jnp.where(el == m1, lane, LANES), axis=-1, keepdims=True)
    el2 = jnp.where(lane == i1, NEG_INF, el)
    m2 = jnp.max(el2, axis=-1, keepdims=True)
    i2 = jnp.min(jnp.where(el2 == m2, lane, LANES), axis=-1, keepdims=True)
    e2 = jnp.exp(m2 - m1)
    g1 = w_grp / (1.0 + e2)
    g2 = w_grp * e2 / (1.0 + e2)
    route = jnp.where(lane == 0, (i1 - N_GROUPS).astype(F32), 0.0)
    route = jnp.where(lane == 1, (i2 - N_GROUPS).astype(F32), route)
    route = jnp.where(lane == 2, g1, route)
    route_ref[...] = jnp.where(lane == 3, g2, route)


def _mid(x, y_a, o_b, lse_b, gates, mem_k, mem_v, mk_map, mv_map, n_b, rows_per_batch, w, tm, xn_all, row0):
    n = x.shape[0]
    row = lambda width: pl.BlockSpec((tm, width), lambda i: (i, 0))
    slabs = pl.BlockSpec((SWA_PAIRS * len(B_GROUPS), tm, LANES), lambda i: (0, i, 0))
    in_specs = [
        row(D_MODEL), row(A_WIDTH), slabs, slabs, row(2 * D_MODEL),
        pl.BlockSpec((n_b, N_MEM, MEM_WIDTH), mk_map), pl.BlockSpec((n_b, N_MEM, MEM_WIDTH), mv_map),
        _const_spec((A_WIDTH, D_MODEL)), _const_spec((B_GROUP_WIDTH, D_MODEL)), _const_spec((D_MODEL, D_MODEL)),
        _const_spec((1, D_MODEL)), _const_spec((D_MODEL, MEM_WIDTH)), _const_spec((MEM_WIDTH, D_MODEL)),
        _const_spec((1, D_MODEL)), _const_spec((D_MODEL, LANES)), _const_spec((1, LANES)),
        pl.BlockSpec(memory_space=pl.ANY),
    ]
    args = [x, y_a, o_b, lse_b, gates, mem_k, mem_v, w['wa'], w['wb'], w['wo'], w['gm'], w['wq'], w['wmo'],
            w['gf'], w['wr'], w['br'], xn_all]
    blk0 = row0 // tm
    return pl.pallas_call(
        functools.partial(_mid_kernel, rows_per_batch=rows_per_batch, exact=w['wa'].dtype == F32),
        grid=(n // tm,),
        in_specs=in_specs,
        out_specs=[row(D_MODEL), pl.BlockSpec((tm, D_MODEL), lambda i: (i + blk0, 0)), row(LANES)],
        out_shape=[
            jax.ShapeDtypeStruct((n, D_MODEL), F32),
            jax.ShapeDtypeStruct(xn_all.shape, F32),
            jax.ShapeDtypeStruct((n, LANES), F32),
        ],
        input_output_aliases={len(args) - 1: 1},
        compiler_params=_params(1),
        name="mid",
    )(*args)


def _row_copy(src_hbm, idx, dst_buf, slot, j, sem):
    return pltpu.make_async_copy(src_hbm.at[pl.ds(idx, 1), :], dst_buf.at[slot, pl.ds(j, 1), :], sem.at[slot])


def _gather_start(idx_ref, src_hbm, dst_buf, slot, sem, n_rows):
    for j in range(n_rows):
        _row_copy(src_hbm, idx_ref[0, 0, j], dst_buf, slot, j, sem).start(priority=j % 2)


def _gather_wait(src_hbm, dst_buf, slot, sem, n_rows):
    for j in range(n_rows):
        _row_copy(src_hbm, 0, dst_buf, slot, j, sem).wait()


def _experts_kernel(meta_ref, be_ref, idx_ref, idx_next_ref, x_hbm, wg_ref, wu_ref, wd_ref, o_ref, xbuf, sem):
    i = pl.program_id(0)
    n_used = meta_ref[0]
    slot = i % 2

    @pl.when(i == 0)
    def _():
        _gather_start(idx_ref, x_hbm, xbuf, 0, sem, MOE_ROWS)

    @pl.when(i + 1 < n_used)
    def _():
        _gather_start(idx_next_ref, x_hbm, xbuf, 1 - slot, sem, MOE_ROWS)

    @pl.when(i < n_used)
    def _():
        _gather_wait(x_hbm, xbuf, slot, sem, MOE_ROWS)
        xb = xbuf[slot].astype(BF16)
        hg = _dot(xb, wg_ref[0].astype(BF16))
        hu = _dot(xb, wu_ref[0].astype(BF16))
        hh = (jax.nn.silu(hg) * hu).astype(BF16)
        o_ref[...] = _dot(hh, wd_ref[0].astype(BF16))

    @pl.when(i >= n_used)
    def _():
        o_ref[...] = jnp.zeros_like(o_ref)


def _experts(xn_all, row_tok, block_e, n_used, w_gate, w_up, w_down):
    n_blocks = block_e.shape[0]
    idx3 = row_tok.reshape(n_blocks, 1, MOE_ROWS)
    idx_spec = lambda f: pl.BlockSpec((1, 1, MOE_ROWS), f, memory_space=pltpu.SMEM)
    grid_spec = pltpu.PrefetchScalarGridSpec(
        num_scalar_prefetch=2,
        grid=(n_blocks,),
        in_specs=[
            idx_spec(lambda i, meta, be: (i, 0, 0)),
            idx_spec(lambda i, meta, be: (jnp.minimum(i + 1, n_blocks - 1), 0, 0)),
            pl.BlockSpec(memory_space=pl.ANY),
            pl.BlockSpec((1, D_MODEL, EXPERT_FF), lambda i, meta, be: (be[i], 0, 0)),
            pl.BlockSpec((1, D_MODEL, EXPERT_FF), lambda i, meta, be: (be[i], 0, 0)),
            pl.BlockSpec((1, EXPERT_FF, D_MODEL), lambda i, meta, be: (be[i], 0, 0)),
        ],
        out_specs=pl.BlockSpec((MOE_ROWS, D_MODEL), lambda i, meta, be: (i, 0)),
        scratch_shapes=[pltpu.VMEM((2, MOE_ROWS, D_MODEL), F32), pltpu.SemaphoreType.DMA((2,))],
    )
    return pl.pallas_call(
        _experts_kernel,
        grid_spec=grid_spec,
        out_shape=jax.ShapeDtypeStruct((n_blocks * MOE_ROWS, D_MODEL), F32),
        compiler_params=_params(1),
        name="experts",
    )(n_used.reshape(1), block_e, idx3, idx3, xn_all, w_gate, w_up, w_down)


def _combine_kernel(pos_ref, pos_next_ref, x_ref, route_ref, yb_hbm, g_ref, o_ref, ybuf, sem, *, n_tiles):
    i = pl.program_id(0)
    tm = x_ref.shape[0]
    slot = i % 2

    @pl.when(i == 0)
    def _():
        _gather_start(pos_ref, yb_hbm, ybuf, 0, sem, 2 * tm)

    if n_tiles > 1:
        @pl.when(i + 1 < n_tiles)
        def _():
            _gather_start(pos_next_ref, yb_hbm, ybuf, 1 - slot, sem, 2 * tm)

    _gather_wait(yb_hbm, ybuf, slot, sem, 2 * tm)
    route = route_ref[...]
    y = x_ref[...] + (route[:, 2:3] * ybuf[slot, :tm, :] + route[:, 3:4] * ybuf[slot, tm:, :])
    o_ref[...] = _rms(y, g_ref[...])


def _combine(x2, route, pos, yb, g_final, tm):
    n = x2.shape[0]
    n_tiles = n // tm
    pos_spec = lambda f: pl.BlockSpec((1, 1, 2 * tm), f, memory_space=pltpu.SMEM)
    return pl.pallas_call(
        functools.partial(_combine_kernel, n_tiles=n_tiles),
        grid=(n_tiles,),
        in_specs=[
            pos_spec(lambda i: (i, 0, 0)),
            pos_spec(lambda i: (jnp.minimum(i + 1, n_tiles - 1), 0, 0)),
            pl.BlockSpec((tm, D_MODEL), lambda i: (i, 0)),
            pl.BlockSpec((tm, LANES), lambda i: (i, 0)),
            pl.BlockSpec(memory_space=pl.ANY),
            _const_spec((1, D_MODEL)),
        ],
        out_specs=pl.BlockSpec((tm, D_MODEL), lambda i: (i, 0)),
        out_shape=jax.ShapeDtypeStruct((n, D_MODEL), F32),
        scratch_shapes=[pltpu.VMEM((2, 2 * tm, D_MODEL), F32), pltpu.SemaphoreType.DMA((2,))],
        compiler_params=_params(1),
        name="combine",
    )(pos, pos, x2, route, yb, g_final.reshape(1, -1))


def _dispatch(eid):
    n_tok = eid.shape[0]
    n_rows = n_tok * TOP_K
    n_blocks = n_rows // MOE_ROWS + N_EXPERTS
    flat_e = eid.reshape(-1)
    onehot = (flat_e[:, None] == jnp.arange(N_EXPERTS, dtype=jnp.int32)[None, :]).astype(jnp.int32)
    csum = jnp.cumsum(onehot, axis=0)
    counts = csum[-1]
    rank = jnp.sum((csum - onehot) * onehot, axis=1)
    padded = (counts + MOE_ROWS - 1) // MOE_ROWS * MOE_ROWS
    pad_end = jnp.cumsum(padded)
    pad_start = pad_end - padded
    dest = pad_start[flat_e] + rank
    flat_tok = jnp.arange(n_rows, dtype=jnp.int32) // TOP_K
    row_tok = jnp.zeros((n_blocks * MOE_ROWS,), jnp.int32).at[dest].set(flat_tok, unique_indices=True)
    block_start = jnp.arange(n_blocks, dtype=jnp.int32) * MOE_ROWS
    block_e = jnp.minimum(jnp.sum((pad_end[None, :] <= block_start[:, None]).astype(jnp.int32), axis=1), N_EXPERTS - 1)
    n_used = (pad_end[-1] // MOE_ROWS).astype(jnp.int32)
    return row_tok, block_e, n_used, dest.reshape(n_tok, TOP_K).astype(jnp.int32)


def _tile_pos(pos, tm):
    n = pos.shape[0]
    return pos.reshape(n // tm, tm, TOP_K).transpose(0, 2, 1).reshape(n // tm, 1, TOP_K * tm)


def _slabs_to_rows(qkv_slabs):
    order = np.array([_slab_position(j) for j in range(SWA_SLABS)])
    return jnp.transpose(qkv_slabs[order], (1, 0, 2)).reshape(qkv_slabs.shape[1], B_PROJ)


def _rows_to_pair_slabs(t):
    n = t.shape[0]
    return jnp.transpose(t.reshape(n, len(B_GROUPS), SWA_PAIRS, LANES), (2, 1, 0, 3)).reshape(-1, n, LANES)


def kernel(x_prompt, x_sample, state_rwkv, state_shift, cache_swa_w128, cache_swa_w512, cache_swa_w2048,
           cache_mem_kv, mem_prompt, norm_mix_g, w_in, b_gate, rwkv_mu, rwkv_w0, rwkv_w2, rwkv_a0, rwkv_a2,
           rwkv_g2, rwkv_k_k, rwkv_k_a, rwkv_r_k, rwkv_lnx_g, rwkv_lnx_b, w_branch_a, w_branch_b, w_out,
           norm_mem_g, norm_memkv_g, w_mem_q, w_mem_kv, w_mem_out, norm_ffn_g, w_router_group, b_router_group,
           w_router_expert, b_router_expert, w_exp_gate, w_exp_up, w_exp_down, norm_final_g):
    bsz, s_len, _ = x_prompt.shape
    dbs, t_len, _ = x_sample.shape
    n_p, n_s = bsz * s_len, dbs * t_len
    rw = dict(rwkv_mu=rwkv_mu, rwkv_w0=rwkv_w0, rwkv_w2=rwkv_w2, rwkv_a0=rwkv_a0, rwkv_a2=rwkv_a2, rwkv_g2=rwkv_g2,
              rwkv_k_k=rwkv_k_k, rwkv_k_a=rwkv_k_a, rwkv_r_k=rwkv_r_k, rwkv_lnx_g=rwkv_lnx_g, rwkv_lnx_b=rwkv_lnx_b)
    pad = LANES - N_GROUPS - N_EXPERTS
    mid_w_s = dict(
        wa=w_branch_a, wb=w_branch_b, wo=w_out, gm=norm_mem_g.reshape(1, -1), wq=w_mem_q, wmo=w_mem_out,
        gf=norm_ffn_g.reshape(1, -1),
        wr=jnp.concatenate([w_router_group, w_router_expert, jnp.zeros((D_MODEL, pad), F32)], axis=1),
        br=jnp.concatenate([b_router_group, b_router_expert, jnp.zeros((pad,), F32)]).reshape(1, -1))
    mid_w_p = {k: (v.astype(BF16) if k.startswith('w') else v) for k, v in mid_w_s.items()}
    w_in_b = w_in.astype(BF16)

    xp = x_prompt.reshape(n_p, D_MODEL)
    tm = 512
    ua_p, qkv_p, gates_p, *p_caches = _in_proj(xp, norm_mix_g, w_in_b, b_gate, _rope_tables(jnp.arange(s_len)), tm,
                                               cache_seq=(bsz, s_len))
    s0 = jnp.zeros((bsz, A_HEADS, A_HEAD_DIM, A_HEAD_DIM), F32)
    ya_p, st_p, shift_p = _rwkv(ua_p.reshape(bsz, s_len, A_PROJ), jnp.zeros((bsz, A_PROJ), F32), s0, rw,
                                RWKV_CHUNK, s_len)
    ob_p, lse_p = _swa_prompt(qkv_p, bsz, s_len)
    p_bufs = [c.reshape(c.shape[:3] + (B_HEADS_PER_GROUP, B_HEAD_DIM)) for c in p_caches]

    memkv = _norm_matmul(mem_prompt.reshape(bsz * N_MEM, D_MODEL), norm_memkv_g, w_mem_kv.astype(BF16), 256)
    memkv3 = memkv.reshape(bsz, N_MEM, 2 * MEM_WIDTH)
    mem_kv_prompt = memkv3.reshape(bsz, N_MEM, 2, MEM_HEADS, MEM_HEAD_DIM).transpose(0, 2, 1, 3, 4)
    tiles_per_batch = s_len // tm
    xn_all = jnp.zeros((n_p + n_s, D_MODEL), F32)
    x2_p, xn_all, route_p = _mid(
        xp, ya_p.reshape(n_p, A_WIDTH), ob_p, lse_p, gates_p, memkv3, memkv3,
        lambda i: (i // tiles_per_batch, 0, 0), lambda i: (i // tiles_per_batch, 0, 1),
        1, s_len, mid_w_p, tm, xn_all, 0)

    xs = x_sample.reshape(n_s, D_MODEL)
    pos_s = PAST_LEN + (jnp.arange(n_s) % t_len)
    ua_s, qkv_s, gates_s = _in_proj(xs, norm_mix_g, w_in, b_gate, _rope_tables(pos_s), n_s)
    t_pad = 8
    ua_s3 = jnp.pad(ua_s.reshape(dbs, t_len, A_PROJ), ((0, 0), (0, t_pad - t_len), (0, 0)))
    ya_s, st_s, shift_s = _rwkv(ua_s3, state_shift, state_rwkv, rw, 0, t_len)
    ya_s = ya_s[:, :t_len].reshape(n_s, A_WIDTH)
    caches = [jnp.transpose(c, (0, 1, 3, 4, 2)) for c in (cache_swa_w128, cache_swa_w512, cache_swa_w2048)]
    ob_s, lse_s, nb0, nb1, nb2 = _swa_sample(_slabs_to_rows(qkv_s).reshape(dbs, t_len, B_PROJ), caches)
    s_bufs = [jnp.transpose(nb, (0, 1, 4, 2, 3)) for nb in (nb0, nb1, nb2)]
    mem_s = cache_mem_kv.reshape(dbs, 2 * N_MEM, MEM_WIDTH)
    tm_s = 32
    x2_s, xn_all, route_s = _mid(
        xs, ya_s, _rows_to_pair_slabs(ob_s.reshape(n_s, B_WIDTH)), _rows_to_pair_slabs(lse_s.reshape(n_s, B_WIDTH)),
        gates_s, mem_s, mem_s,
        lambda i: (i, 0, 0), lambda i: (i, 1, 0), tm_s // t_len, t_len, mid_w_s, tm_s, xn_all, n_p)

    route = jnp.concatenate([route_p, route_s], axis=0)
    eid = route[:, :TOP_K].astype(jnp.int32)
    row_tok, block_e, n_used, pos = _dispatch(eid)
    yb = _experts(xn_all, row_tok, block_e, n_used, w_exp_gate, w_exp_up, w_exp_down)
    tm_c = 128
    y_p = _combine(x2_p, route_p, _tile_pos(pos[:n_p], tm_c), yb, norm_final_g, tm_c)
    y_s = _combine(x2_s, route_s, _tile_pos(pos[n_p:], tm_c), yb, norm_final_g, tm_c)

    return (y_p.reshape(bsz, s_len, D_MODEL), y_s.reshape(dbs, t_len, D_MODEL),
            st_p, shift_p.reshape(bsz, A_PROJ), p_bufs[0], p_bufs[1], p_bufs[2], mem_kv_prompt,
            st_s, shift_s.reshape(dbs, A_PROJ), s_bufs[0], s_bufs[1], s_bufs[2])
```

```python
import functools
import math

import jax
import jax.numpy as jnp
import numpy as np
from jax import lax
from jax.experimental import pallas as pl
from jax.experimental.pallas import tpu as pltpu

F32 = jnp.float32
BF16 = jnp.bfloat16

D_MODEL = 1024
A_HEADS = 8
A_HEAD_DIM = 64
A_WIDTH = A_HEADS * A_HEAD_DIM
A_DECAY_LORA = 64
A_ICLR_LORA = 64
A_GATE_LORA = 128
A_PROJ = 3 * A_WIDTH + A_DECAY_LORA + A_ICLR_LORA + A_GATE_LORA
A_LNX_EPS = 64e-5
B_GROUPS = ((128, 1), (512, 4), (2048, 16))
B_HEADS_PER_GROUP = 4
B_HEAD_DIM = 64
B_GROUP_WIDTH = B_HEADS_PER_GROUP * B_HEAD_DIM
B_WIDTH = B_GROUP_WIDTH * len(B_GROUPS)
B_PROJ = 3 * B_WIDTH
ROPE_THETA = 500000.0
ROPE_DIM = B_HEAD_DIM // 4
ROPE_HALF = ROPE_DIM // 2
SWA_BLOCK = 128
N_MEM = 256
MEM_HEADS = 4
MEM_HEAD_DIM = 128
MEM_WIDTH = MEM_HEADS * MEM_HEAD_DIM
N_GROUPS = 4
EXPERTS_PER_GROUP = 8
N_EXPERTS = N_GROUPS * EXPERTS_PER_GROUP
TOP_K = 2
EXPERT_FF = 512
RMS_EPS = 1e-6
PAST_LEN = 8192

LANES = 128
SWA_SLABS = B_PROJ // LANES
SWA_PAIR = LANES // B_HEAD_DIM
SWA_PAIRS = B_GROUP_WIDTH // LANES
SWA_QUAD = 4
MID_SUB = 2
VMEM_LIMIT = 56 * 1024 * 1024
RWKV_CHUNK = 64
MOE_ROWS = 256
NEG_INF = float("-inf")


def _dot(a, b, precision=None):
    return jnp.dot(a, b, preferred_element_type=F32, precision=precision)


def _dot_nt(a, b, precision=None):
    return lax.dot_general(a, b, (((1,), (1,)), ((), ())), preferred_element_type=F32, precision=precision)


def _dot_tn(a, b, precision=None):
    return lax.dot_general(a, b, (((0,), (0,)), ((), ())), preferred_element_type=F32, precision=precision)


def _bf16_round(t):
    return t.astype(BF16).astype(F32)


_NN = (((1,), (0,)), ((), ()))
_NT = (((1,), (1,)), ((), ()))


def _mm(a, b, exact, dims=_NN):
    dg = lambda x, y: lax.dot_general(x, y, dims, preferred_element_type=F32)
    if not exact:
        return dg(a.astype(BF16), b.astype(BF16))
    a, b = a.astype(F32), b.astype(F32)
    a_hi, b_hi = a.astype(BF16), b.astype(BF16)
    a_lo = (a - a_hi.astype(F32)).astype(BF16)
    b_lo = (b - b_hi.astype(F32)).astype(BF16)
    return dg(a_hi, b_hi) + (dg(a_hi, b_lo) + dg(a_lo, b_hi))


def _contract_round(t, exact):
    return t if exact else _bf16_round(t)


def _rms(x, g):
    return x * lax.rsqrt(jnp.mean(x * x, axis=-1, keepdims=True) + RMS_EPS) * g


def _const_spec(shape):
    nd = len(shape)
    return pl.BlockSpec(shape, lambda *_: (0,) * nd, pipeline_mode=pl.Buffered(1))


def _params(n_axes):
    return pltpu.CompilerParams(dimension_semantics=("arbitrary",) * n_axes, vmem_limit_bytes=VMEM_LIMIT)


def _slab_position(j):
    kind, hs = divmod(j, SWA_SLABS // 3)
    g, p = divmod(hs, SWA_PAIRS)
    return p * (3 * len(B_GROUPS)) + kind * len(B_GROUPS) + g


def _in_proj_kernel(x_ref, g_ref, w_ref, bg_ref, rc_ref, rs1_ref, rs2_ref, ua_ref, qkv_ref, gate_ref, *cache_refs, exact):
    tm = x_ref.shape[0]
    xn = _rms(x_ref[...], g_ref[...])
    xn = xn if exact else xn.astype(BF16)
    mm = lambda w: _mm(xn, w, exact)
    ua_ref[...] = mm(w_ref[:, :A_PROJ])
    rc, rs1, rs2 = rc_ref[...], rs1_ref[...], rs2_ref[...]
    n_rot = 2 * B_WIDTH // LANES
    for j in range(SWA_SLABS):
        if j % 2 == 0:
            lo = A_PROJ + j * LANES
            s2 = mm(w_ref[:, lo:lo + 2 * LANES])
        s = s2[:, (j % 2) * LANES:(j % 2 + 1) * LANES]
        if j < n_rot:
            s = s * rc + pltpu.roll(s, LANES - ROPE_HALF, 1) * rs1 + pltpu.roll(s, ROPE_HALF, 1) * rs2
        qkv_ref[_slab_position(j)] = s
        kind, hs = divmod(j, SWA_SLABS // 3)
        if cache_refs and kind > 0:
            g, p = divmod(hs, SWA_PAIRS)
            rows = cache_refs[g].shape[2]
            cache_refs[g][0, kind - 1, :, p * LANES:(p + 1) * LANES] = s[tm - rows:, :]
    gate_ref[...] = jax.nn.sigmoid(mm(w_ref[:, A_PROJ + B_PROJ:]) + bg_ref[...])


def _rope_tables(pos):
    inv_freq = ROPE_THETA ** (-jnp.arange(ROPE_HALF, dtype=F32) * 2.0 / ROPE_DIM)
    ang = pos.astype(F32)[:, None] * inv_freq[None, :]
    cos, sin = jnp.cos(ang), jnp.sin(ang)
    n = pos.shape[0]
    rest = B_HEAD_DIM - ROPE_DIM
    c = jnp.concatenate([cos, cos, jnp.ones((n, rest), F32)], axis=1)
    s1 = jnp.concatenate([-sin, jnp.zeros((n, ROPE_HALF + rest), F32)], axis=1)
    s2 = jnp.concatenate([jnp.zeros((n, ROPE_HALF), F32), sin, jnp.zeros((n, rest), F32)], axis=1)
    rep = LANES // B_HEAD_DIM
    return tuple(jnp.tile(t, (1, rep)) for t in (c, s1, s2))


def _in_proj(x, g, w, b_gate, tables, tm, cache_seq=None):
    n = x.shape[0]
    p_rows = tables[0].shape[0]
    t_tiles = p_rows // tm
    in_proj_w = w.shape[1]
    tab_spec = pl.BlockSpec((tm, LANES), lambda i: (i % t_tiles, 0))
    cache_specs, cache_shapes = [], []
    if cache_seq is not None:
        n_seq, s_len = cache_seq
        tiles = s_len // tm
        for window, _ in B_GROUPS:
            keep = min(window, s_len)
            rows = min(tm, keep)
            first_tile = tiles - keep // rows
            cache_specs.append(pl.BlockSpec(
                (1, 2, rows, B_GROUP_WIDTH),
                lambda i, tiles=tiles, first_tile=first_tile: (i // tiles, 0, jnp.maximum(i % tiles - first_tile, 0), 0)))
            cache_shapes.append(jax.ShapeDtypeStruct((n_seq, 2, keep, B_GROUP_WIDTH), F32))
    return pl.pallas_call(
        functools.partial(_in_proj_kernel, exact=w.dtype == F32),
        grid=(n // tm,),
        in_specs=[
            pl.BlockSpec((tm, D_MODEL), lambda i: (i, 0)),
            _const_spec((1, D_MODEL)),
            _const_spec((D_MODEL, in_proj_w)),
            _const_spec((1, 2 * D_MODEL)),
            tab_spec, tab_spec, tab_spec,
        ],
        out_specs=[
            pl.BlockSpec((tm, A_PROJ), lambda i: (i, 0)),
            pl.BlockSpec((SWA_SLABS, tm, LANES), lambda i: (0, i, 0)),
            pl.BlockSpec((tm, 2 * D_MODEL), lambda i: (i, 0)),
        ] + cache_specs,
        out_shape=[
            jax.ShapeDtypeStruct((n, A_PROJ), F32),
            jax.ShapeDtypeStruct((SWA_SLABS, n, LANES), F32),
            jax.ShapeDtypeStruct((n, 2 * D_MODEL), F32),
        ] + cache_shapes,
        compiler_params=_params(1),
        name="in_proj",
    )(x, g.reshape(1, -1), w, b_gate.reshape(1, -1), *tables)


def _norm_matmul_kernel(x_ref, g_ref, w_ref, o_ref):
    o_ref[...] = _dot(_rms(x_ref[...], g_ref[...]).astype(BF16), w_ref[...])


def _norm_matmul(x, g, w_bf16, tm):
    n, d = x.shape
    dout = w_bf16.shape[1]
    return pl.pallas_call(
        _norm_matmul_kernel,
        grid=(n // tm,),
        in_specs=[pl.BlockSpec((tm, d), lambda i: (i, 0)), _const_spec((1, d)), _const_spec((d, dout))],
        out_specs=pl.BlockSpec((tm, dout), lambda i: (i, 0)),
        out_shape=jax.ShapeDtypeStruct((n, dout), F32),
        compiler_params=_params(1),
        name="norm_matmul",
    )(x, g.reshape(1, -1), w_bf16)


RWKV_HEADS_PER_PACK = 4
RWKV_PACK_WIDTH = RWKV_HEADS_PER_PACK * A_HEAD_DIM
RWKV_PACKS = A_HEADS // RWKV_HEADS_PER_PACK
RWKV_STEP_SEQS = 1


def _shift_rows(u, first_prev):
    row = lax.broadcasted_iota(jnp.int32, (u.shape[0], 1), 0)
    return jnp.where(row == 0, first_prev, pltpu.roll(u, 1, 0))


def _head_sum(x, bd, exact, pieces=1):
    ones = bd.astype(BF16)
    total, rest = None, x
    for _ in range(3 if exact else pieces):
        piece = rest.astype(BF16)
        rest = rest - piece.astype(F32)
        part = _dot(piece, ones)
        total = part if total is None else total + part
    return total


def _rwkv_features(u, u_prev, w_refs, bd, exact):
    mu_ref, w0_ref, w2_ref, a0_ref, a2_ref, g2_ref, kk_ref, ka_ref = w_refs
    um = u + (u_prev - u) * mu_ref[...]
    o1, o2, o3 = A_WIDTH, 2 * A_WIDTH, 3 * A_WIDTH
    o4 = o3 + A_DECAY_LORA
    o5 = o4 + A_ICLR_LORA
    r, k, v = um[:, :o1], um[:, o1:o2], um[:, o2:o3]
    xw, xa, xg = um[:, o3:o4], um[:, o4:o5], um[:, o5:]
    lora = lambda t, w_ref: _mm(t, w_ref[...], exact)
    w = -jax.nn.softplus(-(w0_ref[...] + lora(jnp.tanh(xw), w2_ref))) - 0.5
    e = jnp.exp(w)
    a = jax.nn.sigmoid(a0_ref[...] + lora(xa, a2_ref))
    g = lora(jax.nn.sigmoid(xg), g2_ref)
    kk = k * kk_ref[...]
    kkn = kk / jnp.maximum(jnp.sqrt(_head_sum(kk * kk, bd, exact, pieces=2)), 1e-12)
    k2 = k * (1.0 + (a - 1.0) * ka_ref[...])
    return r, k2, v, e, a, g, kkn


def _rwkv_output(y, r, k2, v, g, rk_ref, lng_ref, lnb_ref, bd, exact):
    inv_n = 1.0 / A_HEAD_DIM
    mean = _head_sum(y, bd, exact) * inv_n
    yc = y - mean
    var = _head_sum(yc * yc, bd, exact) * inv_n
    yn = yc * lax.rsqrt(var + A_LNX_EPS) * lng_ref[...] + lnb_ref[...]
    bonus = _head_sum(r * k2 * rk_ref[...], bd, exact) * v
    return (yn + bonus) * g


def _rwkv_chunk_kernel(u_ref, sh0_ref, s0_ref, mu_ref, w0_ref, w2_ref, a0_ref, a2_ref, g2_ref, kk_ref, ka_ref,
                       rk_ref, lng_ref, lnb_ref, bd_ref, y_ref, sfin_ref, shout_ref, st_scr, prev_scr, *, n_steps):
    c = pl.program_id(0)
    n_b, chunk, _ = u_ref.shape
    hd, hpp, pw_ = A_HEAD_DIM, RWKV_HEADS_PER_PACK, RWKV_PACK_WIDTH
    bf = lambda t: t.astype(BF16)

    @pl.when(c == 0)
    def _():
        for b in range(n_b):
            prev_scr[b:b + 1, :] = sh0_ref[b]
            for p in range(RWKV_PACKS):
                st_scr[b, p] = jnp.concatenate([s0_ref[b, p * hpp + h] for h in range(hpp)], axis=1)

    u_b = [u_ref[b] for b in range(n_b)]
    u = jnp.concatenate(u_b, axis=0)
    u_prev = jnp.concatenate([_shift_rows(u_b[b], prev_scr[b:b + 1, :]) for b in range(n_b)], axis=0)
    for b in range(n_b):
        prev_scr[b:b + 1, :] = u_b[b][chunk - 1:chunk, :]
    bd = bd_ref[...]
    r, k2, v, e, a, g, kkn = _rwkv_features(
        u, u_prev, (mu_ref, w0_ref, w2_ref, a0_ref, a2_ref, g2_ref, kk_ref, ka_ref), bd, False)
    b_ = kkn * a

    li = lax.broadcasted_iota(jnp.int32, (chunk, chunk), 0)
    lj = lax.broadcasted_iota(jnp.int32, (chunk, chunk), 1)
    tri = bf((li >= lj).astype(F32))
    e_hi = bf(e)
    e_rest = e - e_hi.astype(F32)
    e_mid = bf(e_rest)
    e_lo = bf(e_rest - e_mid.astype(F32))
    cums, ends = [], []
    for b in range(n_b):
        rs = slice(b * chunk, (b + 1) * chunk)
        cb = _dot(tri, e_hi[rs]) + (_dot(tri, e_mid[rs]) + _dot(tri, e_lo[rs]))
        cums.append(cb)
        ends.append(jnp.broadcast_to(cb[chunk - 1:chunk, :], (chunk, A_WIDTH)))
    cum = jnp.concatenate(cums, axis=0)
    cum_end = jnp.concatenate(ends, axis=0)
    grow = jnp.exp(cum)
    to_end = jnp.exp(cum - cum_end)
    at = bf(-kkn * jnp.exp(e - cum))
    rt = bf(r * jnp.exp(-cum))
    bt = bf(b_ * grow)
    kt = bf(k2 * grow)
    bh = bf(b_ * to_end)
    kh = bf(k2 * to_end)
    vb = bf(v)
    dec_end = jnp.exp(-cum_end)

    lane_head = lax.broadcasted_iota(jnp.int32, (1, pw_), 1) // hd
    head_mask = [lane_head == h for h in range(hpp)]

    def block_diag(x):
        return jnp.concatenate([jnp.where(head_mask[h], x, jnp.zeros_like(x)) for h in range(hpp)], axis=0)

    assert chunk == hd
    ti = lax.broadcasted_iota(jnp.int32, (chunk, pw_), 0)
    tj = lax.broadcasted_iota(jnp.int32, (chunk, pw_), 1) % chunk
    strict = ti > tj
    incl = ti >= tj
    eye = (ti == tj).astype(F32)
    n_sq = int(math.log2(chunk)) - 1

    streams = [(b, p, slice(b * chunk, (b + 1) * chunk), slice(p * pw_, (p + 1) * pw_))
               for b in range(n_b) for p in range(RWKV_PACKS)]
    at_s = [at[rs, cs] for _, _, rs, cs in streams]
    rt_s = [rt[rs, cs] for _, _, rs, cs in streams]
    v_s = [vb[rs, cs] for _, _, rs, cs in streams]
    m = [_dot_nt(jnp.concatenate([a_, r_], axis=0),
                 jnp.concatenate([block_diag(bt[rs, cs]), block_diag(kt[rs, cs])], axis=0))
         for a_, r_, (_, _, rs, cs) in zip(at_s, rt_s, streams)]
    a_ab = [jnp.where(strict, x[:chunk, :pw_], 0.0) for x in m]
    akv = [_dot(bf(jnp.where(strict, x[:chunk, pw_:], 0.0)), block_diag(vs)) for x, vs in zip(m, v_s)]
    m_r = [bf(jnp.concatenate([jnp.where(incl, x[chunk:, :pw_], 0.0), jnp.where(incl, x[chunk:, pw_:], 0.0)], axis=1))
           for x in m]
    tinv = [eye + x for x in a_ab]
    pw = a_ab
    for _ in range(n_sq):
        pw = [_dot(bf(x), block_diag(bf(x))) for x in pw]
        tinv = [t + _dot(bf(t), block_diag(bf(x))) for t, x in zip(tinv, pw)]
    w12 = [_dot(bf(t), jnp.concatenate([block_diag(a_), block_diag(bf(x))], axis=1))
           for t, a_, x in zip(tinv, at_s, akv)]

    st = [st_scr[b, p] for b, p, _, _ in streams]
    x = [_dot_nt(jnp.concatenate([bf(w[:, :pw_]), r_], axis=0), block_diag(bf(s))) for w, r_, s in zip(w12, rt_s, st)]
    uu = [bf(xi[:chunk] + w[:, pw_:]) for xi, w in zip(x, w12)]
    ys = [xi[chunk:] + _dot(mr, jnp.concatenate([block_diag(ui), block_diag(vs)], axis=0))
          for xi, mr, ui, vs in zip(x, m_r, uu, v_s)]
    for i, (b, p, rs, cs) in enumerate(streams):
        upd = _dot_tn(jnp.concatenate([uu[i], v_s[i]], axis=0), jnp.concatenate([bh[rs, cs], kh[rs, cs]], axis=0))
        diag = functools.reduce(lambda s_, t_: s_ + t_,
                                [jnp.where(head_mask[h], upd[h * hd:(h + 1) * hd, :], 0.0) for h in range(hpp)])
        st_scr[b, p] = st[i] * dec_end[rs.start:rs.start + 1, cs] + diag
    y = jnp.concatenate([jnp.concatenate(ys[b * RWKV_PACKS:(b + 1) * RWKV_PACKS], axis=1) for b in range(n_b)], axis=0)
    out = _rwkv_output(y, r, k2, v, g, rk_ref, lng_ref, lnb_ref, bd, False)
    for b in range(n_b):
        y_ref[b] = out[b * chunk:(b + 1) * chunk]

    @pl.when(c == n_steps - 1)
    def _():
        for b in range(n_b):
            shout_ref[b] = u_b[b][chunk - 1:chunk, :]
            for p in range(RWKV_PACKS):
                for h in range(hpp):
                    sfin_ref[b, p * hpp + h] = st_scr[b, p][:, h * hd:(h + 1) * hd]


def _rwkv_step_kernel(u_ref, sh0_ref, s0_ref, mu_ref, w0_ref, w2_ref, a0_ref, a2_ref, g2_ref, kk_ref, ka_ref,
                      rk_ref, lng_ref, lnb_ref, bd_ref, y_ref, sfin_ref, shout_ref, *, t_valid, exact):
    hd = A_HEAD_DIM
    n_b, n_rows, _ = u_ref.shape
    u_b = [u_ref[b] for b in range(n_b)]
    u = jnp.concatenate(u_b, axis=0)
    u_prev = jnp.concatenate([_shift_rows(u_b[b], sh0_ref[b]) for b in range(n_b)], axis=0)
    bd = bd_ref[...]
    r, k2, v, e, a, g, kkn = _rwkv_features(
        u, u_prev, (mu_ref, w0_ref, w2_ref, a0_ref, a2_ref, g2_ref, kk_ref, ka_ref), bd, exact)
    rnd = lambda t: _contract_round(t, exact)
    decay = jnp.exp(-e)
    b_ = kkn * a
    eye = (lax.broadcasted_iota(jnp.int32, (hd, hd), 0) == lax.broadcasted_iota(jnp.int32, (hd, hd), 1)).astype(F32)
    to_col = lambda t: jnp.sum(eye * t, axis=1, keepdims=True)
    to_row = lambda t: jnp.sum(eye * t, axis=0, keepdims=True)
    y_seq = []
    for b in range(n_b):
        y_heads = []
        for h in range(A_HEADS):
            sl = slice(h * hd, (h + 1) * hd)
            s = s0_ref[b, h]
            y_rows = []
            for t in range(t_valid):
                tt = slice(b * n_rows + t, b * n_rows + t + 1)
                sa = jnp.sum(rnd(s) * rnd(-kkn[tt, sl]), axis=1, keepdims=True)
                s = s * decay[tt, sl] + sa * b_[tt, sl] + to_col(v[tt, sl]) * k2[tt, sl]
                y_rows.append(to_row(jnp.sum(rnd(s) * rnd(r[tt, sl]), axis=1, keepdims=True)))
            sfin_ref[b, h] = s
            y_rows.append(jnp.zeros((n_rows - t_valid, hd), F32))
            y_heads.append(jnp.concatenate(y_rows, axis=0))
        y_seq.append(jnp.concatenate(y_heads, axis=1))
    y = jnp.concatenate(y_seq, axis=0)
    out = _rwkv_output(y, r, k2, v, g, rk_ref, lng_ref, lnb_ref, bd, exact)
    for b in range(n_b):
        y_ref[b] = out[b * n_rows:(b + 1) * n_rows]
        shout_ref[b] = u_b[b][t_valid - 1:t_valid, :]


def _rwkv(u_a, shift0, s0, p, chunk, t_valid):
    bsz, t_len, _ = u_a.shape
    hd = A_HEAD_DIM
    bd = jnp.asarray(np.kron(np.eye(A_HEADS, dtype=np.float32), np.ones((hd, hd), np.float32)))
    row = lambda t: t.reshape(1, -1)
    if chunk:
        assert t_valid == t_len and t_len % chunk == 0
        grid = (t_len // chunk,)
        kern = functools.partial(_rwkv_chunk_kernel, n_steps=grid[0])
        scratch = [pltpu.VMEM((bsz, RWKV_PACKS, hd, RWKV_PACK_WIDTH), F32), pltpu.VMEM((bsz, A_PROJ), F32)]
        u_spec = pl.BlockSpec((bsz, chunk, A_PROJ), lambda c: (0, c, 0))
        y_spec = pl.BlockSpec((bsz, chunk, A_WIDTH), lambda c: (0, c, 0))
        state_spec = pl.BlockSpec((bsz, A_HEADS, hd, hd), lambda c: (0, 0, 0, 0))
        shift_spec = pl.BlockSpec((bsz, 1, A_PROJ), lambda c: (0, 0, 0))
    else:
        n_b = math.gcd(bsz, RWKV_STEP_SEQS)
        grid = (bsz // n_b,)
        kern = functools.partial(_rwkv_step_kernel, t_valid=t_valid, exact=True)
        scratch = []
        u_spec = pl.BlockSpec((n_b, t_len, A_PROJ), lambda b: (b, 0, 0))
        y_spec = pl.BlockSpec((n_b, t_len, A_WIDTH), lambda b: (b, 0, 0))
        state_spec = pl.BlockSpec((n_b, A_HEADS, hd, hd), lambda b: (b, 0, 0, 0))
        shift_spec = pl.BlockSpec((n_b, 1, A_PROJ), lambda b: (b, 0, 0))
    return pl.pallas_call(
        kern,
        grid=grid,
        in_specs=[
            u_spec, shift_spec, state_spec,
            _const_spec((1, A_PROJ)), _const_spec((1, A_WIDTH)), _const_spec((A_DECAY_LORA, A_WIDTH)),
            _const_spec((1, A_WIDTH)), _const_spec((A_ICLR_LORA, A_WIDTH)), _const_spec((A_GATE_LORA, A_WIDTH)),
            _const_spec((1, A_WIDTH)), _const_spec((1, A_WIDTH)), _const_spec((1, A_WIDTH)),
            _const_spec((1, A_WIDTH)), _const_spec((1, A_WIDTH)), _const_spec((A_WIDTH, A_WIDTH)),
        ],
        out_specs=[y_spec, state_spec, shift_spec],
        out_shape=[
            jax.ShapeDtypeStruct((bsz, t_len, A_WIDTH), F32),
            jax.ShapeDtypeStruct((bsz, A_HEADS, hd, hd), F32),
            jax.ShapeDtypeStruct((bsz, 1, A_PROJ), F32),
        ],
        scratch_shapes=scratch,
        compiler_params=_params(1),
        name="rwkv7",
    )(u_a, shift0.reshape(bsz, 1, A_PROJ), s0, row(p['rwkv_mu']), row(p['rwkv_w0']), p['rwkv_w2'],
      row(p['rwkv_a0']), p['rwkv_a2'], p['rwkv_g2'], row(p['rwkv_k_k']), row(p['rwkv_k_a']),
      row(p['rwkv_r_k']), row(p['rwkv_lnx_g']), row(p['rwkv_lnx_b']), bd)


def _swa_prompt_kernel(qkv_ref, o_ref, lse_ref):
    blk, hd, n_g = SWA_BLOCK, B_HEAD_DIM, len(B_GROUPS)
    s_len = qkv_ref.shape[1]
    scale = hd ** -0.5
    rows2 = SWA_PAIR * blk
    r_i = lax.broadcasted_iota(jnp.int32, (rows2, 1), 0)
    qi = r_i % blk
    own_lanes = (lax.broadcasted_iota(jnp.int32, (1, LANES), 1) // hd) == (r_i // blk)
    head0_lanes = own_lanes[:blk]
    ki2 = lax.broadcasted_iota(jnp.int32, (rows2, 2 * blk), 1)
    band2 = (qi + blk - ki2 >= 0) & (qi - ki2 <= 0)
    causal1 = lax.broadcasted_iota(jnp.int32, (rows2, blk), 1) <= qi

    def attend(g, specs):
        dil = B_GROUPS[g][1]

        def rows(kind, st):
            idx = pl.ds(st, blk, stride=dil) if dil > 1 else pl.ds(st, blk)
            return qkv_ref[kind * n_g + g, idx, :]

        qs, kbs, vbs, masks = [], [], [], []
        for st, prev, first in specs:
            q = rows(0, st)
            qs.append(jnp.where(own_lanes, jnp.concatenate([q] * SWA_PAIR, axis=0), 0.0).astype(BF16))
            if prev is None:
                kbs.append(rows(1, st).astype(BF16))
                vbs.append(rows(2, st).astype(BF16))
                masks.append(causal1)
            else:
                kbs.append(jnp.concatenate([rows(1, prev), rows(1, st)], axis=0).astype(BF16))
                vbs.append(jnp.concatenate([rows(2, prev), rows(2, st)], axis=0).astype(BF16))
                masks.append(band2 & (ki2 >= jnp.where(first, blk, 0)))
        scores = [_dot_nt(q, kb) * scale for q, kb in zip(qs, kbs)]
        probs, lses = [], []
        for sc, mk in zip(scores, masks):
            sc = jnp.where(mk, sc, NEG_INF)
            m = jnp.max(sc, axis=-1, keepdims=True)
            p = jnp.exp(sc - m)
            den = jnp.sum(p, axis=-1, keepdims=True)
            probs.append((p / den).astype(BF16))
            lses.append(m + jnp.log(den))
        outs = [_dot(p, vb) for p, vb in zip(probs, vbs)]
        for (st, _, _), o2, l2 in zip(specs, outs, lses):
            idx = pl.ds(st, blk, stride=dil) if dil > 1 else pl.ds(st, blk)
            o_ref[g, idx, :] = jnp.where(head0_lanes, o2[:blk], o2[blk:])
            lse_ref[g, idx, :] = jnp.where(head0_lanes, l2[:blk], l2[blk:])

    for g, (_, dil) in enumerate(B_GROUPS):
        n_blk = s_len // dil // blk
        n_quads = dil * n_blk // SWA_QUAD

        def quad(it, carry, g=g, dil=dil, n_blk=n_blk):
            specs = []
            for j in range(SWA_QUAD):
                if n_blk >= SWA_QUAD:
                    e = it * SWA_QUAD + j
                    r, n = e // n_blk, e % n_blk
                    specs.append((n * (blk * dil) + r, jnp.maximum(n - 1, 0) * (blk * dil) + r, n == 0))
                else:
                    r = it * (SWA_QUAD // n_blk) + j // n_blk
                    n = j % n_blk
                    specs.append((n * (blk * dil) + r, None if n == 0 else (n - 1) * (blk * dil) + r, False))
            attend(g, specs)
            return carry

        lax.fori_loop(0, n_quads, quad, 0)


def _swa_prompt(qkv_slabs, n_seq, s_len):
    n_g = len(B_GROUPS)
    n = n_seq * s_len
    out_spec = pl.BlockSpec((n_g, s_len, LANES), lambda b, p: (p, b, 0))
    shp = jax.ShapeDtypeStruct((SWA_PAIRS * n_g, n, LANES), F32)
    return pl.pallas_call(
        _swa_prompt_kernel,
        grid=(n_seq, SWA_PAIRS),
        in_specs=[pl.BlockSpec((3 * n_g, s_len, LANES), lambda b, p: (p, b, 0))],
        out_specs=[out_spec, out_spec],
        out_shape=[shp, shp],
        compiler_params=_params(2),
        name="swa_prompt",
    )(qkv_slabs)


def _swa_sample_kernel(qkv_ref, c0_ref, c1_ref, c2_ref, o_ref, lse_ref, n0_ref, n1_ref, n2_ref, *, t_len, exact):
    hd = B_HEAD_DIM
    nh = B_HEADS_PER_GROUP
    gw = B_GROUP_WIDTH
    qkv = qkv_ref[0]
    scale = hd ** -0.5
    rnd = lambda t: _contract_round(t, exact)
    t_col = lax.broadcasted_iota(jnp.int32, (t_len, 1), 0)
    i_n = lax.broadcasted_iota(jnp.int32, (t_len, t_len), 1)
    units = []
    for gi, ((window, dil), c_ref, n_ref) in enumerate(zip(B_GROUPS, (c0_ref, c1_ref, c2_ref), (n0_ref, n1_ref, n2_ref))):
        buf_len = c_ref.shape[4]
        j_c = lax.broadcasted_iota(jnp.int32, (t_len, buf_len), 1)
        ok_c = (j_c >= t_col) & (((j_c - t_col) & (dil - 1)) == 0)
        ok_n = (i_n <= t_col) & (((t_col - i_n) & (dil - 1)) == 0)
        for h in range(nh):
            lo = gi * gw + h * hd
            units.append(dict(
                q=qkv[:, lo:lo + hd], k_new=qkv[:, B_WIDTH + lo:B_WIDTH + lo + hd],
                v_new=qkv[:, 2 * B_WIDTH + lo:2 * B_WIDTH + lo + hd],
                kt=c_ref[0, 0, h], vt=c_ref[0, 1, h], ok_c=ok_c, ok_n=ok_n))
    s_cs = [jnp.where(u["ok_c"], _mm(u["q"], u["kt"], exact) * scale, NEG_INF) for u in units]
    probs, p_news, lses = [], [], []
    for u, s_c in zip(units, s_cs):
        qr, kr = rnd(u["q"]), rnd(u["k_new"])
        s_n = jnp.zeros((t_len, t_len), F32)
        for i in range(t_len):
            s_n = jnp.where(i_n == i, jnp.sum(qr * kr[i:i + 1, :], axis=-1, keepdims=True) * scale, s_n)
        s_n = jnp.where(u["ok_n"], s_n, NEG_INF)
        m = jnp.maximum(jnp.max(s_c, axis=-1, keepdims=True), jnp.max(s_n, axis=-1, keepdims=True))
        p_c = jnp.exp(s_c - m)
        p_n = jnp.exp(s_n - m)
        den = jnp.sum(p_c, axis=-1, keepdims=True) + jnp.sum(p_n, axis=-1, keepdims=True)
        probs.append(p_c / den)
        p_news.append(rnd(p_n / den))
        lses.append(jnp.broadcast_to(m + jnp.log(den), (t_len, hd)))
    outs = [_mm(p, u["vt"], exact, _NT) for p, u in zip(probs, units)]
    for k, (u, p_n) in enumerate(zip(units, p_news)):
        vr = rnd(u["v_new"])
        for i in range(t_len):
            outs[k] = outs[k] + p_n[:, i:i + 1] * vr[i:i + 1, :]
    pad_rows = 8 - t_len
    place = (lax.broadcasted_iota(jnp.int32, (8, LANES), 0) + (LANES - t_len)
             == lax.broadcasted_iota(jnp.int32, (8, LANES), 1)).astype(BF16)
    tail_lane = lax.broadcasted_iota(jnp.int32, (1, LANES), 1) >= LANES - t_len
    for gi, (c_ref, n_ref) in enumerate(zip((c0_ref, c1_ref, c2_ref), (n0_ref, n1_ref, n2_ref))):
        buf_len = c_ref.shape[4]
        for j in range(2):
            new = jnp.pad(qkv[:, (j + 1) * B_WIDTH + gi * gw:(j + 1) * B_WIDTH + (gi + 1) * gw], ((0, pad_rows), (0, 0)))
            cols, rest = None, new
            for _ in range(3):
                piece = rest.astype(BF16)
                rest = rest - piece.astype(F32)
                part = _dot_tn(piece, place)
                cols = part if cols is None else cols + part
            nxt = pltpu.roll(c_ref[0, j].reshape(gw, buf_len), buf_len - t_len, 1)
            tail = jnp.where(tail_lane, cols, nxt[:, buf_len - LANES:])
            full = tail if buf_len == LANES else jnp.concatenate([nxt[:, :buf_len - LANES], tail], axis=1)
            n_ref[0, j] = full.reshape(nh, hd, buf_len)
    o_ref[0] = jnp.concatenate(outs, axis=1)
    lse_ref[0] = jnp.concatenate(lses, axis=1)


def _swa_sample(qkv, caches):
    bsz, t_len, _ = qkv.shape
    cache_specs = [pl.BlockSpec((1,) + c.shape[1:], lambda b: (b, 0, 0, 0, 0)) for c in caches]
    row_spec = pl.BlockSpec((1, t_len, B_WIDTH), lambda b: (b, 0, 0))
    return pl.pallas_call(
        functools.partial(_swa_sample_kernel, t_len=t_len, exact=True),
        grid=(bsz,),
        in_specs=[pl.BlockSpec((1, t_len, B_PROJ), lambda b: (b, 0, 0))] + cache_specs,
        out_specs=[row_spec, row_spec] + cache_specs,
        out_shape=[jax.ShapeDtypeStruct((bsz, t_len, B_WIDTH), F32)] * 2
        + [jax.ShapeDtypeStruct(c.shape, F32) for c in caches],
        compiler_params=_params(1),
        name="swa_sample",
    )(qkv, *caches)


def _route(logits):
    lane = lax.broadcasted_iota(jnp.int32, logits.shape, 1)
    gl = jnp.where(lane < N_GROUPS, logits, NEG_INF)
    gmax = jnp.max(gl, axis=-1, keepdims=True)
    grp = jnp.min(jnp.where(gl == gmax, lane, LANES), axis=-1, keepdims=True)
    w_grp = 1.0 / jnp.sum(jnp.exp(gl - gmax), axis=-1, keepdims=True)
    first = N_GROUPS + grp * EXPERTS_PER_GROUP
    el = jnp.where((lane >= first) & (lane < first + EXPERTS_PER_GROUP), logits, NEG_INF)
    m1 = jnp.max(el, axis=-1, keepdims=True)
    i1 = jnp.min(jnp.where(el == m1, lane, LANES), axis=-1, keepdims=True)
    el2 = jnp.where(lane == i1, NEG_INF, el)
    m2 = jnp.max(el2, axis=-1, keepdims=True)
    i2 = jnp.min(jnp.where(el2 == m2, lane, LANES), axis=-1, keepdims=True)
    e2 = jnp.exp(m2 - m1)
    g1 = w_grp / (1.0 + e2)
    g2 = w_grp * e2 / (1.0 + e2)
    route = jnp.where(lane == 0, (i1 - N_GROUPS).astype(F32), 0.0)
    route = jnp.where(lane == 1, (i2 - N_GROUPS).astype(F32), route)
    route = jnp.where(lane == 2, g1, route)
    return jnp.where(lane == 3, g2, route)


def _mid_kernel(x_ref, ya_ref, ob_ref, lse_ref, gate_ref, mk_ref, mv_ref, wa_ref, wb_ref, wo_ref, gm_ref,
                wq_ref, wmo_ref, gf_ref, wr_ref, br_ref, xn_all_ref, x2_ref, xn_ref, route_ref,
                *, rows_per_batch, exact, n_sub):
    del xn_all_ref
    tm = x_ref.shape[0]
    sub = tm // n_sub
    pieces = [slice(i * sub, (i + 1) * sub) for i in range(n_sub)]
    n_g = len(B_GROUPS)
    rnd = lambda t: _contract_round(t, exact)
    mm = lambda a, b, dims=_NN: _mm(a, b, exact, dims)

    def mix(rs):
        yb_pairs = []
        for p in range(SWA_PAIRS):
            lses = [lse_ref[p * n_g + g, rs, :] for g in range(n_g)]
            m = functools.reduce(jnp.maximum, lses)
            es = [jnp.exp(l - m) for l in lses]
            den = functools.reduce(lambda a, b: a + b, es)
            yb_pairs.append(functools.reduce(lambda a, b: a + b,
                                             [rnd(es[g] / den) * rnd(ob_ref[p * n_g + g, rs, :]) for g in range(n_g)]))
        return jnp.concatenate(yb_pairs, axis=1)

    yb = [mix(rs) for rs in pieces]
    ma = [mm(ya_ref[rs, :], wa_ref[...]) for rs in pieces]
    mb = [mm(t, wb_ref[...]) for t in yb]
    merged = [gate_ref[rs, :D_MODEL] * a + gate_ref[rs, D_MODEL:] * b for rs, a, b in zip(pieces, ma, mb)]
    x1 = [x_ref[rs, :] + mm(t, wo_ref[...]) for rs, t in zip(pieces, merged)]

    q = [mm(_rms(t, gm_ref[...]), wq_ref[...]) for t in x1]
    n_b = mk_ref.shape[0]
    mk = mk_ref[...].reshape(n_b * N_MEM, MEM_WIDTH)
    mv = mv_ref[...].reshape(n_b * N_MEM, MEM_WIDTH)
    if not exact:
        mk, mv = mk.astype(BF16), mv.astype(BF16)
    if n_b > 1:
        assert n_sub == 1
        rb = lax.broadcasted_iota(jnp.int32, (tm, n_b * N_MEM), 0) // rows_per_batch
        cb = lax.broadcasted_iota(jnp.int32, (tm, n_b * N_MEM), 1) // N_MEM
        same = rb == cb
    units = [(i, slice(h * MEM_HEAD_DIM, (h + 1) * MEM_HEAD_DIM)) for i in range(n_sub) for h in range(MEM_HEADS)]
    scores = [mm(q[i][:, sl], mk[:, sl], _NT) * (MEM_HEAD_DIM ** -0.5) for i, sl in units]
    probs = []
    for s_ in scores:
        if n_b > 1:
            s_ = jnp.where(same, s_, NEG_INF)
        p = jnp.exp(s_ - jnp.max(s_, axis=-1, keepdims=True))
        probs.append(p / jnp.sum(p, axis=-1, keepdims=True))
    pv = [mm(p, mv[:, sl]) for p, (_, sl) in zip(probs, units)]
    att = [jnp.concatenate(pv[i * MEM_HEADS:(i + 1) * MEM_HEADS], axis=1) for i in range(n_sub)]
    x2 = [a + mm(t, wmo_ref[...]) for a, t in zip(x1, att)]
    xn = [_rms(t, gf_ref[...]) for t in x2]
    logits = [mm(t, wr_ref[...]) + br_ref[...] for t in xn]
    for rs, a, b, c in zip(pieces, x2, xn, logits):
        x2_ref[rs, :] = a
        xn_ref[rs, :] = b
        route_ref[rs, :] = _route(c)


def _mid(x, y_a, o_b, lse_b, gates, mem_k, mem_v, mk_map, mv_map, n_b, rows_per_batch, w, tm, xn_all, row0):
    n = x.shape[0]
    row = lambda width: pl.BlockSpec((tm, width), lambda i: (i, 0))
    slabs = pl.BlockSpec((SWA_PAIRS * len(B_GROUPS), tm, LANES), lambda i: (0, i, 0))
    in_specs = [
        row(D_MODEL), row(A_WIDTH), slabs, slabs, row(2 * D_MODEL),
        pl.BlockSpec((n_b, N_MEM, MEM_WIDTH), mk_map), pl.BlockSpec((n_b, N_MEM, MEM_WIDTH), mv_map),
        _const_spec((A_WIDTH, D_MODEL)), _const_spec((B_GROUP_WIDTH, D_MODEL)), _const_spec((D_MODEL, D_MODEL)),
        _const_spec((1, D_MODEL)), _const_spec((D_MODEL, MEM_WIDTH)), _const_spec((MEM_WIDTH, D_MODEL)),
        _const_spec((1, D_MODEL)), _const_spec((D_MODEL, LANES)), _const_spec((1, LANES)),
        pl.BlockSpec(memory_space=pl.ANY),
    ]
    args = [x, y_a, o_b, lse_b, gates, mem_k, mem_v, w['wa'], w['wb'], w['wo'], w['gm'], w['wq'], w['wmo'],
            w['gf'], w['wr'], w['br'], xn_all]
    blk0 = row0 // tm
    return pl.pallas_call(
        functools.partial(_mid_kernel, rows_per_batch=rows_per_batch, exact=w['wa'].dtype == F32,
                          n_sub=MID_SUB if n_b == 1 else 1),
        grid=(n // tm,),
        in_specs=in_specs,
        out_specs=[row(D_MODEL), pl.BlockSpec((tm, D_MODEL), lambda i: (i + blk0, 0)), row(LANES)],
        out_shape=[
            jax.ShapeDtypeStruct((n, D_MODEL), F32),
            jax.ShapeDtypeStruct(xn_all.shape, F32),
            jax.ShapeDtypeStruct((n, LANES), F32),
        ],
        input_output_aliases={len(args) - 1: 1},
        compiler_params=_params(1),
        name="mid",
    )(*args)


def _row_copy(src_hbm, idx, dst_buf, slot, j, sem):
    return pltpu.make_async_copy(src_hbm.at[pl.ds(idx, 1), :], dst_buf.at[slot, pl.ds(j, 1), :], sem.at[slot])


def _gather_start(idx_ref, src_hbm, dst_buf, slot, sem, n_rows):
    for j in range(n_rows):
        _row_copy(src_hbm, idx_ref[0, 0, j], dst_buf, slot, j, sem).start(priority=j % 2)


def _gather_wait(src_hbm, dst_buf, slot, sem, n_rows):
    for j in range(n_rows):
        _row_copy(src_hbm, 0, dst_buf, slot, j, sem).wait()


def _experts_kernel(meta_ref, be_ref, idx_ref, idx_next_ref, x_hbm, wg_ref, wu_ref, wd_ref, o_ref, xbuf, sem):
    i = pl.program_id(0)
    n_used = meta_ref[0]
    slot = i % 2

    @pl.when(i == 0)
    def _():
        _gather_start(idx_ref, x_hbm, xbuf, 0, sem, MOE_ROWS)

    @pl.when(i + 1 < n_used)
    def _():
        _gather_start(idx_next_ref, x_hbm, xbuf, 1 - slot, sem, MOE_ROWS)

    @pl.when(i < n_used)
    def _():
        _gather_wait(x_hbm, xbuf, slot, sem, MOE_ROWS)
        xb = xbuf[slot].astype(BF16)
        hg = _dot(xb, wg_ref[0].astype(BF16))
        hu = _dot(xb, wu_ref[0].astype(BF16))
        hh = (jax.nn.silu(hg) * hu).astype(BF16)
        o_ref[...] = _dot(hh, wd_ref[0].astype(BF16))

    @pl.when(i >= n_used)
    def _():
        o_ref[...] = jnp.zeros_like(o_ref)


def _experts(xn_all, row_tok, block_e, n_used, w_gate, w_up, w_down):
    n_blocks = block_e.shape[0]
    idx3 = row_tok.reshape(n_blocks, 1, MOE_ROWS)
    idx_spec = lambda f: pl.BlockSpec((1, 1, MOE_ROWS), f, memory_space=pltpu.SMEM)
    grid_spec = pltpu.PrefetchScalarGridSpec(
        num_scalar_prefetch=2,
        grid=(n_blocks,),
        in_specs=[
            idx_spec(lambda i, meta, be: (i, 0, 0)),
            idx_spec(lambda i, meta, be: (jnp.minimum(i + 1, n_blocks - 1), 0, 0)),
            pl.BlockSpec(memory_space=pl.ANY),
            pl.BlockSpec((1, D_MODEL, EXPERT_FF), lambda i, meta, be: (be[i], 0, 0)),
            pl.BlockSpec((1, D_MODEL, EXPERT_FF), lambda i, meta, be: (be[i], 0, 0)),
            pl.BlockSpec((1, EXPERT_FF, D_MODEL), lambda i, meta, be: (be[i], 0, 0)),
        ],
        out_specs=pl.BlockSpec((MOE_ROWS, D_MODEL), lambda i, meta, be: (i, 0)),
        scratch_shapes=[pltpu.VMEM((2, MOE_ROWS, D_MODEL), F32), pltpu.SemaphoreType.DMA((2,))],
    )
    return pl.pallas_call(
        _experts_kernel,
        grid_spec=grid_spec,
        out_shape=jax.ShapeDtypeStruct((n_blocks * MOE_ROWS, D_MODEL), F32),
        compiler_params=_params(1),
        name="experts",
    )(n_used.reshape(1), block_e, idx3, idx3, xn_all, w_gate, w_up, w_down)


def _combine_kernel(pos_ref, pos_next_ref, x_ref, route_ref, yb_hbm, g_ref, o_ref, ybuf, sem, *, n_tiles):
    i = pl.program_id(0)
    tm = x_ref.shape[0]
    slot = i % 2

    @pl.when(i == 0)
    def _():
        _gather_start(pos_ref, yb_hbm, ybuf, 0, sem, 2 * tm)

    if n_tiles > 1:
        @pl.when(i + 1 < n_tiles)
        def _():
            _gather_start(pos_next_ref, yb_hbm, ybuf, 1 - slot, sem, 2 * tm)

    _gather_wait(yb_hbm, ybuf, slot, sem, 2 * tm)
    route = route_ref[...]
    y = x_ref[...] + (route[:, 2:3] * ybuf[slot, :tm, :] + route[:, 3:4] * ybuf[slot, tm:, :])
    o_ref[...] = _rms(y, g_ref[...])


def _combine(x2, route, pos, yb, g_final, tm):
    n = x2.shape[0]
    n_tiles = n // tm
    pos_spec = lambda f: pl.BlockSpec((1, 1, 2 * tm), f, memory_space=pltpu.SMEM)
    return pl.pallas_call(
        functools.partial(_combine_kernel, n_tiles=n_tiles),
        grid=(n_tiles,),
        in_specs=[
            pos_spec(lambda i: (i, 0, 0)),
            pos_spec(lambda i: (jnp.minimum(i + 1, n_tiles - 1), 0, 0)),
            pl.BlockSpec((tm, D_MODEL), lambda i: (i, 0)),
            pl.BlockSpec((tm, LANES), lambda i: (i, 0)),
            pl.BlockSpec(memory_space=pl.ANY),
            _const_spec((1, D_MODEL)),
        ],
        out_specs=pl.BlockSpec((tm, D_MODEL), lambda i: (i, 0)),
        out_shape=jax.ShapeDtypeStruct((n, D_MODEL), F32),
        scratch_shapes=[pltpu.VMEM((2, 2 * tm, D_MODEL), F32), pltpu.SemaphoreType.DMA((2,))],
        compiler_params=_params(1),
        name="combine",
    )(pos, pos, x2, route, yb, g_final.reshape(1, -1))


def _dispatch(eid):
    n_tok = eid.shape[0]
    n_rows = n_tok * TOP_K
    n_blocks = n_rows // MOE_ROWS + N_EXPERTS
    flat_e = eid.reshape(-1)
    onehot = (flat_e[:, None] == jnp.arange(N_EXPERTS, dtype=jnp.int32)[None, :]).astype(jnp.int32)
    csum = jnp.cumsum(onehot, axis=0)
    counts = csum[-1]
    rank = jnp.sum((csum - onehot) * onehot, axis=1)
    padded = (counts + MOE_ROWS - 1) // MOE_ROWS * MOE_ROWS
    pad_end = jnp.cumsum(padded)
    pad_start = pad_end - padded
    dest = pad_start[flat_e] + rank
    flat_tok = jnp.arange(n_rows, dtype=jnp.int32) // TOP_K
    row_tok = jnp.zeros((n_blocks * MOE_ROWS,), jnp.int32).at[dest].set(flat_tok, unique_indices=True)
    block_start = jnp.arange(n_blocks, dtype=jnp.int32) * MOE_ROWS
    block_e = jnp.minimum(jnp.sum((pad_end[None, :] <= block_start[:, None]).astype(jnp.int32), axis=1), N_EXPERTS - 1)
    n_used = (pad_end[-1] // MOE_ROWS).astype(jnp.int32)
    return row_tok, block_e, n_used, dest.reshape(n_tok, TOP_K).astype(jnp.int32)


def _tile_pos(pos, tm):
    n = pos.shape[0]
    return pos.reshape(n // tm, tm, TOP_K).transpose(0, 2, 1).reshape(n // tm, 1, TOP_K * tm)


def _slabs_to_rows(qkv_slabs):
    order = np.array([_slab_position(j) for j in range(SWA_SLABS)])
    return jnp.transpose(qkv_slabs[order], (1, 0, 2)).reshape(qkv_slabs.shape[1], B_PROJ)


def _rows_to_pair_slabs(t):
    n = t.shape[0]
    return jnp.transpose(t.reshape(n, len(B_GROUPS), SWA_PAIRS, LANES), (2, 1, 0, 3)).reshape(-1, n, LANES)


def kernel(x_prompt, x_sample, state_rwkv, state_shift, cache_swa_w128, cache_swa_w512, cache_swa_w2048,
           cache_mem_kv, mem_prompt, norm_mix_g, w_in, b_gate, rwkv_mu, rwkv_w0, rwkv_w2, rwkv_a0, rwkv_a2,
           rwkv_g2, rwkv_k_k, rwkv_k_a, rwkv_r_k, rwkv_lnx_g, rwkv_lnx_b, w_branch_a, w_branch_b, w_out,
           norm_mem_g, norm_memkv_g, w_mem_q, w_mem_kv, w_mem_out, norm_ffn_g, w_router_group, b_router_group,
           w_router_expert, b_router_expert, w_exp_gate, w_exp_up, w_exp_down, norm_final_g):
    bsz, s_len, _ = x_prompt.shape
    dbs, t_len, _ = x_sample.shape
    n_p, n_s = bsz * s_len, dbs * t_len
    rw = dict(rwkv_mu=rwkv_mu, rwkv_w0=rwkv_w0, rwkv_w2=rwkv_w2, rwkv_a0=rwkv_a0, rwkv_a2=rwkv_a2, rwkv_g2=rwkv_g2,
              rwkv_k_k=rwkv_k_k, rwkv_k_a=rwkv_k_a, rwkv_r_k=rwkv_r_k, rwkv_lnx_g=rwkv_lnx_g, rwkv_lnx_b=rwkv_lnx_b)
    pad = LANES - N_GROUPS - N_EXPERTS
    mid_w_s = dict(
        wa=w_branch_a, wb=w_branch_b, wo=w_out, gm=norm_mem_g.reshape(1, -1), wq=w_mem_q, wmo=w_mem_out,
        gf=norm_ffn_g.reshape(1, -1),
        wr=jnp.concatenate([w_router_group, w_router_expert, jnp.zeros((D_MODEL, pad), F32)], axis=1),
        br=jnp.concatenate([b_router_group, b_router_expert, jnp.zeros((pad,), F32)]).reshape(1, -1))
    mid_w_p = {k: (v.astype(BF16) if k.startswith('w') else v) for k, v in mid_w_s.items()}
    w_in_b = w_in.astype(BF16)

    xp = x_prompt.reshape(n_p, D_MODEL)
    tm = 512
    ua_p, qkv_p, gates_p, *p_caches = _in_proj(xp, norm_mix_g, w_in_b, b_gate, _rope_tables(jnp.arange(s_len)), tm,
                                               cache_seq=(bsz, s_len))
    s0 = jnp.zeros((bsz, A_HEADS, A_HEAD_DIM, A_HEAD_DIM), F32)
    ya_p, st_p, shift_p = _rwkv(ua_p.reshape(bsz, s_len, A_PROJ), jnp.zeros((bsz, A_PROJ), F32), s0, rw,
                                RWKV_CHUNK, s_len)
    ob_p, lse_p = _swa_prompt(qkv_p, bsz, s_len)
    p_bufs = [c.reshape(c.shape[:3] + (B_HEADS_PER_GROUP, B_HEAD_DIM)) for c in p_caches]

    memkv = _norm_matmul(mem_prompt.reshape(bsz * N_MEM, D_MODEL), norm_memkv_g, w_mem_kv.astype(BF16), 256)
    memkv3 = memkv.reshape(bsz, N_MEM, 2 * MEM_WIDTH)
    mem_kv_prompt = memkv3.reshape(bsz, N_MEM, 2, MEM_HEADS, MEM_HEAD_DIM).transpose(0, 2, 1, 3, 4)
    tiles_per_batch = s_len // tm
    xn_all = jnp.zeros((n_p + n_s, D_MODEL), F32)
    x2_p, xn_all, route_p = _mid(
        xp, ya_p.reshape(n_p, A_WIDTH), ob_p, lse_p, gates_p, memkv3, memkv3,
        lambda i: (i // tiles_per_batch, 0, 0), lambda i: (i // tiles_per_batch, 0, 1),
        1, s_len, mid_w_p, tm, xn_all, 0)

    xs = x_sample.reshape(n_s, D_MODEL)
    pos_s = PAST_LEN + (jnp.arange(n_s) % t_len)
    ua_s, qkv_s, gates_s = _in_proj(xs, norm_mix_g, w_in, b_gate, _rope_tables(pos_s), n_s)
    t_pad = 8
    ua_s3 = jnp.pad(ua_s.reshape(dbs, t_len, A_PROJ), ((0, 0), (0, t_pad - t_len), (0, 0)))
    ya_s, st_s, shift_s = _rwkv(ua_s3, state_shift, state_rwkv, rw, 0, t_len)
    ya_s = ya_s[:, :t_len].reshape(n_s, A_WIDTH)
    caches = [jnp.transpose(c, (0, 1, 3, 4, 2)) for c in (cache_swa_w128, cache_swa_w512, cache_swa_w2048)]
    ob_s, lse_s, nb0, nb1, nb2 = _swa_sample(_slabs_to_rows(qkv_s).reshape(dbs, t_len, B_PROJ), caches)
    s_bufs = [jnp.transpose(nb, (0, 1, 4, 2, 3)) for nb in (nb0, nb1, nb2)]
    mem_s = cache_mem_kv.reshape(dbs, 2 * N_MEM, MEM_WIDTH)
    tm_s = 32
    x2_s, xn_all, route_s = _mid(
        xs, ya_s, _rows_to_pair_slabs(ob_s.reshape(n_s, B_WIDTH)), _rows_to_pair_slabs(lse_s.reshape(n_s, B_WIDTH)),
        gates_s, mem_s, mem_s,
        lambda i: (i, 0, 0), lambda i: (i, 1, 0), tm_s // t_len, t_len, mid_w_s, tm_s, xn_all, n_p)

    route = jnp.concatenate([route_p, route_s], axis=0)
    eid = route[:, :TOP_K].astype(jnp.int32)
    row_tok, block_e, n_used, pos = _dispatch(eid)
    yb = _experts(xn_all, row_tok, block_e, n_used, w_exp_gate, w_exp_up, w_exp_down)
    tm_c = 128
    y_p = _combine(x2_p, route_p, _tile_pos(pos[:n_p], tm_c), yb, norm_final_g, tm_c)
    y_s = _combine(x2_s, route_s, _tile_pos(pos[n_p:], tm_c), yb, norm_final_g, tm_c)

    return (y_p.reshape(bsz, s_len, D_MODEL), y_s.reshape(dbs, t_len, D_MODEL),
            st_p, shift_p.reshape(bsz, A_PROJ), p_bufs[0], p_bufs[1], p_bufs[2], mem_kv_prompt,
            st_s, shift_s.reshape(dbs, A_PROJ), s_bufs[0], s_bufs[1], s_bufs[2])
```

```python
import functools
import math

import jax
import jax.numpy as jnp
import numpy as np
from jax import lax
from jax.experimental import pallas as pl
from jax.experimental.pallas import tpu as pltpu

F32 = jnp.float32
BF16 = jnp.bfloat16

D_MODEL = 1024
A_HEADS = 8
A_HEAD_DIM = 64
A_WIDTH = A_HEADS * A_HEAD_DIM
A_DECAY_LORA = 64
A_ICLR_LORA = 64
A_GATE_LORA = 128
A_PROJ = 3 * A_WIDTH + A_DECAY_LORA + A_ICLR_LORA + A_GATE_LORA
A_LNX_EPS = 64e-5
B_GROUPS = ((128, 1), (512, 4), (2048, 16))
B_HEADS_PER_GROUP = 4
B_HEAD_DIM = 64
B_GROUP_WIDTH = B_HEADS_PER_GROUP * B_HEAD_DIM
B_WIDTH = B_GROUP_WIDTH * len(B_GROUPS)
B_PROJ = 3 * B_WIDTH
ROPE_THETA = 500000.0
ROPE_DIM = B_HEAD_DIM // 4
ROPE_HALF = ROPE_DIM // 2
SWA_BLOCK = 128
N_MEM = 256
MEM_HEADS = 4
MEM_HEAD_DIM = 128
MEM_WIDTH = MEM_HEADS * MEM_HEAD_DIM
N_GROUPS = 4
EXPERTS_PER_GROUP = 8
N_EXPERTS = N_GROUPS * EXPERTS_PER_GROUP
TOP_K = 2
EXPERT_FF = 512
RMS_EPS = 1e-6
PAST_LEN = 8192

LANES = 128
SWA_SLABS = B_PROJ // LANES
SWA_PAIR = LANES // B_HEAD_DIM
SWA_PAIRS = B_GROUP_WIDTH // LANES
SWA_QUAD = 4
MID_SUB = 2
VMEM_LIMIT = 56 * 1024 * 1024
RWKV_CHUNK = 64
MOE_ROWS = 256
NEG_INF = float("-inf")


def _dot(a, b, precision=None):
    return jnp.dot(a, b, preferred_element_type=F32, precision=precision)


def _dot_nt(a, b, precision=None):
    return lax.dot_general(a, b, (((1,), (1,)), ((), ())), preferred_element_type=F32, precision=precision)


def _dot_tn(a, b, precision=None):
    return lax.dot_general(a, b, (((0,), (0,)), ((), ())), preferred_element_type=F32, precision=precision)


def _bf16_round(t):
    return t.astype(BF16).astype(F32)


_NN = (((1,), (0,)), ((), ()))
_NT = (((1,), (1,)), ((), ()))


def _mm(a, b, exact, dims=_NN):
    dg = lambda x, y: lax.dot_general(x, y, dims, preferred_element_type=F32)
    if not exact:
        return dg(a.astype(BF16), b.astype(BF16))
    a, b = a.astype(F32), b.astype(F32)
    a_hi, b_hi = a.astype(BF16), b.astype(BF16)
    a_lo = (a - a_hi.astype(F32)).astype(BF16)
    b_lo = (b - b_hi.astype(F32)).astype(BF16)
    return dg(a_hi, b_hi) + (dg(a_hi, b_lo) + dg(a_lo, b_hi))


def _contract_round(t, exact):
    return t if exact else _bf16_round(t)


def _rms(x, g):
    return x * lax.rsqrt(jnp.mean(x * x, axis=-1, keepdims=True) + RMS_EPS) * g


def _const_spec(shape):
    nd = len(shape)
    return pl.BlockSpec(shape, lambda *_: (0,) * nd, pipeline_mode=pl.Buffered(1))


def _params(n_axes):
    return pltpu.CompilerParams(dimension_semantics=("arbitrary",) * n_axes, vmem_limit_bytes=VMEM_LIMIT)


def _slab_position(j):
    kind, hs = divmod(j, SWA_SLABS // 3)
    g, p = divmod(hs, SWA_PAIRS)
    return p * (3 * len(B_GROUPS)) + kind * len(B_GROUPS) + g


def _in_proj_kernel(x_ref, g_ref, w_ref, bg_ref, rc_ref, rs1_ref, rs2_ref, ua_ref, qkv_ref, gate_ref, *cache_refs, exact):
    tm = x_ref.shape[0]
    xn = _rms(x_ref[...], g_ref[...])
    xn = xn if exact else xn.astype(BF16)
    mm = lambda w: _mm(xn, w, exact)
    ua_ref[...] = mm(w_ref[:, :A_PROJ])
    rc, rs1, rs2 = rc_ref[...], rs1_ref[...], rs2_ref[...]
    n_rot = 2 * B_WIDTH // LANES
    for j in range(SWA_SLABS):
        if j % 2 == 0:
            lo = A_PROJ + j * LANES
            s2 = mm(w_ref[:, lo:lo + 2 * LANES])
        s = s2[:, (j % 2) * LANES:(j % 2 + 1) * LANES]
        if j < n_rot:
            s = s * rc + pltpu.roll(s, LANES - ROPE_HALF, 1) * rs1 + pltpu.roll(s, ROPE_HALF, 1) * rs2
        qkv_ref[_slab_position(j)] = s
        kind, hs = divmod(j, SWA_SLABS // 3)
        if cache_refs and kind > 0:
            g, p = divmod(hs, SWA_PAIRS)
            rows = cache_refs[g].shape[2]
            cache_refs[g][0, kind - 1, :, p * LANES:(p + 1) * LANES] = s[tm - rows:, :]
    gate_ref[...] = jax.nn.sigmoid(mm(w_ref[:, A_PROJ + B_PROJ:]) + bg_ref[...])


def _rope_tables(pos):
    inv_freq = ROPE_THETA ** (-jnp.arange(ROPE_HALF, dtype=F32) * 2.0 / ROPE_DIM)
    ang = pos.astype(F32)[:, None] * inv_freq[None, :]
    cos, sin = jnp.cos(ang), jnp.sin(ang)
    n = pos.shape[0]
    rest = B_HEAD_DIM - ROPE_DIM
    c = jnp.concatenate([cos, cos, jnp.ones((n, rest), F32)], axis=1)
    s1 = jnp.concatenate([-sin, jnp.zeros((n, ROPE_HALF + rest), F32)], axis=1)
    s2 = jnp.concatenate([jnp.zeros((n, ROPE_HALF), F32), sin, jnp.zeros((n, rest), F32)], axis=1)
    rep = LANES // B_HEAD_DIM
    return tuple(jnp.tile(t, (1, rep)) for t in (c, s1, s2))


def _in_proj(x, g, w, b_gate, tables, tm, cache_seq=None):
    n = x.shape[0]
    p_rows = tables[0].shape[0]
    t_tiles = p_rows // tm
    in_proj_w = w.shape[1]
    tab_spec = pl.BlockSpec((tm, LANES), lambda i: (i % t_tiles, 0))
    cache_specs, cache_shapes = [], []
    if cache_seq is not None:
        n_seq, s_len = cache_seq
        tiles = s_len // tm
        for window, _ in B_GROUPS:
            keep = min(window, s_len)
            rows = min(tm, keep)
            first_tile = tiles - keep // rows
            cache_specs.append(pl.BlockSpec(
                (1, 2, rows, B_GROUP_WIDTH),
                lambda i, tiles=tiles, first_tile=first_tile: (i // tiles, 0, jnp.maximum(i % tiles - first_tile, 0), 0)))
            cache_shapes.append(jax.ShapeDtypeStruct((n_seq, 2, keep, B_GROUP_WIDTH), F32))
    return pl.pallas_call(
        functools.partial(_in_proj_kernel, exact=w.dtype == F32),
        grid=(n // tm,),
        in_specs=[
            pl.BlockSpec((tm, D_MODEL), lambda i: (i, 0)),
            _const_spec((1, D_MODEL)),
            _const_spec((D_MODEL, in_proj_w)),
            _const_spec((1, 2 * D_MODEL)),
            tab_spec, tab_spec, tab_spec,
        ],
        out_specs=[
            pl.BlockSpec((tm, A_PROJ), lambda i: (i, 0)),
            pl.BlockSpec((SWA_SLABS, tm, LANES), lambda i: (0, i, 0)),
            pl.BlockSpec((tm, 2 * D_MODEL), lambda i: (i, 0)),
        ] + cache_specs,
        out_shape=[
            jax.ShapeDtypeStruct((n, A_PROJ), F32),
            jax.ShapeDtypeStruct((SWA_SLABS, n, LANES), F32),
            jax.ShapeDtypeStruct((n, 2 * D_MODEL), F32),
        ] + cache_shapes,
        compiler_params=_params(1),
        name="in_proj",
    )(x, g.reshape(1, -1), w, b_gate.reshape(1, -1), *tables)


def _norm_matmul_kernel(x_ref, g_ref, w_ref, o_ref):
    o_ref[...] = _dot(_rms(x_ref[...], g_ref[...]).astype(BF16), w_ref[...])


def _norm_matmul(x, g, w_bf16, tm):
    n, d = x.shape
    dout = w_bf16.shape[1]
    return pl.pallas_call(
        _norm_matmul_kernel,
        grid=(n // tm,),
        in_specs=[pl.BlockSpec((tm, d), lambda i: (i, 0)), _const_spec((1, d)), _const_spec((d, dout))],
        out_specs=pl.BlockSpec((tm, dout), lambda i: (i, 0)),
        out_shape=jax.ShapeDtypeStruct((n, dout), F32),
        compiler_params=_params(1),
        name="norm_matmul",
    )(x, g.reshape(1, -1), w_bf16)


RWKV_HEADS_PER_PACK = 4
RWKV_PACK_WIDTH = RWKV_HEADS_PER_PACK * A_HEAD_DIM
RWKV_PACKS = A_HEADS // RWKV_HEADS_PER_PACK
RWKV_STEP_SEQS = 1


def _shift_rows(u, first_prev):
    row = lax.broadcasted_iota(jnp.int32, (u.shape[0], 1), 0)
    return jnp.where(row == 0, first_prev, pltpu.roll(u, 1, 0))


def _head_sum(x, bd, exact, pieces=1):
    ones = bd.astype(BF16)
    total, rest = None, x
    for _ in range(3 if exact else pieces):
        piece = rest.astype(BF16)
        rest = rest - piece.astype(F32)
        part = _dot(piece, ones)
        total = part if total is None else total + part
    return total


def _rwkv_features(u, u_prev, w_refs, bd, exact):
    mu_ref, w0_ref, w2_ref, a0_ref, a2_ref, g2_ref, kk_ref, ka_ref = w_refs
    um = u + (u_prev - u) * mu_ref[...]
    o1, o2, o3 = A_WIDTH, 2 * A_WIDTH, 3 * A_WIDTH
    o4 = o3 + A_DECAY_LORA
    o5 = o4 + A_ICLR_LORA
    r, k, v = um[:, :o1], um[:, o1:o2], um[:, o2:o3]
    xw, xa, xg = um[:, o3:o4], um[:, o4:o5], um[:, o5:]
    lora = lambda t, w_ref: _mm(t, w_ref[...], exact)
    w = -jax.nn.softplus(-(w0_ref[...] + lora(jnp.tanh(xw), w2_ref))) - 0.5
    e = jnp.exp(w)
    a = jax.nn.sigmoid(a0_ref[...] + lora(xa, a2_ref))
    g = lora(jax.nn.sigmoid(xg), g2_ref)
    kk = k * kk_ref[...]
    kkn = kk / jnp.maximum(jnp.sqrt(_head_sum(kk * kk, bd, exact, pieces=2)), 1e-12)
    k2 = k * (1.0 + (a - 1.0) * ka_ref[...])
    return r, k2, v, e, a, g, kkn


def _rwkv_output(y, r, k2, v, g, rk_ref, lng_ref, lnb_ref, bd, exact):
    inv_n = 1.0 / A_HEAD_DIM
    mean = _head_sum(y, bd, exact) * inv_n
    yc = y - mean
    var = _head_sum(yc * yc, bd, exact) * inv_n
    yn = yc * lax.rsqrt(var + A_LNX_EPS) * lng_ref[...] + lnb_ref[...]
    bonus = _head_sum(r * k2 * rk_ref[...], bd, exact) * v
    return (yn + bonus) * g


def _rwkv_chunk_kernel(u_ref, sh0_ref, s0_ref, mu_ref, w0_ref, w2_ref, a0_ref, a2_ref, g2_ref, kk_ref, ka_ref,
                       rk_ref, lng_ref, lnb_ref, bd_ref, y_ref, sfin_ref, shout_ref, st_scr, prev_scr, *, n_steps):
    c = pl.program_id(0)
    n_b, chunk, _ = u_ref.shape
    hd, hpp, pw_ = A_HEAD_DIM, RWKV_HEADS_PER_PACK, RWKV_PACK_WIDTH
    bf = lambda t: t.astype(BF16)

    @pl.when(c == 0)
    def _():
        for b in range(n_b):
            prev_scr[b:b + 1, :] = sh0_ref[b]
            for p in range(RWKV_PACKS):
                st_scr[b, p] = jnp.concatenate([s0_ref[b, p * hpp + h] for h in range(hpp)], axis=1)

    u_b = [u_ref[b] for b in range(n_b)]
    u = jnp.concatenate(u_b, axis=0)
    u_prev = jnp.concatenate([_shift_rows(u_b[b], prev_scr[b:b + 1, :]) for b in range(n_b)], axis=0)
    for b in range(n_b):
        prev_scr[b:b + 1, :] = u_b[b][chunk - 1:chunk, :]
    bd = bd_ref[...]
    r, k2, v, e, a, g, kkn = _rwkv_features(
        u, u_prev, (mu_ref, w0_ref, w2_ref, a0_ref, a2_ref, g2_ref, kk_ref, ka_ref), bd, False)
    b_ = kkn * a

    li = lax.broadcasted_iota(jnp.int32, (chunk, chunk), 0)
    lj = lax.broadcasted_iota(jnp.int32, (chunk, chunk), 1)
    tri = bf((li >= lj).astype(F32))
    e_hi = bf(e)
    e_rest = e - e_hi.astype(F32)
    e_mid = bf(e_rest)
    e_lo = bf(e_rest - e_mid.astype(F32))
    cums, ends = [], []
    for b in range(n_b):
        rs = slice(b * chunk, (b + 1) * chunk)
        cb = _dot(tri, e_hi[rs]) + (_dot(tri, e_mid[rs]) + _dot(tri, e_lo[rs]))
        cums.append(cb)
        ends.append(jnp.broadcast_to(cb[chunk - 1:chunk, :], (chunk, A_WIDTH)))
    cum = jnp.concatenate(cums, axis=0)
    cum_end = jnp.concatenate(ends, axis=0)
    grow = jnp.exp(cum)
    to_end = jnp.exp(cum - cum_end)
    at = bf(-kkn * jnp.exp(e - cum))
    rt = bf(r * jnp.exp(-cum))
    bt = bf(b_ * grow)
    kt = bf(k2 * grow)
    bh = bf(b_ * to_end)
    kh = bf(k2 * to_end)
    vb = bf(v)
    dec_end = jnp.exp(-cum_end)

    lane_head = lax.broadcasted_iota(jnp.int32, (1, pw_), 1) // hd
    head_mask = [lane_head == h for h in range(hpp)]

    def block_diag(x):
        return jnp.concatenate([jnp.where(head_mask[h], x, jnp.zeros_like(x)) for h in range(hpp)], axis=0)

    assert chunk == hd
    ti = lax.broadcasted_iota(jnp.int32, (chunk, pw_), 0)
    tj = lax.broadcasted_iota(jnp.int32, (chunk, pw_), 1) % chunk
    strict = ti > tj
    incl = ti >= tj
    eye = (ti == tj).astype(F32)
    n_sq = int(math.log2(chunk)) - 1

    streams = [(b, p, slice(b * chunk, (b + 1) * chunk), slice(p * pw_, (p + 1) * pw_))
               for b in range(n_b) for p in range(RWKV_PACKS)]
    at_s = [at[rs, cs] for _, _, rs, cs in streams]
    rt_s = [rt[rs, cs] for _, _, rs, cs in streams]
    v_s = [vb[rs, cs] for _, _, rs, cs in streams]
    m = [_dot_nt(jnp.concatenate([a_, r_], axis=0),
                 jnp.concatenate([block_diag(bt[rs, cs]), block_diag(kt[rs, cs])], axis=0))
         for a_, r_, (_, _, rs, cs) in zip(at_s, rt_s, streams)]
    a_ab = [jnp.where(strict, x[:chunk, :pw_], 0.0) for x in m]
    akv = [_dot(bf(jnp.where(strict, x[:chunk, pw_:], 0.0)), block_diag(vs)) for x, vs in zip(m, v_s)]
    m_r = [bf(jnp.concatenate([jnp.where(incl, x[chunk:, :pw_], 0.0), jnp.where(incl, x[chunk:, pw_:], 0.0)], axis=1))
           for x in m]
    tinv = [eye + x for x in a_ab]
    pw = a_ab
    for _ in range(n_sq):
        pw = [_dot(bf(x), block_diag(bf(x))) for x in pw]
        tinv = [t + _dot(bf(t), block_diag(bf(x))) for t, x in zip(tinv, pw)]
    w12 = [_dot(bf(t), jnp.concatenate([block_diag(a_), block_diag(bf(x))], axis=1))
           for t, a_, x in zip(tinv, at_s, akv)]

    st = [st_scr[b, p] for b, p, _, _ in streams]
    x = [_dot_nt(jnp.concatenate([bf(w[:, :pw_]), r_], axis=0), block_diag(bf(s))) for w, r_, s in zip(w12, rt_s, st)]
    uu = [bf(xi[:chunk] + w[:, pw_:]) for xi, w in zip(x, w12)]
    ys = [xi[chunk:] + _dot(mr, jnp.concatenate([block_diag(ui), block_diag(vs)], axis=0))
          for xi, mr, ui, vs in zip(x, m_r, uu, v_s)]
    for i, (b, p, rs, cs) in enumerate(streams):
        upd = _dot_tn(jnp.concatenate([uu[i], v_s[i]], axis=0), jnp.concatenate([bh[rs, cs], kh[rs, cs]], axis=0))
        diag = functools.reduce(lambda s_, t_: s_ + t_,
                                [jnp.where(head_mask[h], upd[h * hd:(h + 1) * hd, :], 0.0) for h in range(hpp)])
        st_scr[b, p] = st[i] * dec_end[rs.start:rs.start + 1, cs] + diag
    y = jnp.concatenate([jnp.concatenate(ys[b * RWKV_PACKS:(b + 1) * RWKV_PACKS], axis=1) for b in range(n_b)], axis=0)
    out = _rwkv_output(y, r, k2, v, g, rk_ref, lng_ref, lnb_ref, bd, False)
    for b in range(n_b):
        y_ref[b] = out[b * chunk:(b + 1) * chunk]

    @pl.when(c == n_steps - 1)
    def _():
        for b in range(n_b):
            shout_ref[b] = u_b[b][chunk - 1:chunk, :]
            for p in range(RWKV_PACKS):
                for h in range(hpp):
                    sfin_ref[b, p * hpp + h] = st_scr[b, p][:, h * hd:(h + 1) * hd]


def _rwkv_step_kernel(u_ref, sh0_ref, s0_ref, mu_ref, w0_ref, w2_ref, a0_ref, a2_ref, g2_ref, kk_ref, ka_ref,
                      rk_ref, lng_ref, lnb_ref, bd_ref, y_ref, sfin_ref, shout_ref, *, t_valid, exact):
    hd = A_HEAD_DIM
    n_b, n_rows, _ = u_ref.shape
    u_b = [u_ref[b] for b in range(n_b)]
    u = jnp.concatenate(u_b, axis=0)
    u_prev = jnp.concatenate([_shift_rows(u_b[b], sh0_ref[b]) for b in range(n_b)], axis=0)
    bd = bd_ref[...]
    r, k2, v, e, a, g, kkn = _rwkv_features(
        u, u_prev, (mu_ref, w0_ref, w2_ref, a0_ref, a2_ref, g2_ref, kk_ref, ka_ref), bd, exact)
    rnd = lambda t: _contract_round(t, exact)
    decay = jnp.exp(-e)
    b_ = kkn * a
    eye = (lax.broadcasted_iota(jnp.int32, (hd, hd), 0) == lax.broadcasted_iota(jnp.int32, (hd, hd), 1)).astype(F32)
    to_col = lambda t: jnp.sum(eye * t, axis=1, keepdims=True)
    to_row = lambda t: jnp.sum(eye * t, axis=0, keepdims=True)
    y_seq = []
    for b in range(n_b):
        y_heads = []
        for h in range(A_HEADS):
            sl = slice(h * hd, (h + 1) * hd)
            s = s0_ref[b, h]
            y_rows = []
            for t in range(t_valid):
                tt = slice(b * n_rows + t, b * n_rows + t + 1)
                sa = jnp.sum(rnd(s) * rnd(-kkn[tt, sl]), axis=1, keepdims=True)
                s = s * decay[tt, sl] + sa * b_[tt, sl] + to_col(v[tt, sl]) * k2[tt, sl]
                y_rows.append(to_row(jnp.sum(rnd(s) * rnd(r[tt, sl]), axis=1, keepdims=True)))
            sfin_ref[b, h] = s
            y_rows.append(jnp.zeros((n_rows - t_valid, hd), F32))
            y_heads.append(jnp.concatenate(y_rows, axis=0))
        y_seq.append(jnp.concatenate(y_heads, axis=1))
    y = jnp.concatenate(y_seq, axis=0)
    out = _rwkv_output(y, r, k2, v, g, rk_ref, lng_ref, lnb_ref, bd, exact)
    for b in range(n_b):
        y_ref[b] = out[b * n_rows:(b + 1) * n_rows]
        shout_ref[b] = u_b[b][t_valid - 1:t_valid, :]


def _rwkv(u_a, shift0, s0, p, chunk, t_valid):
    bsz, t_len, _ = u_a.shape
    hd = A_HEAD_DIM
    bd = jnp.asarray(np.kron(np.eye(A_HEADS, dtype=np.float32), np.ones((hd, hd), np.float32)))
    row = lambda t: t.reshape(1, -1)
    if chunk:
        assert t_valid == t_len and t_len % chunk == 0
        grid = (t_len // chunk,)
        kern = functools.partial(_rwkv_chunk_kernel, n_steps=grid[0])
        scratch = [pltpu.VMEM((bsz, RWKV_PACKS, hd, RWKV_PACK_WIDTH), F32), pltpu.VMEM((bsz, A_PROJ), F32)]
        u_spec = pl.BlockSpec((bsz, chunk, A_PROJ), lambda c: (0, c, 0))
        y_spec = pl.BlockSpec((bsz, chunk, A_WIDTH), lambda c: (0, c, 0))
        state_spec = pl.BlockSpec((bsz, A_HEADS, hd, hd), lambda c: (0, 0, 0, 0))
        shift_spec = pl.BlockSpec((bsz, 1, A_PROJ), lambda c: (0, 0, 0))
    else:
        n_b = math.gcd(bsz, RWKV_STEP_SEQS)
        grid = (bsz // n_b,)
        kern = functools.partial(_rwkv_step_kernel, t_valid=t_valid, exact=True)
        scratch = []
        u_spec = pl.BlockSpec((n_b, t_len, A_PROJ), lambda b: (b, 0, 0))
        y_spec = pl.BlockSpec((n_b, t_len, A_WIDTH), lambda b: (b, 0, 0))
        state_spec = pl.BlockSpec((n_b, A_HEADS, hd, hd), lambda b: (b, 0, 0, 0))
        shift_spec = pl.BlockSpec((n_b, 1, A_PROJ), lambda b: (b, 0, 0))
    return pl.pallas_call(
        kern,
        grid=grid,
        in_specs=[
            u_spec, shift_spec, state_spec,
            _const_spec((1, A_PROJ)), _const_spec((1, A_WIDTH)), _const_spec((A_DECAY_LORA, A_WIDTH)),
            _const_spec((1, A_WIDTH)), _const_spec((A_ICLR_LORA, A_WIDTH)), _const_spec((A_GATE_LORA, A_WIDTH)),
            _const_spec((1, A_WIDTH)), _const_spec((1, A_WIDTH)), _const_spec((1, A_WIDTH)),
            _const_spec((1, A_WIDTH)), _const_spec((1, A_WIDTH)), _const_spec((A_WIDTH, A_WIDTH)),
        ],
        out_specs=[y_spec, state_spec, shift_spec],
        out_shape=[
            jax.ShapeDtypeStruct((bsz, t_len, A_WIDTH), F32),
            jax.ShapeDtypeStruct((bsz, A_HEADS, hd, hd), F32),
            jax.ShapeDtypeStruct((bsz, 1, A_PROJ), F32),
        ],
        scratch_shapes=scratch,
        compiler_params=_params(1),
        name="rwkv7",
    )(u_a, shift0.reshape(bsz, 1, A_PROJ), s0, row(p['rwkv_mu']), row(p['rwkv_w0']), p['rwkv_w2'],
      row(p['rwkv_a0']), p['rwkv_a2'], p['rwkv_g2'], row(p['rwkv_k_k']), row(p['rwkv_k_a']),
      row(p['rwkv_r_k']), row(p['rwkv_lnx_g']), row(p['rwkv_lnx_b']), bd)


def _swa_prompt_kernel(qkv_ref, o_ref, lse_ref):
    blk, hd, n_g = SWA_BLOCK, B_HEAD_DIM, len(B_GROUPS)
    s_len = qkv_ref.shape[1]
    scale = hd ** -0.5
    rows2 = SWA_PAIR * blk
    r_i = lax.broadcasted_iota(jnp.int32, (rows2, 1), 0)
    qi = r_i % blk
    own_lanes = (lax.broadcasted_iota(jnp.int32, (1, LANES), 1) // hd) == (r_i // blk)
    head0_lanes = own_lanes[:blk]
    ki2 = lax.broadcasted_iota(jnp.int32, (rows2, 2 * blk), 1)
    band2 = (qi + blk - ki2 >= 0) & (qi - ki2 <= 0)
    causal1 = lax.broadcasted_iota(jnp.int32, (rows2, blk), 1) <= qi

    def attend(g, specs):
        dil = B_GROUPS[g][1]

        def rows(kind, st):
            idx = pl.ds(st, blk, stride=dil) if dil > 1 else pl.ds(st, blk)
            return qkv_ref[kind * n_g + g, idx, :]

        qs, kbs, vbs, masks = [], [], [], []
        for st, prev, first in specs:
            q = rows(0, st) * scale
            qs.append(jnp.where(own_lanes, jnp.concatenate([q] * SWA_PAIR, axis=0), 0.0).astype(BF16))
            if prev is None:
                kbs.append(rows(1, st).astype(BF16))
                vbs.append(rows(2, st).astype(BF16))
                masks.append(causal1)
            else:
                kbs.append(jnp.concatenate([rows(1, prev), rows(1, st)], axis=0).astype(BF16))
                vbs.append(jnp.concatenate([rows(2, prev), rows(2, st)], axis=0).astype(BF16))
                masks.append(band2 & (ki2 >= jnp.where(first, blk, 0)))
        scores = [_dot_nt(q, kb) for q, kb in zip(qs, kbs)]
        probs, dens, lses = [], [], []
        for sc, mk in zip(scores, masks):
            sc = jnp.where(mk, sc, NEG_INF)
            m = jnp.max(sc, axis=-1, keepdims=True)
            p = jnp.exp(sc - m)
            den = jnp.sum(p, axis=-1, keepdims=True)
            probs.append(p.astype(BF16))
            dens.append(den)
            lses.append(m + jnp.log(den))
        outs = [_dot(p, vb) / den for p, vb, den in zip(probs, vbs, dens)]
        for (st, _, _), o2, l2 in zip(specs, outs, lses):
            idx = pl.ds(st, blk, stride=dil) if dil > 1 else pl.ds(st, blk)
            o_ref[g, idx, :] = jnp.where(head0_lanes, o2[:blk], o2[blk:])
            lse_ref[g, idx, :] = jnp.where(head0_lanes, l2[:blk], l2[blk:])

    for g, (_, dil) in enumerate(B_GROUPS):
        n_blk = s_len // dil // blk
        n_quads = dil * n_blk // SWA_QUAD

        def quad(it, carry, g=g, dil=dil, n_blk=n_blk):
            specs = []
            for j in range(SWA_QUAD):
                if n_blk >= SWA_QUAD:
                    e = it * SWA_QUAD + j
                    r, n = e // n_blk, e % n_blk
                    specs.append((n * (blk * dil) + r, jnp.maximum(n - 1, 0) * (blk * dil) + r, n == 0))
                else:
                    r = it * (SWA_QUAD // n_blk) + j // n_blk
                    n = j % n_blk
                    specs.append((n * (blk * dil) + r, None if n == 0 else (n - 1) * (blk * dil) + r, False))
            attend(g, specs)
            return carry

        lax.fori_loop(0, n_quads, quad, 0)


def _swa_prompt(qkv_slabs, n_seq, s_len):
    n_g = len(B_GROUPS)
    n = n_seq * s_len
    out_spec = pl.BlockSpec((n_g, s_len, LANES), lambda b, p: (p, b, 0))
    shp = jax.ShapeDtypeStruct((SWA_PAIRS * n_g, n, LANES), F32)
    return pl.pallas_call(
        _swa_prompt_kernel,
        grid=(n_seq, SWA_PAIRS),
        in_specs=[pl.BlockSpec((3 * n_g, s_len, LANES), lambda b, p: (p, b, 0))],
        out_specs=[out_spec, out_spec],
        out_shape=[shp, shp],
        compiler_params=_params(2),
        name="swa_prompt",
    )(qkv_slabs)


def _swa_sample_kernel(qkv_ref, c0_ref, c1_ref, c2_ref, o_ref, lse_ref, n0_ref, n1_ref, n2_ref, *, t_len, exact):
    hd = B_HEAD_DIM
    nh = B_HEADS_PER_GROUP
    gw = B_GROUP_WIDTH
    qkv = qkv_ref[0]
    scale = hd ** -0.5
    rnd = lambda t: _contract_round(t, exact)
    t_col = lax.broadcasted_iota(jnp.int32, (t_len, 1), 0)
    i_n = lax.broadcasted_iota(jnp.int32, (t_len, t_len), 1)
    units = []
    for gi, ((window, dil), c_ref, n_ref) in enumerate(zip(B_GROUPS, (c0_ref, c1_ref, c2_ref), (n0_ref, n1_ref, n2_ref))):
        buf_len = c_ref.shape[4]
        j_c = lax.broadcasted_iota(jnp.int32, (t_len, buf_len), 1)
        ok_c = (j_c >= t_col) & (((j_c - t_col) & (dil - 1)) == 0)
        ok_n = (i_n <= t_col) & (((t_col - i_n) & (dil - 1)) == 0)
        for h in range(nh):
            lo = gi * gw + h * hd
            units.append(dict(
                q=qkv[:, lo:lo + hd], k_new=qkv[:, B_WIDTH + lo:B_WIDTH + lo + hd],
                v_new=qkv[:, 2 * B_WIDTH + lo:2 * B_WIDTH + lo + hd],
                kt=c_ref[0, 0, h], vt=c_ref[0, 1, h], ok_c=ok_c, ok_n=ok_n))
    s_cs = [jnp.where(u["ok_c"], _mm(u["q"], u["kt"], exact) * scale, NEG_INF) for u in units]
    probs, p_news, lses = [], [], []
    for u, s_c in zip(units, s_cs):
        qr, kr = rnd(u["q"]), rnd(u["k_new"])
        s_n = jnp.zeros((t_len, t_len), F32)
        for i in range(t_len):
            s_n = jnp.where(i_n == i, jnp.sum(qr * kr[i:i + 1, :], axis=-1, keepdims=True) * scale, s_n)
        s_n = jnp.where(u["ok_n"], s_n, NEG_INF)
        m = jnp.maximum(jnp.max(s_c, axis=-1, keepdims=True), jnp.max(s_n, axis=-1, keepdims=True))
        p_c = jnp.exp(s_c - m)
        p_n = jnp.exp(s_n - m)
        den = jnp.sum(p_c, axis=-1, keepdims=True) + jnp.sum(p_n, axis=-1, keepdims=True)
        probs.append(p_c / den)
        p_news.append(rnd(p_n / den))
        lses.append(jnp.broadcast_to(m + jnp.log(den), (t_len, hd)))
    outs = [_mm(p, u["vt"], exact, _NT) for p, u in zip(probs, units)]
    for k, (u, p_n) in enumerate(zip(units, p_news)):
        vr = rnd(u["v_new"])
        for i in range(t_len):
            outs[k] = outs[k] + p_n[:, i:i + 1] * vr[i:i + 1, :]
    pad_rows = 8 - t_len
    place = (lax.broadcasted_iota(jnp.int32, (8, LANES), 0) + (LANES - t_len)
             == lax.broadcasted_iota(jnp.int32, (8, LANES), 1)).astype(BF16)
    tail_lane = lax.broadcasted_iota(jnp.int32, (1, LANES), 1) >= LANES - t_len
    for gi, (c_ref, n_ref) in enumerate(zip((c0_ref, c1_ref, c2_ref), (n0_ref, n1_ref, n2_ref))):
        buf_len = c_ref.shape[4]
        for j in range(2):
            new = jnp.pad(qkv[:, (j + 1) * B_WIDTH + gi * gw:(j + 1) * B_WIDTH + (gi + 1) * gw], ((0, pad_rows), (0, 0)))
            cols, rest = None, new
            for _ in range(3):
                piece = rest.astype(BF16)
                rest = rest - piece.astype(F32)
                part = _dot_tn(piece, place)
                cols = part if cols is None else cols + part
            nxt = pltpu.roll(c_ref[0, j].reshape(gw, buf_len), buf_len - t_len, 1)
            tail = jnp.where(tail_lane, cols, nxt[:, buf_len - LANES:])
            full = tail if buf_len == LANES else jnp.concatenate([nxt[:, :buf_len - LANES], tail], axis=1)
            n_ref[0, j] = full.reshape(nh, hd, buf_len)
    o_ref[0] = jnp.concatenate(outs, axis=1)
    lse_ref[0] = jnp.concatenate(lses, axis=1)


def _swa_sample(qkv, caches):
    bsz, t_len, _ = qkv.shape
    cache_specs = [pl.BlockSpec((1,) + c.shape[1:], lambda b: (b, 0, 0, 0, 0)) for c in caches]
    row_spec = pl.BlockSpec((1, t_len, B_WIDTH), lambda b: (b, 0, 0))
    return pl.pallas_call(
        functools.partial(_swa_sample_kernel, t_len=t_len, exact=True),
        grid=(bsz,),
        in_specs=[pl.BlockSpec((1, t_len, B_PROJ), lambda b: (b, 0, 0))] + cache_specs,
        out_specs=[row_spec, row_spec] + cache_specs,
        out_shape=[jax.ShapeDtypeStruct((bsz, t_len, B_WIDTH), F32)] * 2
        + [jax.ShapeDtypeStruct(c.shape, F32) for c in caches],
        compiler_params=_params(1),
        name="swa_sample",
    )(qkv, *caches)


def _route(logits):
    lane = lax.broadcasted_iota(jnp.int32, logits.shape, 1)
    gl = jnp.where(lane < N_GROUPS, logits, NEG_INF)
    gmax = jnp.max(gl, axis=-1, keepdims=True)
    grp = jnp.min(jnp.where(gl == gmax, lane, LANES), axis=-1, keepdims=True)
    w_grp = 1.0 / jnp.sum(jnp.exp(gl - gmax), axis=-1, keepdims=True)
    first = N_GROUPS + grp * EXPERTS_PER_GROUP
    el = jnp.where((lane >= first) & (lane < first + EXPERTS_PER_GROUP), logits, NEG_INF)
    m1 = jnp.max(el, axis=-1, keepdims=True)
    i1 = jnp.min(jnp.where(el == m1, lane, LANES), axis=-1, keepdims=True)
    el2 = jnp.where(lane == i1, NEG_INF, el)
    m2 = jnp.max(el2, axis=-1, keepdims=True)
    i2 = jnp.min(jnp.where(el2 == m2, lane, LANES), axis=-1, keepdims=True)
    e2 = jnp.exp(m2 - m1)
    g1 = w_grp / (1.0 + e2)
    g2 = w_grp * e2 / (1.0 + e2)
    route = jnp.where(lane == 0, (i1 - N_GROUPS).astype(F32), 0.0)
    route = jnp.where(lane == 1, (i2 - N_GROUPS).astype(F32), route)
    route = jnp.where(lane == 2, g1, route)
    return jnp.where(lane == 3, g2, route)


def _mid_kernel(x_ref, ya_ref, ob_ref, lse_ref, gate_ref, mk_ref, mv_ref, wa_ref, wb_ref, wo_ref, gm_ref,
                wq_ref, wmo_ref, gf_ref, wr_ref, br_ref, xn_all_ref, x2_ref, xn_ref, route_ref,
                *, rows_per_batch, exact, n_sub):
    del xn_all_ref
    tm = x_ref.shape[0]
    sub = tm // n_sub
    pieces = [slice(i * sub, (i + 1) * sub) for i in range(n_sub)]
    n_g = len(B_GROUPS)
    rnd = lambda t: _contract_round(t, exact)
    mm = lambda a, b, dims=_NN: _mm(a, b, exact, dims)

    def mix(rs):
        yb_pairs = []
        for p in range(SWA_PAIRS):
            lses = [lse_ref[p * n_g + g, rs, :] for g in range(n_g)]
            m = functools.reduce(jnp.maximum, lses)
            es = [jnp.exp(l - m) for l in lses]
            den = functools.reduce(lambda a, b: a + b, es)
            yb_pairs.append(functools.reduce(lambda a, b: a + b,
                                             [rnd(es[g] / den) * rnd(ob_ref[p * n_g + g, rs, :]) for g in range(n_g)]))
        return jnp.concatenate(yb_pairs, axis=1)

    yb = [mix(rs) for rs in pieces]
    ma = [mm(ya_ref[rs, :], wa_ref[...]) for rs in pieces]
    mb = [mm(t, wb_ref[...]) for t in yb]
    merged = [gate_ref[rs, :D_MODEL] * a + gate_ref[rs, D_MODEL:] * b for rs, a, b in zip(pieces, ma, mb)]
    x1 = [x_ref[rs, :] + mm(t, wo_ref[...]) for rs, t in zip(pieces, merged)]

    q = [mm(_rms(t, gm_ref[...]), wq_ref[...]) for t in x1]
    n_b = mk_ref.shape[0]
    mk = mk_ref[...].reshape(n_b * N_MEM, MEM_WIDTH)
    mv = mv_ref[...].reshape(n_b * N_MEM, MEM_WIDTH)
    if not exact:
        mk, mv = mk.astype(BF16), mv.astype(BF16)
    if n_b > 1:
        assert n_sub == 1
        rb = lax.broadcasted_iota(jnp.int32, (tm, n_b * N_MEM), 0) // rows_per_batch
        cb = lax.broadcasted_iota(jnp.int32, (tm, n_b * N_MEM), 1) // N_MEM
        same = rb == cb
    units = [(i, slice(h * MEM_HEAD_DIM, (h + 1) * MEM_HEAD_DIM)) for i in range(n_sub) for h in range(MEM_HEADS)]
    scores = [mm(q[i][:, sl], mk[:, sl], _NT) * (MEM_HEAD_DIM ** -0.5) for i, sl in units]
    probs = []
    for s_ in scores:
        if n_b > 1:
            s_ = jnp.where(same, s_, NEG_INF)
        p = jnp.exp(s_ - jnp.max(s_, axis=-1, keepdims=True))
        probs.append(p / jnp.sum(p, axis=-1, keepdims=True))
    pv = [mm(p, mv[:, sl]) for p, (_, sl) in zip(probs, units)]
    att = [jnp.concatenate(pv[i * MEM_HEADS:(i + 1) * MEM_HEADS], axis=1) for i in range(n_sub)]
    x2 = [a + mm(t, wmo_ref[...]) for a, t in zip(x1, att)]
    xn = [_rms(t, gf_ref[...]) for t in x2]
    logits = [mm(t, wr_ref[...]) + br_ref[...] for t in xn]
    for rs, a, b, c in zip(pieces, x2, xn, logits):
        x2_ref[rs, :] = a
        xn_ref[rs, :] = b
        route_ref[rs, :] = _route(c)


def _mid(x, y_a, o_b, lse_b, gates, mem_k, mem_v, mk_map, mv_map, n_b, rows_per_batch, w, tm, xn_all, row0):
    n = x.shape[0]
    row = lambda width: pl.BlockSpec((tm, width), lambda i: (i, 0))
    slabs = pl.BlockSpec((SWA_PAIRS * len(B_GROUPS), tm, LANES), lambda i: (0, i, 0))
    in_specs = [
        row(D_MODEL), row(A_WIDTH), slabs, slabs, row(2 * D_MODEL),
        pl.BlockSpec((n_b, N_MEM, MEM_WIDTH), mk_map), pl.BlockSpec((n_b, N_MEM, MEM_WIDTH), mv_map),
        _const_spec((A_WIDTH, D_MODEL)), _const_spec((B_GROUP_WIDTH, D_MODEL)), _const_spec((D_MODEL, D_MODEL)),
        _const_spec((1, D_MODEL)), _const_spec((D_MODEL, MEM_WIDTH)), _const_spec((MEM_WIDTH, D_MODEL)),
        _const_spec((1, D_MODEL)), _const_spec((D_MODEL, LANES)), _const_spec((1, LANES)),
        pl.BlockSpec(memory_space=pl.ANY),
    ]
    args = [x, y_a, o_b, lse_b, gates, mem_k, mem_v, w['wa'], w['wb'], w['wo'], w['gm'], w['wq'], w['wmo'],
            w['gf'], w['wr'], w['br'], xn_all]
    blk0 = row0 // tm
    return pl.pallas_call(
        functools.partial(_mid_kernel, rows_per_batch=rows_per_batch, exact=w['wa'].dtype == F32,
                          n_sub=MID_SUB if n_b == 1 else 1),
        grid=(n // tm,),
        in_specs=in_specs,
        out_specs=[row(D_MODEL), pl.BlockSpec((tm, D_MODEL), lambda i: (i + blk0, 0)), row(LANES)],
        out_shape=[
            jax.ShapeDtypeStruct((n, D_MODEL), F32),
            jax.ShapeDtypeStruct(xn_all.shape, F32),
            jax.ShapeDtypeStruct((n, LANES), F32),
        ],
        input_output_aliases={len(args) - 1: 1},
        compiler_params=_params(1),
        name="mid",
    )(*args)


def _row_copy(src_hbm, idx, dst_buf, slot, j, sem):
    return pltpu.make_async_copy(src_hbm.at[pl.ds(idx, 1), :], dst_buf.at[slot, pl.ds(j, 1), :], sem.at[slot])


def _gather_start(idx_ref, src_hbm, dst_buf, slot, sem, n_rows):
    for j in range(n_rows):
        _row_copy(src_hbm, idx_ref[0, 0, j], dst_buf, slot, j, sem).start(priority=j % 2)


def _gather_wait(src_hbm, dst_buf, slot, sem, n_rows):
    for j in range(n_rows):
        _row_copy(src_hbm, 0, dst_buf, slot, j, sem).wait()


def _experts_kernel(meta_ref, be_ref, idx_ref, idx_next_ref, x_hbm, wg_ref, wu_ref, wd_ref, o_ref, xbuf, sem):
    i = pl.program_id(0)
    n_used = meta_ref[0]
    slot = i % 2

    @pl.when(i == 0)
    def _():
        _gather_start(idx_ref, x_hbm, xbuf, 0, sem, MOE_ROWS)

    @pl.when(i + 1 < n_used)
    def _():
        _gather_start(idx_next_ref, x_hbm, xbuf, 1 - slot, sem, MOE_ROWS)

    @pl.when(i < n_used)
    def _():
        _gather_wait(x_hbm, xbuf, slot, sem, MOE_ROWS)
        xb = xbuf[slot].astype(BF16)
        hg = _dot(xb, wg_ref[0].astype(BF16))
        hu = _dot(xb, wu_ref[0].astype(BF16))
        hh = (jax.nn.silu(hg) * hu).astype(BF16)
        o_ref[...] = _dot(hh, wd_ref[0].astype(BF16))

    @pl.when(i >= n_used)
    def _():
        o_ref[...] = jnp.zeros_like(o_ref)


def _experts(xn_all, row_tok, block_e, n_used, w_gate, w_up, w_down):
    n_blocks = block_e.shape[0]
    idx3 = row_tok.reshape(n_blocks, 1, MOE_ROWS)
    idx_spec = lambda f: pl.BlockSpec((1, 1, MOE_ROWS), f, memory_space=pltpu.SMEM)
    grid_spec = pltpu.PrefetchScalarGridSpec(
        num_scalar_prefetch=2,
        grid=(n_blocks,),
        in_specs=[
            idx_spec(lambda i, meta, be: (i, 0, 0)),
            idx_spec(lambda i, meta, be: (jnp.minimum(i + 1, n_blocks - 1), 0, 0)),
            pl.BlockSpec(memory_space=pl.ANY),
            pl.BlockSpec((1, D_MODEL, EXPERT_FF), lambda i, meta, be: (be[i], 0, 0)),
            pl.BlockSpec((1, D_MODEL, EXPERT_FF), lambda i, meta, be: (be[i], 0, 0)),
            pl.BlockSpec((1, EXPERT_FF, D_MODEL), lambda i, meta, be: (be[i], 0, 0)),
        ],
        out_specs=pl.BlockSpec((MOE_ROWS, D_MODEL), lambda i, meta, be: (i, 0)),
        scratch_shapes=[pltpu.VMEM((2, MOE_ROWS, D_MODEL), F32), pltpu.SemaphoreType.DMA((2,))],
    )
    return pl.pallas_call(
        _experts_kernel,
        grid_spec=grid_spec,
        out_shape=jax.ShapeDtypeStruct((n_blocks * MOE_ROWS, D_MODEL), F32),
        compiler_params=_params(1),
        name="experts",
    )(n_used.reshape(1), block_e, idx3, idx3, xn_all, w_gate, w_up, w_down)


def _combine_kernel(pos_ref, pos_next_ref, x_ref, route_ref, yb_hbm, g_ref, o_ref, ybuf, sem, *, n_tiles):
    i = pl.program_id(0)
    tm = x_ref.shape[0]
    slot = i % 2

    @pl.when(i == 0)
    def _():
        _gather_start(pos_ref, yb_hbm, ybuf, 0, sem, 2 * tm)

    if n_tiles > 1:
        @pl.when(i + 1 < n_tiles)
        def _():
            _gather_start(pos_next_ref, yb_hbm, ybuf, 1 - slot, sem, 2 * tm)

    _gather_wait(yb_hbm, ybuf, slot, sem, 2 * tm)
    route = route_ref[...]
    y = x_ref[...] + (route[:, 2:3] * ybuf[slot, :tm, :] + route[:, 3:4] * ybuf[slot, tm:, :])
    o_ref[...] = _rms(y, g_ref[...])


def _combine(x2, route, pos, yb, g_final, tm):
    n = x2.shape[0]
    n_tiles = n // tm
    pos_spec = lambda f: pl.BlockSpec((1, 1, 2 * tm), f, memory_space=pltpu.SMEM)
    return pl.pallas_call(
        functools.partial(_combine_kernel, n_tiles=n_tiles),
        grid=(n_tiles,),
        in_specs=[
            pos_spec(lambda i: (i, 0, 0)),
            pos_spec(lambda i: (jnp.minimum(i + 1, n_tiles - 1), 0, 0)),
            pl.BlockSpec((tm, D_MODEL), lambda i: (i, 0)),
            pl.BlockSpec((tm, LANES), lambda i: (i, 0)),
            pl.BlockSpec(memory_space=pl.ANY),
            _const_spec((1, D_MODEL)),
        ],
        out_specs=pl.BlockSpec((tm, D_MODEL), lambda i: (i, 0)),
        out_shape=jax.ShapeDtypeStruct((n, D_MODEL), F32),
        scratch_shapes=[pltpu.VMEM((2, 2 * tm, D_MODEL), F32), pltpu.SemaphoreType.DMA((2,))],
        compiler_params=_params(1),
        name="combine",
    )(pos, pos, x2, route, yb, g_final.reshape(1, -1))


def _dispatch(eid):
    n_tok = eid.shape[0]
    n_rows = n_tok * TOP_K
    n_blocks = n_rows // MOE_ROWS + N_EXPERTS
    flat_e = eid.reshape(-1)
    onehot = (flat_e[:, None] == jnp.arange(N_EXPERTS, dtype=jnp.int32)[None, :]).astype(jnp.int32)
    csum = jnp.cumsum(onehot, axis=0)
    counts = csum[-1]
    rank = jnp.sum((csum - onehot) * onehot, axis=1)
    padded = (counts + MOE_ROWS - 1) // MOE_ROWS * MOE_ROWS
    pad_end = jnp.cumsum(padded)
    pad_start = pad_end - padded
    dest = pad_start[flat_e] + rank
    flat_tok = jnp.arange(n_rows, dtype=jnp.int32) // TOP_K
    row_tok = jnp.zeros((n_blocks * MOE_ROWS,), jnp.int32).at[dest].set(flat_tok, unique_indices=True)
    block_start = jnp.arange(n_blocks, dtype=jnp.int32) * MOE_ROWS
    block_e = jnp.minimum(jnp.sum((pad_end[None, :] <= block_start[:, None]).astype(jnp.int32), axis=1), N_EXPERTS - 1)
    n_used = (pad_end[-1] // MOE_ROWS).astype(jnp.int32)
    return row_tok, block_e, n_used, dest.reshape(n_tok, TOP_K).astype(jnp.int32)


def _tile_pos(pos, tm):
    n = pos.shape[0]
    return pos.reshape(n // tm, tm, TOP_K).transpose(0, 2, 1).reshape(n // tm, 1, TOP_K * tm)


def _slabs_to_rows(qkv_slabs):
    order = np.array([_slab_position(j) for j in range(SWA_SLABS)])
    return jnp.transpose(qkv_slabs[order], (1, 0, 2)).reshape(qkv_slabs.shape[1], B_PROJ)


def _rows_to_pair_slabs(t):
    n = t.shape[0]
    return jnp.transpose(t.reshape(n, len(B_GROUPS), SWA_PAIRS, LANES), (2, 1, 0, 3)).reshape(-1, n, LANES)


def kernel(x_prompt, x_sample, state_rwkv, state_shift, cache_swa_w128, cache_swa_w512, cache_swa_w2048,
           cache_mem_kv, mem_prompt, norm_mix_g, w_in, b_gate, rwkv_mu, rwkv_w0, rwkv_w2, rwkv_a0, rwkv_a2,
           rwkv_g2, rwkv_k_k, rwkv_k_a, rwkv_r_k, rwkv_lnx_g, rwkv_lnx_b, w_branch_a, w_branch_b, w_out,
           norm_mem_g, norm_memkv_g, w_mem_q, w_mem_kv, w_mem_out, norm_ffn_g, w_router_group, b_router_group,
           w_router_expert, b_router_expert, w_exp_gate, w_exp_up, w_exp_down, norm_final_g):
    bsz, s_len, _ = x_prompt.shape
    dbs, t_len, _ = x_sample.shape
    n_p, n_s = bsz * s_len, dbs * t_len
    rw = dict(rwkv_mu=rwkv_mu, rwkv_w0=rwkv_w0, rwkv_w2=rwkv_w2, rwkv_a0=rwkv_a0, rwkv_a2=rwkv_a2, rwkv_g2=rwkv_g2,
              rwkv_k_k=rwkv_k_k, rwkv_k_a=rwkv_k_a, rwkv_r_k=rwkv_r_k, rwkv_lnx_g=rwkv_lnx_g, rwkv_lnx_b=rwkv_lnx_b)
    pad = LANES - N_GROUPS - N_EXPERTS
    mid_w_s = dict(
        wa=w_branch_a, wb=w_branch_b, wo=w_out, gm=norm_mem_g.reshape(1, -1), wq=w_mem_q, wmo=w_mem_out,
        gf=norm_ffn_g.reshape(1, -1),
        wr=jnp.concatenate([w_router_group, w_router_expert, jnp.zeros((D_MODEL, pad), F32)], axis=1),
        br=jnp.concatenate([b_router_group, b_router_expert, jnp.zeros((pad,), F32)]).reshape(1, -1))
    mid_w_p = {k: (v.astype(BF16) if k.startswith('w') else v) for k, v in mid_w_s.items()}
    w_in_b = w_in.astype(BF16)

    xp = x_prompt.reshape(n_p, D_MODEL)
    tm = 512
    ua_p, qkv_p, gates_p, *p_caches = _in_proj(xp, norm_mix_g, w_in_b, b_gate, _rope_tables(jnp.arange(s_len)), tm,
                                               cache_seq=(bsz, s_len))
    s0 = jnp.zeros((bsz, A_HEADS, A_HEAD_DIM, A_HEAD_DIM), F32)
    ya_p, st_p, shift_p = _rwkv(ua_p.reshape(bsz, s_len, A_PROJ), jnp.zeros((bsz, A_PROJ), F32), s0, rw,
                                RWKV_CHUNK, s_len)
    ob_p, lse_p = _swa_prompt(qkv_p, bsz, s_len)
    p_bufs = [c.reshape(c.shape[:3] + (B_HEADS_PER_GROUP, B_HEAD_DIM)) for c in p_caches]

    memkv = _norm_matmul(mem_prompt.reshape(bsz * N_MEM, D_MODEL), norm_memkv_g, w_mem_kv.astype(BF16), 256)
    memkv3 = memkv.reshape(bsz, N_MEM, 2 * MEM_WIDTH)
    mem_kv_prompt = memkv3.reshape(bsz, N_MEM, 2, MEM_HEADS, MEM_HEAD_DIM).transpose(0, 2, 1, 3, 4)
    tiles_per_batch = s_len // tm
    xn_all = jnp.zeros((n_p + n_s, D_MODEL), F32)
    x2_p, xn_all, route_p = _mid(
        xp, ya_p.reshape(n_p, A_WIDTH), ob_p, lse_p, gates_p, memkv3, memkv3,
        lambda i: (i // tiles_per_batch, 0, 0), lambda i: (i // tiles_per_batch, 0, 1),
        1, s_len, mid_w_p, tm, xn_all, 0)

    xs = x_sample.reshape(n_s, D_MODEL)
    pos_s = PAST_LEN + (jnp.arange(n_s) % t_len)
    ua_s, qkv_s, gates_s = _in_proj(xs, norm_mix_g, w_in, b_gate, _rope_tables(pos_s), n_s)
    t_pad = 8
    ua_s3 = jnp.pad(ua_s.reshape(dbs, t_len, A_PROJ), ((0, 0), (0, t_pad - t_len), (0, 0)))
    ya_s, st_s, shift_s = _rwkv(ua_s3, state_shift, state_rwkv, rw, 0, t_len)
    ya_s = ya_s[:, :t_len].reshape(n_s, A_WIDTH)
    caches = [jnp.transpose(c, (0, 1, 3, 4, 2)) for c in (cache_swa_w128, cache_swa_w512, cache_swa_w2048)]
    ob_s, lse_s, nb0, nb1, nb2 = _swa_sample(_slabs_to_rows(qkv_s).reshape(dbs, t_len, B_PROJ), caches)
    s_bufs = [jnp.transpose(nb, (0, 1, 4, 2, 3)) for nb in (nb0, nb1, nb2)]
    mem_s = cache_mem_kv.reshape(dbs, 2 * N_MEM, MEM_WIDTH)
    tm_s = 32
    x2_s, xn_all, route_s = _mid(
        xs, ya_s, _rows_to_pair_slabs(ob_s.reshape(n_s, B_WIDTH)), _rows_to_pair_slabs(lse_s.reshape(n_s, B_WIDTH)),
        gates_s, mem_s, mem_s,
        lambda i: (i, 0, 0), lambda i: (i, 1, 0), tm_s // t_len, t_len, mid_w_s, tm_s, xn_all, n_p)

    route = jnp.concatenate([route_p, route_s], axis=0)
    eid = route[:, :TOP_K].astype(jnp.int32)
    row_tok, block_e, n_used, pos = _dispatch(eid)
    yb = _experts(xn_all, row_tok, block_e, n_used, w_exp_gate, w_exp_up, w_exp_down)
    tm_c = 256
    y_p = _combine(x2_p, route_p, _tile_pos(pos[:n_p], tm_c), yb, norm_final_g, tm_c)
    tm_c = min(tm_c, n_s)
    y_s = _combine(x2_s, route_s, _tile_pos(pos[n_p:], tm_c), yb, norm_final_g, tm_c)

    return (y_p.reshape(bsz, s_len, D_MODEL), y_s.reshape(dbs, t_len, D_MODEL),
            st_p, shift_p.reshape(bsz, A_PROJ), p_bufs[0], p_bufs[1], p_bufs[2], mem_kv_prompt,
            st_s, shift_s.reshape(dbs, A_PROJ), s_bufs[0], s_bufs[1], s_bufs[2])
```

```python
import functools
import math

import jax
import jax.numpy as jnp
import numpy as np
from jax import lax
from jax.experimental import pallas as pl
from jax.experimental.pallas import tpu as pltpu

F32 = jnp.float32
BF16 = jnp.bfloat16

D_MODEL = 1024
A_HEADS = 8
A_HEAD_DIM = 64
A_WIDTH = A_HEADS * A_HEAD_DIM
A_DECAY_LORA = 64
A_ICLR_LORA = 64
A_GATE_LORA = 128
A_PROJ = 3 * A_WIDTH + A_DECAY_LORA + A_ICLR_LORA + A_GATE_LORA
A_LNX_EPS = 64e-5
B_GROUPS = ((128, 1), (512, 4), (2048, 16))
B_HEADS_PER_GROUP = 4
B_HEAD_DIM = 64
B_GROUP_WIDTH = B_HEADS_PER_GROUP * B_HEAD_DIM
B_WIDTH = B_GROUP_WIDTH * len(B_GROUPS)
B_PROJ = 3 * B_WIDTH
ROPE_THETA = 500000.0
ROPE_DIM = B_HEAD_DIM // 4
ROPE_HALF = ROPE_DIM // 2
SWA_BLOCK = 128
N_MEM = 256
MEM_HEADS = 4
MEM_HEAD_DIM = 128
MEM_WIDTH = MEM_HEADS * MEM_HEAD_DIM
N_GROUPS = 4
EXPERTS_PER_GROUP = 8
N_EXPERTS = N_GROUPS * EXPERTS_PER_GROUP
TOP_K = 2
EXPERT_FF = 512
RMS_EPS = 1e-6
PAST_LEN = 8192

LANES = 128
SWA_SLABS = B_PROJ // LANES
SWA_PAIR = LANES // B_HEAD_DIM
SWA_PAIRS = B_GROUP_WIDTH // LANES
SWA_QUAD = 4
MID_SUB = 2
VMEM_LIMIT = 56 * 1024 * 1024
RWKV_CHUNK = 64
MOE_ROWS = 256
MOE_RING = 3
NEG_INF = float("-inf")


def _dot(a, b, precision=None):
    return jnp.dot(a, b, preferred_element_type=F32, precision=precision)


def _dot_nt(a, b, precision=None):
    return lax.dot_general(a, b, (((1,), (1,)), ((), ())), preferred_element_type=F32, precision=precision)


def _dot_tn(a, b, precision=None):
    return lax.dot_general(a, b, (((0,), (0,)), ((), ())), preferred_element_type=F32, precision=precision)


def _bf16_round(t):
    return t.astype(BF16).astype(F32)


_NN = (((1,), (0,)), ((), ()))
_NT = (((1,), (1,)), ((), ()))


def _mm(a, b, exact, dims=_NN):
    dg = lambda x, y: lax.dot_general(x, y, dims, preferred_element_type=F32)
    if not exact:
        return dg(a.astype(BF16), b.astype(BF16))
    a, b = a.astype(F32), b.astype(F32)
    a_hi, b_hi = a.astype(BF16), b.astype(BF16)
    a_lo = (a - a_hi.astype(F32)).astype(BF16)
    b_lo = (b - b_hi.astype(F32)).astype(BF16)
    return dg(a_hi, b_hi) + (dg(a_hi, b_lo) + dg(a_lo, b_hi))


def _contract_round(t, exact):
    return t if exact else _bf16_round(t)


def _rms(x, g):
    return x * lax.rsqrt(jnp.mean(x * x, axis=-1, keepdims=True) + RMS_EPS) * g


def _const_spec(shape):
    nd = len(shape)
    return pl.BlockSpec(shape, lambda *_: (0,) * nd, pipeline_mode=pl.Buffered(1))


def _params(n_axes):
    return pltpu.CompilerParams(dimension_semantics=("arbitrary",) * n_axes, vmem_limit_bytes=VMEM_LIMIT)


def _slab_position(j):
    kind, hs = divmod(j, SWA_SLABS // 3)
    g, p = divmod(hs, SWA_PAIRS)
    return p * (3 * len(B_GROUPS)) + kind * len(B_GROUPS) + g


def _in_proj_kernel(x_ref, g_ref, w_ref, bg_ref, rc_ref, rs1_ref, rs2_ref, ua_ref, qkv_ref, gate_ref, *cache_refs, exact):
    tm = x_ref.shape[0]
    xn = _rms(x_ref[...], g_ref[...])
    xn = xn if exact else xn.astype(BF16)
    mm = lambda w: _mm(xn, w, exact)
    ua_ref[...] = mm(w_ref[:, :A_PROJ])
    rc, rs1, rs2 = rc_ref[...], rs1_ref[...], rs2_ref[...]
    n_rot = 2 * B_WIDTH // LANES
    for j in range(SWA_SLABS):
        if j % 2 == 0:
            lo = A_PROJ + j * LANES
            s2 = mm(w_ref[:, lo:lo + 2 * LANES])
        s = s2[:, (j % 2) * LANES:(j % 2 + 1) * LANES]
        if j < n_rot:
            s = s * rc + pltpu.roll(s, LANES - ROPE_HALF, 1) * rs1 + pltpu.roll(s, ROPE_HALF, 1) * rs2
        qkv_ref[_slab_position(j)] = s
        kind, hs = divmod(j, SWA_SLABS // 3)
        if cache_refs and kind > 0:
            g, p = divmod(hs, SWA_PAIRS)
            rows = cache_refs[g].shape[2]
            cache_refs[g][0, kind - 1, :, p * LANES:(p + 1) * LANES] = s[tm - rows:, :]
    gate_ref[...] = jax.nn.sigmoid(mm(w_ref[:, A_PROJ + B_PROJ:]) + bg_ref[...])


def _rope_tables(pos):
    inv_freq = ROPE_THETA ** (-jnp.arange(ROPE_HALF, dtype=F32) * 2.0 / ROPE_DIM)
    ang = pos.astype(F32)[:, None] * inv_freq[None, :]
    cos, sin = jnp.cos(ang), jnp.sin(ang)
    n = pos.shape[0]
    rest = B_HEAD_DIM - ROPE_DIM
    c = jnp.concatenate([cos, cos, jnp.ones((n, rest), F32)], axis=1)
    s1 = jnp.concatenate([-sin, jnp.zeros((n, ROPE_HALF + rest), F32)], axis=1)
    s2 = jnp.concatenate([jnp.zeros((n, ROPE_HALF), F32), sin, jnp.zeros((n, rest), F32)], axis=1)
    rep = LANES // B_HEAD_DIM
    return tuple(jnp.tile(t, (1, rep)) for t in (c, s1, s2))


def _in_proj(x, g, w, b_gate, tables, tm, cache_seq=None):
    n = x.shape[0]
    p_rows = tables[0].shape[0]
    t_tiles = p_rows // tm
    in_proj_w = w.shape[1]
    tab_spec = pl.BlockSpec((tm, LANES), lambda i: (i % t_tiles, 0))
    cache_specs, cache_shapes = [], []
    if cache_seq is not None:
        n_seq, s_len = cache_seq
        tiles = s_len // tm
        for window, _ in B_GROUPS:
            keep = min(window, s_len)
            rows = min(tm, keep)
            first_tile = tiles - keep // rows
            cache_specs.append(pl.BlockSpec(
                (1, 2, rows, B_GROUP_WIDTH),
                lambda i, tiles=tiles, first_tile=first_tile: (i // tiles, 0, jnp.maximum(i % tiles - first_tile, 0), 0)))
            cache_shapes.append(jax.ShapeDtypeStruct((n_seq, 2, keep, B_GROUP_WIDTH), F32))
    return pl.pallas_call(
        functools.partial(_in_proj_kernel, exact=w.dtype == F32),
        grid=(n // tm,),
        in_specs=[
            pl.BlockSpec((tm, D_MODEL), lambda i: (i, 0)),
            _const_spec((1, D_MODEL)),
            _const_spec((D_MODEL, in_proj_w)),
            _const_spec((1, 2 * D_MODEL)),
            tab_spec, tab_spec, tab_spec,
        ],
        out_specs=[
            pl.BlockSpec((tm, A_PROJ), lambda i: (i, 0)),
            pl.BlockSpec((SWA_SLABS, tm, LANES), lambda i: (0, i, 0)),
            pl.BlockSpec((tm, 2 * D_MODEL), lambda i: (i, 0)),
        ] + cache_specs,
        out_shape=[
            jax.ShapeDtypeStruct((n, A_PROJ), F32),
            jax.ShapeDtypeStruct((SWA_SLABS, n, LANES), F32),
            jax.ShapeDtypeStruct((n, 2 * D_MODEL), F32),
        ] + cache_shapes,
        compiler_params=_params(1),
        name="in_proj",
    )(x, g.reshape(1, -1), w, b_gate.reshape(1, -1), *tables)


def _norm_matmul_kernel(x_ref, g_ref, w_ref, o_ref):
    o_ref[...] = _dot(_rms(x_ref[...], g_ref[...]).astype(BF16), w_ref[...])


def _norm_matmul(x, g, w_bf16, tm):
    n, d = x.shape
    dout = w_bf16.shape[1]
    return pl.pallas_call(
        _norm_matmul_kernel,
        grid=(n // tm,),
        in_specs=[pl.BlockSpec((tm, d), lambda i: (i, 0)), _const_spec((1, d)), _const_spec((d, dout))],
        out_specs=pl.BlockSpec((tm, dout), lambda i: (i, 0)),
        out_shape=jax.ShapeDtypeStruct((n, dout), F32),
        compiler_params=_params(1),
        name="norm_matmul",
    )(x, g.reshape(1, -1), w_bf16)


RWKV_HEADS_PER_PACK = 4
RWKV_PACK_WIDTH = RWKV_HEADS_PER_PACK * A_HEAD_DIM
RWKV_PACKS = A_HEADS // RWKV_HEADS_PER_PACK
RWKV_STEP_SEQS = 1


def _shift_rows(u, first_prev):
    row = lax.broadcasted_iota(jnp.int32, (u.shape[0], 1), 0)
    return jnp.where(row == 0, first_prev, pltpu.roll(u, 1, 0))


def _head_sum(x, bd, exact, pieces=1):
    ones = bd.astype(BF16)
    total, rest = None, x
    for _ in range(3 if exact else pieces):
        piece = rest.astype(BF16)
        rest = rest - piece.astype(F32)
        part = _dot(piece, ones)
        total = part if total is None else total + part
    return total


def _rwkv_features(u, u_prev, w_refs, bd, exact):
    mu_ref, w0_ref, w2_ref, a0_ref, a2_ref, g2_ref, kk_ref, ka_ref = w_refs
    um = u + (u_prev - u) * mu_ref[...]
    o1, o2, o3 = A_WIDTH, 2 * A_WIDTH, 3 * A_WIDTH
    o4 = o3 + A_DECAY_LORA
    o5 = o4 + A_ICLR_LORA
    r, k, v = um[:, :o1], um[:, o1:o2], um[:, o2:o3]
    xw, xa, xg = um[:, o3:o4], um[:, o4:o5], um[:, o5:]
    lora = lambda t, w_ref: _mm(t, w_ref[...], exact)
    w = -jax.nn.softplus(-(w0_ref[...] + lora(jnp.tanh(xw), w2_ref))) - 0.5
    e = jnp.exp(w)
    a = jax.nn.sigmoid(a0_ref[...] + lora(xa, a2_ref))
    g = lora(jax.nn.sigmoid(xg), g2_ref)
    kk = k * kk_ref[...]
    kkn = kk / jnp.maximum(jnp.sqrt(_head_sum(kk * kk, bd, exact, pieces=2)), 1e-12)
    k2 = k * (1.0 + (a - 1.0) * ka_ref[...])
    return r, k2, v, e, a, g, kkn


def _rwkv_output(y, r, k2, v, g, rk_ref, lng_ref, lnb_ref, bd, exact):
    inv_n = 1.0 / A_HEAD_DIM
    mean = _head_sum(y, bd, exact) * inv_n
    yc = y - mean
    var = _head_sum(yc * yc, bd, exact) * inv_n
    yn = yc * lax.rsqrt(var + A_LNX_EPS) * lng_ref[...] + lnb_ref[...]
    bonus = _head_sum(r * k2 * rk_ref[...], bd, exact) * v
    return (yn + bonus) * g


def _rwkv_chunk_kernel(u_ref, sh0_ref, s0_ref, mu_ref, w0_ref, w2_ref, a0_ref, a2_ref, g2_ref, kk_ref, ka_ref,
                       rk_ref, lng_ref, lnb_ref, bd_ref, y_ref, sfin_ref, shout_ref, st_scr, prev_scr, *, n_steps):
    c = pl.program_id(0)
    n_b, chunk, _ = u_ref.shape
    hd, hpp, pw_ = A_HEAD_DIM, RWKV_HEADS_PER_PACK, RWKV_PACK_WIDTH
    bf = lambda t: t.astype(BF16)

    @pl.when(c == 0)
    def _():
        for b in range(n_b):
            prev_scr[b:b + 1, :] = sh0_ref[b]
            for p in range(RWKV_PACKS):
                st_scr[b, p] = jnp.concatenate([s0_ref[b, p * hpp + h] for h in range(hpp)], axis=1)

    u_b = [u_ref[b] for b in range(n_b)]
    u = jnp.concatenate(u_b, axis=0)
    u_prev = jnp.concatenate([_shift_rows(u_b[b], prev_scr[b:b + 1, :]) for b in range(n_b)], axis=0)
    for b in range(n_b):
        prev_scr[b:b + 1, :] = u_b[b][chunk - 1:chunk, :]
    bd = bd_ref[...]
    r, k2, v, e, a, g, kkn = _rwkv_features(
        u, u_prev, (mu_ref, w0_ref, w2_ref, a0_ref, a2_ref, g2_ref, kk_ref, ka_ref), bd, False)
    b_ = kkn * a

    li = lax.broadcasted_iota(jnp.int32, (chunk, chunk), 0)
    lj = lax.broadcasted_iota(jnp.int32, (chunk, chunk), 1)
    tri = bf((li >= lj).astype(F32))
    e_hi = bf(e)
    e_rest = e - e_hi.astype(F32)
    e_mid = bf(e_rest)
    e_lo = bf(e_rest - e_mid.astype(F32))
    cums, ends = [], []
    for b in range(n_b):
        rs = slice(b * chunk, (b + 1) * chunk)
        cb = _dot(tri, e_hi[rs]) + (_dot(tri, e_mid[rs]) + _dot(tri, e_lo[rs]))
        cums.append(cb)
        ends.append(jnp.broadcast_to(cb[chunk - 1:chunk, :], (chunk, A_WIDTH)))
    cum = jnp.concatenate(cums, axis=0)
    cum_end = jnp.concatenate(ends, axis=0)
    grow = jnp.exp(cum)
    to_end = jnp.exp(cum - cum_end)
    at = bf(-kkn * jnp.exp(e - cum))
    rt = bf(r * jnp.exp(-cum))
    bt = bf(b_ * grow)
    kt = bf(k2 * grow)
    bh = bf(b_ * to_end)
    kh = bf(k2 * to_end)
    vb = bf(v)
    dec_end = jnp.exp(-cum_end)

    lane_head = lax.broadcasted_iota(jnp.int32, (1, pw_), 1) // hd
    head_mask = [lane_head == h for h in range(hpp)]

    def block_diag(x):
        return jnp.concatenate([jnp.where(head_mask[h], x, jnp.zeros_like(x)) for h in range(hpp)], axis=0)

    assert chunk == hd
    ti = lax.broadcasted_iota(jnp.int32, (chunk, pw_), 0)
    tj = lax.broadcasted_iota(jnp.int32, (chunk, pw_), 1) % chunk
    strict = ti > tj
    incl = ti >= tj
    eye = (ti == tj).astype(F32)
    n_sq = int(math.log2(chunk)) - 1

    streams = [(b, p, slice(b * chunk, (b + 1) * chunk), slice(p * pw_, (p + 1) * pw_))
               for b in range(n_b) for p in range(RWKV_PACKS)]
    at_s = [at[rs, cs] for _, _, rs, cs in streams]
    rt_s = [rt[rs, cs] for _, _, rs, cs in streams]
    v_s = [vb[rs, cs] for _, _, rs, cs in streams]
    m = [_dot_nt(jnp.concatenate([a_, r_], axis=0),
                 jnp.concatenate([block_diag(bt[rs, cs]), block_diag(kt[rs, cs])], axis=0))
         for a_, r_, (_, _, rs, cs) in zip(at_s, rt_s, streams)]
    a_ab = [jnp.where(strict, x[:chunk, :pw_], 0.0) for x in m]
    akv = [_dot(bf(jnp.where(strict, x[:chunk, pw_:], 0.0)), block_diag(vs)) for x, vs in zip(m, v_s)]
    m_r = [bf(jnp.concatenate([jnp.where(incl, x[chunk:, :pw_], 0.0), jnp.where(incl, x[chunk:, pw_:], 0.0)], axis=1))
           for x in m]
    tinv = [eye + x for x in a_ab]
    pw = a_ab
    for _ in range(n_sq):
        pw = [_dot(bf(x), block_diag(bf(x))) for x in pw]
        tinv = [t + _dot(bf(t), block_diag(bf(x))) for t, x in zip(tinv, pw)]
    w12 = [_dot(bf(t), jnp.concatenate([block_diag(a_), block_diag(bf(x))], axis=1))
           for t, a_, x in zip(tinv, at_s, akv)]

    st = [st_scr[b, p] for b, p, _, _ in streams]
    x = [_dot_nt(jnp.concatenate([bf(w[:, :pw_]), r_], axis=0), block_diag(bf(s))) for w, r_, s in zip(w12, rt_s, st)]
    uu = [bf(xi[:chunk] + w[:, pw_:]) for xi, w in zip(x, w12)]
    ys = [xi[chunk:] + _dot(mr, jnp.concatenate([block_diag(ui), block_diag(vs)], axis=0))
          for xi, mr, ui, vs in zip(x, m_r, uu, v_s)]
    for i, (b, p, rs, cs) in enumerate(streams):
        upd = _dot_tn(jnp.concatenate([uu[i], v_s[i]], axis=0), jnp.concatenate([bh[rs, cs], kh[rs, cs]], axis=0))
        diag = functools.reduce(lambda s_, t_: s_ + t_,
                                [jnp.where(head_mask[h], upd[h * hd:(h + 1) * hd, :], 0.0) for h in range(hpp)])
        st_scr[b, p] = st[i] * dec_end[rs.start:rs.start + 1, cs] + diag
    y = jnp.concatenate([jnp.concatenate(ys[b * RWKV_PACKS:(b + 1) * RWKV_PACKS], axis=1) for b in range(n_b)], axis=0)
    out = _rwkv_output(y, r, k2, v, g, rk_ref, lng_ref, lnb_ref, bd, False)
    for b in range(n_b):
        y_ref[b] = out[b * chunk:(b + 1) * chunk]

    @pl.when(c == n_steps - 1)
    def _():
        for b in range(n_b):
            shout_ref[b] = u_b[b][chunk - 1:chunk, :]
            for p in range(RWKV_PACKS):
                for h in range(hpp):
                    sfin_ref[b, p * hpp + h] = st_scr[b, p][:, h * hd:(h + 1) * hd]


def _rwkv_step_kernel(u_ref, sh0_ref, s0_ref, mu_ref, w0_ref, w2_ref, a0_ref, a2_ref, g2_ref, kk_ref, ka_ref,
                      rk_ref, lng_ref, lnb_ref, bd_ref, y_ref, sfin_ref, shout_ref, *, t_valid, exact):
    hd = A_HEAD_DIM
    n_b, n_rows, _ = u_ref.shape
    u_b = [u_ref[b] for b in range(n_b)]
    u = jnp.concatenate(u_b, axis=0)
    u_prev = jnp.concatenate([_shift_rows(u_b[b], sh0_ref[b]) for b in range(n_b)], axis=0)
    bd = bd_ref[...]
    r, k2, v, e, a, g, kkn = _rwkv_features(
        u, u_prev, (mu_ref, w0_ref, w2_ref, a0_ref, a2_ref, g2_ref, kk_ref, ka_ref), bd, exact)
    rnd = lambda t: _contract_round(t, exact)
    decay = jnp.exp(-e)
    b_ = kkn * a
    eye = (lax.broadcasted_iota(jnp.int32, (hd, hd), 0) == lax.broadcasted_iota(jnp.int32, (hd, hd), 1)).astype(F32)
    to_col = lambda t: jnp.sum(eye * t, axis=1, keepdims=True)
    to_row = lambda t: jnp.sum(eye * t, axis=0, keepdims=True)
    y_seq = []
    for b in range(n_b):
        y_heads = []
        for h in range(A_HEADS):
            sl = slice(h * hd, (h + 1) * hd)
            s = s0_ref[b, h]
            y_rows = []
            for t in range(t_valid):
                tt = slice(b * n_rows + t, b * n_rows + t + 1)
                sa = jnp.sum(rnd(s) * rnd(-kkn[tt, sl]), axis=1, keepdims=True)
                s = s * decay[tt, sl] + sa * b_[tt, sl] + to_col(v[tt, sl]) * k2[tt, sl]
                y_rows.append(to_row(jnp.sum(rnd(s) * rnd(r[tt, sl]), axis=1, keepdims=True)))
            sfin_ref[b, h] = s
            y_rows.append(jnp.zeros((n_rows - t_valid, hd), F32))
            y_heads.append(jnp.concatenate(y_rows, axis=0))
        y_seq.append(jnp.concatenate(y_heads, axis=1))
    y = jnp.concatenate(y_seq, axis=0)
    out = _rwkv_output(y, r, k2, v, g, rk_ref, lng_ref, lnb_ref, bd, exact)
    for b in range(n_b):
        y_ref[b] = out[b * n_rows:(b + 1) * n_rows]
        shout_ref[b] = u_b[b][t_valid - 1:t_valid, :]


def _rwkv(u_a, shift0, s0, p, chunk, t_valid):
    bsz, t_len, _ = u_a.shape
    hd = A_HEAD_DIM
    bd = jnp.asarray(np.kron(np.eye(A_HEADS, dtype=np.float32), np.ones((hd, hd), np.float32)))
    row = lambda t: t.reshape(1, -1)
    if chunk:
        assert t_valid == t_len and t_len % chunk == 0
        grid = (t_len // chunk,)
        kern = functools.partial(_rwkv_chunk_kernel, n_steps=grid[0])
        scratch = [pltpu.VMEM((bsz, RWKV_PACKS, hd, RWKV_PACK_WIDTH), F32), pltpu.VMEM((bsz, A_PROJ), F32)]
        u_spec = pl.BlockSpec((bsz, chunk, A_PROJ), lambda c: (0, c, 0))
        y_spec = pl.BlockSpec((bsz, chunk, A_WIDTH), lambda c: (0, c, 0))
        state_spec = pl.BlockSpec((bsz, A_HEADS, hd, hd), lambda c: (0, 0, 0, 0))
        shift_spec = pl.BlockSpec((bsz, 1, A_PROJ), lambda c: (0, 0, 0))
    else:
        n_b = math.gcd(bsz, RWKV_STEP_SEQS)
        grid = (bsz // n_b,)
        kern = functools.partial(_rwkv_step_kernel, t_valid=t_valid, exact=True)
        scratch = []
        u_spec = pl.BlockSpec((n_b, t_len, A_PROJ), lambda b: (b, 0, 0))
        y_spec = pl.BlockSpec((n_b, t_len, A_WIDTH), lambda b: (b, 0, 0))
        state_spec = pl.BlockSpec((n_b, A_HEADS, hd, hd), lambda b: (b, 0, 0, 0))
        shift_spec = pl.BlockSpec((n_b, 1, A_PROJ), lambda b: (b, 0, 0))
    return pl.pallas_call(
        kern,
        grid=grid,
        in_specs=[
            u_spec, shift_spec, state_spec,
            _const_spec((1, A_PROJ)), _const_spec((1, A_WIDTH)), _const_spec((A_DECAY_LORA, A_WIDTH)),
            _const_spec((1, A_WIDTH)), _const_spec((A_ICLR_LORA, A_WIDTH)), _const_spec((A_GATE_LORA, A_WIDTH)),
            _const_spec((1, A_WIDTH)), _const_spec((1, A_WIDTH)), _const_spec((1, A_WIDTH)),
            _const_spec((1, A_WIDTH)), _const_spec((1, A_WIDTH)), _const_spec((A_WIDTH, A_WIDTH)),
        ],
        out_specs=[y_spec, state_spec, shift_spec],
        out_shape=[
            jax.ShapeDtypeStruct((bsz, t_len, A_WIDTH), F32),
            jax.ShapeDtypeStruct((bsz, A_HEADS, hd, hd), F32),
            jax.ShapeDtypeStruct((bsz, 1, A_PROJ), F32),
        ],
        scratch_shapes=scratch,
        compiler_params=_params(1),
        name="rwkv7",
    )(u_a, shift0.reshape(bsz, 1, A_PROJ), s0, row(p['rwkv_mu']), row(p['rwkv_w0']), p['rwkv_w2'],
      row(p['rwkv_a0']), p['rwkv_a2'], p['rwkv_g2'], row(p['rwkv_k_k']), row(p['rwkv_k_a']),
      row(p['rwkv_r_k']), row(p['rwkv_lnx_g']), row(p['rwkv_lnx_b']), bd)


def _swa_prompt_kernel(qkv_ref, o_ref, lse_ref):
    blk, hd, n_g = SWA_BLOCK, B_HEAD_DIM, len(B_GROUPS)
    s_len = qkv_ref.shape[1]
    scale = hd ** -0.5
    rows2 = SWA_PAIR * blk
    r_i = lax.broadcasted_iota(jnp.int32, (rows2, 1), 0)
    qi = r_i % blk
    own_lanes = (lax.broadcasted_iota(jnp.int32, (1, LANES), 1) // hd) == (r_i // blk)
    head0_lanes = own_lanes[:blk]
    ki2 = lax.broadcasted_iota(jnp.int32, (rows2, 2 * blk), 1)
    band2 = (qi + blk - ki2 >= 0) & (qi - ki2 <= 0)
    causal1 = lax.broadcasted_iota(jnp.int32, (rows2, blk), 1) <= qi

    def attend(g, specs):
        dil = B_GROUPS[g][1]

        def rows(kind, st):
            idx = pl.ds(st, blk, stride=dil) if dil > 1 else pl.ds(st, blk)
            return qkv_ref[kind * n_g + g, idx, :]

        qs, kbs, vbs, masks = [], [], [], []
        for st, prev, first in specs:
            q = rows(0, st) * scale
            qs.append(jnp.where(own_lanes, jnp.concatenate([q] * SWA_PAIR, axis=0), 0.0).astype(BF16))
            if prev is None:
                kbs.append(rows(1, st).astype(BF16))
                vbs.append(rows(2, st).astype(BF16))
                masks.append(causal1)
            else:
                kbs.append(jnp.concatenate([rows(1, prev), rows(1, st)], axis=0).astype(BF16))
                vbs.append(jnp.concatenate([rows(2, prev), rows(2, st)], axis=0).astype(BF16))
                masks.append(band2 & (ki2 >= jnp.where(first, blk, 0)))
        scores = [_dot_nt(q, kb) for q, kb in zip(qs, kbs)]
        probs, dens, lses = [], [], []
        for sc, mk in zip(scores, masks):
            sc = jnp.where(mk, sc, NEG_INF)
            m = jnp.max(sc, axis=-1, keepdims=True)
            p = jnp.exp(sc - m)
            den = jnp.sum(p, axis=-1, keepdims=True)
            probs.append(p.astype(BF16))
            dens.append(den)
            lses.append(m + jnp.log(den))
        outs = [_dot(p, vb) / den for p, vb, den in zip(probs, vbs, dens)]
        for (st, _, _), o2, l2 in zip(specs, outs, lses):
            idx = pl.ds(st, blk, stride=dil) if dil > 1 else pl.ds(st, blk)
            o_ref[g, idx, :] = jnp.where(head0_lanes, o2[:blk], o2[blk:])
            lse_ref[g, idx, :] = jnp.where(head0_lanes, l2[:blk], l2[blk:])

    for g, (_, dil) in enumerate(B_GROUPS):
        n_blk = s_len // dil // blk
        n_quads = dil * n_blk // SWA_QUAD

        def quad(it, carry, g=g, dil=dil, n_blk=n_blk):
            specs = []
            for j in range(SWA_QUAD):
                if n_blk >= SWA_QUAD:
                    e = it * SWA_QUAD + j
                    r, n = e // n_blk, e % n_blk
                    specs.append((n * (blk * dil) + r, jnp.maximum(n - 1, 0) * (blk * dil) + r, n == 0))
                else:
                    r = it * (SWA_QUAD // n_blk) + j // n_blk
                    n = j % n_blk
                    specs.append((n * (blk * dil) + r, None if n == 0 else (n - 1) * (blk * dil) + r, False))
            attend(g, specs)
            return carry

        lax.fori_loop(0, n_quads, quad, 0)


def _swa_prompt(qkv_slabs, n_seq, s_len):
    n_g = len(B_GROUPS)
    n = n_seq * s_len
    out_spec = pl.BlockSpec((n_g, s_len, LANES), lambda b, p: (p, b, 0))
    shp = jax.ShapeDtypeStruct((SWA_PAIRS * n_g, n, LANES), F32)
    return pl.pallas_call(
        _swa_prompt_kernel,
        grid=(n_seq, SWA_PAIRS),
        in_specs=[pl.BlockSpec((3 * n_g, s_len, LANES), lambda b, p: (p, b, 0))],
        out_specs=[out_spec, out_spec],
        out_shape=[shp, shp],
        compiler_params=_params(2),
        name="swa_prompt",
    )(qkv_slabs)


def _swa_sample_kernel(qkv_ref, c0_ref, c1_ref, c2_ref, o_ref, lse_ref, n0_ref, n1_ref, n2_ref, *, t_len, exact):
    hd = B_HEAD_DIM
    nh = B_HEADS_PER_GROUP
    gw = B_GROUP_WIDTH
    qkv = qkv_ref[0]
    scale = hd ** -0.5
    rnd = lambda t: _contract_round(t, exact)
    t_col = lax.broadcasted_iota(jnp.int32, (t_len, 1), 0)
    i_n = lax.broadcasted_iota(jnp.int32, (t_len, t_len), 1)
    units = []
    for gi, ((window, dil), c_ref, n_ref) in enumerate(zip(B_GROUPS, (c0_ref, c1_ref, c2_ref), (n0_ref, n1_ref, n2_ref))):
        buf_len = c_ref.shape[4]
        j_c = lax.broadcasted_iota(jnp.int32, (t_len, buf_len), 1)
        ok_c = (j_c >= t_col) & (((j_c - t_col) & (dil - 1)) == 0)
        ok_n = (i_n <= t_col) & (((t_col - i_n) & (dil - 1)) == 0)
        for h in range(nh):
            lo = gi * gw + h * hd
            units.append(dict(
                q=qkv[:, lo:lo + hd], k_new=qkv[:, B_WIDTH + lo:B_WIDTH + lo + hd],
                v_new=qkv[:, 2 * B_WIDTH + lo:2 * B_WIDTH + lo + hd],
                kt=c_ref[0, 0, h], vt=c_ref[0, 1, h], ok_c=ok_c, ok_n=ok_n))
    s_cs = [jnp.where(u["ok_c"], _mm(u["q"], u["kt"], exact) * scale, NEG_INF) for u in units]
    probs, p_news, lses = [], [], []
    for u, s_c in zip(units, s_cs):
        qr, kr = rnd(u["q"]), rnd(u["k_new"])
        s_n = jnp.zeros((t_len, t_len), F32)
        for i in range(t_len):
            s_n = jnp.where(i_n == i, jnp.sum(qr * kr[i:i + 1, :], axis=-1, keepdims=True) * scale, s_n)
        s_n = jnp.where(u["ok_n"], s_n, NEG_INF)
        m = jnp.maximum(jnp.max(s_c, axis=-1, keepdims=True), jnp.max(s_n, axis=-1, keepdims=True))
        p_c = jnp.exp(s_c - m)
        p_n = jnp.exp(s_n - m)
        den = jnp.sum(p_c, axis=-1, keepdims=True) + jnp.sum(p_n, axis=-1, keepdims=True)
        probs.append(p_c / den)
        p_news.append(rnd(p_n / den))
        lses.append(jnp.broadcast_to(m + jnp.log(den), (t_len, hd)))
    outs = [_mm(p, u["vt"], exact, _NT) for p, u in zip(probs, units)]
    for k, (u, p_n) in enumerate(zip(units, p_news)):
        vr = rnd(u["v_new"])
        for i in range(t_len):
            outs[k] = outs[k] + p_n[:, i:i + 1] * vr[i:i + 1, :]
    pad_rows = 8 - t_len
    place = (lax.broadcasted_iota(jnp.int32, (8, LANES), 0) + (LANES - t_len)
             == lax.broadcasted_iota(jnp.int32, (8, LANES), 1)).astype(BF16)
    tail_lane = lax.broadcasted_iota(jnp.int32, (1, LANES), 1) >= LANES - t_len
    for gi, (c_ref, n_ref) in enumerate(zip((c0_ref, c1_ref, c2_ref), (n0_ref, n1_ref, n2_ref))):
        buf_len = c_ref.shape[4]
        for j in range(2):
            new = jnp.pad(qkv[:, (j + 1) * B_WIDTH + gi * gw:(j + 1) * B_WIDTH + (gi + 1) * gw], ((0, pad_rows), (0, 0)))
            cols, rest = None, new
            for _ in range(3):
                piece = rest.astype(BF16)
                rest = rest - piece.astype(F32)
                part = _dot_tn(piece, place)
                cols = part if cols is None else cols + part
            nxt = pltpu.roll(c_ref[0, j].reshape(gw, buf_len), buf_len - t_len, 1)
            tail = jnp.where(tail_lane, cols, nxt[:, buf_len - LANES:])
            full = tail if buf_len == LANES else jnp.concatenate([nxt[:, :buf_len - LANES], tail], axis=1)
            n_ref[0, j] = full.reshape(nh, hd, buf_len)
    o_ref[0] = jnp.concatenate(outs, axis=1)
    lse_ref[0] = jnp.concatenate(lses, axis=1)


def _swa_sample(qkv, caches):
    bsz, t_len, _ = qkv.shape
    cache_specs = [pl.BlockSpec((1,) + c.shape[1:], lambda b: (b, 0, 0, 0, 0)) for c in caches]
    row_spec = pl.BlockSpec((1, t_len, B_WIDTH), lambda b: (b, 0, 0))
    return pl.pallas_call(
        functools.partial(_swa_sample_kernel, t_len=t_len, exact=True),
        grid=(bsz,),
        in_specs=[pl.BlockSpec((1, t_len, B_PROJ), lambda b: (b, 0, 0))] + cache_specs,
        out_specs=[row_spec, row_spec] + cache_specs,
        out_shape=[jax.ShapeDtypeStruct((bsz, t_len, B_WIDTH), F32)] * 2
        + [jax.ShapeDtypeStruct(c.shape, F32) for c in caches],
        compiler_params=_params(1),
        name="swa_sample",
    )(qkv, *caches)


def _route(logits):
    lane = lax.broadcasted_iota(jnp.int32, logits.shape, 1)
    gl = jnp.where(lane < N_GROUPS, logits, NEG_INF)
    gmax = jnp.max(gl, axis=-1, keepdims=True)
    grp = jnp.min(jnp.where(gl == gmax, lane, LANES), axis=-1, keepdims=True)
    w_grp = 1.0 / jnp.sum(jnp.exp(gl - gmax), axis=-1, keepdims=True)
    first = N_GROUPS + grp * EXPERTS_PER_GROUP
    el = jnp.where((lane >= first) & (lane < first + EXPERTS_PER_GROUP), logits, NEG_INF)
    m1 = jnp.max(el, axis=-1, keepdims=True)
    i1 = jnp.min(jnp.where(el == m1, lane, LANES), axis=-1, keepdims=True)
    el2 = jnp.where(lane == i1, NEG_INF, el)
    m2 = jnp.max(el2, axis=-1, keepdims=True)
    i2 = jnp.min(jnp.where(el2 == m2, lane, LANES), axis=-1, keepdims=True)
    e2 = jnp.exp(m2 - m1)
    g1 = w_grp / (1.0 + e2)
    g2 = w_grp * e2 / (1.0 + e2)
    route = jnp.where(lane == 0, (i1 - N_GROUPS).astype(F32), 0.0)
    route = jnp.where(lane == 1, (i2 - N_GROUPS).astype(F32), route)
    route = jnp.where(lane == 2, g1, route)
    return jnp.where(lane == 3, g2, route)


def _mid_kernel(x_ref, ya_ref, ob_ref, lse_ref, gate_ref, mk_ref, mv_ref, wa_ref, wb_ref, wo_ref, gm_ref,
                wq_ref, wmo_ref, gf_ref, wr_ref, br_ref, xn_all_ref, x2_ref, xn_ref, route_ref,
                *, rows_per_batch, exact, n_sub):
    del xn_all_ref
    tm = x_ref.shape[0]
    sub = tm // n_sub
    pieces = [slice(i * sub, (i + 1) * sub) for i in range(n_sub)]
    n_g = len(B_GROUPS)
    rnd = lambda t: _contract_round(t, exact)
    mm = lambda a, b, dims=_NN: _mm(a, b, exact, dims)

    def mix(rs):
        yb_pairs = []
        for p in range(SWA_PAIRS):
            lses = [lse_ref[p * n_g + g, rs, :] for g in range(n_g)]
            m = functools.reduce(jnp.maximum, lses)
            es = [jnp.exp(l - m) for l in lses]
            den = functools.reduce(lambda a, b: a + b, es)
            yb_pairs.append(functools.reduce(lambda a, b: a + b,
                                             [rnd(es[g] / den) * rnd(ob_ref[p * n_g + g, rs, :]) for g in range(n_g)]))
        return jnp.concatenate(yb_pairs, axis=1)

    yb = [mix(rs) for rs in pieces]
    ma = [mm(ya_ref[rs, :], wa_ref[...]) for rs in pieces]
    mb = [mm(t, wb_ref[...]) for t in yb]
    merged = [gate_ref[rs, :D_MODEL] * a + gate_ref[rs, D_MODEL:] * b for rs, a, b in zip(pieces, ma, mb)]
    x1 = [x_ref[rs, :] + mm(t, wo_ref[...]) for rs, t in zip(pieces, merged)]

    q = [mm(_rms(t, gm_ref[...]), wq_ref[...]) for t in x1]
    n_b = mk_ref.shape[0]
    mk = mk_ref[...].reshape(n_b * N_MEM, MEM_WIDTH)
    mv = mv_ref[...].reshape(n_b * N_MEM, MEM_WIDTH)
    if not exact:
        mk, mv = mk.astype(BF16), mv.astype(BF16)
    if n_b > 1:
        assert n_sub == 1
        rb = lax.broadcasted_iota(jnp.int32, (tm, n_b * N_MEM), 0) // rows_per_batch
        cb = lax.broadcasted_iota(jnp.int32, (tm, n_b * N_MEM), 1) // N_MEM
        same = rb == cb
    units = [(i, slice(h * MEM_HEAD_DIM, (h + 1) * MEM_HEAD_DIM)) for i in range(n_sub) for h in range(MEM_HEADS)]
    scores = [mm(q[i][:, sl], mk[:, sl], _NT) * (MEM_HEAD_DIM ** -0.5) for i, sl in units]
    probs = []
    for s_ in scores:
        if n_b > 1:
            s_ = jnp.where(same, s_, NEG_INF)
        p = jnp.exp(s_ - jnp.max(s_, axis=-1, keepdims=True))
        probs.append(p / jnp.sum(p, axis=-1, keepdims=True))
    pv = [mm(p, mv[:, sl]) for p, (_, sl) in zip(probs, units)]
    att = [jnp.concatenate(pv[i * MEM_HEADS:(i + 1) * MEM_HEADS], axis=1) for i in range(n_sub)]
    x2 = [a + mm(t, wmo_ref[...]) for a, t in zip(x1, att)]
    xn = [_rms(t, gf_ref[...]) for t in x2]
    logits = [mm(t, wr_ref[...]) + br_ref[...] for t in xn]
    for rs, a, b, c in zip(pieces, x2, xn, logits):
        x2_ref[rs, :] = a
        xn_ref[rs, :] = b
        route_ref[rs, :] = _route(c)


def _mid(x, y_a, o_b, lse_b, gates, mem_k, mem_v, mk_map, mv_map, n_b, rows_per_batch, w, tm, xn_all, row0):
    n = x.shape[0]
    row = lambda width: pl.BlockSpec((tm, width), lambda i: (i, 0))
    slabs = pl.BlockSpec((SWA_PAIRS * len(B_GROUPS), tm, LANES), lambda i: (0, i, 0))
    in_specs = [
        row(D_MODEL), row(A_WIDTH), slabs, slabs, row(2 * D_MODEL),
        pl.BlockSpec((n_b, N_MEM, MEM_WIDTH), mk_map), pl.BlockSpec((n_b, N_MEM, MEM_WIDTH), mv_map),
        _const_spec((A_WIDTH, D_MODEL)), _const_spec((B_GROUP_WIDTH, D_MODEL)), _const_spec((D_MODEL, D_MODEL)),
        _const_spec((1, D_MODEL)), _const_spec((D_MODEL, MEM_WIDTH)), _const_spec((MEM_WIDTH, D_MODEL)),
        _const_spec((1, D_MODEL)), _const_spec((D_MODEL, LANES)), _const_spec((1, LANES)),
        pl.BlockSpec(memory_space=pl.ANY),
    ]
    args = [x, y_a, o_b, lse_b, gates, mem_k, mem_v, w['wa'], w['wb'], w['wo'], w['gm'], w['wq'], w['wmo'],
            w['gf'], w['wr'], w['br'], xn_all]
    blk0 = row0 // tm
    return pl.pallas_call(
        functools.partial(_mid_kernel, rows_per_batch=rows_per_batch, exact=w['wa'].dtype == F32,
                          n_sub=MID_SUB if n_b == 1 else 1),
        grid=(n // tm,),
        in_specs=in_specs,
        out_specs=[row(D_MODEL), pl.BlockSpec((tm, D_MODEL), lambda i: (i + blk0, 0)), row(LANES)],
        out_shape=[
            jax.ShapeDtypeStruct((n, D_MODEL), F32),
            jax.ShapeDtypeStruct(xn_all.shape, F32),
            jax.ShapeDtypeStruct((n, LANES), F32),
        ],
        input_output_aliases={len(args) - 1: 1},
        compiler_params=_params(1),
        name="mid",
    )(*args)


def _row_copy(src_hbm, idx, dst_buf, slot, j, sem):
    return pltpu.make_async_copy(src_hbm.at[pl.ds(idx, 1), :], dst_buf.at[slot, pl.ds(j, 1), :], sem.at[slot])


def _gather_start(idx_ref, src_hbm, dst_buf, slot, sem, n_rows):
    for j in range(n_rows):
        _row_copy(src_hbm, idx_ref[0, 0, j], dst_buf, slot, j, sem).start(priority=j % 2)


def _gather_wait(src_hbm, dst_buf, slot, sem, n_rows):
    for j in range(n_rows):
        _row_copy(src_hbm, 0, dst_buf, slot, j, sem).wait()


def _experts_kernel(meta_ref, be_ref, idx_ref, idx_1_ref, idx_2_ref, x_hbm, wg_ref, wu_ref, wd_ref, o_ref, xbuf, sem):
    i = pl.program_id(0)
    n_used = meta_ref[0]
    slot = i % MOE_RING

    @pl.when(i == 0)
    def _():
        _gather_start(idx_ref, x_hbm, xbuf, 0, sem, MOE_ROWS)

    @pl.when((i == 0) & (n_used > 1))
    def _():
        _gather_start(idx_1_ref, x_hbm, xbuf, 1, sem, MOE_ROWS)

    @pl.when(i + 2 < n_used)
    def _():
        _gather_start(idx_2_ref, x_hbm, xbuf, (i + 2) % MOE_RING, sem, MOE_ROWS)

    @pl.when(i < n_used)
    def _():
        _gather_wait(x_hbm, xbuf, slot, sem, MOE_ROWS)
        xb = xbuf[slot].astype(BF16)
        hg = _dot(xb, wg_ref[0].astype(BF16))
        hu = _dot(xb, wu_ref[0].astype(BF16))
        hh = (jax.nn.silu(hg) * hu).astype(BF16)
        o_ref[...] = _dot(hh, wd_ref[0].astype(BF16))

    @pl.when(i >= n_used)
    def _():
        o_ref[...] = jnp.zeros_like(o_ref)


def _experts(xn_all, row_tok, block_e, n_used, w_gate, w_up, w_down):
    n_blocks = block_e.shape[0]
    idx3 = row_tok.reshape(n_blocks, 1, MOE_ROWS)
    idx_spec = lambda f: pl.BlockSpec((1, 1, MOE_ROWS), f, memory_space=pltpu.SMEM)
    grid_spec = pltpu.PrefetchScalarGridSpec(
        num_scalar_prefetch=2,
        grid=(n_blocks,),
        in_specs=[
            idx_spec(lambda i, meta, be: (i, 0, 0)),
            idx_spec(lambda i, meta, be: (jnp.minimum(i + 1, n_blocks - 1), 0, 0)),
            idx_spec(lambda i, meta, be: (jnp.minimum(i + 2, n_blocks - 1), 0, 0)),
            pl.BlockSpec(memory_space=pl.ANY),
            pl.BlockSpec((1, D_MODEL, EXPERT_FF), lambda i, meta, be: (be[i], 0, 0)),
            pl.BlockSpec((1, D_MODEL, EXPERT_FF), lambda i, meta, be: (be[i], 0, 0)),
            pl.BlockSpec((1, EXPERT_FF, D_MODEL), lambda i, meta, be: (be[i], 0, 0)),
        ],
        out_specs=pl.BlockSpec((MOE_ROWS, D_MODEL), lambda i, meta, be: (i, 0)),
        scratch_shapes=[pltpu.VMEM((MOE_RING, MOE_ROWS, D_MODEL), F32), pltpu.SemaphoreType.DMA((MOE_RING,))],
    )
    return pl.pallas_call(
        _experts_kernel,
        grid_spec=grid_spec,
        out_shape=jax.ShapeDtypeStruct((n_blocks * MOE_ROWS, D_MODEL), F32),
        compiler_params=_params(1),
        name="experts",
    )(n_used.reshape(1), block_e, idx3, idx3, idx3, xn_all, w_gate, w_up, w_down)


def _combine_kernel(pos_ref, pos_next_ref, x_ref, route_ref, yb_hbm, g_ref, o_ref, ybuf, sem, *, n_tiles):
    i = pl.program_id(0)
    tm = x_ref.shape[0]
    slot = i % 2

    @pl.when(i == 0)
    def _():
        _gather_start(pos_ref, yb_hbm, ybuf, 0, sem, 2 * tm)

    if n_tiles > 1:
        @pl.when(i + 1 < n_tiles)
        def _():
            _gather_start(pos_next_ref, yb_hbm, ybuf, 1 - slot, sem, 2 * tm)

    _gather_wait(yb_hbm, ybuf, slot, sem, 2 * tm)
    route = route_ref[...]
    y = x_ref[...] + (route[:, 2:3] * ybuf[slot, :tm, :] + route[:, 3:4] * ybuf[slot, tm:, :])
    o_ref[...] = _rms(y, g_ref[...])


def _combine(x2, route, pos, yb, g_final, tm):
    n = x2.shape[0]
    n_tiles = n // tm
    pos_spec = lambda f: pl.BlockSpec((1, 1, 2 * tm), f, memory_space=pltpu.SMEM)
    return pl.pallas_call(
        functools.partial(_combine_kernel, n_tiles=n_tiles),
        grid=(n_tiles,),
        in_specs=[
            pos_spec(lambda i: (i, 0, 0)),
            pos_spec(lambda i: (jnp.minimum(i + 1, n_tiles - 1), 0, 0)),
            pl.BlockSpec((tm, D_MODEL), lambda i: (i, 0)),
            pl.BlockSpec((tm, LANES), lambda i: (i, 0)),
            pl.BlockSpec(memory_space=pl.ANY),
            _const_spec((1, D_MODEL)),
        ],
        out_specs=pl.BlockSpec((tm, D_MODEL), lambda i: (i, 0)),
        out_shape=jax.ShapeDtypeStruct((n, D_MODEL), F32),
        scratch_shapes=[pltpu.VMEM((2, 2 * tm, D_MODEL), F32), pltpu.SemaphoreType.DMA((2,))],
        compiler_params=_params(1),
        name="combine",
    )(pos, pos, x2, route, yb, g_final.reshape(1, -1))


def _dispatch(eid):
    n_tok = eid.shape[0]
    n_rows = n_tok * TOP_K
    n_blocks = n_rows // MOE_ROWS + N_EXPERTS
    flat_e = eid.reshape(-1)
    onehot = (flat_e[:, None] == jnp.arange(N_EXPERTS, dtype=jnp.int32)[None, :]).astype(jnp.int32)
    csum = jnp.cumsum(onehot, axis=0)
    counts = csum[-1]
    rank = jnp.sum((csum - onehot) * onehot, axis=1)
    padded = (counts + MOE_ROWS - 1) // MOE_ROWS * MOE_ROWS
    pad_end = jnp.cumsum(padded)
    pad_start = pad_end - padded
    dest = pad_start[flat_e] + rank
    flat_tok = jnp.arange(n_rows, dtype=jnp.int32) // TOP_K
    row_tok = jnp.zeros((n_blocks * MOE_ROWS,), jnp.int32).at[dest].set(flat_tok, unique_indices=True)
    block_start = jnp.arange(n_blocks, dtype=jnp.int32) * MOE_ROWS
    block_e = jnp.minimum(jnp.sum((pad_end[None, :] <= block_start[:, None]).astype(jnp.int32), axis=1), N_EXPERTS - 1)
    n_used = (pad_end[-1] // MOE_ROWS).astype(jnp.int32)
    return row_tok, block_e, n_used, dest.reshape(n_tok, TOP_K).astype(jnp.int32)


def _tile_pos(pos, tm):
    n = pos.shape[0]
    return pos.reshape(n // tm, tm, TOP_K).transpose(0, 2, 1).reshape(n // tm, 1, TOP_K * tm)


def _slabs_to_rows(qkv_slabs):
    order = np.array([_slab_position(j) for j in range(SWA_SLABS)])
    return jnp.transpose(qkv_slabs[order], (1, 0, 2)).reshape(qkv_slabs.shape[1], B_PROJ)


def _rows_to_pair_slabs(t):
    n = t.shape[0]
    return jnp.transpose(t.reshape(n, len(B_GROUPS), SWA_PAIRS, LANES), (2, 1, 0, 3)).reshape(-1, n, LANES)


def kernel(x_prompt, x_sample, state_rwkv, state_shift, cache_swa_w128, cache_swa_w512, cache_swa_w2048,
           cache_mem_kv, mem_prompt, norm_mix_g, w_in, b_gate, rwkv_mu, rwkv_w0, rwkv_w2, rwkv_a0, rwkv_a2,
           rwkv_g2, rwkv_k_k, rwkv_k_a, rwkv_r_k, rwkv_lnx_g, rwkv_lnx_b, w_branch_a, w_branch_b, w_out,
           norm_mem_g, norm_memkv_g, w_mem_q, w_mem_kv, w_mem_out, norm_ffn_g, w_router_group, b_router_group,
           w_router_expert, b_router_expert, w_exp_gate, w_exp_up, w_exp_down, norm_final_g):
    bsz, s_len, _ = x_prompt.shape
    dbs, t_len, _ = x_sample.shape
    n_p, n_s = bsz * s_len, dbs * t_len
    rw = dict(rwkv_mu=rwkv_mu, rwkv_w0=rwkv_w0, rwkv_w2=rwkv_w2, rwkv_a0=rwkv_a0, rwkv_a2=rwkv_a2, rwkv_g2=rwkv_g2,
              rwkv_k_k=rwkv_k_k, rwkv_k_a=rwkv_k_a, rwkv_r_k=rwkv_r_k, rwkv_lnx_g=rwkv_lnx_g, rwkv_lnx_b=rwkv_lnx_b)
    pad = LANES - N_GROUPS - N_EXPERTS
    mid_w_s = dict(
        wa=w_branch_a, wb=w_branch_b, wo=w_out, gm=norm_mem_g.reshape(1, -1), wq=w_mem_q, wmo=w_mem_out,
        gf=norm_ffn_g.reshape(1, -1),
        wr=jnp.concatenate([w_router_group, w_router_expert, jnp.zeros((D_MODEL, pad), F32)], axis=1),
        br=jnp.concatenate([b_router_group, b_router_expert, jnp.zeros((pad,), F32)]).reshape(1, -1))
    mid_w_p = {k: (v.astype(BF16) if k.startswith('w') else v) for k, v in mid_w_s.items()}
    w_in_b = w_in.astype(BF16)

    xp = x_prompt.reshape(n_p, D_MODEL)
    tm = 512
    ua_p, qkv_p, gates_p, *p_caches = _in_proj(xp, norm_mix_g, w_in_b, b_gate, _rope_tables(jnp.arange(s_len)), tm,
                                               cache_seq=(bsz, s_len))
    s0 = jnp.zeros((bsz, A_HEADS, A_HEAD_DIM, A_HEAD_DIM), F32)
    ya_p, st_p, shift_p = _rwkv(ua_p.reshape(bsz, s_len, A_PROJ), jnp.zeros((bsz, A_PROJ), F32), s0, rw,
                                RWKV_CHUNK, s_len)
    ob_p, lse_p = _swa_prompt(qkv_p, bsz, s_len)
    p_bufs = [c.reshape(c.shape[:3] + (B_HEADS_PER_GROUP, B_HEAD_DIM)) for c in p_caches]

    memkv = _norm_matmul(mem_prompt.reshape(bsz * N_MEM, D_MODEL), norm_memkv_g, w_mem_kv.astype(BF16), 256)
    memkv3 = memkv.reshape(bsz, N_MEM, 2 * MEM_WIDTH)
    mem_kv_prompt = memkv3.reshape(bsz, N_MEM, 2, MEM_HEADS, MEM_HEAD_DIM).transpose(0, 2, 1, 3, 4)
    tiles_per_batch = s_len // tm
    xn_all = jnp.zeros((n_p + n_s, D_MODEL), F32)
    x2_p, xn_all, route_p = _mid(
        xp, ya_p.reshape(n_p, A_WIDTH), ob_p, lse_p, gates_p, memkv3, memkv3,
        lambda i: (i // tiles_per_batch, 0, 0), lambda i: (i // tiles_per_batch, 0, 1),
        1, s_len, mid_w_p, tm, xn_all, 0)

    xs = x_sample.reshape(n_s, D_MODEL)
    pos_s = PAST_LEN + (jnp.arange(n_s) % t_len)
    ua_s, qkv_s, gates_s = _in_proj(xs, norm_mix_g, w_in, b_gate, _rope_tables(pos_s), n_s)
    t_pad = 8
    ua_s3 = jnp.pad(ua_s.reshape(dbs, t_len, A_PROJ), ((0, 0), (0, t_pad - t_len), (0, 0)))
    ya_s, st_s, shift_s = _rwkv(ua_s3, state_shift, state_rwkv, rw, 0, t_len)
    ya_s = ya_s[:, :t_len].reshape(n_s, A_WIDTH)
    caches = [jnp.transpose(c, (0, 1, 3, 4, 2)) for c in (cache_swa_w128, cache_swa_w512, cache_swa_w2048)]
    ob_s, lse_s, nb0, nb1, nb2 = _swa_sample(_slabs_to_rows(qkv_s).reshape(dbs, t_len, B_PROJ), caches)
    s_bufs = [jnp.transpose(nb, (0, 1, 4, 2, 3)) for nb in (nb0, nb1, nb2)]
    mem_s = cache_mem_kv.reshape(dbs, 2 * N_MEM, MEM_WIDTH)
    tm_s = 32
    x2_s, xn_all, route_s = _mid(
        xs, ya_s, _rows_to_pair_slabs(ob_s.reshape(n_s, B_WIDTH)), _rows_to_pair_slabs(lse_s.reshape(n_s, B_WIDTH)),
        gates_s, mem_s, mem_s,
        lambda i: (i, 0, 0), lambda i: (i, 1, 0), tm_s // t_len, t_len, mid_w_s, tm_s, xn_all, n_p)

    route = jnp.concatenate([route_p, route_s], axis=0)
    eid = route[:, :TOP_K].astype(jnp.int32)
    row_tok, block_e, n_used, pos = _dispatch(eid)
    yb = _experts(xn_all, row_tok, block_e, n_used, w_exp_gate, w_exp_up, w_exp_down)
    tm_c = 256
    y_p = _combine(x2_p, route_p, _tile_pos(pos[:n_p], tm_c), yb, norm_final_g, tm_c)
    tm_c = min(tm_c, n_s)
    y_s = _combine(x2_s, route_s, _tile_pos(pos[n_p:], tm_c), yb, norm_final_g, tm_c)

    return (y_p.reshape(bsz, s_len, D_MODEL), y_s.reshape(dbs, t_len, D_MODEL),
            st_p, shift_p.reshape(bsz, A_PROJ), p_bufs[0], p_bufs[1], p_bufs[2], mem_kv_prompt,
            st_s, shift_s.reshape(dbs, A_PROJ), s_bufs[0], s_bufs[1], s_bufs[2])
```

```python
import functools
import math

import jax
import jax.numpy as jnp
import numpy as np
from jax import lax
from jax.experimental import pallas as pl
from jax.experimental.pallas import tpu as pltpu

F32 = jnp.float32
BF16 = jnp.bfloat16

D_MODEL = 1024
A_HEADS = 8
A_HEAD_DIM = 64
A_WIDTH = A_HEADS * A_HEAD_DIM
A_DECAY_LORA = 64
A_ICLR_LORA = 64
A_GATE_LORA = 128
A_PROJ = 3 * A_WIDTH + A_DECAY_LORA + A_ICLR_LORA + A_GATE_LORA
A_LNX_EPS = 64e-5
B_GROUPS = ((128, 1), (512, 4), (2048, 16))
B_HEADS_PER_GROUP = 4
B_HEAD_DIM = 64
B_GROUP_WIDTH = B_HEADS_PER_GROUP * B_HEAD_DIM
B_WIDTH = B_GROUP_WIDTH * len(B_GROUPS)
B_PROJ = 3 * B_WIDTH
ROPE_THETA = 500000.0
ROPE_DIM = B_HEAD_DIM // 4
ROPE_HALF = ROPE_DIM // 2
SWA_BLOCK = 128
N_MEM = 256
MEM_HEADS = 4
MEM_HEAD_DIM = 128
MEM_WIDTH = MEM_HEADS * MEM_HEAD_DIM
N_GROUPS = 4
EXPERTS_PER_GROUP = 8
N_EXPERTS = N_GROUPS * EXPERTS_PER_GROUP
TOP_K = 2
EXPERT_FF = 512
RMS_EPS = 1e-6
PAST_LEN = 8192

LANES = 128
SWA_SLABS = B_PROJ // LANES
SWA_PAIR = LANES // B_HEAD_DIM
SWA_PAIRS = B_GROUP_WIDTH // LANES
SWA_QUAD = 4
MID_SUB = 2
VMEM_LIMIT = 56 * 1024 * 1024
RWKV_CHUNK = 64
MOE_ROWS = 256
MOE_RING = 4
NEG_INF = float("-inf")


def _dot(a, b, precision=None):
    return jnp.dot(a, b, preferred_element_type=F32, precision=precision)


def _dot_nt(a, b, precision=None):
    return lax.dot_general(a, b, (((1,), (1,)), ((), ())), preferred_element_type=F32, precision=precision)


def _dot_tn(a, b, precision=None):
    return lax.dot_general(a, b, (((0,), (0,)), ((), ())), preferred_element_type=F32, precision=precision)


def _bf16_round(t):
    return t.astype(BF16).astype(F32)


_NN = (((1,), (0,)), ((), ()))
_NT = (((1,), (1,)), ((), ()))


def _mm(a, b, exact, dims=_NN):
    dg = lambda x, y: lax.dot_general(x, y, dims, preferred_element_type=F32)
    if not exact:
        return dg(a.astype(BF16), b.astype(BF16))
    a, b = a.astype(F32), b.astype(F32)
    a_hi, b_hi = a.astype(BF16), b.astype(BF16)
    a_lo = (a - a_hi.astype(F32)).astype(BF16)
    b_lo = (b - b_hi.astype(F32)).astype(BF16)
    return dg(a_hi, b_hi) + (dg(a_hi, b_lo) + dg(a_lo, b_hi))


def _contract_round(t, exact):
    return t if exact else _bf16_round(t)


def _rms(x, g):
    return x * lax.rsqrt(jnp.mean(x * x, axis=-1, keepdims=True) + RMS_EPS) * g


def _const_spec(shape):
    nd = len(shape)
    return pl.BlockSpec(shape, lambda *_: (0,) * nd, pipeline_mode=pl.Buffered(1))


def _params(n_axes):
    return pltpu.CompilerParams(dimension_semantics=("arbitrary",) * n_axes, vmem_limit_bytes=VMEM_LIMIT)


def _slab_position(j):
    kind, hs = divmod(j, SWA_SLABS // 3)
    g, p = divmod(hs, SWA_PAIRS)
    return p * (3 * len(B_GROUPS)) + kind * len(B_GROUPS) + g


def _in_proj_kernel(x_ref, g_ref, w_ref, bg_ref, rc_ref, rs1_ref, rs2_ref, ua_ref, qkv_ref, gate_ref, *cache_refs, exact):
    tm = x_ref.shape[0]
    xn = _rms(x_ref[...], g_ref[...])
    xn = xn if exact else xn.astype(BF16)
    mm = lambda w: _mm(xn, w, exact)
    ua_ref[...] = mm(w_ref[:, :A_PROJ])
    rc, rs1, rs2 = rc_ref[...], rs1_ref[...], rs2_ref[...]
    n_rot = 2 * B_WIDTH // LANES
    for j in range(SWA_SLABS):
        if j % 2 == 0:
            lo = A_PROJ + j * LANES
            s2 = mm(w_ref[:, lo:lo + 2 * LANES])
        s = s2[:, (j % 2) * LANES:(j % 2 + 1) * LANES]
        if j < n_rot:
            s = s * rc + pltpu.roll(s, LANES - ROPE_HALF, 1) * rs1 + pltpu.roll(s, ROPE_HALF, 1) * rs2
        qkv_ref[_slab_position(j)] = s
        kind, hs = divmod(j, SWA_SLABS // 3)
        if cache_refs and kind > 0:
            g, p = divmod(hs, SWA_PAIRS)
            rows = cache_refs[g].shape[2]
            cache_refs[g][0, kind - 1, :, p * LANES:(p + 1) * LANES] = s[tm - rows:, :]
    gate_ref[...] = jax.nn.sigmoid(mm(w_ref[:, A_PROJ + B_PROJ:]) + bg_ref[...])


def _rope_tables(pos):
    inv_freq = ROPE_THETA ** (-jnp.arange(ROPE_HALF, dtype=F32) * 2.0 / ROPE_DIM)
    ang = pos.astype(F32)[:, None] * inv_freq[None, :]
    cos, sin = jnp.cos(ang), jnp.sin(ang)
    n = pos.shape[0]
    rest = B_HEAD_DIM - ROPE_DIM
    c = jnp.concatenate([cos, cos, jnp.ones((n, rest), F32)], axis=1)
    s1 = jnp.concatenate([-sin, jnp.zeros((n, ROPE_HALF + rest), F32)], axis=1)
    s2 = jnp.concatenate([jnp.zeros((n, ROPE_HALF), F32), sin, jnp.zeros((n, rest), F32)], axis=1)
    rep = LANES // B_HEAD_DIM
    return tuple(jnp.tile(t, (1, rep)) for t in (c, s1, s2))


def _in_proj(x, g, w, b_gate, tables, tm, cache_seq=None):
    n = x.shape[0]
    p_rows = tables[0].shape[0]
    t_tiles = p_rows // tm
    in_proj_w = w.shape[1]
    tab_spec = pl.BlockSpec((tm, LANES), lambda i: (i % t_tiles, 0))
    cache_specs, cache_shapes = [], []
    if cache_seq is not None:
        n_seq, s_len = cache_seq
        tiles = s_len // tm
        for window, _ in B_GROUPS:
            keep = min(window, s_len)
            rows = min(tm, keep)
            first_tile = tiles - keep // rows
            cache_specs.append(pl.BlockSpec(
                (1, 2, rows, B_GROUP_WIDTH),
                lambda i, tiles=tiles, first_tile=first_tile: (i // tiles, 0, jnp.maximum(i % tiles - first_tile, 0), 0)))
            cache_shapes.append(jax.ShapeDtypeStruct((n_seq, 2, keep, B_GROUP_WIDTH), F32))
    return pl.pallas_call(
        functools.partial(_in_proj_kernel, exact=w.dtype == F32),
        grid=(n // tm,),
        in_specs=[
            pl.BlockSpec((tm, D_MODEL), lambda i: (i, 0)),
            _const_spec((1, D_MODEL)),
            _const_spec((D_MODEL, in_proj_w)),
            _const_spec((1, 2 * D_MODEL)),
            tab_spec, tab_spec, tab_spec,
        ],
        out_specs=[
            pl.BlockSpec((tm, A_PROJ), lambda i: (i, 0)),
            pl.BlockSpec((SWA_SLABS, tm, LANES), lambda i: (0, i, 0)),
            pl.BlockSpec((tm, 2 * D_MODEL), lambda i: (i, 0)),
        ] + cache_specs,
        out_shape=[
            jax.ShapeDtypeStruct((n, A_PROJ), F32),
            jax.ShapeDtypeStruct((SWA_SLABS, n, LANES), F32),
            jax.ShapeDtypeStruct((n, 2 * D_MODEL), F32),
        ] + cache_shapes,
        compiler_params=_params(1),
        name="in_proj",
    )(x, g.reshape(1, -1), w, b_gate.reshape(1, -1), *tables)


def _norm_matmul_kernel(x_ref, g_ref, w_ref, o_ref):
    o_ref[...] = _dot(_rms(x_ref[...], g_ref[...]).astype(BF16), w_ref[...])


def _norm_matmul(x, g, w_bf16, tm):
    n, d = x.shape
    dout = w_bf16.shape[1]
    return pl.pallas_call(
        _norm_matmul_kernel,
        grid=(n // tm,),
        in_specs=[pl.BlockSpec((tm, d), lambda i: (i, 0)), _const_spec((1, d)), _const_spec((d, dout))],
        out_specs=pl.BlockSpec((tm, dout), lambda i: (i, 0)),
        out_shape=jax.ShapeDtypeStruct((n, dout), F32),
        compiler_params=_params(1),
        name="norm_matmul",
    )(x, g.reshape(1, -1), w_bf16)


RWKV_HEADS_PER_PACK = 4
RWKV_PACK_WIDTH = RWKV_HEADS_PER_PACK * A_HEAD_DIM
RWKV_PACKS = A_HEADS // RWKV_HEADS_PER_PACK
RWKV_STEP_SEQS = 1


def _shift_rows(u, first_prev):
    row = lax.broadcasted_iota(jnp.int32, (u.shape[0], 1), 0)
    return jnp.where(row == 0, first_prev, pltpu.roll(u, 1, 0))


def _head_sum(x, bd, exact, pieces=1):
    ones = bd.astype(BF16)
    total, rest = None, x
    for _ in range(3 if exact else pieces):
        piece = rest.astype(BF16)
        rest = rest - piece.astype(F32)
        part = _dot(piece, ones)
        total = part if total is None else total + part
    return total


def _rwkv_features(u, u_prev, w_refs, bd, exact):
    mu_ref, w0_ref, w2_ref, a0_ref, a2_ref, g2_ref, kk_ref, ka_ref = w_refs
    um = u + (u_prev - u) * mu_ref[...]
    o1, o2, o3 = A_WIDTH, 2 * A_WIDTH, 3 * A_WIDTH
    o4 = o3 + A_DECAY_LORA
    o5 = o4 + A_ICLR_LORA
    r, k, v = um[:, :o1], um[:, o1:o2], um[:, o2:o3]
    xw, xa, xg = um[:, o3:o4], um[:, o4:o5], um[:, o5:]
    lora = lambda t, w_ref: _mm(t, w_ref[...], exact)
    w = -jax.nn.softplus(-(w0_ref[...] + lora(jnp.tanh(xw), w2_ref))) - 0.5
    e = jnp.exp(w)
    a = jax.nn.sigmoid(a0_ref[...] + lora(xa, a2_ref))
    g = lora(jax.nn.sigmoid(xg), g2_ref)
    kk = k * kk_ref[...]
    kkn = kk / jnp.maximum(jnp.sqrt(_head_sum(kk * kk, bd, exact, pieces=2)), 1e-12)
    k2 = k * (1.0 + (a - 1.0) * ka_ref[...])
    return r, k2, v, e, a, g, kkn


def _rwkv_output(y, r, k2, v, g, rk_ref, lng_ref, lnb_ref, bd, exact):
    inv_n = 1.0 / A_HEAD_DIM
    mean = _head_sum(y, bd, exact) * inv_n
    yc = y - mean
    var = _head_sum(yc * yc, bd, exact) * inv_n
    yn = yc * lax.rsqrt(var + A_LNX_EPS) * lng_ref[...] + lnb_ref[...]
    bonus = _head_sum(r * k2 * rk_ref[...], bd, exact) * v
    return (yn + bonus) * g


def _rwkv_chunk_kernel(u_ref, sh0_ref, s0_ref, mu_ref, w0_ref, w2_ref, a0_ref, a2_ref, g2_ref, kk_ref, ka_ref,
                       rk_ref, lng_ref, lnb_ref, bd_ref, y_ref, sfin_ref, shout_ref, st_scr, prev_scr, *, n_steps):
    c = pl.program_id(0)
    n_b, chunk, _ = u_ref.shape
    hd, hpp, pw_ = A_HEAD_DIM, RWKV_HEADS_PER_PACK, RWKV_PACK_WIDTH
    bf = lambda t: t.astype(BF16)

    @pl.when(c == 0)
    def _():
        for b in range(n_b):
            prev_scr[b:b + 1, :] = sh0_ref[b]
            for p in range(RWKV_PACKS):
                st_scr[b, p] = jnp.concatenate([s0_ref[b, p * hpp + h] for h in range(hpp)], axis=1)

    u_b = [u_ref[b] for b in range(n_b)]
    u = jnp.concatenate(u_b, axis=0)
    u_prev = jnp.concatenate([_shift_rows(u_b[b], prev_scr[b:b + 1, :]) for b in range(n_b)], axis=0)
    for b in range(n_b):
        prev_scr[b:b + 1, :] = u_b[b][chunk - 1:chunk, :]
    bd = bd_ref[...]
    r, k2, v, e, a, g, kkn = _rwkv_features(
        u, u_prev, (mu_ref, w0_ref, w2_ref, a0_ref, a2_ref, g2_ref, kk_ref, ka_ref), bd, False)
    b_ = kkn * a

    li = lax.broadcasted_iota(jnp.int32, (chunk, chunk), 0)
    lj = lax.broadcasted_iota(jnp.int32, (chunk, chunk), 1)
    tri = bf((li >= lj).astype(F32))
    e_hi = bf(e)
    e_rest = e - e_hi.astype(F32)
    e_mid = bf(e_rest)
    e_lo = bf(e_rest - e_mid.astype(F32))
    cums, ends = [], []
    for b in range(n_b):
        rs = slice(b * chunk, (b + 1) * chunk)
        cb = _dot(tri, e_hi[rs]) + (_dot(tri, e_mid[rs]) + _dot(tri, e_lo[rs]))
        cums.append(cb)
        ends.append(jnp.broadcast_to(cb[chunk - 1:chunk, :], (chunk, A_WIDTH)))
    cum = jnp.concatenate(cums, axis=0)
    cum_end = jnp.concatenate(ends, axis=0)
    grow = jnp.exp(cum)
    to_end = jnp.exp(cum - cum_end)
    at = bf(-kkn * jnp.exp(e - cum))
    rt = bf(r * jnp.exp(-cum))
    bt = bf(b_ * grow)
    kt = bf(k2 * grow)
    bh = bf(b_ * to_end)
    kh = bf(k2 * to_end)
    vb = bf(v)
    dec_end = jnp.exp(-cum_end)

    lane_head = lax.broadcasted_iota(jnp.int32, (1, pw_), 1) // hd
    head_mask = [lane_head == h for h in range(hpp)]

    def block_diag(x):
        return jnp.concatenate([jnp.where(head_mask[h], x, jnp.zeros_like(x)) for h in range(hpp)], axis=0)

    assert chunk == hd
    ti = lax.broadcasted_iota(jnp.int32, (chunk, pw_), 0)
    tj = lax.broadcasted_iota(jnp.int32, (chunk, pw_), 1) % chunk
    strict = ti > tj
    incl = ti >= tj
    eye = (ti == tj).astype(F32)
    n_sq = int(math.log2(chunk)) - 1

    streams = [(b, p, slice(b * chunk, (b + 1) * chunk), slice(p * pw_, (p + 1) * pw_))
               for b in range(n_b) for p in range(RWKV_PACKS)]
    at_s = [at[rs, cs] for _, _, rs, cs in streams]
    rt_s = [rt[rs, cs] for _, _, rs, cs in streams]
    v_s = [vb[rs, cs] for _, _, rs, cs in streams]
    m = [_dot_nt(jnp.concatenate([a_, r_], axis=0),
                 jnp.concatenate([block_diag(bt[rs, cs]), block_diag(kt[rs, cs])], axis=0))
         for a_, r_, (_, _, rs, cs) in zip(at_s, rt_s, streams)]
    a_ab = [jnp.where(strict, x[:chunk, :pw_], 0.0) for x in m]
    akv = [_dot(bf(jnp.where(strict, x[:chunk, pw_:], 0.0)), block_diag(vs)) for x, vs in zip(m, v_s)]
    m_r = [bf(jnp.concatenate([jnp.where(incl, x[chunk:, :pw_], 0.0), jnp.where(incl, x[chunk:, pw_:], 0.0)], axis=1))
           for x in m]
    tinv = [eye + x for x in a_ab]
    pw = a_ab
    for _ in range(n_sq):
        pw = [_dot(bf(x), block_diag(bf(x))) for x in pw]
        tinv = [t + _dot(bf(t), block_diag(bf(x))) for t, x in zip(tinv, pw)]
    w12 = [_dot(bf(t), jnp.concatenate([block_diag(a_), block_diag(bf(x))], axis=1))
           for t, a_, x in zip(tinv, at_s, akv)]

    st = [st_scr[b, p] for b, p, _, _ in streams]
    x = [_dot_nt(jnp.concatenate([bf(w[:, :pw_]), r_], axis=0), block_diag(bf(s))) for w, r_, s in zip(w12, rt_s, st)]
    uu = [bf(xi[:chunk] + w[:, pw_:]) for xi, w in zip(x, w12)]
    ys = [xi[chunk:] + _dot(mr, jnp.concatenate([block_diag(ui), block_diag(vs)], axis=0))
          for xi, mr, ui, vs in zip(x, m_r, uu, v_s)]
    for i, (b, p, rs, cs) in enumerate(streams):
        upd = _dot_tn(jnp.concatenate([uu[i], v_s[i]], axis=0), jnp.concatenate([bh[rs, cs], kh[rs, cs]], axis=0))
        diag = functools.reduce(lambda s_, t_: s_ + t_,
                                [jnp.where(head_mask[h], upd[h * hd:(h + 1) * hd, :], 0.0) for h in range(hpp)])
        st_scr[b, p] = st[i] * dec_end[rs.start:rs.start + 1, cs] + diag
    y = jnp.concatenate([jnp.concatenate(ys[b * RWKV_PACKS:(b + 1) * RWKV_PACKS], axis=1) for b in range(n_b)], axis=0)
    out = _rwkv_output(y, r, k2, v, g, rk_ref, lng_ref, lnb_ref, bd, False)
    for b in range(n_b):
        y_ref[b] = out[b * chunk:(b + 1) * chunk]

    @pl.when(c == n_steps - 1)
    def _():
        for b in range(n_b):
            shout_ref[b] = u_b[b][chunk - 1:chunk, :]
            for p in range(RWKV_PACKS):
                for h in range(hpp):
                    sfin_ref[b, p * hpp + h] = st_scr[b, p][:, h * hd:(h + 1) * hd]


def _rwkv_step_kernel(u_ref, sh0_ref, s0_ref, mu_ref, w0_ref, w2_ref, a0_ref, a2_ref, g2_ref, kk_ref, ka_ref,
                      rk_ref, lng_ref, lnb_ref, bd_ref, y_ref, sfin_ref, shout_ref, *, t_valid, exact):
    hd = A_HEAD_DIM
    n_b, n_rows, _ = u_ref.shape
    u_b = [u_ref[b] for b in range(n_b)]
    u = jnp.concatenate(u_b, axis=0)
    u_prev = jnp.concatenate([_shift_rows(u_b[b], sh0_ref[b]) for b in range(n_b)], axis=0)
    bd = bd_ref[...]
    r, k2, v, e, a, g, kkn = _rwkv_features(
        u, u_prev, (mu_ref, w0_ref, w2_ref, a0_ref, a2_ref, g2_ref, kk_ref, ka_ref), bd, exact)
    rnd = lambda t: _contract_round(t, exact)
    decay = jnp.exp(-e)
    b_ = kkn * a
    eye = (lax.broadcasted_iota(jnp.int32, (hd, hd), 0) == lax.broadcasted_iota(jnp.int32, (hd, hd), 1)).astype(F32)
    to_col = lambda t: jnp.sum(eye * t, axis=1, keepdims=True)
    to_row = lambda t: jnp.sum(eye * t, axis=0, keepdims=True)
    y_seq = []
    for b in range(n_b):
        y_heads = []
        for h in range(A_HEADS):
            sl = slice(h * hd, (h + 1) * hd)
            s = s0_ref[b, h]
            y_rows = []
            for t in range(t_valid):
                tt = slice(b * n_rows + t, b * n_rows + t + 1)
                sa = jnp.sum(rnd(s) * rnd(-kkn[tt, sl]), axis=1, keepdims=True)
                s = s * decay[tt, sl] + sa * b_[tt, sl] + to_col(v[tt, sl]) * k2[tt, sl]
                y_rows.append(to_row(jnp.sum(rnd(s) * rnd(r[tt, sl]), axis=1, keepdims=True)))
            sfin_ref[b, h] = s
            y_rows.append(jnp.zeros((n_rows - t_valid, hd), F32))
            y_heads.append(jnp.concatenate(y_rows, axis=0))
        y_seq.append(jnp.concatenate(y_heads, axis=1))
    y = jnp.concatenate(y_seq, axis=0)
    out = _rwkv_output(y, r, k2, v, g, rk_ref, lng_ref, lnb_ref, bd, exact)
    for b in range(n_b):
        y_ref[b] = out[b * n_rows:(b + 1) * n_rows]
        shout_ref[b] = u_b[b][t_valid - 1:t_valid, :]


def _rwkv(u_a, shift0, s0, p, chunk, t_valid):
    bsz, t_len, _ = u_a.shape
    hd = A_HEAD_DIM
    bd = jnp.asarray(np.kron(np.eye(A_HEADS, dtype=np.float32), np.ones((hd, hd), np.float32)))
    row = lambda t: t.reshape(1, -1)
    if chunk:
        assert t_valid == t_len and t_len % chunk == 0
        grid = (t_len // chunk,)
        kern = functools.partial(_rwkv_chunk_kernel, n_steps=grid[0])
        scratch = [pltpu.VMEM((bsz, RWKV_PACKS, hd, RWKV_PACK_WIDTH), F32), pltpu.VMEM((bsz, A_PROJ), F32)]
        u_spec = pl.BlockSpec((bsz, chunk, A_PROJ), lambda c: (0, c, 0))
        y_spec = pl.BlockSpec((bsz, chunk, A_WIDTH), lambda c: (0, c, 0))
        state_spec = pl.BlockSpec((bsz, A_HEADS, hd, hd), lambda c: (0, 0, 0, 0))
        shift_spec = pl.BlockSpec((bsz, 1, A_PROJ), lambda c: (0, 0, 0))
    else:
        n_b = math.gcd(bsz, RWKV_STEP_SEQS)
        grid = (bsz // n_b,)
        kern = functools.partial(_rwkv_step_kernel, t_valid=t_valid, exact=True)
        scratch = []
        u_spec = pl.BlockSpec((n_b, t_len, A_PROJ), lambda b: (b, 0, 0))
        y_spec = pl.BlockSpec((n_b, t_len, A_WIDTH), lambda b: (b, 0, 0))
        state_spec = pl.BlockSpec((n_b, A_HEADS, hd, hd), lambda b: (b, 0, 0, 0))
        shift_spec = pl.BlockSpec((n_b, 1, A_PROJ), lambda b: (b, 0, 0))
    return pl.pallas_call(
        kern,
        grid=grid,
        in_specs=[
            u_spec, shift_spec, state_spec,
            _const_spec((1, A_PROJ)), _const_spec((1, A_WIDTH)), _const_spec((A_DECAY_LORA, A_WIDTH)),
            _const_spec((1, A_WIDTH)), _const_spec((A_ICLR_LORA, A_WIDTH)), _const_spec((A_GATE_LORA, A_WIDTH)),
            _const_spec((1, A_WIDTH)), _const_spec((1, A_WIDTH)), _const_spec((1, A_WIDTH)),
            _const_spec((1, A_WIDTH)), _const_spec((1, A_WIDTH)), _const_spec((A_WIDTH, A_WIDTH)),
        ],
        out_specs=[y_spec, state_spec, shift_spec],
        out_shape=[
            jax.ShapeDtypeStruct((bsz, t_len, A_WIDTH), F32),
            jax.ShapeDtypeStruct((bsz, A_HEADS, hd, hd), F32),
            jax.ShapeDtypeStruct((bsz, 1, A_PROJ), F32),
        ],
        scratch_shapes=scratch,
        compiler_params=_params(1),
        name="rwkv7",
    )(u_a, shift0.reshape(bsz, 1, A_PROJ), s0, row(p['rwkv_mu']), row(p['rwkv_w0']), p['rwkv_w2'],
      row(p['rwkv_a0']), p['rwkv_a2'], p['rwkv_g2'], row(p['rwkv_k_k']), row(p['rwkv_k_a']),
      row(p['rwkv_r_k']), row(p['rwkv_lnx_g']), row(p['rwkv_lnx_b']), bd)


def _swa_prompt_kernel(qkv_ref, o_ref, lse_ref):
    blk, hd, n_g = SWA_BLOCK, B_HEAD_DIM, len(B_GROUPS)
    s_len = qkv_ref.shape[1]
    scale = hd ** -0.5
    rows2 = SWA_PAIR * blk
    r_i = lax.broadcasted_iota(jnp.int32, (rows2, 1), 0)
    qi = r_i % blk
    own_lanes = (lax.broadcasted_iota(jnp.int32, (1, LANES), 1) // hd) == (r_i // blk)
    head0_lanes = own_lanes[:blk]
    ki2 = lax.broadcasted_iota(jnp.int32, (rows2, 2 * blk), 1)
    band2 = (qi + blk - ki2 >= 0) & (qi - ki2 <= 0)
    causal1 = lax.broadcasted_iota(jnp.int32, (rows2, blk), 1) <= qi

    def attend(g, specs):
        dil = B_GROUPS[g][1]

        def rows(kind, st):
            idx = pl.ds(st, blk, stride=dil) if dil > 1 else pl.ds(st, blk)
            return qkv_ref[kind * n_g + g, idx, :]

        qs, kbs, vbs, masks = [], [], [], []
        for st, prev, first in specs:
            q = rows(0, st) * scale
            qs.append(jnp.where(own_lanes, jnp.concatenate([q] * SWA_PAIR, axis=0), 0.0).astype(BF16))
            if prev is None:
                kbs.append(rows(1, st).astype(BF16))
                vbs.append(rows(2, st).astype(BF16))
                masks.append(causal1)
            else:
                kbs.append(jnp.concatenate([rows(1, prev), rows(1, st)], axis=0).astype(BF16))
                vbs.append(jnp.concatenate([rows(2, prev), rows(2, st)], axis=0).astype(BF16))
                masks.append(band2 & (ki2 >= jnp.where(first, blk, 0)))
        scores = [_dot_nt(q, kb) for q, kb in zip(qs, kbs)]
        probs, dens, lses = [], [], []
        for sc, mk in zip(scores, masks):
            sc = jnp.where(mk, sc, NEG_INF)
            m = jnp.max(sc, axis=-1, keepdims=True)
            p = jnp.exp(sc - m)
            den = jnp.sum(p, axis=-1, keepdims=True)
            probs.append(p.astype(BF16))
            dens.append(den)
            lses.append(m + jnp.log(den))
        outs = [_dot(p, vb) / den for p, vb, den in zip(probs, vbs, dens)]
        for (st, _, _), o2, l2 in zip(specs, outs, lses):
            idx = pl.ds(st, blk, stride=dil) if dil > 1 else pl.ds(st, blk)
            o_ref[g, idx, :] = jnp.where(head0_lanes, o2[:blk], o2[blk:])
            lse_ref[g, idx, :] = jnp.where(head0_lanes, l2[:blk], l2[blk:])

    for g, (_, dil) in enumerate(B_GROUPS):
        n_blk = s_len // dil // blk
        n_quads = dil * n_blk // SWA_QUAD

        def quad(it, carry, g=g, dil=dil, n_blk=n_blk):
            specs = []
            for j in range(SWA_QUAD):
                if n_blk >= SWA_QUAD:
                    e = it * SWA_QUAD + j
                    r, n = e // n_blk, e % n_blk
                    specs.append((n * (blk * dil) + r, jnp.maximum(n - 1, 0) * (blk * dil) + r, n == 0))
                else:
                    r = it * (SWA_QUAD // n_blk) + j // n_blk
                    n = j % n_blk
                    specs.append((n * (blk * dil) + r, None if n == 0 else (n - 1) * (blk * dil) + r, False))
            attend(g, specs)
            return carry

        lax.fori_loop(0, n_quads, quad, 0)


def _swa_prompt(qkv_slabs, n_seq, s_len):
    n_g = len(B_GROUPS)
    n = n_seq * s_len
    out_spec = pl.BlockSpec((n_g, s_len, LANES), lambda b, p: (p, b, 0))
    shp = jax.ShapeDtypeStruct((SWA_PAIRS * n_g, n, LANES), F32)
    return pl.pallas_call(
        _swa_prompt_kernel,
        grid=(n_seq, SWA_PAIRS),
        in_specs=[pl.BlockSpec((3 * n_g, s_len, LANES), lambda b, p: (p, b, 0))],
        out_specs=[out_spec, out_spec],
        out_shape=[shp, shp],
        compiler_params=_params(2),
        name="swa_prompt",
    )(qkv_slabs)


def _swa_sample_kernel(qkv_ref, c0_ref, c1_ref, c2_ref, o_ref, lse_ref, n0_ref, n1_ref, n2_ref, *, t_len, exact):
    hd = B_HEAD_DIM
    nh = B_HEADS_PER_GROUP
    gw = B_GROUP_WIDTH
    qkv = qkv_ref[0]
    scale = hd ** -0.5
    rnd = lambda t: _contract_round(t, exact)
    t_col = lax.broadcasted_iota(jnp.int32, (t_len, 1), 0)
    i_n = lax.broadcasted_iota(jnp.int32, (t_len, t_len), 1)
    units = []
    for gi, ((window, dil), c_ref, n_ref) in enumerate(zip(B_GROUPS, (c0_ref, c1_ref, c2_ref), (n0_ref, n1_ref, n2_ref))):
        buf_len = c_ref.shape[4]
        j_c = lax.broadcasted_iota(jnp.int32, (t_len, buf_len), 1)
        ok_c = (j_c >= t_col) & (((j_c - t_col) & (dil - 1)) == 0)
        ok_n = (i_n <= t_col) & (((t_col - i_n) & (dil - 1)) == 0)
        for h in range(nh):
            lo = gi * gw + h * hd
            units.append(dict(
                q=qkv[:, lo:lo + hd], k_new=qkv[:, B_WIDTH + lo:B_WIDTH + lo + hd],
                v_new=qkv[:, 2 * B_WIDTH + lo:2 * B_WIDTH + lo + hd],
                kt=c_ref[0, 0, h], vt=c_ref[0, 1, h], ok_c=ok_c, ok_n=ok_n))
    s_cs = [jnp.where(u["ok_c"], _mm(u["q"], u["kt"], exact) * scale, NEG_INF) for u in units]
    probs, p_news, lses = [], [], []
    for u, s_c in zip(units, s_cs):
        qr, kr = rnd(u["q"]), rnd(u["k_new"])
        s_n = jnp.zeros((t_len, t_len), F32)
        for i in range(t_len):
            s_n = jnp.where(i_n == i, jnp.sum(qr * kr[i:i + 1, :], axis=-1, keepdims=True) * scale, s_n)
        s_n = jnp.where(u["ok_n"], s_n, NEG_INF)
        m = jnp.maximum(jnp.max(s_c, axis=-1, keepdims=True), jnp.max(s_n, axis=-1, keepdims=True))
        p_c = jnp.exp(s_c - m)
        p_n = jnp.exp(s_n - m)
        den = jnp.sum(p_c, axis=-1, keepdims=True) + jnp.sum(p_n, axis=-1, keepdims=True)
        probs.append(p_c / den)
        p_news.append(rnd(p_n / den))
        lses.append(jnp.broadcast_to(m + jnp.log(den), (t_len, hd)))
    outs = [_mm(p, u["vt"], exact, _NT) for p, u in zip(probs, units)]
    for k, (u, p_n) in enumerate(zip(units, p_news)):
        vr = rnd(u["v_new"])
        for i in range(t_len):
            outs[k] = outs[k] + p_n[:, i:i + 1] * vr[i:i + 1, :]
    pad_rows = 8 - t_len
    place = (lax.broadcasted_iota(jnp.int32, (8, LANES), 0) + (LANES - t_len)
             == lax.broadcasted_iota(jnp.int32, (8, LANES), 1)).astype(BF16)
    tail_lane = lax.broadcasted_iota(jnp.int32, (1, LANES), 1) >= LANES - t_len
    for gi, (c_ref, n_ref) in enumerate(zip((c0_ref, c1_ref, c2_ref), (n0_ref, n1_ref, n2_ref))):
        buf_len = c_ref.shape[4]
        for j in range(2):
            new = jnp.pad(qkv[:, (j + 1) * B_WIDTH + gi * gw:(j + 1) * B_WIDTH + (gi + 1) * gw], ((0, pad_rows), (0, 0)))
            cols, rest = None, new
            for _ in range(3):
                piece = rest.astype(BF16)
                rest = rest - piece.astype(F32)
                part = _dot_tn(piece, place)
                cols = part if cols is None else cols + part
            nxt = pltpu.roll(c_ref[0, j].reshape(gw, buf_len), buf_len - t_len, 1)
            tail = jnp.where(tail_lane, cols, nxt[:, buf_len - LANES:])
            full = tail if buf_len == LANES else jnp.concatenate([nxt[:, :buf_len - LANES], tail], axis=1)
            n_ref[0, j] = full.reshape(nh, hd, buf_len)
    o_ref[0] = jnp.concatenate(outs, axis=1)
    lse_ref[0] = jnp.concatenate(lses, axis=1)


def _swa_sample(qkv, caches):
    bsz, t_len, _ = qkv.shape
    cache_specs = [pl.BlockSpec((1,) + c.shape[1:], lambda b: (b, 0, 0, 0, 0)) for c in caches]
    row_spec = pl.BlockSpec((1, t_len, B_WIDTH), lambda b: (b, 0, 0))
    return pl.pallas_call(
        functools.partial(_swa_sample_kernel, t_len=t_len, exact=True),
        grid=(bsz,),
        in_specs=[pl.BlockSpec((1, t_len, B_PROJ), lambda b: (b, 0, 0))] + cache_specs,
        out_specs=[row_spec, row_spec] + cache_specs,
        out_shape=[jax.ShapeDtypeStruct((bsz, t_len, B_WIDTH), F32)] * 2
        + [jax.ShapeDtypeStruct(c.shape, F32) for c in caches],
        compiler_params=_params(1),
        name="swa_sample",
    )(qkv, *caches)


def _route(logits):
    lane = lax.broadcasted_iota(jnp.int32, logits.shape, 1)
    gl = jnp.where(lane < N_GROUPS, logits, NEG_INF)
    gmax = jnp.max(gl, axis=-1, keepdims=True)
    grp = jnp.min(jnp.where(gl == gmax, lane, LANES), axis=-1, keepdims=True)
    w_grp = 1.0 / jnp.sum(jnp.exp(gl - gmax), axis=-1, keepdims=True)
    first = N_GROUPS + grp * EXPERTS_PER_GROUP
    el = jnp.where((lane >= first) & (lane < first + EXPERTS_PER_GROUP), logits, NEG_INF)
    m1 = jnp.max(el, axis=-1, keepdims=True)
    i1 = jnp.min(jnp.where(el == m1, lane, LANES), axis=-1, keepdims=True)
    el2 = jnp.where(lane == i1, NEG_INF, el)
    m2 = jnp.max(el2, axis=-1, keepdims=True)
    i2 = jnp.min(jnp.where(el2 == m2, lane, LANES), axis=-1, keepdims=True)
    e2 = jnp.exp(m2 - m1)
    g1 = w_grp / (1.0 + e2)
    g2 = w_grp * e2 / (1.0 + e2)
    route = jnp.where(lane == 0, (i1 - N_GROUPS).astype(F32), 0.0)
    route = jnp.where(lane == 1, (i2 - N_GROUPS).astype(F32), route)
    route = jnp.where(lane == 2, g1, route)
    return jnp.where(lane == 3, g2, route)


def _mid_kernel(x_ref, ya_ref, ob_ref, lse_ref, gate_ref, mk_ref, mv_ref, wa_ref, wb_ref, wo_ref, gm_ref,
                wq_ref, wmo_ref, gf_ref, wr_ref, br_ref, xn_all_ref, x2_ref, xn_ref, route_ref,
                *, rows_per_batch, exact, n_sub):
    del xn_all_ref
    tm = x_ref.shape[0]
    sub = tm // n_sub
    pieces = [slice(i * sub, (i + 1) * sub) for i in range(n_sub)]
    n_g = len(B_GROUPS)
    rnd = lambda t: _contract_round(t, exact)
    mm = lambda a, b, dims=_NN: _mm(a, b, exact, dims)

    def mix(rs):
        yb_pairs = []
        for p in range(SWA_PAIRS):
            lses = [lse_ref[p * n_g + g, rs, :] for g in range(n_g)]
            m = functools.reduce(jnp.maximum, lses)
            es = [jnp.exp(l - m) for l in lses]
            den = functools.reduce(lambda a, b: a + b, es)
            yb_pairs.append(functools.reduce(lambda a, b: a + b,
                                             [rnd(es[g] / den) * rnd(ob_ref[p * n_g + g, rs, :]) for g in range(n_g)]))
        return jnp.concatenate(yb_pairs, axis=1)

    yb = [mix(rs) for rs in pieces]
    ma = [mm(ya_ref[rs, :], wa_ref[...]) for rs in pieces]
    mb = [mm(t, wb_ref[...]) for t in yb]
    merged = [gate_ref[rs, :D_MODEL] * a + gate_ref[rs, D_MODEL:] * b for rs, a, b in zip(pieces, ma, mb)]
    x1 = [x_ref[rs, :] + mm(t, wo_ref[...]) for rs, t in zip(pieces, merged)]

    q = [mm(_rms(t, gm_ref[...]), wq_ref[...]) for t in x1]
    n_b = mk_ref.shape[0]
    mk = mk_ref[...].reshape(n_b * N_MEM, MEM_WIDTH)
    mv = mv_ref[...].reshape(n_b * N_MEM, MEM_WIDTH)
    if not exact:
        mk, mv = mk.astype(BF16), mv.astype(BF16)
    if n_b > 1:
        assert n_sub == 1
        rb = lax.broadcasted_iota(jnp.int32, (tm, n_b * N_MEM), 0) // rows_per_batch
        cb = lax.broadcasted_iota(jnp.int32, (tm, n_b * N_MEM), 1) // N_MEM
        same = rb == cb
    units = [(i, slice(h * MEM_HEAD_DIM, (h + 1) * MEM_HEAD_DIM)) for i in range(n_sub) for h in range(MEM_HEADS)]
    scores = [mm(q[i][:, sl], mk[:, sl], _NT) * (MEM_HEAD_DIM ** -0.5) for i, sl in units]
    probs = []
    for s_ in scores:
        if n_b > 1:
            s_ = jnp.where(same, s_, NEG_INF)
        p = jnp.exp(s_ - jnp.max(s_, axis=-1, keepdims=True))
        probs.append(p / jnp.sum(p, axis=-1, keepdims=True))
    pv = [mm(p, mv[:, sl]) for p, (_, sl) in zip(probs, units)]
    att = [jnp.concatenate(pv[i * MEM_HEADS:(i + 1) * MEM_HEADS], axis=1) for i in range(n_sub)]
    x2 = [a + mm(t, wmo_ref[...]) for a, t in zip(x1, att)]
    xn = [_rms(t, gf_ref[...]) for t in x2]
    logits = [mm(t, wr_ref[...]) + br_ref[...] for t in xn]
    for rs, a, b, c in zip(pieces, x2, xn, logits):
        x2_ref[rs, :] = a
        xn_ref[rs, :] = b
        route_ref[rs, :] = _route(c)


def _mid(x, y_a, o_b, lse_b, gates, mem_k, mem_v, mk_map, mv_map, n_b, rows_per_batch, w, tm, xn_all, row0):
    n = x.shape[0]
    row = lambda width: pl.BlockSpec((tm, width), lambda i: (i, 0))
    slabs = pl.BlockSpec((SWA_PAIRS * len(B_GROUPS), tm, LANES), lambda i: (0, i, 0))
    in_specs = [
        row(D_MODEL), row(A_WIDTH), slabs, slabs, row(2 * D_MODEL),
        pl.BlockSpec((n_b, N_MEM, MEM_WIDTH), mk_map), pl.BlockSpec((n_b, N_MEM, MEM_WIDTH), mv_map),
        _const_spec((A_WIDTH, D_MODEL)), _const_spec((B_GROUP_WIDTH, D_MODEL)), _const_spec((D_MODEL, D_MODEL)),
        _const_spec((1, D_MODEL)), _const_spec((D_MODEL, MEM_WIDTH)), _const_spec((MEM_WIDTH, D_MODEL)),
        _const_spec((1, D_MODEL)), _const_spec((D_MODEL, LANES)), _const_spec((1, LANES)),
        pl.BlockSpec(memory_space=pl.ANY),
    ]
    args = [x, y_a, o_b, lse_b, gates, mem_k, mem_v, w['wa'], w['wb'], w['wo'], w['gm'], w['wq'], w['wmo'],
            w['gf'], w['wr'], w['br'], xn_all]
    blk0 = row0 // tm
    return pl.pallas_call(
        functools.partial(_mid_kernel, rows_per_batch=rows_per_batch, exact=w['wa'].dtype == F32,
                          n_sub=MID_SUB if n_b == 1 else 1),
        grid=(n // tm,),
        in_specs=in_specs,
        out_specs=[row(D_MODEL), pl.BlockSpec((tm, D_MODEL), lambda i: (i + blk0, 0)), row(LANES)],
        out_shape=[
            jax.ShapeDtypeStruct((n, D_MODEL), F32),
            jax.ShapeDtypeStruct(xn_all.shape, F32),
            jax.ShapeDtypeStruct((n, LANES), F32),
        ],
        input_output_aliases={len(args) - 1: 1},
        compiler_params=_params(1),
        name="mid",
    )(*args)


def _row_copy(src_hbm, idx, dst_buf, slot, j, sem):
    return pltpu.make_async_copy(src_hbm.at[pl.ds(idx, 1), :], dst_buf.at[slot, pl.ds(j, 1), :], sem.at[slot])


def _gather_start(idx_ref, src_hbm, dst_buf, slot, sem, n_rows):
    for j in range(n_rows):
        _row_copy(src_hbm, idx_ref[0, 0, j], dst_buf, slot, j, sem).start(priority=j % 2)


def _gather_wait(src_hbm, dst_buf, slot, sem, n_rows):
    for j in range(n_rows):
        _row_copy(src_hbm, 0, dst_buf, slot, j, sem).wait()


def _experts_kernel(meta_ref, be_ref, idx_ref, idx_1_ref, idx_2_ref, idx_3_ref, x_hbm, wg_ref, wu_ref, wd_ref, o_ref,
                    xbuf, sem):
    i = pl.program_id(0)
    n_used = meta_ref[0]
    slot = i % MOE_RING

    @pl.when(i == 0)
    def _():
        _gather_start(idx_ref, x_hbm, xbuf, 0, sem, MOE_ROWS)

    @pl.when((i == 0) & (n_used > 1))
    def _():
        _gather_start(idx_1_ref, x_hbm, xbuf, 1, sem, MOE_ROWS)

    @pl.when((i == 0) & (n_used > 2))
    def _():
        _gather_start(idx_2_ref, x_hbm, xbuf, 2, sem, MOE_ROWS)

    @pl.when(i + 3 < n_used)
    def _():
        _gather_start(idx_3_ref, x_hbm, xbuf, (i + 3) % MOE_RING, sem, MOE_ROWS)

    @pl.when(i < n_used)
    def _():
        _gather_wait(x_hbm, xbuf, slot, sem, MOE_ROWS)
        xb = xbuf[slot].astype(BF16)
        hg = _dot(xb, wg_ref[0].astype(BF16))
        hu = _dot(xb, wu_ref[0].astype(BF16))
        hh = (jax.nn.silu(hg) * hu).astype(BF16)
        o_ref[...] = _dot(hh, wd_ref[0].astype(BF16))

    @pl.when(i >= n_used)
    def _():
        o_ref[...] = jnp.zeros_like(o_ref)


def _experts(xn_all, row_tok, block_e, n_used, w_gate, w_up, w_down):
    n_blocks = block_e.shape[0]
    idx3 = row_tok.reshape(n_blocks, 1, MOE_ROWS)
    idx_spec = lambda f: pl.BlockSpec((1, 1, MOE_ROWS), f, memory_space=pltpu.SMEM)
    grid_spec = pltpu.PrefetchScalarGridSpec(
        num_scalar_prefetch=2,
        grid=(n_blocks,),
        in_specs=[
            idx_spec(lambda i, meta, be: (i, 0, 0)),
            idx_spec(lambda i, meta, be: (jnp.minimum(i + 1, n_blocks - 1), 0, 0)),
            idx_spec(lambda i, meta, be: (jnp.minimum(i + 2, n_blocks - 1), 0, 0)),
            idx_spec(lambda i, meta, be: (jnp.minimum(i + 3, n_blocks - 1), 0, 0)),
            pl.BlockSpec(memory_space=pl.ANY),
            pl.BlockSpec((1, D_MODEL, EXPERT_FF), lambda i, meta, be: (be[i], 0, 0)),
            pl.BlockSpec((1, D_MODEL, EXPERT_FF), lambda i, meta, be: (be[i], 0, 0)),
            pl.BlockSpec((1, EXPERT_FF, D_MODEL), lambda i, meta, be: (be[i], 0, 0)),
        ],
        out_specs=pl.BlockSpec((MOE_ROWS, D_MODEL), lambda i, meta, be: (i, 0)),
        scratch_shapes=[pltpu.VMEM((MOE_RING, MOE_ROWS, D_MODEL), F32), pltpu.SemaphoreType.DMA((MOE_RING,))],
    )
    return pl.pallas_call(
        _experts_kernel,
        grid_spec=grid_spec,
        out_shape=jax.ShapeDtypeStruct((n_blocks * MOE_ROWS, D_MODEL), F32),
        compiler_params=_params(1),
        name="experts",
    )(n_used.reshape(1), block_e, idx3, idx3, idx3, idx3, xn_all, w_gate, w_up, w_down)


def _combine_kernel(pos_ref, pos_next_ref, x_ref, route_ref, yb_hbm, g_ref, o_ref, ybuf, sem, *, n_tiles):
    i = pl.program_id(0)
    tm = x_ref.shape[0]
    slot = i % 2

    @pl.when(i == 0)
    def _():
        _gather_start(pos_ref, yb_hbm, ybuf, 0, sem, 2 * tm)

    if n_tiles > 1:
        @pl.when(i + 1 < n_tiles)
        def _():
            _gather_start(pos_next_ref, yb_hbm, ybuf, 1 - slot, sem, 2 * tm)

    _gather_wait(yb_hbm, ybuf, slot, sem, 2 * tm)
    route = route_ref[...]
    y = x_ref[...] + (route[:, 2:3] * ybuf[slot, :tm, :] + route[:, 3:4] * ybuf[slot, tm:, :])
    o_ref[...] = _rms(y, g_ref[...])


def _combine(x2, route, pos, yb, g_final, tm):
    n = x2.shape[0]
    n_tiles = n // tm
    pos_spec = lambda f: pl.BlockSpec((1, 1, 2 * tm), f, memory_space=pltpu.SMEM)
    return pl.pallas_call(
        functools.partial(_combine_kernel, n_tiles=n_tiles),
        grid=(n_tiles,),
        in_specs=[
            pos_spec(lambda i: (i, 0, 0)),
            pos_spec(lambda i: (jnp.minimum(i + 1, n_tiles - 1), 0, 0)),
            pl.BlockSpec((tm, D_MODEL), lambda i: (i, 0)),
            pl.BlockSpec((tm, LANES), lambda i: (i, 0)),
            pl.BlockSpec(memory_space=pl.ANY),
            _const_spec((1, D_MODEL)),
        ],
        out_specs=pl.BlockSpec((tm, D_MODEL), lambda i: (i, 0)),
        out_shape=jax.ShapeDtypeStruct((n, D_MODEL), F32),
        scratch_shapes=[pltpu.VMEM((2, 2 * tm, D_MODEL), F32), pltpu.SemaphoreType.DMA((2,))],
        compiler_params=_params(1),
        name="combine",
    )(pos, pos, x2, route, yb, g_final.reshape(1, -1))


def _dispatch(eid):
    n_tok = eid.shape[0]
    n_rows = n_tok * TOP_K
    n_blocks = n_rows // MOE_ROWS + N_EXPERTS
    flat_e = eid.reshape(-1)
    onehot = (flat_e[:, None] == jnp.arange(N_EXPERTS, dtype=jnp.int32)[None, :]).astype(jnp.int32)
    csum = jnp.cumsum(onehot, axis=0)
    counts = csum[-1]
    rank = jnp.sum((csum - onehot) * onehot, axis=1)
    padded = (counts + MOE_ROWS - 1) // MOE_ROWS * MOE_ROWS
    pad_end = jnp.cumsum(padded)
    pad_start = pad_end - padded
    dest = pad_start[flat_e] + rank
    flat_tok = jnp.arange(n_rows, dtype=jnp.int32) // TOP_K
    row_tok = jnp.zeros((n_blocks * MOE_ROWS,), jnp.int32).at[dest].set(flat_tok, unique_indices=True)
    block_start = jnp.arange(n_blocks, dtype=jnp.int32) * MOE_ROWS
    block_e = jnp.minimum(jnp.sum((pad_end[None, :] <= block_start[:, None]).astype(jnp.int32), axis=1), N_EXPERTS - 1)
    n_used = (pad_end[-1] // MOE_ROWS).astype(jnp.int32)
    return row_tok, block_e, n_used, dest.reshape(n_tok, TOP_K).astype(jnp.int32)


def _tile_pos(pos, tm):
    n = pos.shape[0]
    return pos.reshape(n // tm, tm, TOP_K).transpose(0, 2, 1).reshape(n // tm, 1, TOP_K * tm)


def _slabs_to_rows(qkv_slabs):
    order = np.array([_slab_position(j) for j in range(SWA_SLABS)])
    return jnp.transpose(qkv_slabs[order], (1, 0, 2)).reshape(qkv_slabs.shape[1], B_PROJ)


def _rows_to_pair_slabs(t):
    n = t.shape[0]
    return jnp.transpose(t.reshape(n, len(B_GROUPS), SWA_PAIRS, LANES), (2, 1, 0, 3)).reshape(-1, n, LANES)


def kernel(x_prompt, x_sample, state_rwkv, state_shift, cache_swa_w128, cache_swa_w512, cache_swa_w2048,
           cache_mem_kv, mem_prompt, norm_mix_g, w_in, b_gate, rwkv_mu, rwkv_w0, rwkv_w2, rwkv_a0, rwkv_a2,
           rwkv_g2, rwkv_k_k, rwkv_k_a, rwkv_r_k, rwkv_lnx_g, rwkv_lnx_b, w_branch_a, w_branch_b, w_out,
           norm_mem_g, norm_memkv_g, w_mem_q, w_mem_kv, w_mem_out, norm_ffn_g, w_router_group, b_router_group,
           w_router_expert, b_router_expert, w_exp_gate, w_exp_up, w_exp_down, norm_final_g):
    bsz, s_len, _ = x_prompt.shape
    dbs, t_len, _ = x_sample.shape
    n_p, n_s = bsz * s_len, dbs * t_len
    rw = dict(rwkv_mu=rwkv_mu, rwkv_w0=rwkv_w0, rwkv_w2=rwkv_w2, rwkv_a0=rwkv_a0, rwkv_a2=rwkv_a2, rwkv_g2=rwkv_g2,
              rwkv_k_k=rwkv_k_k, rwkv_k_a=rwkv_k_a, rwkv_r_k=rwkv_r_k, rwkv_lnx_g=rwkv_lnx_g, rwkv_lnx_b=rwkv_lnx_b)
    pad = LANES - N_GROUPS - N_EXPERTS
    mid_w_s = dict(
        wa=w_branch_a, wb=w_branch_b, wo=w_out, gm=norm_mem_g.reshape(1, -1), wq=w_mem_q, wmo=w_mem_out,
        gf=norm_ffn_g.reshape(1, -1),
        wr=jnp.concatenate([w_router_group, w_router_expert, jnp.zeros((D_MODEL, pad), F32)], axis=1),
        br=jnp.concatenate([b_router_group, b_router_expert, jnp.zeros((pad,), F32)]).reshape(1, -1))
    mid_w_p = {k: (v.astype(BF16) if k.startswith('w') else v) for k, v in mid_w_s.items()}
    w_in_b = w_in.astype(BF16)

    xp = x_prompt.reshape(n_p, D_MODEL)
    tm = 512
    ua_p, qkv_p, gates_p, *p_caches = _in_proj(xp, norm_mix_g, w_in_b, b_gate, _rope_tables(jnp.arange(s_len)), tm,
                                               cache_seq=(bsz, s_len))
    s0 = jnp.zeros((bsz, A_HEADS, A_HEAD_DIM, A_HEAD_DIM), F32)
    ya_p, st_p, shift_p = _rwkv(ua_p.reshape(bsz, s_len, A_PROJ), jnp.zeros((bsz, A_PROJ), F32), s0, rw,
                                RWKV_CHUNK, s_len)
    ob_p, lse_p = _swa_prompt(qkv_p, bsz, s_len)
    p_bufs = [c.reshape(c.shape[:3] + (B_HEADS_PER_GROUP, B_HEAD_DIM)) for c in p_caches]

    memkv = _norm_matmul(mem_prompt.reshape(bsz * N_MEM, D_MODEL), norm_memkv_g, w_mem_kv.astype(BF16), 256)
    memkv3 = memkv.reshape(bsz, N_MEM, 2 * MEM_WIDTH)
    mem_kv_prompt = memkv3.reshape(bsz, N_MEM, 2, MEM_HEADS, MEM_HEAD_DIM).transpose(0, 2, 1, 3, 4)
    tiles_per_batch = s_len // tm
    xn_all = jnp.zeros((n_p + n_s, D_MODEL), F32)
    x2_p, xn_all, route_p = _mid(
        xp, ya_p.reshape(n_p, A_WIDTH), ob_p, lse_p, gates_p, memkv3, memkv3,
        lambda i: (i // tiles_per_batch, 0, 0), lambda i: (i // tiles_per_batch, 0, 1),
        1, s_len, mid_w_p, tm, xn_all, 0)

    xs = x_sample.reshape(n_s, D_MODEL)
    pos_s = PAST_LEN + (jnp.arange(n_s) % t_len)
    ua_s, qkv_s, gates_s = _in_proj(xs, norm_mix_g, w_in, b_gate, _rope_tables(pos_s), n_s)
    t_pad = 8
    ua_s3 = jnp.pad(ua_s.reshape(dbs, t_len, A_PROJ), ((0, 0), (0, t_pad - t_len), (0, 0)))
    ya_s, st_s, shift_s = _rwkv(ua_s3, state_shift, state_rwkv, rw, 0, t_len)
    ya_s = ya_s[:, :t_len].reshape(n_s, A_WIDTH)
    caches = [jnp.transpose(c, (0, 1, 3, 4, 2)) for c in (cache_swa_w128, cache_swa_w512, cache_swa_w2048)]
    ob_s, lse_s, nb0, nb1, nb2 = _swa_sample(_slabs_to_rows(qkv_s).reshape(dbs, t_len, B_PROJ), caches)
    s_bufs = [jnp.transpose(nb, (0, 1, 4, 2, 3)) for nb in (nb0, nb1, nb2)]
    mem_s = cache_mem_kv.reshape(dbs, 2 * N_MEM, MEM_WIDTH)
    tm_s = 32
    x2_s, xn_all, route_s = _mid(
        xs, ya_s, _rows_to_pair_slabs(ob_s.reshape(n_s, B_WIDTH)), _rows_to_pair_slabs(lse_s.reshape(n_s, B_WIDTH)),
        gates_s, mem_s, mem_s,
        lambda i: (i, 0, 0), lambda i: (i, 1, 0), tm_s // t_len, t_len, mid_w_s, tm_s, xn_all, n_p)

    route = jnp.concatenate([route_p, route_s], axis=0)
    eid = route[:, :TOP_K].astype(jnp.int32)
    row_tok, block_e, n_used, pos = _dispatch(eid)
    yb = _experts(xn_all, row_tok, block_e, n_used, w_exp_gate, w_exp_up, w_exp_down)
    tm_c = 256
    y_p = _combine(x2_p, route_p, _tile_pos(pos[:n_p], tm_c), yb, norm_final_g, tm_c)
    tm_c = min(tm_c, n_s)
    y_s = _combine(x2_s, route_s, _tile_pos(pos[n_p:], tm_c), yb, norm_final_g, tm_c)

    return (y_p.reshape(bsz, s_len, D_MODEL), y_s.reshape(dbs, t_len, D_MODEL),
            st_p, shift_p.reshape(bsz, A_PROJ), p_bufs[0], p_bufs[1], p_bufs[2], mem_kv_prompt,
            st_s, shift_s.reshape(dbs, A_PROJ), s_bufs[0], s_bufs[1], s_bufs[2])
```
